```python
import math
import jax, jax.numpy as jnp
from jax import lax
import numpy as np

D_MODEL = 2048
BATCH = 4
SEQ = 4096
DEPTH = 2

HG_HEADS = 4
HG_DK = 128
HG_DV = 128
HG_WIDTH = HG_HEADS * HG_DV
HG_CHUNK = 128
SW_Q_HEADS = 16
SW_KV_HEADS = 2
SW_HEAD_DIM = 64
SW_WIDTH = SW_Q_HEADS * SW_HEAD_DIM
SW_KV_WIDTH = SW_KV_HEADS * SW_HEAD_DIM
SW_WINDOW = 128
SW_BLOCK = 128
SG_GROUPS = 4
SG_GROUP_DIM = 128
SG_WIDTH = SG_GROUPS * SG_GROUP_DIM
SG_CHUNK = 128
D_MIX = HG_WIDTH + SW_WIDTH + SG_WIDTH
IN_WIDTHS = (HG_WIDTH, HG_HEADS * HG_DK, HG_WIDTH, HG_WIDTH, SW_WIDTH, SW_KV_WIDTH, SW_KV_WIDTH, SG_WIDTH, SG_WIDTH)
D_IN = sum(IN_WIDTHS)
N_EXPERTS = 64
TOP_K = 8
D_EXPERT = 512
D_SHARED = 512
N_EXPERT_GROUPS = 8
TOPK_GROUPS = 4
ROUTED_SCALE = 2.5
MOE_BLOCK = 512
ALPHA = (2 * DEPTH) ** 0.25
BETA = (8 * DEPTH) ** -0.25
LN_EPS = 1e-5
RMS_EPS = 1e-6

kernel_name = 'hybrid_hgrn2_swa_gmlp_moe_deepnorm'


def layer_norm(x, g, b):
    xf = x.astype(jnp.float32)
    mu = jnp.mean(xf, axis=-1, keepdims=True)
    xc = xf - mu
    var = jnp.mean(xc * xc, axis=-1, keepdims=True)
    return (xc * lax.rsqrt(var + LN_EPS) * g.astype(jnp.float32) + b.astype(jnp.float32)).astype(x.dtype)


def rms_norm(x, g):
    xf = x.astype(jnp.float32)
    return xf * lax.rsqrt(jnp.mean(xf * xf, axis=-1, keepdims=True) + RMS_EPS) * g.astype(jnp.float32)


def hgrn_lower_bounds(lb_logits):
    p = jax.nn.softmax(lb_logits.astype(jnp.float32), axis=0)
    c = jnp.cumsum(p, axis=0)
    return c - c[0:1]


def hgrn2_mixer(q, f_logit, i, g, lb, gain):
    B, S, _ = q.shape
    nc = S // HG_CHUNK
    f32 = jnp.float32
    q = jax.nn.silu(q.astype(f32))
    f_logit = f_logit.astype(f32)
    lb = lb.astype(f32)
    log_f = jnp.logaddexp(jnp.log(lb), jnp.log1p(-lb) + jax.nn.log_sigmoid(f_logit))
    k = (1.0 - lb) * jax.nn.sigmoid(-f_logit)

    def to_chunks(a):
        return a.reshape(B, nc, HG_CHUNK, HG_HEADS, -1).transpose(1, 0, 3, 2, 4)

    tri = jnp.tril(jnp.ones((HG_CHUNK, HG_CHUNK), bool))[:, :, None]

    def step(state, inp):
        qc, kc, vc, lfc = inp
        b = jnp.cumsum(lfc, axis=2)
        o_inter = jnp.einsum('bhtk,bhkv->bhtv', qc * jnp.exp(b), state)
        diff = b[:, :, :, None, :] - b[:, :, None, :, :]
        decay = jnp.exp(jnp.where(tri, diff, -jnp.inf))
        scores = jnp.einsum('bhtsk,bhsk->bhts', qc[:, :, :, None, :] * decay, kc)
        o_intra = jnp.einsum('bhts,bhsv->bhtv', scores, vc)
        b_end = b[:, :, -1, :]
        k_dec = kc * jnp.exp(b_end[:, :, None, :] - b)
        state = jnp.exp(b_end)[..., None] * state + jnp.einsum('bhsk,bhsv->bhkv', k_dec, vc)
        return state, o_inter + o_intra

    s0 = jnp.zeros((B, HG_HEADS, HG_DK, HG_DV), f32)
    _, o = lax.scan(step, s0, (to_chunks(q), to_chunks(k), to_chunks(i.astype(f32)), to_chunks(log_f)))
    o = o.transpose(1, 0, 3, 2, 4).reshape(B, S, HG_HEADS, HG_DV)
    o = rms_norm(o, gain.reshape(HG_HEADS, HG_DV))
    return o.reshape(B, S, HG_WIDTH) * jax.nn.silu(g.astype(f32))


def swa_mixer(q, k, v, sinks, gain):
    B, S, _ = q.shape
    nb = S // SW_BLOCK
    G = SW_Q_HEADS // SW_KV_HEADS
    f32 = jnp.float32
    qb = q.astype(f32).reshape(B, nb, SW_BLOCK, SW_KV_HEADS, G, SW_HEAD_DIM)
    pad = ((0, 0), (SW_BLOCK, 0), (0, 0))
    kp = jnp.pad(k.astype(f32), pad).reshape(B, nb + 1, SW_BLOCK, SW_KV_HEADS, SW_HEAD_DIM)
    vp = jnp.pad(v.astype(f32), pad).reshape(B, nb + 1, SW_BLOCK, SW_KV_HEADS, SW_HEAD_DIM)
    kband = jnp.concatenate([kp[:, :-1], kp[:, 1:]], axis=2)
    vband = jnp.concatenate([vp[:, :-1], vp[:, 1:]], axis=2)
    scores = jnp.einsum('bntkgd,bnskd->bnkgts', qb, kband) * (SW_HEAD_DIM ** -0.5)
    t = jnp.arange(SW_BLOCK)[:, None]
    s = jnp.arange(2 * SW_BLOCK)[None, :]
    rel = t + SW_BLOCK - s
    band = (rel >= 0) & (rel < SW_WINDOW)
    key_pos = jnp.arange(nb)[:, None] * SW_BLOCK + jnp.arange(2 * SW_BLOCK)[None, :] - SW_BLOCK
    mask = band[None] & (key_pos >= 0)[:, None, :]
    scores = jnp.where(mask[None, :, None, None], scores, -jnp.inf)
    sink = sinks.astype(f32).reshape(1, 1, SW_KV_HEADS, G, 1, 1)
    m = jnp.maximum(scores.max(axis=-1, keepdims=True), sink)
    p = jnp.exp(scores - m)
    denom = p.sum(axis=-1, keepdims=True) + jnp.exp(sink - m)
    o = jnp.einsum('bnkgts,bnskd->bntkgd', p / denom, vband)
    return rms_norm(o.reshape(B, S, SW_WIDTH), gain)


def sgu_mixer(u, v, ln_g, ln_b, w_s, b_s, gain):
    B, S, _ = u.shape
    nc = S // SG_CHUNK
    f32 = jnp.float32
    u = jax.nn.gelu(u.astype(f32), approximate=False)
    v = jax.nn.gelu(v.astype(f32), approximate=False)
    v = layer_norm(v, ln_g, ln_b)
    v = v.reshape(B, nc, SG_CHUNK, SG_GROUPS, SG_GROUP_DIM)
    w = w_s.astype(f32) * jnp.tril(jnp.ones((SG_CHUNK, SG_CHUNK), f32))[None]
    mixed = jnp.einsum('gts,bnsgc->bntgc', w, v) + b_s.astype(f32).T[None, None, :, :, None]
    o = u * mixed.reshape(B, S, SG_WIDTH)
    return rms_norm(o, gain)


def swiglu(x, wg, wu, wd):
    return (jax.nn.silu(x @ wg) * (x @ wu)) @ wd


def moe_ffn(x, router_w, router_bias, w_gate, w_up, w_down, sh_gate, sh_up, sh_down):
    B, S, D = x.shape
    T = B * S
    TK = T * TOP_K
    xf = x.reshape(T, D)
    scores = jax.nn.sigmoid((xf @ router_w).astype(jnp.float32))
    sel = scores + router_bias.astype(jnp.float32)
    per_group = N_EXPERTS // N_EXPERT_GROUPS
    grp_score = lax.top_k(sel.reshape(T, N_EXPERT_GROUPS, per_group), 2)[0].sum(-1)
    _, top_grp = lax.top_k(grp_score, TOPK_GROUPS)
    grp_mask = jnp.any(top_grp[..., None] == jnp.arange(N_EXPERT_GROUPS), axis=-2)
    sel = jnp.where(jnp.repeat(grp_mask, per_group, axis=-1), sel, -jnp.inf)
    _, eidx = lax.top_k(sel, TOP_K)
    gates = jnp.take_along_axis(scores, eidx, axis=-1)
    gates = gates / gates.sum(-1, keepdims=True) * ROUTED_SCALE

    flat_e = eidx.reshape(TK).astype(jnp.int32)
    flat_g = gates.reshape(TK)
    sorted_e, order = lax.sort((flat_e, jnp.arange(TK, dtype=jnp.int32)), num_keys=1, is_stable=True)
    counts = jnp.bincount(flat_e, length=N_EXPERTS)
    starts = jnp.cumsum(counts) - counts
    padded = (counts + MOE_BLOCK - 1) // MOE_BLOCK * MOE_BLOCK
    pend = jnp.cumsum(padded)
    pstart = pend - padded
    dest = pstart[sorted_e] + (jnp.arange(TK) - starts[sorted_e])
    n_blocks = TK // MOE_BLOCK + N_EXPERTS
    P = n_blocks * MOE_BLOCK
    slot = jnp.full((P,), TK, jnp.int32).at[dest].set(order)
    valid = slot < TK
    tok = jnp.where(valid, slot // TOP_K, 0)
    wts = jnp.where(valid, flat_g[jnp.minimum(slot, TK - 1)], 0.0)
    block_e = jnp.minimum(jnp.searchsorted(pend, jnp.arange(n_blocks) * MOE_BLOCK, side='right'), N_EXPERTS - 1)

    def body(y, inp):
        tk, wt, e = inp
        xb = xf[tk]
        yb = swiglu(xb, w_gate[e], w_up[e], w_down[e]) * wt[:, None]
        return y.at[tk].add(yb.astype(y.dtype)), None

    y0 = swiglu(xf, sh_gate, sh_up, sh_down)
    y, _ = lax.scan(body, y0, (tok.reshape(n_blocks, MOE_BLOCK), wts.reshape(n_blocks, MOE_BLOCK), block_e))
    return y.reshape(B, S, D)


def setup_inputs(seed: int = 0) -> dict:
    key = jax.random.key(seed)
    ks = jax.random.split(key, 32)
    n = jax.random.normal
    f32 = jnp.float32
    L, D, E, F = DEPTH, D_MODEL, N_EXPERTS, D_EXPERT
    return {
        'x': n(ks[0], (BATCH, SEQ, D), f32),
        'ln0_gain': 1.0 + 0.01 * n(ks[1], (D,), f32),
        'ln0_bias': 0.01 * n(ks[2], (D,), f32),
        'w_in': n(ks[3], (L, D, D_IN), f32) * D ** -0.5,
        'b_in': 0.01 * n(ks[4], (L, D_IN), f32),
        'hg_lb_logits': n(ks[5], (L, HG_HEADS * HG_DK), f32),
        'sw_sinks': n(ks[6], (L, SW_Q_HEADS), f32),
        'sg_ln_gain': 1.0 + 0.01 * n(ks[7], (L, SG_WIDTH), f32),
        'sg_ln_bias': 0.01 * n(ks[8], (L, SG_WIDTH), f32),
        'sg_w_s': n(ks[9], (L, SG_GROUPS, SG_CHUNK, SG_CHUNK), f32) * SG_CHUNK ** -0.5,
        'sg_b_s': 1.0 + 0.1 * n(ks[10], (L, SG_GROUPS, SG_CHUNK), f32),
        'mix_gain': 1.0 + 0.01 * n(ks[11], (L, D_MIX), f32),
        'w_out': n(ks[12], (L, D_MIX, D), f32) * (D_MIX ** -0.5 * BETA),
        'ln1_gain': 1.0 + 0.01 * n(ks[13], (L, D), f32),
        'ln1_bias': 0.01 * n(ks[14], (L, D), f32),
        'router_w': n(ks[15], (L, D, E), f32) * D ** -0.5,
        'router_bias': 0.01 * n(ks[16], (L, E), f32),
        'exp_w_gate': n(ks[17], (L, E, D, F), f32) * D ** -0.5,
        'exp_w_up': n(ks[18], (L, E, D, F), f32) * D ** -0.5,
        'exp_w_down': n(ks[19], (L, E, F, D), f32) * (F ** -0.5 * BETA),
        'sh_w_gate': n(ks[20], (L, D, D_SHARED), f32) * D ** -0.5,
        'sh_w_up': n(ks[21], (L, D, D_SHARED), f32) * D ** -0.5,
        'sh_w_down': n(ks[22], (L, D_SHARED, D), f32) * (D_SHARED ** -0.5 * BETA),
        'ln2_gain': 1.0 + 0.01 * n(ks[23], (L, D), f32),
        'ln2_bias': 0.01 * n(ks[24], (L, D), f32),
    }


def reference(x, ln0_gain, ln0_bias, w_in, b_in, hg_lb_logits, sw_sinks, sg_ln_gain, sg_ln_bias, sg_w_s, sg_b_s, mix_gain, w_out, ln1_gain, ln1_bias, router_w, router_bias, exp_w_gate, exp_w_up, exp_w_down, sh_w_gate, sh_w_up, sh_w_down, ln2_gain, ln2_bias):
    lbs = hgrn_lower_bounds(hg_lb_logits)
    offs = []
    acc = 0
    for w in IN_WIDTHS[:-1]:
        acc += w
        offs.append(acc)
    a0, a1 = HG_WIDTH, HG_WIDTH + SW_WIDTH
    h = layer_norm(x, ln0_gain, ln0_bias)
    for l in range(DEPTH):
        proj = jnp.einsum('bsd,dn->bsn', h, w_in[l]) + b_in[l]
        hq, hf, hi, hg, aq, ak, av, su, sv = jnp.split(proj, offs, axis=-1)
        o_a = hgrn2_mixer(hq, hf, hi, hg, lbs[l], mix_gain[l, :a0])
        o_b = swa_mixer(aq, ak, av, sw_sinks[l], mix_gain[l, a0:a1])
        o_c = sgu_mixer(su, sv, sg_ln_gain[l], sg_ln_bias[l], sg_w_s[l], sg_b_s[l], mix_gain[l, a1:])
        mixed = jnp.concatenate([o_a, o_b, o_c], axis=-1).astype(h.dtype)
        mixed = jnp.einsum('bsm,md->bsd', mixed, w_out[l])
        h = layer_norm(ALPHA * h + mixed, ln1_gain[l], ln1_bias[l])
        ffn = moe_ffn(h, router_w[l], router_bias[l], exp_w_gate[l], exp_w_up[l], exp_w_down[l], sh_w_gate[l], sh_w_up[l], sh_w_down[l])
        h = layer_norm(ALPHA * h + ffn.astype(h.dtype), ln2_gain[l], ln2_bias[l])
    return h
```

```python
import functools

import numpy as np
import jax
import jax.numpy as jnp
from jax import lax
from jax.experimental import pallas as pl
from jax.experimental.pallas import tpu as pltpu

F32 = jnp.float32
BF16 = jnp.bfloat16

D_MODEL = 2048
DEPTH = 2
HG_HEADS = 4
HG_DK = 128
HG_WIDTH = HG_HEADS * HG_DK
CHUNK = 128
SW_Q_HEADS = 16
SW_KV_HEADS = 2
SW_HEAD_DIM = 64
SW_WIDTH = SW_Q_HEADS * SW_HEAD_DIM
SW_KV_WIDTH = SW_KV_HEADS * SW_HEAD_DIM
SG_GROUPS = 4
SG_WIDTH = SG_GROUPS * CHUNK
D_IN = 4 * HG_WIDTH + SW_WIDTH + 2 * SW_KV_WIDTH + 2 * SG_WIDTH
N_EXPERTS = 64
TOP_K = 8
D_EXPERT = 512
N_EXPERT_GROUPS = 8
TOPK_GROUPS = 4
ROUTED_SCALE = 2.5
ALPHA = (2 * DEPTH) ** 0.25
LN_EPS = 1e-5
RMS_EPS = 1e-6

_SWQ_OFF = 4 * HG_WIDTH
_SGU_OFF = _SWQ_OFF + SW_WIDTH
_SWK_OFF = _SGU_OFF + 2 * SG_WIDTH
_ORIG_SWK_OFF = _SWQ_OFF + SW_WIDTH
_IN_PERM = np.concatenate([np.arange(0, _ORIG_SWK_OFF),
                           np.arange(_ORIG_SWK_OFF + 2 * SW_KV_WIDTH, D_IN),
                           np.arange(_ORIG_SWK_OFF, _ORIG_SWK_OFF + 2 * SW_KV_WIDTH)])

MOE_ROWS = 256
ROUTER_TILE = 512
COMBINE_TILE = 128
VMEM_LIMIT = 56 * 1024 * 1024

_LEVEL_SIZES = (64, 32, 16, 8, 4, 2, 1)
_N_LEVELS = len(_LEVEL_SIZES)


def _hgrn_constants():
    t = np.arange(CHUNK)[:, None]
    u = np.arange(CHUNK)[None, :]
    mats = []
    level = np.full((CHUNK, CHUNK), -1, np.int32)
    for li, s in enumerate(_LEVEL_SIZES):
        blk = t // s
        odd = (blk % 2) == 1
        m_query = (u >= blk * s) & (u <= t)
        m_key = (u > t) & (u <= (blk + 1) * s - 1)
        mats.append(np.where(odd, m_query, m_key))
        pair = ((t // s) % 2 == 1) & ((u // s) == (t // s) - 1)
        level[pair] = li
    mats.append(u <= t)
    mats.append(u > t)
    level[np.arange(CHUNK), np.arange(CHUNK)] = _N_LEVELS
    return np.concatenate(mats, 0).astype(np.float32), level


_HGRN_MSTACK, _HGRN_LEVEL = _hgrn_constants()


def _layer_norm(x, g, b):
    mu = jnp.mean(x, axis=-1, keepdims=True)
    xc = x - mu
    var = jnp.mean(xc * xc, axis=-1, keepdims=True)
    return xc * lax.rsqrt(var + LN_EPS) * g + b


def _silu(x):
    return x / (1.0 + jnp.exp(-x))


def _gelu(x):
    return 0.5 * x * (1.0 + lax.erf(x * (2.0 ** -0.5)))


def _dot(a, b):
    return jnp.dot(a, b, preferred_element_type=F32)


def _dot_nt(a, b):
    return lax.dot_general(a, b, (((1,), (1,)), ((), ())), preferred_element_type=F32)


def _dot_tn(a, b):
    return lax.dot_general(a, b, (((0,), (0,)), ((), ())), preferred_element_type=F32)


def _ln_kernel(x_ref, g_ref, b_ref, of_ref, ob_ref):
    y = _layer_norm(x_ref[...], g_ref[...], b_ref[...])
    of_ref[...] = y
    ob_ref[...] = y.astype(BF16)


def _ln(x, g, b, tm=256):
    t, d = x.shape
    row = pl.BlockSpec((tm, d), lambda i: (i, 0))
    vec = pl.BlockSpec((1, d), lambda i: (0, 0))
    return pl.pallas_call(
        _ln_kernel,
        grid=(t // tm,),
        in_specs=[row, vec, vec],
        out_specs=[row, row],
        out_shape=[jax.ShapeDtypeStruct((t, d), F32), jax.ShapeDtypeStruct((t, d), BF16)],
        name="ln0",
    )(x, g.reshape(1, d), b.reshape(1, d))


def _mm_bias_kernel(a_ref, w_ref, b_ref, o_ref):
    o_ref[...] = _dot(a_ref[...], w_ref[...]) + b_ref[...]


def _in_proj(a, w, b, tm=2048, tn=256):
    t, k = a.shape
    n = w.shape[1]
    tm = min(tm, t)
    return pl.pallas_call(
        _mm_bias_kernel,
        grid=(t // tm, n // tn),
        in_specs=[pl.BlockSpec((tm, k), lambda i, j: (i, 0)),
                  pl.BlockSpec((k, tn), lambda i, j: (0, j)),
                  pl.BlockSpec((1, tn), lambda i, j: (0, j))],
        out_specs=pl.BlockSpec((tm, tn), lambda i, j: (i, j)),
        out_shape=jax.ShapeDtypeStruct((t, n), F32),
        compiler_params=pltpu.CompilerParams(vmem_limit_bytes=VMEM_LIMIT),
        name="in_proj",
    )(a, w, b.reshape(1, n))


def _hgrn_kernel(q_ref, f_ref, i_ref, g_ref, llb_ref, l1lb_ref, oml_ref, gain_ref,
                 mstack_ref, level_ref, o_ref, state_ref):
    c = pl.program_id(1)

    @pl.when(c == 0)
    def _():
        state_ref[...] = jnp.zeros_like(state_ref)

    q = _silu(q_ref[...])
    z = f_ref[...]
    ez = jnp.exp(-jnp.abs(z))
    log_sig = jnp.minimum(z, 0.0) - jnp.log(1.0 + ez)
    sig_neg = jnp.where(z >= 0.0, ez, 1.0) / (1.0 + ez)
    k = oml_ref[...] * sig_neg
    a = llb_ref[...]
    bb = l1lb_ref[...] + log_sig
    log_f = jnp.maximum(a, bb) + jnp.log(1.0 + jnp.exp(-jnp.abs(a - bb)))

    lf_hi = log_f.astype(BF16)
    lf_lo = (log_f - lf_hi.astype(F32)).astype(BF16)
    m = mstack_ref[...]
    sums = _dot(m, lf_hi) + _dot(m, lf_lo)
    decay = jnp.exp(sums)

    rows = lax.broadcasted_iota(jnp.int32, (CHUNK, 1), 0)
    factors = []
    for li, s in enumerate(_LEVEL_SIZES):
        is_query = ((rows // s) % 2) == 1
        factors.append((jnp.where(is_query, q, k) * decay[li * CHUNK:(li + 1) * CHUNK]).astype(BF16))
    cum = _N_LEVELS * CHUNK
    q_in = (q * decay[cum:cum + CHUNK]).astype(BF16)
    k_dec = (k * decay[cum + CHUNK:cum + 2 * CHUNK]).astype(BF16)
    end_decay = decay[cum + CHUNK - 1:cum + CHUNK]
    qb = q.astype(BF16)
    kb = k.astype(BF16)
    vb = i_ref[...].astype(BF16)
    level = level_ref[...]
    gate = _silu(g_ref[...])
    gain = gain_ref[...]

    for h in range(HG_HEADS):
        hs = slice(h * HG_DK, (h + 1) * HG_DK)
        scores = jnp.zeros((CHUNK, CHUNK), F32)
        for li in range(_N_LEVELS):
            fl = factors[li][:, hs]
            scores = jnp.where(level == li, _dot_nt(fl, fl), scores)
        scores = jnp.where(level == _N_LEVELS, _dot_nt(qb[:, hs], kb[:, hs]), scores)
        st = state_ref[h]
        o = _dot(scores.astype(BF16), vb[:, hs]) + _dot_nt(q_in[:, hs], st.astype(BF16))
        state_ref[h] = st * end_decay[:, hs] + _dot_tn(vb[:, hs], k_dec[:, hs])
        ms = jnp.mean(o * o, axis=-1, keepdims=True)
        o = o * lax.rsqrt(ms + RMS_EPS) * gain[:, hs] * gate[:, hs]
        o_ref[:, hs] = o.astype(o_ref.dtype)


def _hgrn(proj3, llb, l1lb, oml, gain):
    b, s, _ = proj3.shape
    w = HG_WIDTH

    def col(j):
        return pl.BlockSpec((None, CHUNK, w), lambda bi, ci, j=j: (bi, ci, j))

    vec = pl.BlockSpec((1, w), lambda bi, ci: (0, 0))
    nm = _HGRN_MSTACK.shape[0]
    return pl.pallas_call(
        _hgrn_kernel,
        grid=(b, s // CHUNK),
        in_specs=[col(0), col(1), col(2), col(3), vec, vec, vec, vec,
                  pl.BlockSpec((nm, CHUNK), lambda bi, ci: (0, 0)),
                  pl.BlockSpec((CHUNK, CHUNK), lambda bi, ci: (0, 0))],
        out_specs=pl.BlockSpec((None, CHUNK, w), lambda bi, ci: (bi, ci, 0)),
        out_shape=jax.ShapeDtypeStruct((b, s, w), BF16),
        scratch_shapes=[pltpu.VMEM((HG_HEADS, HG_DK, HG_DK), F32)],
        compiler_params=pltpu.CompilerParams(dimension_semantics=("parallel", "arbitrary"),
                                             vmem_limit_bytes=VMEM_LIMIT),
        name="hgrn2",
    )(proj3, proj3, proj3, proj3, llb.reshape(1, w), l1lb.reshape(1, w), oml.reshape(1, w),
      gain.reshape(1, w), jnp.asarray(_HGRN_MSTACK, BF16), jnp.asarray(_HGRN_LEVEL))


def _swa_kernel(q_ref, kp_ref, kc_ref, vp_ref, vc_ref, sink_ref, gain_ref, o_ref):
    n = pl.program_id(1)
    g = SW_Q_HEADS // SW_KV_HEADS
    hd = SW_HEAD_DIM
    q = q_ref[...]
    kband = jnp.concatenate([kp_ref[...], kc_ref[...]], axis=0).astype(BF16)
    vband = jnp.concatenate([vp_ref[...], vc_ref[...]], axis=0).astype(BF16)
    t = lax.broadcasted_iota(jnp.int32, (g * CHUNK, 2 * CHUNK), 0) % CHUNK
    s = lax.broadcasted_iota(jnp.int32, (g * CHUNK, 2 * CHUNK), 1)
    rel = t + CHUNK - s
    mask = (rel >= 0) & (rel < CHUNK) & ((s >= CHUNK) | (n > 0))
    outs = []
    for kv in range(SW_KV_HEADS):
        qs = jnp.concatenate([q[:, (kv * g + j) * hd:(kv * g + j + 1) * hd] for j in range(g)], axis=0)
        sc = _dot_nt(qs.astype(BF16), kband[:, kv * hd:(kv + 1) * hd]) * (hd ** -0.5)
        sc = jnp.where(mask, sc, -jnp.inf)
        sink = sink_ref[kv]
        mx = jnp.maximum(jnp.max(sc, axis=-1, keepdims=True), sink)
        p = jnp.exp(sc - mx)
        denom = jnp.sum(p, axis=-1, keepdims=True) + jnp.exp(sink - mx)
        o = _dot((p / denom).astype(BF16), vband[:, kv * hd:(kv + 1) * hd])
        outs.extend(o[j * CHUNK:(j + 1) * CHUNK] for j in range(g))
    o = jnp.concatenate(outs, axis=-1)
    ms = jnp.mean(o * o, axis=-1, keepdims=True)
    o_ref[...] = (o * lax.rsqrt(ms + RMS_EPS) * gain_ref[...]).astype(o_ref.dtype)


def _swa(proj3, sinks, gain):
    b, s, _ = proj3.shape
    q_col = _SWQ_OFF // SW_WIDTH
    k_col = _SWK_OFF // SW_KV_WIDTH
    v_col = k_col + 1
    g = SW_Q_HEADS // SW_KV_HEADS
    sink_rows = jnp.repeat(sinks.astype(F32).reshape(SW_KV_HEADS, g), CHUNK, axis=1)[..., None]

    def kv_spec(col, prev):
        if prev:
            return pl.BlockSpec((None, CHUNK, SW_KV_WIDTH), lambda bi, ni: (bi, jnp.maximum(ni - 1, 0), col))
        return pl.BlockSpec((None, CHUNK, SW_KV_WIDTH), lambda bi, ni: (bi, ni, col))

    return pl.pallas_call(
        _swa_kernel,
        grid=(b, s // CHUNK),
        in_specs=[pl.BlockSpec((None, CHUNK, SW_WIDTH), lambda bi, ni: (bi, ni, q_col)),
                  kv_spec(k_col, True), kv_spec(k_col, False),
                  kv_spec(v_col, True), kv_spec(v_col, False),
                  pl.BlockSpec((SW_KV_HEADS, g * CHUNK, 1), lambda bi, ni: (0, 0, 0)),
                  pl.BlockSpec((1, SW_WIDTH), lambda bi, ni: (0, 0))],
        out_specs=pl.BlockSpec((None, CHUNK, SW_WIDTH), lambda bi, ni: (bi, ni, 0)),
        out_shape=jax.ShapeDtypeStruct((b, s, SW_WIDTH), BF16),
        compiler_params=pltpu.CompilerParams(dimension_semantics=("parallel", "parallel"),
                                             vmem_limit_bytes=VMEM_LIMIT),
        name="swa",
    )(proj3, proj3, proj3, proj3, proj3, sink_rows, gain.reshape(1, SW_WIDTH))


def _sgu_kernel(u_ref, v_ref, lng_ref, lnb_ref, w_ref, bs_ref, gain_ref, o_ref):
    u = _gelu(u_ref[...])
    v = _layer_norm(_gelu(v_ref[...]), lng_ref[...], lnb_ref[...]).astype(BF16)
    r = lax.broadcasted_iota(jnp.int32, (CHUNK, CHUNK), 0)
    c = lax.broadcasted_iota(jnp.int32, (CHUNK, CHUNK), 1)
    tril = c <= r
    bs = bs_ref[...]
    parts = []
    for gi in range(SG_GROUPS):
        gs = slice(gi * CHUNK, (gi + 1) * CHUNK)
        w = jnp.where(tril, w_ref[gi], 0.0).astype(BF16)
        parts.append(_dot(w, v[:, gs]) + bs[:, gi:gi + 1])
    o = u * jnp.concatenate(parts, axis=-1)
    ms = jnp.mean(o * o, axis=-1, keepdims=True)
    o_ref[...] = (o * lax.rsqrt(ms + RMS_EPS) * gain_ref[...]).astype(o_ref.dtype)


def _sgu(proj3, ln_g, ln_b, w_s, b_s, gain):
    b, s, _ = proj3.shape
    w = SG_WIDTH
    u_col = _SGU_OFF // SG_WIDTH
    vec = pl.BlockSpec((1, w), lambda bi, ci: (0, 0))
    return pl.pallas_call(
        _sgu_kernel,
        grid=(b, s // CHUNK),
        in_specs=[pl.BlockSpec((None, CHUNK, w), lambda bi, ci: (bi, ci, u_col)),
                  pl.BlockSpec((None, CHUNK, w), lambda bi, ci: (bi, ci, u_col + 1)),
                  vec, vec,
                  pl.BlockSpec((SG_GROUPS, CHUNK, CHUNK), lambda bi, ci: (0, 0, 0)),
                  pl.BlockSpec((CHUNK, SG_GROUPS), lambda bi, ci: (0, 0)),
                  vec],
        out_specs=pl.BlockSpec((None, CHUNK, w), lambda bi, ci: (bi, ci, 0)),
        out_shape=jax.ShapeDtypeStruct((b, s, w), BF16),
        compiler_params=pltpu.CompilerParams(dimension_semantics=("parallel", "parallel"),
                                             vmem_limit_bytes=VMEM_LIMIT),
        name="sgu",
    )(proj3, proj3, ln_g.reshape(1, w), ln_b.reshape(1, w), w_s, b_s.T, gain.reshape(1, w))


def _out_proj_kernel(oa_ref, ob_ref, oc_ref, w_ref, h_ref, g_ref, b_ref, of_ref):
    mixed = jnp.concatenate([oa_ref[...], ob_ref[...], oc_ref[...]], axis=-1)
    y = ALPHA * h_ref[...] + _dot(mixed, w_ref[...])
    of_ref[...] = _layer_norm(y, g_ref[...], b_ref[...])


def _out_proj(oa, ob, oc, w, h, g, b, tm=256):
    t, d = h.shape
    vec = pl.BlockSpec((1, d), lambda i: (0, 0))
    return pl.pallas_call(
        _out_proj_kernel,
        grid=(t // tm,),
        in_specs=[pl.BlockSpec((tm, oa.shape[1]), lambda i: (i, 0)),
                  pl.BlockSpec((tm, ob.shape[1]), lambda i: (i, 0)),
                  pl.BlockSpec((tm, oc.shape[1]), lambda i: (i, 0)),
                  pl.BlockSpec(w.shape, lambda i: (0, 0)),
                  pl.BlockSpec((tm, d), lambda i: (i, 0)), vec, vec],
        out_specs=pl.BlockSpec((tm, d), lambda i: (i, 0)),
        out_shape=jax.ShapeDtypeStruct((t, d), F32),
        compiler_params=pltpu.CompilerParams(vmem_limit_bytes=VMEM_LIMIT),
        name="out_proj_ln1",
    )(oa, ob, oc, w, h, g.reshape(1, d), b.reshape(1, d))


def _first_index_of_max(x, iota, size, axis):
    mx = jnp.max(x, axis=axis, keepdims=True)
    idx = jnp.min(jnp.where(x == mx, iota, size), axis=axis, keepdims=True)
    return mx, idx


def _router_kernel(h_ref, wt_ref, bias_ref, upper_ref, eidx_ref, gate_ref, rank_ref, count_ref, run_ref):
    i = pl.program_id(0)

    @pl.when(i == 0)
    def _():
        run_ref[...] = jnp.zeros_like(run_ref)

    tm = h_ref.shape[0]
    per_group = N_EXPERTS // N_EXPERT_GROUPS
    logits = lax.dot_general(wt_ref[...], h_ref[...], (((1,), (1,)), ((), ())),
                             precision=lax.Precision.HIGHEST, preferred_element_type=F32)
    scores = 1.0 / (1.0 + jnp.exp(-logits))
    sel = scores + bias_ref[...]
    sel3 = sel.reshape(N_EXPERT_GROUPS, per_group, tm)
    io3 = lax.broadcasted_iota(jnp.int32, sel3.shape, 1)
    m1, i1 = _first_index_of_max(sel3, io3, per_group, 1)
    m2 = jnp.max(jnp.where(io3 == i1, -jnp.inf, sel3), axis=1, keepdims=True)
    grp = (m1 + m2).reshape(N_EXPERT_GROUPS, tm)
    iog = lax.broadcasted_iota(jnp.int32, grp.shape, 0)
    keep = jnp.zeros(grp.shape, jnp.bool_)
    for _ in range(TOPK_GROUPS):
        _, gi = _first_index_of_max(grp, iog, N_EXPERT_GROUPS, 0)
        hit = iog == gi
        keep = keep | hit
        grp = jnp.where(hit, -jnp.inf, grp)
    keep3 = jnp.broadcast_to(keep.reshape(N_EXPERT_GROUPS, 1, tm), sel3.shape)
    cand = jnp.where(keep3, sel3, -jnp.inf).reshape(N_EXPERTS, tm)
    ioe = lax.broadcasted_iota(jnp.int32, cand.shape, 0)
    chosen = jnp.zeros(cand.shape, F32)
    idxs, gvals, hits = [], [], []
    for _ in range(TOP_K):
        _, ei = _first_index_of_max(cand, ioe, N_EXPERTS, 0)
        hit = ioe == ei
        idxs.append(ei)
        gvals.append(jnp.sum(jnp.where(hit, scores, 0.0), axis=0, keepdims=True))
        hits.append(hit)
        chosen = jnp.where(hit, 1.0, chosen)
        cand = jnp.where(hit, -jnp.inf, cand)
    gsum = functools.reduce(lambda a, b: a + b, gvals)
    prefix = _dot(chosen.astype(BF16), upper_ref[...])
    base = run_ref[...] + prefix
    for kk in range(TOP_K):
        eidx_ref[kk:kk + 1, :] = idxs[kk]
        gate_ref[kk:kk + 1, :] = gvals[kk] / gsum * ROUTED_SCALE
        rank_ref[kk:kk + 1, :] = jnp.sum(jnp.where(hits[kk], base, 0.0), axis=0, keepdims=True).astype(jnp.int32)
    run = run_ref[...] + jnp.sum(chosen, axis=1, keepdims=True)
    run_ref[...] = run
    count_ref[...] = run.astype(jnp.int32)


def _router(h, router_w, router_bias, tm=ROUTER_TILE):
    t, d = h.shape
    tm = min(tm, t)
    upper = jnp.asarray(np.triu(np.ones((tm, tm), np.float32), 1), BF16)
    slot = pl.BlockSpec((TOP_K, tm), lambda i: (0, i))
    return pl.pallas_call(
        _router_kernel,
        grid=(t // tm,),
        in_specs=[pl.BlockSpec((tm, d), lambda i: (i, 0)),
                  pl.BlockSpec((N_EXPERTS, d), lambda i: (0, 0)),
                  pl.BlockSpec((N_EXPERTS, 1), lambda i: (0, 0)),
                  pl.BlockSpec((tm, tm), lambda i: (0, 0))],
        out_specs=[slot, slot, slot, pl.BlockSpec((N_EXPERTS, 1), lambda i: (0, 0))],
        out_shape=[jax.ShapeDtypeStruct((TOP_K, t), jnp.int32),
                   jax.ShapeDtypeStruct((TOP_K, t), F32),
                   jax.ShapeDtypeStruct((TOP_K, t), jnp.int32),
                   jax.ShapeDtypeStruct((N_EXPERTS, 1), jnp.int32)],
        scratch_shapes=[pltpu.VMEM((N_EXPERTS, 1), F32)],
        compiler_params=pltpu.CompilerParams(dimension_semantics=("arbitrary",),
                                             vmem_limit_bytes=VMEM_LIMIT),
        name="router",
    )(h, router_w.T, router_bias.reshape(N_EXPERTS, 1), upper)


def _row_copy(src_hbm, row, dst, slot, r, sem):
    return pltpu.make_async_copy(src_hbm.at[pl.ds(row, 1)], dst.at[slot, pl.ds(r, 1)], sem.at[slot])


def _experts_kernel(be_ref, nu_ref, tokc_ref, tokn_ref, h_hbm, wg_ref, wu_ref, wd_ref, o_ref, xbuf, sem):
    b = pl.program_id(0)
    nu = nu_ref[0]
    rows = xbuf.shape[1]

    def start(tok_ref, slot):
        def body(r, carry):
            _row_copy(h_hbm, tok_ref[0, 0, r], xbuf, slot, r, sem).start()
            return carry
        lax.fori_loop(0, rows, body, 0)

    @pl.when(b == 0)
    def _():
        start(tokc_ref, 0)

    @pl.when(b + 1 < nu)
    def _():
        start(tokn_ref, (b + 1) % 2)

    @pl.when(b < nu)
    def _():
        slot = b % 2
        pltpu.make_async_copy(h_hbm.at[pl.ds(0, rows)], xbuf.at[slot], sem.at[slot]).wait()
        x = xbuf[slot].astype(BF16)
        act = _silu(_dot(x, wg_ref[...])) * _dot(x, wu_ref[...])
        o_ref[...] = _dot(act.astype(BF16), wd_ref[...])

    @pl.when(b >= nu)
    def _():
        o_ref[...] = jnp.zeros_like(o_ref)


def _experts(h, tok_blocks, block_e, n_used, wg, wu, wd):
    nb, _, rows = tok_blocks.shape
    t, d = h.shape
    f = wg.shape[-1]

    def clamp(b, nu):
        return jnp.minimum(b, nu[0] - 1)

    grid_spec = pltpu.PrefetchScalarGridSpec(
        num_scalar_prefetch=2,
        grid=(nb,),
        in_specs=[
            pl.BlockSpec((1, 1, rows), lambda b, be, nu: (clamp(b, nu), 0, 0), memory_space=pltpu.SMEM),
            pl.BlockSpec((1, 1, rows), lambda b, be, nu: (clamp(b + 1, nu), 0, 0), memory_space=pltpu.SMEM),
            pl.BlockSpec(memory_space=pl.ANY),
            pl.BlockSpec((None, d, f), lambda b, be, nu: (be[clamp(b, nu)], 0, 0)),
            pl.BlockSpec((None, d, f), lambda b, be, nu: (be[clamp(b, nu)], 0, 0)),
            pl.BlockSpec((None, f, d), lambda b, be, nu: (be[clamp(b, nu)], 0, 0)),
        ],
        out_specs=pl.BlockSpec((rows, d), lambda b, be, nu: (b, 0)),
        scratch_shapes=[pltpu.VMEM((2, rows, d), F32), pltpu.SemaphoreType.DMA((2,))],
    )
    return pl.pallas_call(
        _experts_kernel,
        grid_spec=grid_spec,
        out_shape=jax.ShapeDtypeStruct((nb * rows, d), F32),
        compiler_params=pltpu.CompilerParams(dimension_semantics=("arbitrary",),
                                             vmem_limit_bytes=VMEM_LIMIT),
        name="experts",
    )(block_e, n_used, tok_blocks, tok_blocks, h, wg, wu, wd)


def _combine_kernel(destc_ref, destn_ref, h_ref, gate_ref, y_hbm, sg_ref, su_ref, sd_ref, g_ref, b_ref,
                    of_ref, ob_ref, ybuf, sem):
    i = pl.program_id(0)
    n = pl.num_programs(0)
    tm = h_ref.shape[0]

    def start(dest_ref, slot):
        def body(r, carry):
            for kk in range(TOP_K):
                pltpu.make_async_copy(y_hbm.at[pl.ds(dest_ref[0, kk, r], 1)],
                                      ybuf.at[slot, kk, pl.ds(r, 1)], sem.at[slot]).start()
            return carry
        lax.fori_loop(0, tm, body, 0)

    @pl.when(i == 0)
    def _():
        start(destc_ref, 0)

    @pl.when(i + 1 < n)
    def _():
        start(destn_ref, (i + 1) % 2)

    slot = i % 2
    h = h_ref[...]
    hb = h.astype(BF16)
    act = _silu(_dot(hb, sg_ref[...])) * _dot(hb, su_ref[...])
    y = ALPHA * h + _dot(act.astype(BF16), sd_ref[...])
    for kk in range(TOP_K):
        pltpu.make_async_copy(y_hbm.at[pl.ds(0, tm)], ybuf.at[slot, kk], sem.at[slot]).wait()
    gates = gate_ref[...]
    for kk in range(TOP_K):
        y = y + gates[:, kk:kk + 1] * ybuf[slot, kk]
    out = _layer_norm(y, g_ref[...], b_ref[...])
    of_ref[...] = out
    ob_ref[...] = out.astype(BF16)


def _combine(h, dest_blocks, gates_t, yg, sg, su, sd, g, b, tm=COMBINE_TILE):
    t, d = h.shape
    n = t // tm
    f = sg.shape[1]
    vec = pl.BlockSpec((1, d), lambda i: (0, 0))
    row = pl.BlockSpec((tm, d), lambda i: (i, 0))
    return pl.pallas_call(
        _combine_kernel,
        grid=(n,),
        in_specs=[pl.BlockSpec((1, TOP_K, tm), lambda i: (i, 0, 0), memory_space=pltpu.SMEM),
                  pl.BlockSpec((1, TOP_K, tm), lambda i: (jnp.minimum(i + 1, n - 1), 0, 0),
                               memory_space=pltpu.SMEM),
                  row,
                  pl.BlockSpec((tm, TOP_K), lambda i: (i, 0)),
                  pl.BlockSpec(memory_space=pl.ANY),
                  pl.BlockSpec((d, f), lambda i: (0, 0)),
                  pl.BlockSpec((d, f), lambda i: (0, 0)),
                  pl.BlockSpec((f, d), lambda i: (0, 0)),
                  vec, vec],
        out_specs=[row, row],
        out_shape=[jax.ShapeDtypeStruct((t, d), F32), jax.ShapeDtypeStruct((t, d), BF16)],
        scratch_shapes=[pltpu.VMEM((2, TOP_K, tm, d), F32), pltpu.SemaphoreType.DMA((2,))],
        compiler_params=pltpu.CompilerParams(dimension_semantics=("arbitrary",),
                                             vmem_limit_bytes=VMEM_LIMIT),
        name="combine_ln2",
    )(dest_blocks, dest_blocks, h, gates_t, yg, sg, su, sd, g.reshape(1, d), b.reshape(1, d))


def _moe(h, router_w, router_bias, wg, wu, wd, sg, su, sd, ln_g, ln_b):
    t, d = h.shape
    tk = t * TOP_K
    n_blocks = tk // MOE_ROWS + N_EXPERTS
    eidx, gates, rank, counts = _router(h, router_w, router_bias)
    counts = counts.reshape(N_EXPERTS)
    padded = (counts + MOE_ROWS - 1) // MOE_ROWS * MOE_ROWS
    pend = jnp.cumsum(padded)
    pstart = pend - padded
    dest = pstart[eidx] + rank
    tok_ids = jnp.broadcast_to(jnp.arange(t, dtype=jnp.int32)[None, :], (TOP_K, t))
    tok = jnp.zeros((n_blocks * MOE_ROWS,), jnp.int32).at[dest.reshape(-1)].set(
        tok_ids.reshape(-1), unique_indices=True, indices_are_sorted=False)
    block_e = jnp.minimum(
        jnp.searchsorted(pend, jnp.arange(n_blocks, dtype=jnp.int32) * MOE_ROWS, side='right'),
        N_EXPERTS - 1).astype(jnp.int32)
    n_used = (pend[-1] // MOE_ROWS).astype(jnp.int32).reshape(1)
    yg = _experts(h, tok.reshape(n_blocks, 1, MOE_ROWS), block_e, n_used, wg, wu, wd)
    tm = min(COMBINE_TILE, t)
    dest_blocks = dest.reshape(TOP_K, t // tm, tm).transpose(1, 0, 2)
    return _combine(h, dest_blocks, gates.T, yg, sg, su, sd, ln_g, ln_b, tm=tm)


def kernel(x, ln0_gain, ln0_bias, w_in, b_in, hg_lb_logits, sw_sinks, sg_ln_gain, sg_ln_bias, sg_w_s, sg_b_s, mix_gain, w_out, ln1_gain, ln1_bias, router_w, router_bias, exp_w_gate, exp_w_up, exp_w_down, sh_w_gate, sh_w_up, sh_w_down, ln2_gain, ln2_bias):
    b, s, d = x.shape
    t = b * s
    p = jax.nn.softmax(hg_lb_logits.astype(F32), axis=0)
    cs = jnp.cumsum(p, axis=0)
    lbs = cs - cs[0:1]
    a0, a1 = HG_WIDTH, HG_WIDTH + SW_WIDTH

    h, hb = _ln(x.reshape(t, d), ln0_gain, ln0_bias)
    for l in range(DEPTH):
        proj = _in_proj(hb, w_in[l][:, _IN_PERM].astype(BF16), b_in[l][_IN_PERM])
        proj3 = proj.reshape(b, s, D_IN)
        lb = lbs[l]
        o_a = _hgrn(proj3, jnp.log(lb), jnp.log1p(-lb), 1.0 - lb, mix_gain[l, :a0])
        o_b = _swa(proj3, sw_sinks[l], mix_gain[l, a0:a1])
        o_c = _sgu(proj3, sg_ln_gain[l], sg_ln_bias[l], sg_w_s[l], sg_b_s[l], mix_gain[l, a1:])
        h = _out_proj(o_a.reshape(t, -1), o_b.reshape(t, -1), o_c.reshape(t, -1),
                      w_out[l].astype(BF16), h, ln1_gain[l], ln1_bias[l])
        h, hb = _moe(h, router_w[l], router_bias[l],
                     exp_w_gate[l].astype(BF16), exp_w_up[l].astype(BF16), exp_w_down[l].astype(BF16),
                     sh_w_gate[l].astype(BF16), sh_w_up[l].astype(BF16), sh_w_down[l].astype(BF16),
                     ln2_gain[l], ln2_bias[l])
    return h.reshape(b, s, d)
```

```python
import functools

import numpy as np
import jax
import jax.numpy as jnp
from jax import lax
from jax.experimental import pallas as pl
from jax.experimental.pallas import tpu as pltpu

F32 = jnp.float32
BF16 = jnp.bfloat16

D_MODEL = 2048
DEPTH = 2
HG_HEADS = 4
HG_DK = 128
HG_WIDTH = HG_HEADS * HG_DK
CHUNK = 128
SW_Q_HEADS = 16
SW_KV_HEADS = 2
SW_HEAD_DIM = 64
SW_WIDTH = SW_Q_HEADS * SW_HEAD_DIM
SW_KV_WIDTH = SW_KV_HEADS * SW_HEAD_DIM
SG_GROUPS = 4
SG_WIDTH = SG_GROUPS * CHUNK
D_IN = 4 * HG_WIDTH + SW_WIDTH + 2 * SW_KV_WIDTH + 2 * SG_WIDTH
N_EXPERTS = 64
TOP_K = 8
D_EXPERT = 512
N_EXPERT_GROUPS = 8
TOPK_GROUPS = 4
ROUTED_SCALE = 2.5
ALPHA = (2 * DEPTH) ** 0.25
LN_EPS = 1e-5
RMS_EPS = 1e-6

_SWQ_OFF = 4 * HG_WIDTH
_SGU_OFF = _SWQ_OFF + SW_WIDTH
_SWK_OFF = _SGU_OFF + 2 * SG_WIDTH


def _regroup_columns(a):
    k0 = _SWQ_OFF + SW_WIDTH
    k1 = k0 + 2 * SW_KV_WIDTH
    return jnp.concatenate([a[..., :k0], a[..., k1:], a[..., k0:k1]], axis=-1)


MOE_ROWS = 256
ROUTER_TILE = 512
COMBINE_TILE = 128
VMEM_LIMIT = 56 * 1024 * 1024

_LEVEL_SIZES = (64, 32, 16, 8, 4, 2, 1)
_N_LEVELS = len(_LEVEL_SIZES)


def _hgrn_constants():
    t = np.arange(CHUNK)[:, None]
    u = np.arange(CHUNK)[None, :]
    mats = []
    level = np.full((CHUNK, CHUNK), -1, np.int32)
    for li, s in enumerate(_LEVEL_SIZES):
        blk = t // s
        odd = (blk % 2) == 1
        m_query = (u >= blk * s) & (u <= t)
        m_key = (u > t) & (u <= (blk + 1) * s - 1)
        mats.append(np.where(odd, m_query, m_key))
        pair = ((t // s) % 2 == 1) & ((u // s) == (t // s) - 1)
        level[pair] = li
    mats.append(u <= t)
    mats.append(u > t)
    level[np.arange(CHUNK), np.arange(CHUNK)] = _N_LEVELS
    return np.concatenate(mats, 0).astype(np.float32), level


_HGRN_MSTACK, _HGRN_LEVEL = _hgrn_constants()


def _layer_norm(x, g, b):
    mu = jnp.mean(x, axis=-1, keepdims=True)
    xc = x - mu
    var = jnp.mean(xc * xc, axis=-1, keepdims=True)
    return xc * lax.rsqrt(var + LN_EPS) * g + b


def _silu(x):
    return x / (1.0 + jnp.exp(-x))


def _gelu(x):
    return 0.5 * x * (1.0 + lax.erf(x * (2.0 ** -0.5)))


def _dot(a, b):
    return jnp.dot(a, b, preferred_element_type=F32)


def _dot_nt(a, b):
    return lax.dot_general(a, b, (((1,), (1,)), ((), ())), preferred_element_type=F32)


def _dot_tn(a, b):
    return lax.dot_general(a, b, (((0,), (0,)), ((), ())), preferred_element_type=F32)


def _ln_kernel(x_ref, g_ref, b_ref, of_ref, ob_ref):
    y = _layer_norm(x_ref[...], g_ref[...], b_ref[...])
    of_ref[...] = y
    ob_ref[...] = y.astype(BF16)


def _ln(x, g, b, tm=256):
    t, d = x.shape
    row = pl.BlockSpec((tm, d), lambda i: (i, 0))
    vec = pl.BlockSpec((1, d), lambda i: (0, 0))
    return pl.pallas_call(
        _ln_kernel,
        grid=(t // tm,),
        in_specs=[row, vec, vec],
        out_specs=[row, row],
        out_shape=[jax.ShapeDtypeStruct((t, d), F32), jax.ShapeDtypeStruct((t, d), BF16)],
        name="ln0",
    )(x, g.reshape(1, d), b.reshape(1, d))


def _mm_bias_kernel(a_ref, w_ref, b_ref, o_ref):
    o_ref[...] = _dot(a_ref[...], w_ref[...]) + b_ref[...]


def _in_proj(a, w, b, tm=2048, tn=256):
    t, k = a.shape
    n = w.shape[1]
    tm = min(tm, t)
    return pl.pallas_call(
        _mm_bias_kernel,
        grid=(t // tm, n // tn),
        in_specs=[pl.BlockSpec((tm, k), lambda i, j: (i, 0)),
                  pl.BlockSpec((k, tn), lambda i, j: (0, j)),
                  pl.BlockSpec((1, tn), lambda i, j: (0, j))],
        out_specs=pl.BlockSpec((tm, tn), lambda i, j: (i, j)),
        out_shape=jax.ShapeDtypeStruct((t, n), F32),
        compiler_params=pltpu.CompilerParams(vmem_limit_bytes=VMEM_LIMIT),
        name="in_proj",
    )(a, w, b.reshape(1, n))


def _hgrn_kernel(q_ref, f_ref, i_ref, g_ref, llb_ref, l1lb_ref, oml_ref, gain_ref,
                 mstack_ref, level_ref, o_ref, state_ref):
    c = pl.program_id(1)

    @pl.when(c == 0)
    def _():
        state_ref[...] = jnp.zeros_like(state_ref)

    q = _silu(q_ref[...])
    z = f_ref[...]
    ez = jnp.exp(-jnp.abs(z))
    log_sig = jnp.minimum(z, 0.0) - jnp.log(1.0 + ez)
    sig_neg = jnp.where(z >= 0.0, ez, 1.0) / (1.0 + ez)
    k = oml_ref[...] * sig_neg
    a = llb_ref[...]
    bb = l1lb_ref[...] + log_sig
    log_f = jnp.maximum(a, bb) + jnp.log(1.0 + jnp.exp(-jnp.abs(a - bb)))

    lf_hi = log_f.astype(BF16)
    lf_lo = (log_f - lf_hi.astype(F32)).astype(BF16)
    m = mstack_ref[...]
    sums = _dot(m, lf_hi) + _dot(m, lf_lo)
    decay = jnp.exp(sums)

    rows = lax.broadcasted_iota(jnp.int32, (CHUNK, 1), 0)
    factors = []
    for li, s in enumerate(_LEVEL_SIZES):
        is_query = ((rows // s) % 2) == 1
        factors.append((jnp.where(is_query, q, k) * decay[li * CHUNK:(li + 1) * CHUNK]).astype(BF16))
    cum = _N_LEVELS * CHUNK
    q_in = (q * decay[cum:cum + CHUNK]).astype(BF16)
    k_dec = (k * decay[cum + CHUNK:cum + 2 * CHUNK]).astype(BF16)
    end_decay = decay[cum + CHUNK - 1:cum + CHUNK]
    qb = q.astype(BF16)
    kb = k.astype(BF16)
    vb = i_ref[...].astype(BF16)
    level = level_ref[...]
    gate = _silu(g_ref[...])
    gain = gain_ref[...]

    for h in range(HG_HEADS):
        hs = slice(h * HG_DK, (h + 1) * HG_DK)
        scores = jnp.zeros((CHUNK, CHUNK), F32)
        for li in range(_N_LEVELS):
            fl = factors[li][:, hs]
            scores = jnp.where(level == li, _dot_nt(fl, fl), scores)
        scores = jnp.where(level == _N_LEVELS, _dot_nt(qb[:, hs], kb[:, hs]), scores)
        st = state_ref[h]
        o = _dot(scores.astype(BF16), vb[:, hs]) + _dot_nt(q_in[:, hs], st.astype(BF16))
        state_ref[h] = st * end_decay[:, hs] + _dot_tn(vb[:, hs], k_dec[:, hs])
        ms = jnp.mean(o * o, axis=-1, keepdims=True)
        o = o * lax.rsqrt(ms + RMS_EPS) * gain[:, hs] * gate[:, hs]
        o_ref[:, hs] = o.astype(o_ref.dtype)


def _hgrn(proj3, llb, l1lb, oml, gain):
    b, s, _ = proj3.shape
    w = HG_WIDTH

    def col(j):
        return pl.BlockSpec((None, CHUNK, w), lambda bi, ci, j=j: (bi, ci, j))

    vec = pl.BlockSpec((1, w), lambda bi, ci: (0, 0))
    nm = _HGRN_MSTACK.shape[0]
    return pl.pallas_call(
        _hgrn_kernel,
        grid=(b, s // CHUNK),
        in_specs=[col(0), col(1), col(2), col(3), vec, vec, vec, vec,
                  pl.BlockSpec((nm, CHUNK), lambda bi, ci: (0, 0)),
                  pl.BlockSpec((CHUNK, CHUNK), lambda bi, ci: (0, 0))],
        out_specs=pl.BlockSpec((None, CHUNK, w), lambda bi, ci: (bi, ci, 0)),
        out_shape=jax.ShapeDtypeStruct((b, s, w), BF16),
        scratch_shapes=[pltpu.VMEM((HG_HEADS, HG_DK, HG_DK), F32)],
        compiler_params=pltpu.CompilerParams(dimension_semantics=("parallel", "arbitrary"),
                                             vmem_limit_bytes=VMEM_LIMIT),
        name="hgrn2",
    )(proj3, proj3, proj3, proj3, llb.reshape(1, w), l1lb.reshape(1, w), oml.reshape(1, w),
      gain.reshape(1, w), jnp.asarray(_HGRN_MSTACK, BF16), jnp.asarray(_HGRN_LEVEL))


def _swa_kernel(q_ref, kp_ref, kc_ref, vp_ref, vc_ref, sink_ref, gain_ref, o_ref):
    n = pl.program_id(1)
    g = SW_Q_HEADS // SW_KV_HEADS
    hd = SW_HEAD_DIM
    q = q_ref[...]
    kband = jnp.concatenate([kp_ref[...], kc_ref[...]], axis=0).astype(BF16)
    vband = jnp.concatenate([vp_ref[...], vc_ref[...]], axis=0).astype(BF16)
    t = lax.broadcasted_iota(jnp.int32, (g * CHUNK, 2 * CHUNK), 0) % CHUNK
    s = lax.broadcasted_iota(jnp.int32, (g * CHUNK, 2 * CHUNK), 1)
    rel = t + CHUNK - s
    mask = (rel >= 0) & (rel < CHUNK) & ((s >= CHUNK) | (n > 0))
    outs = []
    for kv in range(SW_KV_HEADS):
        qs = jnp.concatenate([q[:, (kv * g + j) * hd:(kv * g + j + 1) * hd] for j in range(g)], axis=0)
        sc = _dot_nt(qs.astype(BF16), kband[:, kv * hd:(kv + 1) * hd]) * (hd ** -0.5)
        sc = jnp.where(mask, sc, -jnp.inf)
        sink = sink_ref[kv]
        mx = jnp.maximum(jnp.max(sc, axis=-1, keepdims=True), sink)
        p = jnp.exp(sc - mx)
        denom = jnp.sum(p, axis=-1, keepdims=True) + jnp.exp(sink - mx)
        o = _dot((p / denom).astype(BF16), vband[:, kv * hd:(kv + 1) * hd])
        outs.extend(o[j * CHUNK:(j + 1) * CHUNK] for j in range(g))
    o = jnp.concatenate(outs, axis=-1)
    ms = jnp.mean(o * o, axis=-1, keepdims=True)
    o_ref[...] = (o * lax.rsqrt(ms + RMS_EPS) * gain_ref[...]).astype(o_ref.dtype)


def _swa(proj3, sinks, gain):
    b, s, _ = proj3.shape
    q_col = _SWQ_OFF // SW_WIDTH
    k_col = _SWK_OFF // SW_KV_WIDTH
    v_col = k_col + 1
    g = SW_Q_HEADS // SW_KV_HEADS
    sink_rows = jnp.repeat(sinks.astype(F32).reshape(SW_KV_HEADS, g), CHUNK, axis=1)[..., None]

    def kv_spec(col, prev):
        if prev:
            return pl.BlockSpec((None, CHUNK, SW_KV_WIDTH), lambda bi, ni: (bi, jnp.maximum(ni - 1, 0), col))
        return pl.BlockSpec((None, CHUNK, SW_KV_WIDTH), lambda bi, ni: (bi, ni, col))

    return pl.pallas_call(
        _swa_kernel,
        grid=(b, s // CHUNK),
        in_specs=[pl.BlockSpec((None, CHUNK, SW_WIDTH), lambda bi, ni: (bi, ni, q_col)),
                  kv_spec(k_col, True), kv_spec(k_col, False),
                  kv_spec(v_col, True), kv_spec(v_col, False),
                  pl.BlockSpec((SW_KV_HEADS, g * CHUNK, 1), lambda bi, ni: (0, 0, 0)),
                  pl.BlockSpec((1, SW_WIDTH), lambda bi, ni: (0, 0))],
        out_specs=pl.BlockSpec((None, CHUNK, SW_WIDTH), lambda bi, ni: (bi, ni, 0)),
        out_shape=jax.ShapeDtypeStruct((b, s, SW_WIDTH), BF16),
        compiler_params=pltpu.CompilerParams(dimension_semantics=("parallel", "parallel"),
                                             vmem_limit_bytes=VMEM_LIMIT),
        name="swa",
    )(proj3, proj3, proj3, proj3, proj3, sink_rows, gain.reshape(1, SW_WIDTH))


def _sgu_kernel(u_ref, v_ref, lng_ref, lnb_ref, w_ref, bs_ref, gain_ref, o_ref):
    u = _gelu(u_ref[...])
    v = _layer_norm(_gelu(v_ref[...]), lng_ref[...], lnb_ref[...]).astype(BF16)
    r = lax.broadcasted_iota(jnp.int32, (CHUNK, CHUNK), 0)
    c = lax.broadcasted_iota(jnp.int32, (CHUNK, CHUNK), 1)
    tril = c <= r
    bs = bs_ref[...]
    parts = []
    for gi in range(SG_GROUPS):
        gs = slice(gi * CHUNK, (gi + 1) * CHUNK)
        w = jnp.where(tril, w_ref[gi], 0.0).astype(BF16)
        parts.append(_dot(w, v[:, gs]) + bs[:, gi:gi + 1])
    o = u * jnp.concatenate(parts, axis=-1)
    ms = jnp.mean(o * o, axis=-1, keepdims=True)
    o_ref[...] = (o * lax.rsqrt(ms + RMS_EPS) * gain_ref[...]).astype(o_ref.dtype)


def _sgu(proj3, ln_g, ln_b, w_s, b_s, gain):
    b, s, _ = proj3.shape
    w = SG_WIDTH
    u_col = _SGU_OFF // SG_WIDTH
    vec = pl.BlockSpec((1, w), lambda bi, ci: (0, 0))
    return pl.pallas_call(
        _sgu_kernel,
        grid=(b, s // CHUNK),
        in_specs=[pl.BlockSpec((None, CHUNK, w), lambda bi, ci: (bi, ci, u_col)),
                  pl.BlockSpec((None, CHUNK, w), lambda bi, ci: (bi, ci, u_col + 1)),
                  vec, vec,
                  pl.BlockSpec((SG_GROUPS, CHUNK, CHUNK), lambda bi, ci: (0, 0, 0)),
                  pl.BlockSpec((CHUNK, SG_GROUPS), lambda bi, ci: (0, 0)),
                  vec],
        out_specs=pl.BlockSpec((None, CHUNK, w), lambda bi, ci: (bi, ci, 0)),
        out_shape=jax.ShapeDtypeStruct((b, s, w), BF16),
        compiler_params=pltpu.CompilerParams(dimension_semantics=("parallel", "parallel"),
                                             vmem_limit_bytes=VMEM_LIMIT),
        name="sgu",
    )(proj3, proj3, ln_g.reshape(1, w), ln_b.reshape(1, w), w_s, b_s.T, gain.reshape(1, w))


def _out_proj_kernel(oa_ref, ob_ref, oc_ref, w_ref, h_ref, g_ref, b_ref, of_ref):
    mixed = jnp.concatenate([oa_ref[...], ob_ref[...], oc_ref[...]], axis=-1)
    y = ALPHA * h_ref[...] + _dot(mixed, w_ref[...])
    of_ref[...] = _layer_norm(y, g_ref[...], b_ref[...])


def _out_proj(oa, ob, oc, w, h, g, b, tm=256):
    t, d = h.shape
    vec = pl.BlockSpec((1, d), lambda i: (0, 0))
    return pl.pallas_call(
        _out_proj_kernel,
        grid=(t // tm,),
        in_specs=[pl.BlockSpec((tm, oa.shape[1]), lambda i: (i, 0)),
                  pl.BlockSpec((tm, ob.shape[1]), lambda i: (i, 0)),
                  pl.BlockSpec((tm, oc.shape[1]), lambda i: (i, 0)),
                  pl.BlockSpec(w.shape, lambda i: (0, 0)),
                  pl.BlockSpec((tm, d), lambda i: (i, 0)), vec, vec],
        out_specs=pl.BlockSpec((tm, d), lambda i: (i, 0)),
        out_shape=jax.ShapeDtypeStruct((t, d), F32),
        compiler_params=pltpu.CompilerParams(vmem_limit_bytes=VMEM_LIMIT),
        name="out_proj_ln1",
    )(oa, ob, oc, w, h, g.reshape(1, d), b.reshape(1, d))


def _first_index_of_max(x, iota, size, axis):
    mx = jnp.max(x, axis=axis, keepdims=True)
    idx = jnp.min(jnp.where(x == mx, iota, size), axis=axis, keepdims=True)
    return mx, idx


def _router_kernel(h_ref, wt_ref, bias_ref, upper_ref, eidx_ref, gate_ref, rank_ref, count_ref, run_ref):
    i = pl.program_id(0)

    @pl.when(i == 0)
    def _():
        run_ref[...] = jnp.zeros_like(run_ref)

    tm = h_ref.shape[0]
    per_group = N_EXPERTS // N_EXPERT_GROUPS
    logits = lax.dot_general(wt_ref[...], h_ref[...], (((1,), (1,)), ((), ())),
                             precision=lax.Precision.HIGHEST, preferred_element_type=F32)
    scores = 1.0 / (1.0 + jnp.exp(-logits))
    sel = scores + bias_ref[...]
    sel3 = sel.reshape(N_EXPERT_GROUPS, per_group, tm)
    io3 = lax.broadcasted_iota(jnp.int32, sel3.shape, 1)
    m1, i1 = _first_index_of_max(sel3, io3, per_group, 1)
    m2 = jnp.max(jnp.where(io3 == i1, -jnp.inf, sel3), axis=1, keepdims=True)
    grp = (m1 + m2).reshape(N_EXPERT_GROUPS, tm)
    iog = lax.broadcasted_iota(jnp.int32, grp.shape, 0)
    keep = jnp.zeros(grp.shape, jnp.bool_)
    for _ in range(TOPK_GROUPS):
        _, gi = _first_index_of_max(grp, iog, N_EXPERT_GROUPS, 0)
        hit = iog == gi
        keep = keep | hit
        grp = jnp.where(hit, -jnp.inf, grp)
    keep3 = jnp.broadcast_to(keep.reshape(N_EXPERT_GROUPS, 1, tm), sel3.shape)
    cand = jnp.where(keep3, sel3, -jnp.inf).reshape(N_EXPERTS, tm)
    ioe = lax.broadcasted_iota(jnp.int32, cand.shape, 0)
    chosen = jnp.zeros(cand.shape, F32)
    idxs, gvals, hits = [], [], []
    for _ in range(TOP_K):
        _, ei = _first_index_of_max(cand, ioe, N_EXPERTS, 0)
        hit = ioe == ei
        idxs.append(ei)
        gvals.append(jnp.sum(jnp.where(hit, scores, 0.0), axis=0, keepdims=True))
        hits.append(hit)
        chosen = jnp.where(hit, 1.0, chosen)
        cand = jnp.where(hit, -jnp.inf, cand)
    gsum = functools.reduce(lambda a, b: a + b, gvals)
    prefix = _dot(chosen.astype(BF16), upper_ref[...])
    base = run_ref[...] + prefix
    for kk in range(TOP_K):
        eidx_ref[kk:kk + 1, :] = idxs[kk]
        gate_ref[kk:kk + 1, :] = gvals[kk] / gsum * ROUTED_SCALE
        rank_ref[kk:kk + 1, :] = jnp.sum(jnp.where(hits[kk], base, 0.0), axis=0, keepdims=True).astype(jnp.int32)
    run = run_ref[...] + jnp.sum(chosen, axis=1, keepdims=True)
    run_ref[...] = run
    count_ref[...] = run.astype(jnp.int32)


def _router(h, router_w, router_bias, tm=ROUTER_TILE):
    t, d = h.shape
    tm = min(tm, t)
    upper = jnp.asarray(np.triu(np.ones((tm, tm), np.float32), 1), BF16)
    slot = pl.BlockSpec((TOP_K, tm), lambda i: (0, i))
    return pl.pallas_call(
        _router_kernel,
        grid=(t // tm,),
        in_specs=[pl.BlockSpec((tm, d), lambda i: (i, 0)),
                  pl.BlockSpec((N_EXPERTS, d), lambda i: (0, 0)),
                  pl.BlockSpec((N_EXPERTS, 1), lambda i: (0, 0)),
                  pl.BlockSpec((tm, tm), lambda i: (0, 0))],
        out_specs=[slot, slot, slot, pl.BlockSpec((N_EXPERTS, 1), lambda i: (0, 0))],
        out_shape=[jax.ShapeDtypeStruct((TOP_K, t), jnp.int32),
                   jax.ShapeDtypeStruct((TOP_K, t), F32),
                   jax.ShapeDtypeStruct((TOP_K, t), jnp.int32),
                   jax.ShapeDtypeStruct((N_EXPERTS, 1), jnp.int32)],
        scratch_shapes=[pltpu.VMEM((N_EXPERTS, 1), F32)],
        compiler_params=pltpu.CompilerParams(dimension_semantics=("arbitrary",),
                                             vmem_limit_bytes=VMEM_LIMIT),
        name="router",
    )(h, router_w.T, router_bias.reshape(N_EXPERTS, 1), upper)


_EXPERT_PHASES = 8


def _experts_kernel(be_ref, nu_ref, tokc_ref, tokn_ref, sidp_ref, sidc_ref, h_hbm, wg_ref, wu_ref, wd_ref,
                    y_hbm, xbuf, obuf, gsem, ssem):
    b = pl.program_id(0)
    nu = nu_ref[0]
    rows, d = xbuf.shape[1], xbuf.shape[2]
    f = wg_ref.shape[1]
    slot = b % 2
    other = 1 - slot

    def gather(tok_ref, r, dst_slot):
        return pltpu.make_async_copy(h_hbm.at[pl.ds(tok_ref[0, 0, r], 1)],
                                     xbuf.at[dst_slot, pl.ds(r, 1)], gsem.at[dst_slot])

    def scatter(sid_ref, r, src_slot):
        return pltpu.make_async_copy(obuf.at[src_slot, pl.ds(r, 1)],
                                     y_hbm.at[pl.ds(sid_ref[0, 0, r], 1)], ssem.at[src_slot])

    def wait_gather(s):
        pltpu.make_async_copy(h_hbm.at[pl.ds(0, rows)], xbuf.at[s], gsem.at[s]).wait()

    def wait_scatter(s):
        pltpu.make_async_copy(obuf.at[s], y_hbm.at[pl.ds(0, rows)], ssem.at[s]).wait()

    @pl.when(b == 0)
    def _():
        obuf[...] = jnp.zeros_like(obuf)
        n_slots = y_hbm.shape[0] - 2 * rows
        for s in range(2):
            pltpu.make_async_copy(obuf.at[s], y_hbm.at[pl.ds(n_slots + s * rows, rows)], ssem.at[s]).start()
        for s in range(2):
            wait_scatter(s)

        def body(r, carry):
            gather(tokc_ref, r, 0).start()
            return carry
        lax.fori_loop(0, rows, body, 0)

    def block(with_scatter):
        wait_gather(slot)
        x = xbuf[slot].astype(BF16)
        per = rows // _EXPERT_PHASES
        phase = [0]

        def issue():
            p = phase[0]
            phase[0] += 1
            for r in range(p * per, (p + 1) * per):
                gather(tokn_ref, r, other).start()
                if with_scatter:
                    scatter(sidp_ref, r, other).start()

        acts = []
        fh = f // 2
        for j in range(2):
            issue()
            gate = _dot(x, wg_ref[:, j * fh:(j + 1) * fh])
            issue()
            up = _dot(x, wu_ref[:, j * fh:(j + 1) * fh])
            acts.append((_silu(gate) * up).astype(BF16))
        act = jnp.concatenate(acts, axis=1)
        dq = d // 4
        for j in range(4):
            issue()
            obuf[slot, :, j * dq:(j + 1) * dq] = _dot(act, wd_ref[:, j * dq:(j + 1) * dq])

    @pl.when(b == 0)
    def _():
        block(False)

    @pl.when((b >= 1) & (b < nu))
    def _():
        @pl.when(b >= 2)
        def _():
            wait_scatter(slot)
        block(True)

    @pl.when(b == nu - 1)
    def _():
        wait_gather(other)
        @pl.when(b >= 1)
        def _():
            wait_scatter(other)

        def body(r, carry):
            scatter(sidc_ref, r, slot).start()
            return carry
        lax.fori_loop(0, rows, body, 0)
        wait_scatter(slot)


def _experts(h, tok_blocks, sid_blocks, block_e, n_used, wg, wu, wd, layer, n_slot_rows):
    nb, _, rows = tok_blocks.shape
    t, d = h.shape
    f = wg.shape[-1]

    def clamp(b, nu):
        return jnp.clip(b, 0, nu[0] - 1)

    def smem(shift):
        return pl.BlockSpec((1, 1, rows), lambda b, be, nu: (clamp(b + shift, nu), 0, 0), memory_space=pltpu.SMEM)

    def weight(shape):
        return pl.BlockSpec((None, None) + shape, lambda b, be, nu: (layer, be[clamp(b, nu)], 0, 0))

    grid_spec = pltpu.PrefetchScalarGridSpec(
        num_scalar_prefetch=2,
        grid=(nb,),
        in_specs=[smem(0), smem(1), smem(-1), smem(0),
                  pl.BlockSpec(memory_space=pl.ANY),
                  weight((d, f)), weight((d, f)), weight((f, d))],
        out_specs=pl.BlockSpec(memory_space=pl.ANY),
        scratch_shapes=[pltpu.VMEM((2, rows, d), F32), pltpu.VMEM((2, rows, d), F32),
                        pltpu.SemaphoreType.DMA((2,)), pltpu.SemaphoreType.DMA((2,))],
    )
    return pl.pallas_call(
        _experts_kernel,
        grid_spec=grid_spec,
        out_shape=jax.ShapeDtypeStruct((n_slot_rows + 2 * rows, d), F32),
        compiler_params=pltpu.CompilerParams(dimension_semantics=("arbitrary",),
                                             vmem_limit_bytes=VMEM_LIMIT),
        name="experts",
    )(block_e, n_used, tok_blocks, tok_blocks, sid_blocks, sid_blocks, h, wg, wu, wd)


def _combine_kernel(h_ref, gate_ref, *refs):
    y_refs = refs[:TOP_K]
    sg_ref, su_ref, sd_ref, g_ref, b_ref, of_ref, ob_ref = refs[TOP_K:]
    h = h_ref[...]
    hb = h.astype(BF16)
    act = _silu(_dot(hb, sg_ref[...])) * _dot(hb, su_ref[...])
    y = ALPHA * h + _dot(act.astype(BF16), sd_ref[...])
    gates = gate_ref[...]
    for kk in range(TOP_K):
        y = y + gates[:, kk:kk + 1] * y_refs[kk][...]
    out = _layer_norm(y, g_ref[...], b_ref[...])
    of_ref[...] = out
    ob_ref[...] = out.astype(BF16)


def _combine(h, gates_t, y_slots, sg, su, sd, g, b, tm=COMBINE_TILE):
    t, d = h.shape
    n = t // tm
    f = sg.shape[1]
    vec = pl.BlockSpec((1, d), lambda i: (0, 0))
    row = pl.BlockSpec((tm, d), lambda i: (i, 0))
    planes = [pl.BlockSpec((tm, d), lambda i, kk=kk: (kk * n + i, 0)) for kk in range(TOP_K)]
    return pl.pallas_call(
        _combine_kernel,
        grid=(n,),
        in_specs=[row, pl.BlockSpec((tm, TOP_K), lambda i: (i, 0))] + planes + [
            pl.BlockSpec((d, f), lambda i: (0, 0)),
            pl.BlockSpec((d, f), lambda i: (0, 0)),
            pl.BlockSpec((f, d), lambda i: (0, 0)),
            vec, vec],
        out_specs=[row, row],
        out_shape=[jax.ShapeDtypeStruct((t, d), F32), jax.ShapeDtypeStruct((t, d), BF16)],
        compiler_params=pltpu.CompilerParams(dimension_semantics=("parallel",),
                                             vmem_limit_bytes=VMEM_LIMIT),
        name="combine_ln2",
    )(h, gates_t, *([y_slots] * TOP_K), sg, su, sd, g.reshape(1, d), b.reshape(1, d))


def _moe(h, router_w, router_bias, wg, wu, wd, layer, sg, su, sd, ln_g, ln_b):
    t, d = h.shape
    tk = t * TOP_K
    n_blocks = tk // MOE_ROWS + N_EXPERTS
    n_rows = n_blocks * MOE_ROWS
    eidx, gates, rank, counts = _router(h, router_w, router_bias)
    counts = counts.reshape(N_EXPERTS)
    padded = (counts + MOE_ROWS - 1) // MOE_ROWS * MOE_ROWS
    pend = jnp.cumsum(padded)
    pstart = pend - padded
    dest = pstart[eidx] + rank
    slot_ids = (jnp.arange(TOP_K, dtype=jnp.int32)[:, None] * t + jnp.arange(t, dtype=jnp.int32)[None, :])
    pad_ids = tk + jnp.arange(n_rows, dtype=jnp.int32) % (2 * MOE_ROWS)
    sid = pad_ids.at[dest.reshape(-1)].set(slot_ids.reshape(-1), unique_indices=True)
    tok = jnp.where(sid < tk, sid % t, 0)
    block_start = jnp.arange(n_blocks, dtype=jnp.int32) * MOE_ROWS
    block_e = jnp.minimum(jnp.sum(pend[None, :] <= block_start[:, None], axis=1), N_EXPERTS - 1).astype(jnp.int32)
    n_used = (pend[-1] // MOE_ROWS).astype(jnp.int32).reshape(1)
    y_slots = _experts(h, tok.reshape(n_blocks, 1, MOE_ROWS), sid.reshape(n_blocks, 1, MOE_ROWS),
                       block_e, n_used, wg, wu, wd, layer, tk)
    return _combine(h, gates.T, y_slots, sg, su, sd, ln_g, ln_b, tm=min(COMBINE_TILE, t))


def kernel(x, ln0_gain, ln0_bias, w_in, b_in, hg_lb_logits, sw_sinks, sg_ln_gain, sg_ln_bias, sg_w_s, sg_b_s, mix_gain, w_out, ln1_gain, ln1_bias, router_w, router_bias, exp_w_gate, exp_w_up, exp_w_down, sh_w_gate, sh_w_up, sh_w_down, ln2_gain, ln2_bias):
    b, s, d = x.shape
    t = b * s
    p = jax.nn.softmax(hg_lb_logits.astype(F32), axis=0)
    cs = jnp.cumsum(p, axis=0)
    lbs = cs - cs[0:1]
    a0, a1 = HG_WIDTH, HG_WIDTH + SW_WIDTH

    wg_all, wu_all, wd_all = exp_w_gate.astype(BF16), exp_w_up.astype(BF16), exp_w_down.astype(BF16)
    h, hb = _ln(x.reshape(t, d), ln0_gain, ln0_bias)
    for l in range(DEPTH):
        proj = _in_proj(hb, _regroup_columns(w_in[l]).astype(BF16), _regroup_columns(b_in[l]))
        proj3 = proj.reshape(b, s, D_IN)
        lb = lbs[l]
        o_a = _hgrn(proj3, jnp.log(lb), jnp.log1p(-lb), 1.0 - lb, mix_gain[l, :a0])
        o_b = _swa(proj3, sw_sinks[l], mix_gain[l, a0:a1])
        o_c = _sgu(proj3, sg_ln_gain[l], sg_ln_bias[l], sg_w_s[l], sg_b_s[l], mix_gain[l, a1:])
        h = _out_proj(o_a.reshape(t, -1), o_b.reshape(t, -1), o_c.reshape(t, -1),
                      w_out[l].astype(BF16), h, ln1_gain[l], ln1_bias[l])
        h, hb = _moe(h, router_w[l], router_bias[l], wg_all, wu_all, wd_all, l,
                     sh_w_gate[l].astype(BF16), sh_w_up[l].astype(BF16), sh_w_down[l].astype(BF16),
                     ln2_gain[l], ln2_bias[l])
    return h.reshape(b, s, d)
```

```python
import functools

import numpy as np
import jax
import jax.numpy as jnp
from jax import lax
from jax.experimental import pallas as pl
from jax.experimental.pallas import tpu as pltpu

F32 = jnp.float32
BF16 = jnp.bfloat16

D_MODEL = 2048
DEPTH = 2
HG_HEADS = 4
HG_DK = 128
HG_WIDTH = HG_HEADS * HG_DK
CHUNK = 128
SW_Q_HEADS = 16
SW_KV_HEADS = 2
SW_HEAD_DIM = 64
SW_WIDTH = SW_Q_HEADS * SW_HEAD_DIM
SW_KV_WIDTH = SW_KV_HEADS * SW_HEAD_DIM
SG_GROUPS = 4
SG_WIDTH = SG_GROUPS * CHUNK
D_IN = 4 * HG_WIDTH + SW_WIDTH + 2 * SW_KV_WIDTH + 2 * SG_WIDTH
N_EXPERTS = 64
TOP_K = 8
D_EXPERT = 512
N_EXPERT_GROUPS = 8
TOPK_GROUPS = 4
ROUTED_SCALE = 2.5
ALPHA = (2 * DEPTH) ** 0.25
LN_EPS = 1e-5
RMS_EPS = 1e-6

_SWQ_OFF = 4 * HG_WIDTH
_SGU_OFF = _SWQ_OFF + SW_WIDTH
_SWK_OFF = _SGU_OFF + 2 * SG_WIDTH


def _regroup_columns(a):
    k0 = _SWQ_OFF + SW_WIDTH
    k1 = k0 + 2 * SW_KV_WIDTH
    return jnp.concatenate([a[..., :k0], a[..., k1:], a[..., k0:k1]], axis=-1)


MOE_ROWS = 256
ROUTER_TILE = 512
COMBINE_TILE = 128
VMEM_LIMIT = 56 * 1024 * 1024

_LEVEL_SIZES = (64, 32, 16, 8, 4, 2, 1)
_N_LEVELS = len(_LEVEL_SIZES)


def _hgrn_constants():
    t = np.arange(CHUNK)[:, None]
    u = np.arange(CHUNK)[None, :]
    mats = []
    level = np.full((CHUNK, CHUNK), -1, np.int32)
    for li, s in enumerate(_LEVEL_SIZES):
        blk = t // s
        odd = (blk % 2) == 1
        m_query = (u >= blk * s) & (u <= t)
        m_key = (u > t) & (u <= (blk + 1) * s - 1)
        mats.append(np.where(odd, m_query, m_key))
        pair = ((t // s) % 2 == 1) & ((u // s) == (t // s) - 1)
        level[pair] = li
    mats.append(u <= t)
    mats.append(u > t)
    level[np.arange(CHUNK), np.arange(CHUNK)] = _N_LEVELS
    return np.concatenate(mats, 0).astype(np.float32), level


_HGRN_MSTACK, _HGRN_LEVEL = _hgrn_constants()


def _layer_norm(x, g, b):
    mu = jnp.mean(x, axis=-1, keepdims=True)
    xc = x - mu
    var = jnp.mean(xc * xc, axis=-1, keepdims=True)
    return xc * lax.rsqrt(var + LN_EPS) * g + b


def _silu(x):
    return x / (1.0 + jnp.exp(-x))


def _gelu(x):
    return 0.5 * x * (1.0 + lax.erf(x * (2.0 ** -0.5)))


def _dot(a, b):
    return jnp.dot(a, b, preferred_element_type=F32)


def _dot_nt(a, b):
    return lax.dot_general(a, b, (((1,), (1,)), ((), ())), preferred_element_type=F32)


def _dot_tn(a, b):
    return lax.dot_general(a, b, (((0,), (0,)), ((), ())), preferred_element_type=F32)


def _ln_kernel(x_ref, g_ref, b_ref, of_ref, ob_ref):
    y = _layer_norm(x_ref[...], g_ref[...], b_ref[...])
    of_ref[...] = y
    ob_ref[...] = y.astype(BF16)


def _ln(x, g, b, tm=256):
    t, d = x.shape
    row = pl.BlockSpec((tm, d), lambda i: (i, 0))
    vec = pl.BlockSpec((1, d), lambda i: (0, 0))
    return pl.pallas_call(
        _ln_kernel,
        grid=(t // tm,),
        in_specs=[row, vec, vec],
        out_specs=[row, row],
        out_shape=[jax.ShapeDtypeStruct((t, d), F32), jax.ShapeDtypeStruct((t, d), BF16)],
        name="ln0",
    )(x, g.reshape(1, d), b.reshape(1, d))


def _mm_bias_kernel(a_ref, w_ref, b_ref, o_ref):
    o_ref[...] = _dot(a_ref[...], w_ref[...]) + b_ref[...]


def _in_proj(a, w, b, tm=2048, tn=256):
    t, k = a.shape
    n = w.shape[1]
    tm = min(tm, t)
    return pl.pallas_call(
        _mm_bias_kernel,
        grid=(t // tm, n // tn),
        in_specs=[pl.BlockSpec((tm, k), lambda i, j: (i, 0)),
                  pl.BlockSpec((k, tn), lambda i, j: (0, j)),
                  pl.BlockSpec((1, tn), lambda i, j: (0, j))],
        out_specs=pl.BlockSpec((tm, tn), lambda i, j: (i, j)),
        out_shape=jax.ShapeDtypeStruct((t, n), F32),
        compiler_params=pltpu.CompilerParams(vmem_limit_bytes=VMEM_LIMIT),
        name="in_proj",
    )(a, w, b.reshape(1, n))


def _hgrn_kernel(q_ref, f_ref, i_ref, g_ref, llb_ref, l1lb_ref, oml_ref, gain_ref,
                 mstack_ref, level_ref, o_ref, state_ref):
    c = pl.program_id(1)

    @pl.when(c == 0)
    def _():
        state_ref[...] = jnp.zeros_like(state_ref)

    q = _silu(q_ref[...])
    z = f_ref[...]
    ez = jnp.exp(-jnp.abs(z))
    log_sig = jnp.minimum(z, 0.0) - jnp.log(1.0 + ez)
    sig_neg = jnp.where(z >= 0.0, ez, 1.0) / (1.0 + ez)
    k = oml_ref[...] * sig_neg
    a = llb_ref[...]
    bb = l1lb_ref[...] + log_sig
    log_f = jnp.maximum(a, bb) + jnp.log(1.0 + jnp.exp(-jnp.abs(a - bb)))

    lf_hi = log_f.astype(BF16)
    lf_lo = (log_f - lf_hi.astype(F32)).astype(BF16)
    m = mstack_ref[...]
    sums = _dot(m, lf_hi) + _dot(m, lf_lo)
    decay = jnp.exp(sums)

    rows = lax.broadcasted_iota(jnp.int32, (CHUNK, 1), 0)
    factors = []
    for li, s in enumerate(_LEVEL_SIZES):
        is_query = ((rows // s) % 2) == 1
        factors.append((jnp.where(is_query, q, k) * decay[li * CHUNK:(li + 1) * CHUNK]).astype(BF16))
    cum = _N_LEVELS * CHUNK
    q_in = (q * decay[cum:cum + CHUNK]).astype(BF16)
    k_dec = (k * decay[cum + CHUNK:cum + 2 * CHUNK]).astype(BF16)
    end_decay = decay[cum + CHUNK - 1:cum + CHUNK]
    qb = q.astype(BF16)
    kb = k.astype(BF16)
    vb = i_ref[...].astype(BF16)
    level = level_ref[...]
    gate = _silu(g_ref[...])
    gain = gain_ref[...]

    for h in range(HG_HEADS):
        hs = slice(h * HG_DK, (h + 1) * HG_DK)
        scores = jnp.zeros((CHUNK, CHUNK), F32)
        for li in range(_N_LEVELS):
            fl = factors[li][:, hs]
            scores = jnp.where(level == li, _dot_nt(fl, fl), scores)
        scores = jnp.where(level == _N_LEVELS, _dot_nt(qb[:, hs], kb[:, hs]), scores)
        st = state_ref[h]
        o = _dot(scores.astype(BF16), vb[:, hs]) + _dot_nt(q_in[:, hs], st.astype(BF16))
        state_ref[h] = st * end_decay[:, hs] + _dot_tn(vb[:, hs], k_dec[:, hs])
        ms = jnp.mean(o * o, axis=-1, keepdims=True)
        o = o * lax.rsqrt(ms + RMS_EPS) * gain[:, hs] * gate[:, hs]
        o_ref[:, hs] = o.astype(o_ref.dtype)


def _hgrn(proj3, llb, l1lb, oml, gain):
    b, s, _ = proj3.shape
    w = HG_WIDTH

    def col(j):
        return pl.BlockSpec((None, CHUNK, w), lambda bi, ci, j=j: (bi, ci, j))

    vec = pl.BlockSpec((1, w), lambda bi, ci: (0, 0))
    nm = _HGRN_MSTACK.shape[0]
    return pl.pallas_call(
        _hgrn_kernel,
        grid=(b, s // CHUNK),
        in_specs=[col(0), col(1), col(2), col(3), vec, vec, vec, vec,
                  pl.BlockSpec((nm, CHUNK), lambda bi, ci: (0, 0)),
                  pl.BlockSpec((CHUNK, CHUNK), lambda bi, ci: (0, 0))],
        out_specs=pl.BlockSpec((None, CHUNK, w), lambda bi, ci: (bi, ci, 0)),
        out_shape=jax.ShapeDtypeStruct((b, s, w), BF16),
        scratch_shapes=[pltpu.VMEM((HG_HEADS, HG_DK, HG_DK), F32)],
        compiler_params=pltpu.CompilerParams(dimension_semantics=("parallel", "arbitrary"),
                                             vmem_limit_bytes=VMEM_LIMIT),
        name="hgrn2",
    )(proj3, proj3, proj3, proj3, llb.reshape(1, w), l1lb.reshape(1, w), oml.reshape(1, w),
      gain.reshape(1, w), jnp.asarray(_HGRN_MSTACK, BF16), jnp.asarray(_HGRN_LEVEL))


def _swa_kernel(q_ref, kp_ref, kc_ref, vp_ref, vc_ref, sink_ref, gain_ref, o_ref):
    n = pl.program_id(1)
    g = SW_Q_HEADS // SW_KV_HEADS
    hd = SW_HEAD_DIM
    q = q_ref[...]
    kband = jnp.concatenate([kp_ref[...], kc_ref[...]], axis=0).astype(BF16)
    vband = jnp.concatenate([vp_ref[...], vc_ref[...]], axis=0).astype(BF16)
    t = lax.broadcasted_iota(jnp.int32, (g * CHUNK, 2 * CHUNK), 0) % CHUNK
    s = lax.broadcasted_iota(jnp.int32, (g * CHUNK, 2 * CHUNK), 1)
    rel = t + CHUNK - s
    mask = (rel >= 0) & (rel < CHUNK) & ((s >= CHUNK) | (n > 0))
    outs = []
    for kv in range(SW_KV_HEADS):
        qs = jnp.concatenate([q[:, (kv * g + j) * hd:(kv * g + j + 1) * hd] for j in range(g)], axis=0)
        sc = _dot_nt(qs.astype(BF16), kband[:, kv * hd:(kv + 1) * hd]) * (hd ** -0.5)
        sc = jnp.where(mask, sc, -jnp.inf)
        sink = sink_ref[kv]
        mx = jnp.maximum(jnp.max(sc, axis=-1, keepdims=True), sink)
        p = jnp.exp(sc - mx)
        denom = jnp.sum(p, axis=-1, keepdims=True) + jnp.exp(sink - mx)
        o = _dot((p / denom).astype(BF16), vband[:, kv * hd:(kv + 1) * hd])
        outs.extend(o[j * CHUNK:(j + 1) * CHUNK] for j in range(g))
    o = jnp.concatenate(outs, axis=-1)
    ms = jnp.mean(o * o, axis=-1, keepdims=True)
    o_ref[...] = (o * lax.rsqrt(ms + RMS_EPS) * gain_ref[...]).astype(o_ref.dtype)


def _swa(proj3, sinks, gain):
    b, s, _ = proj3.shape
    q_col = _SWQ_OFF // SW_WIDTH
    k_col = _SWK_OFF // SW_KV_WIDTH
    v_col = k_col + 1
    g = SW_Q_HEADS // SW_KV_HEADS
    sink_rows = jnp.repeat(sinks.astype(F32).reshape(SW_KV_HEADS, g), CHUNK, axis=1)[..., None]

    def kv_spec(col, prev):
        if prev:
            return pl.BlockSpec((None, CHUNK, SW_KV_WIDTH), lambda bi, ni: (bi, jnp.maximum(ni - 1, 0), col))
        return pl.BlockSpec((None, CHUNK, SW_KV_WIDTH), lambda bi, ni: (bi, ni, col))

    return pl.pallas_call(
        _swa_kernel,
        grid=(b, s // CHUNK),
        in_specs=[pl.BlockSpec((None, CHUNK, SW_WIDTH), lambda bi, ni: (bi, ni, q_col)),
                  kv_spec(k_col, True), kv_spec(k_col, False),
                  kv_spec(v_col, True), kv_spec(v_col, False),
                  pl.BlockSpec((SW_KV_HEADS, g * CHUNK, 1), lambda bi, ni: (0, 0, 0)),
                  pl.BlockSpec((1, SW_WIDTH), lambda bi, ni: (0, 0))],
        out_specs=pl.BlockSpec((None, CHUNK, SW_WIDTH), lambda bi, ni: (bi, ni, 0)),
        out_shape=jax.ShapeDtypeStruct((b, s, SW_WIDTH), BF16),
        compiler_params=pltpu.CompilerParams(dimension_semantics=("parallel", "parallel"),
                                             vmem_limit_bytes=VMEM_LIMIT),
        name="swa",
    )(proj3, proj3, proj3, proj3, proj3, sink_rows, gain.reshape(1, SW_WIDTH))


def _sgu_kernel(u_ref, v_ref, lng_ref, lnb_ref, w_ref, bs_ref, gain_ref, o_ref):
    u = _gelu(u_ref[...])
    v = _layer_norm(_gelu(v_ref[...]), lng_ref[...], lnb_ref[...]).astype(BF16)
    r = lax.broadcasted_iota(jnp.int32, (CHUNK, CHUNK), 0)
    c = lax.broadcasted_iota(jnp.int32, (CHUNK, CHUNK), 1)
    tril = c <= r
    bs = bs_ref[...]
    parts = []
    for gi in range(SG_GROUPS):
        gs = slice(gi * CHUNK, (gi + 1) * CHUNK)
        w = jnp.where(tril, w_ref[gi], 0.0).astype(BF16)
        parts.append(_dot(w, v[:, gs]) + bs[:, gi:gi + 1])
    o = u * jnp.concatenate(parts, axis=-1)
    ms = jnp.mean(o * o, axis=-1, keepdims=True)
    o_ref[...] = (o * lax.rsqrt(ms + RMS_EPS) * gain_ref[...]).astype(o_ref.dtype)


def _sgu(proj3, ln_g, ln_b, w_s, b_s, gain):
    b, s, _ = proj3.shape
    w = SG_WIDTH
    u_col = _SGU_OFF // SG_WIDTH
    vec = pl.BlockSpec((1, w), lambda bi, ci: (0, 0))
    return pl.pallas_call(
        _sgu_kernel,
        grid=(b, s // CHUNK),
        in_specs=[pl.BlockSpec((None, CHUNK, w), lambda bi, ci: (bi, ci, u_col)),
                  pl.BlockSpec((None, CHUNK, w), lambda bi, ci: (bi, ci, u_col + 1)),
                  vec, vec,
                  pl.BlockSpec((SG_GROUPS, CHUNK, CHUNK), lambda bi, ci: (0, 0, 0)),
                  pl.BlockSpec((CHUNK, SG_GROUPS), lambda bi, ci: (0, 0)),
                  vec],
        out_specs=pl.BlockSpec((None, CHUNK, w), lambda bi, ci: (bi, ci, 0)),
        out_shape=jax.ShapeDtypeStruct((b, s, w), BF16),
        compiler_params=pltpu.CompilerParams(dimension_semantics=("parallel", "parallel"),
                                             vmem_limit_bytes=VMEM_LIMIT),
        name="sgu",
    )(proj3, proj3, ln_g.reshape(1, w), ln_b.reshape(1, w), w_s, b_s.T, gain.reshape(1, w))


LANES = 128
SLAB = D_MODEL // LANES


def _store_slabs(ref, lead, x):
    n = x.shape[0]
    for j in range(SLAB):
        ref[lead + (pl.ds(j, n, stride=SLAB), slice(None))] = x[:, j * LANES:(j + 1) * LANES]


def _load_slabs(ref, lead, n):
    return jnp.concatenate([ref[lead + (pl.ds(j, n, stride=SLAB), slice(None))] for j in range(SLAB)], axis=1)


def _out_proj_kernel(oa_ref, ob_ref, oc_ref, w_ref, h_ref, g_ref, b_ref, of_ref, os_ref):
    mixed = jnp.concatenate([oa_ref[...], ob_ref[...], oc_ref[...]], axis=-1)
    y = ALPHA * h_ref[...] + _dot(mixed, w_ref[...])
    out = _layer_norm(y, g_ref[...], b_ref[...])
    of_ref[...] = out
    _store_slabs(os_ref, (), out)


def _out_proj(oa, ob, oc, w, h, g, b, tm=256):
    t, d = h.shape
    vec = pl.BlockSpec((1, d), lambda i: (0, 0))
    return pl.pallas_call(
        _out_proj_kernel,
        grid=(t // tm,),
        in_specs=[pl.BlockSpec((tm, oa.shape[1]), lambda i: (i, 0)),
                  pl.BlockSpec((tm, ob.shape[1]), lambda i: (i, 0)),
                  pl.BlockSpec((tm, oc.shape[1]), lambda i: (i, 0)),
                  pl.BlockSpec(w.shape, lambda i: (0, 0)),
                  pl.BlockSpec((tm, d), lambda i: (i, 0)), vec, vec],
        out_specs=[pl.BlockSpec((tm, d), lambda i: (i, 0)),
                   pl.BlockSpec((tm * SLAB, LANES), lambda i: (i, 0))],
        out_shape=[jax.ShapeDtypeStruct((t, d), F32), jax.ShapeDtypeStruct((t * SLAB, LANES), F32)],
        compiler_params=pltpu.CompilerParams(vmem_limit_bytes=VMEM_LIMIT),
        name="out_proj_ln1",
    )(oa, ob, oc, w, h, g.reshape(1, d), b.reshape(1, d))


def _first_index_of_max(x, iota, size, axis):
    mx = jnp.max(x, axis=axis, keepdims=True)
    idx = jnp.min(jnp.where(x == mx, iota, size), axis=axis, keepdims=True)
    return mx, idx


def _router_kernel(h_ref, wt_ref, bias_ref, upper_ref, eidx_ref, gate_ref, rank_ref, count_ref, run_ref):
    i = pl.program_id(0)

    @pl.when(i == 0)
    def _():
        run_ref[...] = jnp.zeros_like(run_ref)

    tm = h_ref.shape[0]
    per_group = N_EXPERTS // N_EXPERT_GROUPS
    logits = lax.dot_general(wt_ref[...], h_ref[...], (((1,), (1,)), ((), ())),
                             precision=lax.Precision.HIGHEST, preferred_element_type=F32)
    scores = 1.0 / (1.0 + jnp.exp(-logits))
    sel = scores + bias_ref[...]
    sel3 = sel.reshape(N_EXPERT_GROUPS, per_group, tm)
    io3 = lax.broadcasted_iota(jnp.int32, sel3.shape, 1)
    m1, i1 = _first_index_of_max(sel3, io3, per_group, 1)
    m2 = jnp.max(jnp.where(io3 == i1, -jnp.inf, sel3), axis=1, keepdims=True)
    grp = (m1 + m2).reshape(N_EXPERT_GROUPS, tm)
    iog = lax.broadcasted_iota(jnp.int32, grp.shape, 0)
    keep = jnp.zeros(grp.shape, jnp.bool_)
    for _ in range(TOPK_GROUPS):
        _, gi = _first_index_of_max(grp, iog, N_EXPERT_GROUPS, 0)
        hit = iog == gi
        keep = keep | hit
        grp = jnp.where(hit, -jnp.inf, grp)
    keep3 = jnp.broadcast_to(keep.reshape(N_EXPERT_GROUPS, 1, tm), sel3.shape)
    cand = jnp.where(keep3, sel3, -jnp.inf).reshape(N_EXPERTS, tm)
    ioe = lax.broadcasted_iota(jnp.int32, cand.shape, 0)
    chosen = jnp.zeros(cand.shape, F32)
    idxs, gvals, hits = [], [], []
    for _ in range(TOP_K):
        _, ei = _first_index_of_max(cand, ioe, N_EXPERTS, 0)
        hit = ioe == ei
        idxs.append(ei)
        gvals.append(jnp.sum(jnp.where(hit, scores, 0.0), axis=0, keepdims=True))
        hits.append(hit)
        chosen = jnp.where(hit, 1.0, chosen)
        cand = jnp.where(hit, -jnp.inf, cand)
    gsum = functools.reduce(lambda a, b: a + b, gvals)
    prefix = _dot(chosen.astype(BF16), upper_ref[...])
    base = run_ref[...] + prefix
    for kk in range(TOP_K):
        eidx_ref[kk:kk + 1, :] = idxs[kk]
        gate_ref[kk:kk + 1, :] = gvals[kk] / gsum * ROUTED_SCALE
        rank_ref[kk:kk + 1, :] = jnp.sum(jnp.where(hits[kk], base, 0.0), axis=0, keepdims=True).astype(jnp.int32)
    run = run_ref[...] + jnp.sum(chosen, axis=1, keepdims=True)
    run_ref[...] = run
    count_ref[...] = run.astype(jnp.int32)


def _router(h, router_w, router_bias, tm=ROUTER_TILE):
    t, d = h.shape
    tm = min(tm, t)
    upper = jnp.asarray(np.triu(np.ones((tm, tm), np.float32), 1), BF16)
    slot = pl.BlockSpec((TOP_K, tm), lambda i: (0, i))
    return pl.pallas_call(
        _router_kernel,
        grid=(t // tm,),
        in_specs=[pl.BlockSpec((tm, d), lambda i: (i, 0)),
                  pl.BlockSpec((N_EXPERTS, d), lambda i: (0, 0)),
                  pl.BlockSpec((N_EXPERTS, 1), lambda i: (0, 0)),
                  pl.BlockSpec((tm, tm), lambda i: (0, 0))],
        out_specs=[slot, slot, slot, pl.BlockSpec((N_EXPERTS, 1), lambda i: (0, 0))],
        out_shape=[jax.ShapeDtypeStruct((TOP_K, t), jnp.int32),
                   jax.ShapeDtypeStruct((TOP_K, t), F32),
                   jax.ShapeDtypeStruct((TOP_K, t), jnp.int32),
                   jax.ShapeDtypeStruct((N_EXPERTS, 1), jnp.int32)],
        scratch_shapes=[pltpu.VMEM((N_EXPERTS, 1), F32)],
        compiler_params=pltpu.CompilerParams(dimension_semantics=("arbitrary",),
                                             vmem_limit_bytes=VMEM_LIMIT),
        name="router",
    )(h, router_w.T, router_bias.reshape(N_EXPERTS, 1), upper)


_EXPERT_PHASES = 8


def _experts_kernel(be_ref, nu_ref, tokc_ref, tokn_ref, sidp_ref, sidc_ref, h_hbm, wg_ref, wu_ref, wd_ref,
                    y_hbm, xbuf0, xbuf1, obuf0, obuf1, gsem, ssem):
    b = pl.program_id(0)
    nu = nu_ref[0]
    xbuf = (xbuf0, xbuf1)
    obuf = (obuf0, obuf1)
    rows = xbuf0.shape[0] // SLAB
    d = wg_ref.shape[0]
    f = wg_ref.shape[1]

    def gather(tok_ref, r, dst_slot):
        src = pl.multiple_of(tok_ref[0, 0, r], SLAB)
        return pltpu.make_async_copy(h_hbm.at[pl.ds(src, SLAB)],
                                     xbuf[dst_slot].at[pl.ds(r * SLAB, SLAB)], gsem.at[dst_slot])

    def scatter(sid_ref, r, src_slot):
        dst = pl.multiple_of(sid_ref[0, 0, r], SLAB)
        return pltpu.make_async_copy(obuf[src_slot].at[pl.ds(r * SLAB, SLAB)],
                                     y_hbm.at[pl.ds(dst, SLAB)], ssem.at[src_slot])

    def wait_gather(s):
        pltpu.make_async_copy(h_hbm.at[pl.ds(0, rows * SLAB)], xbuf[s], gsem.at[s]).wait()

    def wait_scatter(s):
        pltpu.make_async_copy(obuf[s], y_hbm.at[pl.ds(0, rows * SLAB)], ssem.at[s]).wait()

    @pl.when(b == 0)
    def _():
        pad0 = y_hbm.shape[0] - 2 * rows * SLAB
        for s in range(2):
            obuf[s][...] = jnp.zeros_like(obuf[s])
            pltpu.make_async_copy(obuf[s], y_hbm.at[pl.ds(pad0 + s * rows * SLAB, rows * SLAB)],
                                  ssem.at[s]).start()
        for s in range(2):
            wait_scatter(s)

        def body(r, carry):
            src = pl.multiple_of(tokc_ref[0, 0, r], SLAB)
            dst = pl.multiple_of(r * SLAB, SLAB)
            pltpu.make_async_copy(h_hbm.at[pl.ds(src, SLAB)], xbuf0.at[pl.ds(dst, SLAB)], gsem.at[0]).start()
            return carry
        lax.fori_loop(0, rows, body, 0)

    def block(slot, with_scatter):
        other = 1 - slot
        wait_gather(slot)
        x = _load_slabs(xbuf[slot], (), rows).astype(BF16)
        per = rows // _EXPERT_PHASES
        phase = [0]

        def issue():
            p = phase[0]
            phase[0] += 1
            for r in range(p * per, (p + 1) * per):
                gather(tokn_ref, r, other).start()
                if with_scatter:
                    scatter(sidp_ref, r, other).start()

        acts = []
        fh = f // 2
        for j in range(2):
            issue()
            gate = _dot(x, wg_ref[:, j * fh:(j + 1) * fh])
            issue()
            up = _dot(x, wu_ref[:, j * fh:(j + 1) * fh])
            acts.append((_silu(gate) * up).astype(BF16))
        act = jnp.concatenate(acts, axis=1)
        dq = d // 4
        for j in range(4):
            issue()
            res = _dot(act, wd_ref[:, j * dq:(j + 1) * dq])
            for jj in range(dq // LANES):
                obuf[slot][pl.ds(j * (dq // LANES) + jj, rows, stride=SLAB), :] = res[:, jj * LANES:(jj + 1) * LANES]

    def tail(slot):
        other = 1 - slot
        wait_gather(other)

        @pl.when(b >= 1)
        def _():
            wait_scatter(other)

        def body(r, carry):
            src = pl.multiple_of(r * SLAB, SLAB)
            dst = pl.multiple_of(sidc_ref[0, 0, r], SLAB)
            pltpu.make_async_copy(obuf[slot].at[pl.ds(src, SLAB)], y_hbm.at[pl.ds(dst, SLAB)],
                                  ssem.at[slot]).start()
            return carry
        lax.fori_loop(0, rows, body, 0)
        wait_scatter(slot)

    @pl.when(b == 0)
    def _():
        block(0, False)

    for parity in range(2):
        @pl.when((b >= 1) & (b < nu) & (b % 2 == parity))
        def _(parity=parity):
            @pl.when(b >= 2)
            def _():
                wait_scatter(parity)
            block(parity, True)

        @pl.when((b == nu - 1) & (b % 2 == parity))
        def _(parity=parity):
            tail(parity)


def _experts(h_slab, tok_blocks, sid_blocks, block_e, n_used, wg, wu, wd, layer, n_slot_rows):
    nb, _, rows = tok_blocks.shape
    d, f = wg.shape[-2], wg.shape[-1]

    def clamp(b, nu):
        return jnp.maximum(jnp.minimum(b, nu[0] - 1), 0)

    def smem(shift):
        return pl.BlockSpec((1, 1, rows), lambda b, be, nu: (clamp(b + shift, nu), 0, 0), memory_space=pltpu.SMEM)

    def weight(shape):
        return pl.BlockSpec((None, None) + shape, lambda b, be, nu: (layer, be[clamp(b, nu)], 0, 0))

    grid_spec = pltpu.PrefetchScalarGridSpec(
        num_scalar_prefetch=2,
        grid=(nb,),
        in_specs=[smem(0), smem(1), smem(-1), smem(0),
                  pl.BlockSpec(memory_space=pl.ANY),
                  weight((d, f)), weight((d, f)), weight((f, d))],
        out_specs=pl.BlockSpec(memory_space=pl.ANY),
        scratch_shapes=[pltpu.VMEM((rows * SLAB, LANES), F32)] * 4
        + [pltpu.SemaphoreType.DMA((2,)), pltpu.SemaphoreType.DMA((2,))],
    )
    return pl.pallas_call(
        _experts_kernel,
        grid_spec=grid_spec,
        out_shape=jax.ShapeDtypeStruct(((n_slot_rows + 2 * rows) * SLAB, LANES), F32),
        compiler_params=pltpu.CompilerParams(dimension_semantics=("arbitrary",),
                                             vmem_limit_bytes=VMEM_LIMIT),
        name="experts",
    )(block_e, n_used, tok_blocks, tok_blocks, sid_blocks, sid_blocks, h_slab, wg, wu, wd)


def _combine_kernel(h_ref, gate_ref, *refs):
    y_refs = refs[:TOP_K]
    sg_ref, su_ref, sd_ref, g_ref, b_ref, of_ref, ob_ref = refs[TOP_K:]
    h = h_ref[...]
    hb = h.astype(BF16)
    act = _silu(_dot(hb, sg_ref[...])) * _dot(hb, su_ref[...])
    y = ALPHA * h + _dot(act.astype(BF16), sd_ref[...])
    gates = gate_ref[...]
    tm = h.shape[0]
    for kk in range(TOP_K):
        y = y + gates[:, kk:kk + 1] * _load_slabs(y_refs[kk], (), tm)
    out = _layer_norm(y, g_ref[...], b_ref[...])
    of_ref[...] = out
    ob_ref[...] = out.astype(BF16)


def _combine(h, gates_t, y_slots, sg, su, sd, g, b, tm=COMBINE_TILE):
    t, d = h.shape
    n = t // tm
    f = sg.shape[1]
    vec = pl.BlockSpec((1, d), lambda i: (0, 0))
    row = pl.BlockSpec((tm, d), lambda i: (i, 0))
    planes = [pl.BlockSpec((tm * SLAB, LANES), lambda i, kk=kk: (kk * n + i, 0)) for kk in range(TOP_K)]
    return pl.pallas_call(
        _combine_kernel,
        grid=(n,),
        in_specs=[row, pl.BlockSpec((tm, TOP_K), lambda i: (i, 0))] + planes + [
            pl.BlockSpec((d, f), lambda i: (0, 0)),
            pl.BlockSpec((d, f), lambda i: (0, 0)),
            pl.BlockSpec((f, d), lambda i: (0, 0)),
            vec, vec],
        out_specs=[row, row],
        out_shape=[jax.ShapeDtypeStruct((t, d), F32), jax.ShapeDtypeStruct((t, d), BF16)],
        compiler_params=pltpu.CompilerParams(dimension_semantics=("parallel",),
                                             vmem_limit_bytes=VMEM_LIMIT),
        name="combine_ln2",
    )(h, gates_t, *([y_slots] * TOP_K), sg, su, sd, g.reshape(1, d), b.reshape(1, d))


def _moe(h, h_slab, router_w, router_bias, wg, wu, wd, layer, sg, su, sd, ln_g, ln_b):
    t, d = h.shape
    tk = t * TOP_K
    n_blocks = tk // MOE_ROWS + N_EXPERTS
    n_rows = n_blocks * MOE_ROWS
    eidx, gates, rank, counts = _router(h, router_w, router_bias)
    counts = counts.reshape(N_EXPERTS)
    padded = (counts + MOE_ROWS - 1) // MOE_ROWS * MOE_ROWS
    pend = jnp.cumsum(padded)
    pstart = pend - padded
    experts = jnp.arange(N_EXPERTS, dtype=jnp.int32)[:, None, None]
    dest = jnp.sum(jnp.where(eidx[None] == experts, pstart[:, None, None], 0), axis=0) + rank
    slot_ids = (jnp.arange(TOP_K, dtype=jnp.int32)[:, None] * t + jnp.arange(t, dtype=jnp.int32)[None, :])
    pad_ids = tk + jnp.arange(n_rows, dtype=jnp.int32) % (2 * MOE_ROWS)
    sid = pad_ids.at[dest.reshape(-1)].set(slot_ids.reshape(-1), unique_indices=True)
    tok = jnp.where(sid < tk, sid % t, 0)
    block_start = jnp.arange(n_blocks, dtype=jnp.int32) * MOE_ROWS
    block_e = jnp.minimum(jnp.sum(pend[None, :] <= block_start[:, None], axis=1), N_EXPERTS - 1).astype(jnp.int32)
    n_used = (pend[-1] // MOE_ROWS).astype(jnp.int32).reshape(1)
    y_slots = _experts(h_slab, (tok * SLAB).reshape(n_blocks, 1, MOE_ROWS),
                       (sid * SLAB).reshape(n_blocks, 1, MOE_ROWS), block_e, n_used, wg, wu, wd, layer, tk)
    return _combine(h, gates.T, y_slots, sg, su, sd, ln_g, ln_b, tm=min(COMBINE_TILE, t))


def kernel(x, ln0_gain, ln0_bias, w_in, b_in, hg_lb_logits, sw_sinks, sg_ln_gain, sg_ln_bias, sg_w_s, sg_b_s, mix_gain, w_out, ln1_gain, ln1_bias, router_w, router_bias, exp_w_gate, exp_w_up, exp_w_down, sh_w_gate, sh_w_up, sh_w_down, ln2_gain, ln2_bias):
    b, s, d = x.shape
    t = b * s
    p = jax.nn.softmax(hg_lb_logits.astype(F32), axis=0)
    cs = jnp.cumsum(p, axis=0)
    lbs = cs - cs[0:1]
    a0, a1 = HG_WIDTH, HG_WIDTH + SW_WIDTH

    wg_all, wu_all, wd_all = exp_w_gate.astype(BF16), exp_w_up.astype(BF16), exp_w_down.astype(BF16)
    h, hb = _ln(x.reshape(t, d), ln0_gain, ln0_bias)
    for l in range(DEPTH):
        proj = _in_proj(hb, _regroup_columns(w_in[l]).astype(BF16), _regroup_columns(b_in[l]))
        proj3 = proj.reshape(b, s, D_IN)
        lb = lbs[l]
        o_a = _hgrn(proj3, jnp.log(lb), jnp.log1p(-lb), 1.0 - lb, mix_gain[l, :a0])
        o_b = _swa(proj3, sw_sinks[l], mix_gain[l, a0:a1])
        o_c = _sgu(proj3, sg_ln_gain[l], sg_ln_bias[l], sg_w_s[l], sg_b_s[l], mix_gain[l, a1:])
        h, h_slab = _out_proj(o_a.reshape(t, -1), o_b.reshape(t, -1), o_c.reshape(t, -1),
                              w_out[l].astype(BF16), h, ln1_gain[l], ln1_bias[l])
        h, hb = _moe(h, h_slab, router_w[l], router_bias[l], wg_all, wu_all, wd_all, l,
                     sh_w_gate[l].astype(BF16), sh_w_up[l].astype(BF16), sh_w_down[l].astype(BF16),
                     ln2_gain[l], ln2_bias[l])
    return h.reshape(b, s, d)
```

```python
import functools

import numpy as np
import jax
import jax.numpy as jnp
from jax import lax
from jax.experimental import pallas as pl
from jax.experimental.pallas import tpu as pltpu

F32 = jnp.float32
BF16 = jnp.bfloat16

D_MODEL = 2048
DEPTH = 2
HG_HEADS = 4
HG_DK = 128
HG_WIDTH = HG_HEADS * HG_DK
CHUNK = 128
SW_Q_HEADS = 16
SW_KV_HEADS = 2
SW_HEAD_DIM = 64
SW_WIDTH = SW_Q_HEADS * SW_HEAD_DIM
SW_KV_WIDTH = SW_KV_HEADS * SW_HEAD_DIM
SG_GROUPS = 4
SG_WIDTH = SG_GROUPS * CHUNK
D_IN = 4 * HG_WIDTH + SW_WIDTH + 2 * SW_KV_WIDTH + 2 * SG_WIDTH
N_EXPERTS = 64
TOP_K = 8
D_EXPERT = 512
N_EXPERT_GROUPS = 8
TOPK_GROUPS = 4
ROUTED_SCALE = 2.5
ALPHA = (2 * DEPTH) ** 0.25
LN_EPS = 1e-5
RMS_EPS = 1e-6

_SWQ_OFF = 4 * HG_WIDTH
_SGU_OFF = _SWQ_OFF + SW_WIDTH
_SWK_OFF = _SGU_OFF + 2 * SG_WIDTH


def _regroup_columns(a):
    k0 = _SWQ_OFF + SW_WIDTH
    k1 = k0 + 2 * SW_KV_WIDTH
    return jnp.concatenate([a[..., :k0], a[..., k1:], a[..., k0:k1]], axis=-1)


MOE_ROWS = 256
ROUTER_TILE = 512
COMBINE_TILE = 128
VMEM_LIMIT = 56 * 1024 * 1024

_LEVEL_SIZES = (64, 32, 16, 8, 4, 2, 1)
_N_LEVELS = len(_LEVEL_SIZES)


def _hgrn_constants():
    t = np.arange(CHUNK)[:, None]
    u = np.arange(CHUNK)[None, :]
    mats = []
    level = np.full((CHUNK, CHUNK), -1, np.int32)
    for li, s in enumerate(_LEVEL_SIZES):
        blk = t // s
        odd = (blk % 2) == 1
        m_query = (u >= blk * s) & (u <= t)
        m_key = (u > t) & (u <= (blk + 1) * s - 1)
        mats.append(np.where(odd, m_query, m_key))
        pair = ((t // s) % 2 == 1) & ((u // s) == (t // s) - 1)
        level[pair] = li
    mats.append(u <= t)
    mats.append(u > t)
    level[np.arange(CHUNK), np.arange(CHUNK)] = _N_LEVELS
    return np.concatenate(mats, 0).astype(np.float32), level


_HGRN_MSTACK, _HGRN_LEVEL = _hgrn_constants()


def _layer_norm(x, g, b):
    mu = jnp.mean(x, axis=-1, keepdims=True)
    xc = x - mu
    var = jnp.mean(xc * xc, axis=-1, keepdims=True)
    return xc * lax.rsqrt(var + LN_EPS) * g + b


def _silu(x):
    return x / (1.0 + jnp.exp(-x))


def _gelu(x):
    return 0.5 * x * (1.0 + lax.erf(x * (2.0 ** -0.5)))


def _dot(a, b):
    return jnp.dot(a, b, preferred_element_type=F32)


def _dot_nt(a, b):
    return lax.dot_general(a, b, (((1,), (1,)), ((), ())), preferred_element_type=F32)


def _dot_tn(a, b):
    return lax.dot_general(a, b, (((0,), (0,)), ((), ())), preferred_element_type=F32)


def _ln_kernel(x_ref, g_ref, b_ref, of_ref, ob_ref):
    y = _layer_norm(x_ref[...], g_ref[...], b_ref[...])
    of_ref[...] = y
    ob_ref[...] = y.astype(BF16)


def _ln(x, g, b, tm=256):
    t, d = x.shape
    row = pl.BlockSpec((tm, d), lambda i: (i, 0))
    vec = pl.BlockSpec((1, d), lambda i: (0, 0))
    return pl.pallas_call(
        _ln_kernel,
        grid=(t // tm,),
        in_specs=[row, vec, vec],
        out_specs=[row, row],
        out_shape=[jax.ShapeDtypeStruct((t, d), F32), jax.ShapeDtypeStruct((t, d), BF16)],
        name="ln0",
    )(x, g.reshape(1, d), b.reshape(1, d))


def _mm_bias_kernel(a_ref, w_ref, b_ref, o_ref):
    o_ref[...] = _dot(a_ref[...], w_ref[...]) + b_ref[...]


def _in_proj(a, w, b, tm=2048, tn=256):
    t, k = a.shape
    n = w.shape[1]
    tm = min(tm, t)
    return pl.pallas_call(
        _mm_bias_kernel,
        grid=(t // tm, n // tn),
        in_specs=[pl.BlockSpec((tm, k), lambda i, j: (i, 0)),
                  pl.BlockSpec((k, tn), lambda i, j: (0, j)),
                  pl.BlockSpec((1, tn), lambda i, j: (0, j))],
        out_specs=pl.BlockSpec((tm, tn), lambda i, j: (i, j)),
        out_shape=jax.ShapeDtypeStruct((t, n), F32),
        compiler_params=pltpu.CompilerParams(vmem_limit_bytes=VMEM_LIMIT),
        name="in_proj",
    )(a, w, b.reshape(1, n))


def _hgrn_kernel(q_ref, f_ref, i_ref, g_ref, llb_ref, l1lb_ref, oml_ref, gain_ref,
                 mstack_ref, level_ref, o_ref, state_ref):
    c = pl.program_id(1)

    @pl.when(c == 0)
    def _():
        state_ref[...] = jnp.zeros_like(state_ref)

    q = _silu(q_ref[...])
    z = f_ref[...]
    ez = jnp.exp(-jnp.abs(z))
    log_sig = jnp.minimum(z, 0.0) - jnp.log(1.0 + ez)
    sig_neg = jnp.where(z >= 0.0, ez, 1.0) / (1.0 + ez)
    k = oml_ref[...] * sig_neg
    a = llb_ref[...]
    bb = l1lb_ref[...] + log_sig
    log_f = jnp.maximum(a, bb) + jnp.log(1.0 + jnp.exp(-jnp.abs(a - bb)))

    lf_hi = log_f.astype(BF16)
    lf_lo = (log_f - lf_hi.astype(F32)).astype(BF16)
    m = mstack_ref[...]
    sums = _dot(m, lf_hi) + _dot(m, lf_lo)
    decay = jnp.exp(sums)

    rows = lax.broadcasted_iota(jnp.int32, (CHUNK, 1), 0)
    factors = []
    for li, s in enumerate(_LEVEL_SIZES):
        is_query = ((rows // s) % 2) == 1
        factors.append((jnp.where(is_query, q, k) * decay[li * CHUNK:(li + 1) * CHUNK]).astype(BF16))
    cum = _N_LEVELS * CHUNK
    q_in = (q * decay[cum:cum + CHUNK]).astype(BF16)
    k_dec = (k * decay[cum + CHUNK:cum + 2 * CHUNK]).astype(BF16)
    end_decay = decay[cum + CHUNK - 1:cum + CHUNK]
    qb = q.astype(BF16)
    kb = k.astype(BF16)
    vb = i_ref[...].astype(BF16)
    level = level_ref[...]
    gate = _silu(g_ref[...])
    gain = gain_ref[...]

    for h in range(HG_HEADS):
        hs = slice(h * HG_DK, (h + 1) * HG_DK)
        scores = jnp.zeros((CHUNK, CHUNK), F32)
        for li in range(_N_LEVELS):
            fl = factors[li][:, hs]
            scores = jnp.where(level == li, _dot_nt(fl, fl), scores)
        scores = jnp.where(level == _N_LEVELS, _dot_nt(qb[:, hs], kb[:, hs]), scores)
        st = state_ref[h]
        o = _dot(scores.astype(BF16), vb[:, hs]) + _dot_nt(q_in[:, hs], st.astype(BF16))
        state_ref[h] = st * end_decay[:, hs] + _dot_tn(vb[:, hs], k_dec[:, hs])
        ms = jnp.mean(o * o, axis=-1, keepdims=True)
        o = o * lax.rsqrt(ms + RMS_EPS) * gain[:, hs] * gate[:, hs]
        o_ref[:, hs] = o.astype(o_ref.dtype)


def _hgrn(proj3, llb, l1lb, oml, gain):
    b, s, _ = proj3.shape
    w = HG_WIDTH

    def col(j):
        return pl.BlockSpec((None, CHUNK, w), lambda bi, ci, j=j: (bi, ci, j))

    vec = pl.BlockSpec((1, w), lambda bi, ci: (0, 0))
    nm = _HGRN_MSTACK.shape[0]
    return pl.pallas_call(
        _hgrn_kernel,
        grid=(b, s // CHUNK),
        in_specs=[col(0), col(1), col(2), col(3), vec, vec, vec, vec,
                  pl.BlockSpec((nm, CHUNK), lambda bi, ci: (0, 0)),
                  pl.BlockSpec((CHUNK, CHUNK), lambda bi, ci: (0, 0))],
        out_specs=pl.BlockSpec((None, CHUNK, w), lambda bi, ci: (bi, ci, 0)),
        out_shape=jax.ShapeDtypeStruct((b, s, w), BF16),
        scratch_shapes=[pltpu.VMEM((HG_HEADS, HG_DK, HG_DK), F32)],
        compiler_params=pltpu.CompilerParams(dimension_semantics=("parallel", "arbitrary"),
                                             vmem_limit_bytes=VMEM_LIMIT),
        name="hgrn2",
    )(proj3, proj3, proj3, proj3, llb.reshape(1, w), l1lb.reshape(1, w), oml.reshape(1, w),
      gain.reshape(1, w), jnp.asarray(_HGRN_MSTACK, BF16), jnp.asarray(_HGRN_LEVEL))


def _swa_kernel(q_ref, kp_ref, kc_ref, vp_ref, vc_ref, sink_ref, gain_ref, o_ref):
    n = pl.program_id(1)
    g = SW_Q_HEADS // SW_KV_HEADS
    hd = SW_HEAD_DIM
    q = q_ref[...]
    kband = jnp.concatenate([kp_ref[...], kc_ref[...]], axis=0).astype(BF16)
    vband = jnp.concatenate([vp_ref[...], vc_ref[...]], axis=0).astype(BF16)
    t = lax.broadcasted_iota(jnp.int32, (g * CHUNK, 2 * CHUNK), 0) % CHUNK
    s = lax.broadcasted_iota(jnp.int32, (g * CHUNK, 2 * CHUNK), 1)
    rel = t + CHUNK - s
    mask = (rel >= 0) & (rel < CHUNK) & ((s >= CHUNK) | (n > 0))
    outs = []
    for kv in range(SW_KV_HEADS):
        qs = jnp.concatenate([q[:, (kv * g + j) * hd:(kv * g + j + 1) * hd] for j in range(g)], axis=0)
        sc = _dot_nt(qs.astype(BF16), kband[:, kv * hd:(kv + 1) * hd]) * (hd ** -0.5)
        sc = jnp.where(mask, sc, -jnp.inf)
        sink = sink_ref[kv]
        mx = jnp.maximum(jnp.max(sc, axis=-1, keepdims=True), sink)
        p = jnp.exp(sc - mx)
        denom = jnp.sum(p, axis=-1, keepdims=True) + jnp.exp(sink - mx)
        o = _dot((p / denom).astype(BF16), vband[:, kv * hd:(kv + 1) * hd])
        outs.extend(o[j * CHUNK:(j + 1) * CHUNK] for j in range(g))
    o = jnp.concatenate(outs, axis=-1)
    ms = jnp.mean(o * o, axis=-1, keepdims=True)
    o_ref[...] = (o * lax.rsqrt(ms + RMS_EPS) * gain_ref[...]).astype(o_ref.dtype)


def _swa(proj3, sinks, gain):
    b, s, _ = proj3.shape
    q_col = _SWQ_OFF // SW_WIDTH
    k_col = _SWK_OFF // SW_KV_WIDTH
    v_col = k_col + 1
    g = SW_Q_HEADS // SW_KV_HEADS
    sink_rows = jnp.repeat(sinks.astype(F32).reshape(SW_KV_HEADS, g), CHUNK, axis=1)[..., None]

    def kv_spec(col, prev):
        if prev:
            return pl.BlockSpec((None, CHUNK, SW_KV_WIDTH), lambda bi, ni: (bi, jnp.maximum(ni - 1, 0), col))
        return pl.BlockSpec((None, CHUNK, SW_KV_WIDTH), lambda bi, ni: (bi, ni, col))

    return pl.pallas_call(
        _swa_kernel,
        grid=(b, s // CHUNK),
        in_specs=[pl.BlockSpec((None, CHUNK, SW_WIDTH), lambda bi, ni: (bi, ni, q_col)),
                  kv_spec(k_col, True), kv_spec(k_col, False),
                  kv_spec(v_col, True), kv_spec(v_col, False),
                  pl.BlockSpec((SW_KV_HEADS, g * CHUNK, 1), lambda bi, ni: (0, 0, 0)),
                  pl.BlockSpec((1, SW_WIDTH), lambda bi, ni: (0, 0))],
        out_specs=pl.BlockSpec((None, CHUNK, SW_WIDTH), lambda bi, ni: (bi, ni, 0)),
        out_shape=jax.ShapeDtypeStruct((b, s, SW_WIDTH), BF16),
        compiler_params=pltpu.CompilerParams(dimension_semantics=("parallel", "parallel"),
                                             vmem_limit_bytes=VMEM_LIMIT),
        name="swa",
    )(proj3, proj3, proj3, proj3, proj3, sink_rows, gain.reshape(1, SW_WIDTH))


def _sgu_kernel(u_ref, v_ref, lng_ref, lnb_ref, w_ref, bs_ref, gain_ref, o_ref):
    u = _gelu(u_ref[...])
    v = _layer_norm(_gelu(v_ref[...]), lng_ref[...], lnb_ref[...]).astype(BF16)
    r = lax.broadcasted_iota(jnp.int32, (CHUNK, CHUNK), 0)
    c = lax.broadcasted_iota(jnp.int32, (CHUNK, CHUNK), 1)
    tril = c <= r
    bs = bs_ref[...]
    parts = []
    for gi in range(SG_GROUPS):
        gs = slice(gi * CHUNK, (gi + 1) * CHUNK)
        w = jnp.where(tril, w_ref[gi], 0.0).astype(BF16)
        parts.append(_dot(w, v[:, gs]) + bs[:, gi:gi + 1])
    o = u * jnp.concatenate(parts, axis=-1)
    ms = jnp.mean(o * o, axis=-1, keepdims=True)
    o_ref[...] = (o * lax.rsqrt(ms + RMS_EPS) * gain_ref[...]).astype(o_ref.dtype)


def _sgu(proj3, ln_g, ln_b, w_s, b_s, gain):
    b, s, _ = proj3.shape
    w = SG_WIDTH
    u_col = _SGU_OFF // SG_WIDTH
    vec = pl.BlockSpec((1, w), lambda bi, ci: (0, 0))
    return pl.pallas_call(
        _sgu_kernel,
        grid=(b, s // CHUNK),
        in_specs=[pl.BlockSpec((None, CHUNK, w), lambda bi, ci: (bi, ci, u_col)),
                  pl.BlockSpec((None, CHUNK, w), lambda bi, ci: (bi, ci, u_col + 1)),
                  vec, vec,
                  pl.BlockSpec((SG_GROUPS, CHUNK, CHUNK), lambda bi, ci: (0, 0, 0)),
                  pl.BlockSpec((CHUNK, SG_GROUPS), lambda bi, ci: (0, 0)),
                  vec],
        out_specs=pl.BlockSpec((None, CHUNK, w), lambda bi, ci: (bi, ci, 0)),
        out_shape=jax.ShapeDtypeStruct((b, s, w), BF16),
        compiler_params=pltpu.CompilerParams(dimension_semantics=("parallel", "parallel"),
                                             vmem_limit_bytes=VMEM_LIMIT),
        name="sgu",
    )(proj3, proj3, ln_g.reshape(1, w), ln_b.reshape(1, w), w_s, b_s.T, gain.reshape(1, w))


LANES = 128
SLAB = D_MODEL // LANES


def _store_slabs(ref, lead, x):
    n = x.shape[0]
    for j in range(SLAB):
        ref[lead + (pl.ds(j, n, stride=SLAB), slice(None))] = x[:, j * LANES:(j + 1) * LANES]


def _load_slabs(ref, lead, n):
    return jnp.concatenate([ref[lead + (pl.ds(j, n, stride=SLAB), slice(None))] for j in range(SLAB)], axis=1)


def _out_proj_kernel(oa_ref, ob_ref, oc_ref, w_ref, h_ref, g_ref, b_ref, of_ref, os_ref):
    mixed = jnp.concatenate([oa_ref[...], ob_ref[...], oc_ref[...]], axis=-1)
    y = ALPHA * h_ref[...] + _dot(mixed, w_ref[...])
    out = _layer_norm(y, g_ref[...], b_ref[...])
    of_ref[...] = out
    _store_slabs(os_ref, (), out)


def _out_proj(oa, ob, oc, w, h, g, b, tm=256):
    t, d = h.shape
    vec = pl.BlockSpec((1, d), lambda i: (0, 0))
    return pl.pallas_call(
        _out_proj_kernel,
        grid=(t // tm,),
        in_specs=[pl.BlockSpec((tm, oa.shape[1]), lambda i: (i, 0)),
                  pl.BlockSpec((tm, ob.shape[1]), lambda i: (i, 0)),
                  pl.BlockSpec((tm, oc.shape[1]), lambda i: (i, 0)),
                  pl.BlockSpec(w.shape, lambda i: (0, 0)),
                  pl.BlockSpec((tm, d), lambda i: (i, 0)), vec, vec],
        out_specs=[pl.BlockSpec((tm, d), lambda i: (i, 0)),
                   pl.BlockSpec((tm * SLAB, LANES), lambda i: (i, 0))],
        out_shape=[jax.ShapeDtypeStruct((t, d), F32), jax.ShapeDtypeStruct((t * SLAB, LANES), F32)],
        compiler_params=pltpu.CompilerParams(vmem_limit_bytes=VMEM_LIMIT),
        name="out_proj_ln1",
    )(oa, ob, oc, w, h, g.reshape(1, d), b.reshape(1, d))


def _first_index_of_max(x, iota, size, axis):
    mx = jnp.max(x, axis=axis, keepdims=True)
    idx = jnp.min(jnp.where(x == mx, iota, size), axis=axis, keepdims=True)
    return mx, idx


def _router_kernel(h_ref, wt_ref, bias_ref, upper_ref, eidx_ref, gate_ref, rank_ref, count_ref, run_ref):
    i = pl.program_id(0)

    @pl.when(i == 0)
    def _():
        run_ref[...] = jnp.zeros_like(run_ref)

    tm = h_ref.shape[0]
    per_group = N_EXPERTS // N_EXPERT_GROUPS
    logits = lax.dot_general(wt_ref[...], h_ref[...], (((1,), (1,)), ((), ())),
                             precision=lax.Precision.HIGHEST, preferred_element_type=F32)
    scores = 1.0 / (1.0 + jnp.exp(-logits))
    sel = scores + bias_ref[...]
    sel3 = sel.reshape(N_EXPERT_GROUPS, per_group, tm)
    io3 = lax.broadcasted_iota(jnp.int32, sel3.shape, 1)
    m1, i1 = _first_index_of_max(sel3, io3, per_group, 1)
    m2 = jnp.max(jnp.where(io3 == i1, -jnp.inf, sel3), axis=1, keepdims=True)
    grp = (m1 + m2).reshape(N_EXPERT_GROUPS, tm)
    iog = lax.broadcasted_iota(jnp.int32, grp.shape, 0)
    keep = jnp.zeros(grp.shape, jnp.bool_)
    for _ in range(TOPK_GROUPS):
        _, gi = _first_index_of_max(grp, iog, N_EXPERT_GROUPS, 0)
        hit = iog == gi
        keep = keep | hit
        grp = jnp.where(hit, -jnp.inf, grp)
    keep3 = jnp.broadcast_to(keep.reshape(N_EXPERT_GROUPS, 1, tm), sel3.shape)
    cand = jnp.where(keep3, sel3, -jnp.inf).reshape(N_EXPERTS, tm)
    ioe = lax.broadcasted_iota(jnp.int32, cand.shape, 0)
    chosen = jnp.zeros(cand.shape, F32)
    idxs, gvals, hits = [], [], []
    for _ in range(TOP_K):
        _, ei = _first_index_of_max(cand, ioe, N_EXPERTS, 0)
        hit = ioe == ei
        idxs.append(ei)
        gvals.append(jnp.sum(jnp.where(hit, scores, 0.0), axis=0, keepdims=True))
        hits.append(hit)
        chosen = jnp.where(hit, 1.0, chosen)
        cand = jnp.where(hit, -jnp.inf, cand)
    gsum = functools.reduce(lambda a, b: a + b, gvals)
    prefix = _dot(chosen.astype(BF16), upper_ref[...])
    base = run_ref[...] + prefix
    for kk in range(TOP_K):
        eidx_ref[kk:kk + 1, :] = idxs[kk]
        gate_ref[kk:kk + 1, :] = gvals[kk] / gsum * ROUTED_SCALE
        rank_ref[kk:kk + 1, :] = jnp.sum(jnp.where(hits[kk], base, 0.0), axis=0, keepdims=True).astype(jnp.int32)
    run = run_ref[...] + jnp.sum(chosen, axis=1, keepdims=True)
    run_ref[...] = run
    count_ref[...] = run.astype(jnp.int32)


def _router(h, router_w, router_bias, tm=ROUTER_TILE):
    t, d = h.shape
    tm = min(tm, t)
    upper = jnp.asarray(np.triu(np.ones((tm, tm), np.float32), 1), BF16)
    slot = pl.BlockSpec((TOP_K, tm), lambda i: (0, i))
    return pl.pallas_call(
        _router_kernel,
        grid=(t // tm,),
        in_specs=[pl.BlockSpec((tm, d), lambda i: (i, 0)),
                  pl.BlockSpec((N_EXPERTS, d), lambda i: (0, 0)),
                  pl.BlockSpec((N_EXPERTS, 1), lambda i: (0, 0)),
                  pl.BlockSpec((tm, tm), lambda i: (0, 0))],
        out_specs=[slot, slot, slot, pl.BlockSpec((N_EXPERTS, 1), lambda i: (0, 0))],
        out_shape=[jax.ShapeDtypeStruct((TOP_K, t), jnp.int32),
                   jax.ShapeDtypeStruct((TOP_K, t), F32),
                   jax.ShapeDtypeStruct((TOP_K, t), jnp.int32),
                   jax.ShapeDtypeStruct((N_EXPERTS, 1), jnp.int32)],
        scratch_shapes=[pltpu.VMEM((N_EXPERTS, 1), F32)],
        compiler_params=pltpu.CompilerParams(dimension_semantics=("arbitrary",),
                                             vmem_limit_bytes=VMEM_LIMIT),
        name="router",
    )(h, router_w.T, router_bias.reshape(N_EXPERTS, 1), upper)


_ISSUE_UNROLL = 8


def _experts_kernel(be_ref, nu_ref, tokc_ref, tokn_ref, sidp_ref, sidc_ref, h_hbm, wg_ref, wu_ref, wd_ref,
                    y_hbm, xbuf0, xbuf1, obuf0, obuf1, gsem, ssem):
    b = pl.program_id(0)
    nu = nu_ref[0]
    xbuf = (xbuf0, xbuf1)
    obuf = (obuf0, obuf1)
    rows = xbuf0.shape[0] // SLAB
    d = wg_ref.shape[0]
    f = wg_ref.shape[1]

    def gather(tok_ref, r, dst_slot):
        src = pl.multiple_of(tok_ref[0, 0, r], SLAB)
        return pltpu.make_async_copy(h_hbm.at[pl.ds(src, SLAB)],
                                     xbuf[dst_slot].at[pl.ds(r * SLAB, SLAB)], gsem.at[dst_slot])

    def scatter(sid_ref, r, src_slot):
        dst = pl.multiple_of(sid_ref[0, 0, r], SLAB)
        return pltpu.make_async_copy(obuf[src_slot].at[pl.ds(r * SLAB, SLAB)],
                                     y_hbm.at[pl.ds(dst, SLAB)], ssem.at[src_slot])

    def wait_gather(s):
        pltpu.make_async_copy(h_hbm.at[pl.ds(0, rows * SLAB)], xbuf[s], gsem.at[s]).wait()

    def wait_scatter(s):
        pltpu.make_async_copy(obuf[s], y_hbm.at[pl.ds(0, rows * SLAB)], ssem.at[s]).wait()

    @pl.when(b == 0)
    def _():
        pad0 = y_hbm.shape[0] - 2 * rows * SLAB
        for s in range(2):
            obuf[s][...] = jnp.zeros_like(obuf[s])
            pltpu.make_async_copy(obuf[s], y_hbm.at[pl.ds(pad0 + s * rows * SLAB, rows * SLAB)],
                                  ssem.at[s]).start()
        for s in range(2):
            wait_scatter(s)

        def body(r, carry):
            src = pl.multiple_of(tokc_ref[0, 0, r], SLAB)
            dst = pl.multiple_of(r * SLAB, SLAB)
            pltpu.make_async_copy(h_hbm.at[pl.ds(src, SLAB)], xbuf0.at[pl.ds(dst, SLAB)], gsem.at[0]).start()
            return carry
        lax.fori_loop(0, rows, body, 0)

    def block(slot, with_scatter):
        other = 1 - slot
        wait_gather(slot)

        def issue(i, carry):
            for u in range(_ISSUE_UNROLL):
                r = i * _ISSUE_UNROLL + u
                buf_rows = pl.ds(pl.multiple_of(r * SLAB, SLAB), SLAB)
                src = pl.multiple_of(tokn_ref[0, 0, r], SLAB)
                pltpu.make_async_copy(h_hbm.at[pl.ds(src, SLAB)], xbuf[other].at[buf_rows],
                                      gsem.at[other]).start(priority=0)
                if with_scatter:
                    dst = pl.multiple_of(sidp_ref[0, 0, r], SLAB)
                    pltpu.make_async_copy(obuf[other].at[buf_rows], y_hbm.at[pl.ds(dst, SLAB)],
                                          ssem.at[other]).start(priority=1)
            return carry
        lax.fori_loop(0, rows // _ISSUE_UNROLL, issue, 0)

        x = _load_slabs(xbuf[slot], (), rows).astype(BF16)
        acts = []
        fh = f // 2
        for j in range(2):
            gate = _dot(x, wg_ref[:, j * fh:(j + 1) * fh])
            up = _dot(x, wu_ref[:, j * fh:(j + 1) * fh])
            acts.append((_silu(gate) * up).astype(BF16))
        act = jnp.concatenate(acts, axis=1)
        dq = d // 4
        for j in range(4):
            res = _dot(act, wd_ref[:, j * dq:(j + 1) * dq])
            for jj in range(dq // LANES):
                obuf[slot][pl.ds(j * (dq // LANES) + jj, rows, stride=SLAB), :] = res[:, jj * LANES:(jj + 1) * LANES]

    def tail(slot):
        other = 1 - slot
        wait_gather(other)

        @pl.when(b >= 1)
        def _():
            wait_scatter(other)

        def body(r, carry):
            src = pl.multiple_of(r * SLAB, SLAB)
            dst = pl.multiple_of(sidc_ref[0, 0, r], SLAB)
            pltpu.make_async_copy(obuf[slot].at[pl.ds(src, SLAB)], y_hbm.at[pl.ds(dst, SLAB)],
                                  ssem.at[slot]).start()
            return carry
        lax.fori_loop(0, rows, body, 0)
        wait_scatter(slot)

    @pl.when(b == 0)
    def _():
        block(0, False)

    for parity in range(2):
        @pl.when((b >= 1) & (b < nu) & (b % 2 == parity))
        def _(parity=parity):
            @pl.when(b >= 2)
            def _():
                wait_scatter(parity)
            block(parity, True)

        @pl.when((b == nu - 1) & (b % 2 == parity))
        def _(parity=parity):
            tail(parity)


def _experts(h_slab, tok_blocks, sid_blocks, block_e, n_used, wg, wu, wd, layer, n_slot_rows):
    nb, _, rows = tok_blocks.shape
    d, f = wg.shape[-2], wg.shape[-1]

    def clamp(b, nu):
        return jnp.maximum(jnp.minimum(b, nu[0] - 1), 0)

    def smem(shift):
        return pl.BlockSpec((1, 1, rows), lambda b, be, nu: (clamp(b + shift, nu), 0, 0), memory_space=pltpu.SMEM)

    def weight(shape):
        return pl.BlockSpec((None, None) + shape, lambda b, be, nu: (layer, be[clamp(b, nu)], 0, 0))

    grid_spec = pltpu.PrefetchScalarGridSpec(
        num_scalar_prefetch=2,
        grid=(nb,),
        in_specs=[smem(0), smem(1), smem(-1), smem(0),
                  pl.BlockSpec(memory_space=pl.ANY),
                  weight((d, f)), weight((d, f)), weight((f, d))],
        out_specs=pl.BlockSpec(memory_space=pl.ANY),
        scratch_shapes=[pltpu.VMEM((rows * SLAB, LANES), F32)] * 4
        + [pltpu.SemaphoreType.DMA((2,)), pltpu.SemaphoreType.DMA((2,))],
    )
    return pl.pallas_call(
        _experts_kernel,
        grid_spec=grid_spec,
        out_shape=jax.ShapeDtypeStruct(((n_slot_rows + 2 * rows) * SLAB, LANES), F32),
        compiler_params=pltpu.CompilerParams(dimension_semantics=("arbitrary",),
                                             vmem_limit_bytes=VMEM_LIMIT),
        name="experts",
    )(block_e, n_used, tok_blocks, tok_blocks, sid_blocks, sid_blocks, h_slab, wg, wu, wd)


def _combine_kernel(h_ref, gate_ref, *refs):
    y_refs = refs[:TOP_K]
    sg_ref, su_ref, sd_ref, g_ref, b_ref, of_ref, ob_ref = refs[TOP_K:]
    h = h_ref[...]
    hb = h.astype(BF16)
    act = _silu(_dot(hb, sg_ref[...])) * _dot(hb, su_ref[...])
    y = ALPHA * h + _dot(act.astype(BF16), sd_ref[...])
    gates = gate_ref[...]
    tm = h.shape[0]
    for kk in range(TOP_K):
        y = y + gates[:, kk:kk + 1] * _load_slabs(y_refs[kk], (), tm)
    out = _layer_norm(y, g_ref[...], b_ref[...])
    of_ref[...] = out
    ob_ref[...] = out.astype(BF16)


def _combine(h, gates_t, y_slots, sg, su, sd, g, b, tm=COMBINE_TILE):
    t, d = h.shape
    n = t // tm
    f = sg.shape[1]
    vec = pl.BlockSpec((1, d), lambda i: (0, 0))
    row = pl.BlockSpec((tm, d), lambda i: (i, 0))
    planes = [pl.BlockSpec((tm * SLAB, LANES), lambda i, kk=kk: (kk * n + i, 0)) for kk in range(TOP_K)]
    return pl.pallas_call(
        _combine_kernel,
        grid=(n,),
        in_specs=[row, pl.BlockSpec((tm, TOP_K), lambda i: (i, 0))] + planes + [
            pl.BlockSpec((d, f), lambda i: (0, 0)),
            pl.BlockSpec((d, f), lambda i: (0, 0)),
            pl.BlockSpec((f, d), lambda i: (0, 0)),
            vec, vec],
        out_specs=[row, row],
        out_shape=[jax.ShapeDtypeStruct((t, d), F32), jax.ShapeDtypeStruct((t, d), BF16)],
        compiler_params=pltpu.CompilerParams(dimension_semantics=("parallel",),
                                             vmem_limit_bytes=VMEM_LIMIT),
        name="combine_ln2",
    )(h, gates_t, *([y_slots] * TOP_K), sg, su, sd, g.reshape(1, d), b.reshape(1, d))


def _moe(h, h_slab, router_w, router_bias, wg, wu, wd, layer, sg, su, sd, ln_g, ln_b):
    t, d = h.shape
    tk = t * TOP_K
    n_blocks = tk // MOE_ROWS + N_EXPERTS
    n_rows = n_blocks * MOE_ROWS
    eidx, gates, rank, counts = _router(h, router_w, router_bias)
    counts = counts.reshape(N_EXPERTS)
    padded = (counts + MOE_ROWS - 1) // MOE_ROWS * MOE_ROWS
    pend = jnp.cumsum(padded)
    pstart = pend - padded
    experts = jnp.arange(N_EXPERTS, dtype=jnp.int32)[:, None, None]
    dest = jnp.sum(jnp.where(eidx[None] == experts, pstart[:, None, None], 0), axis=0) + rank
    slot_ids = (jnp.arange(TOP_K, dtype=jnp.int32)[:, None] * t + jnp.arange(t, dtype=jnp.int32)[None, :])
    pad_ids = tk + jnp.arange(n_rows, dtype=jnp.int32) % (2 * MOE_ROWS)
    sid = pad_ids.at[dest.reshape(-1)].set(slot_ids.reshape(-1), unique_indices=True)
    tok = jnp.where(sid < tk, sid % t, 0)
    block_start = jnp.arange(n_blocks, dtype=jnp.int32) * MOE_ROWS
    block_e = jnp.minimum(jnp.sum(pend[None, :] <= block_start[:, None], axis=1), N_EXPERTS - 1).astype(jnp.int32)
    n_used = (pend[-1] // MOE_ROWS).astype(jnp.int32).reshape(1)
    y_slots = _experts(h_slab, (tok * SLAB).reshape(n_blocks, 1, MOE_ROWS),
                       (sid * SLAB).reshape(n_blocks, 1, MOE_ROWS), block_e, n_used, wg, wu, wd, layer, tk)
    return _combine(h, gates.T, y_slots, sg, su, sd, ln_g, ln_b, tm=min(COMBINE_TILE, t))


def kernel(x, ln0_gain, ln0_bias, w_in, b_in, hg_lb_logits, sw_sinks, sg_ln_gain, sg_ln_bias, sg_w_s, sg_b_s, mix_gain, w_out, ln1_gain, ln1_bias, router_w, router_bias, exp_w_gate, exp_w_up, exp_w_down, sh_w_gate, sh_w_up, sh_w_down, ln2_gain, ln2_bias):
    b, s, d = x.shape
    t = b * s
    p = jax.nn.softmax(hg_lb_logits.astype(F32), axis=0)
    cs = jnp.cumsum(p, axis=0)
    lbs = cs - cs[0:1]
    a0, a1 = HG_WIDTH, HG_WIDTH + SW_WIDTH

    wg_all, wu_all, wd_all = exp_w_gate.astype(BF16), exp_w_up.astype(BF16), exp_w_down.astype(BF16)
    h, hb = _ln(x.reshape(t, d), ln0_gain, ln0_bias)
    for l in range(DEPTH):
        proj = _in_proj(hb, _regroup_columns(w_in[l]).astype(BF16), _regroup_columns(b_in[l]))
        proj3 = proj.reshape(b, s, D_IN)
        lb = lbs[l]
        o_a = _hgrn(proj3, jnp.log(lb), jnp.log1p(-lb), 1.0 - lb, mix_gain[l, :a0])
        o_b = _swa(proj3, sw_sinks[l], mix_gain[l, a0:a1])
        o_c = _sgu(proj3, sg_ln_gain[l], sg_ln_bias[l], sg_w_s[l], sg_b_s[l], mix_gain[l, a1:])
        h, h_slab = _out_proj(o_a.reshape(t, -1), o_b.reshape(t, -1), o_c.reshape(t, -1),
                              w_out[l].astype(BF16), h, ln1_gain[l], ln1_bias[l])
        h, hb = _moe(h, h_slab, router_w[l], router_bias[l], wg_all, wu_all, wd_all, l,
                     sh_w_gate[l].astype(BF16), sh_w_up[l].astype(BF16), sh_w_down[l].astype(BF16),
                     ln2_gain[l], ln2_bias[l])
    return h.reshape(b, s, d)
```

```python
import functools

import numpy as np
import jax
import jax.numpy as jnp
from jax import lax
from jax.experimental import pallas as pl
from jax.experimental.pallas import tpu as pltpu

F32 = jnp.float32
BF16 = jnp.bfloat16

D_MODEL = 2048
DEPTH = 2
HG_HEADS = 4
HG_DK = 128
HG_WIDTH = HG_HEADS * HG_DK
CHUNK = 128
SW_Q_HEADS = 16
SW_KV_HEADS = 2
SW_HEAD_DIM = 64
SW_WIDTH = SW_Q_HEADS * SW_HEAD_DIM
SW_KV_WIDTH = SW_KV_HEADS * SW_HEAD_DIM
SG_GROUPS = 4
SG_WIDTH = SG_GROUPS * CHUNK
D_IN = 4 * HG_WIDTH + SW_WIDTH + 2 * SW_KV_WIDTH + 2 * SG_WIDTH
N_EXPERTS = 64
TOP_K = 8
D_EXPERT = 512
N_EXPERT_GROUPS = 8
TOPK_GROUPS = 4
ROUTED_SCALE = 2.5
ALPHA = (2 * DEPTH) ** 0.25
LN_EPS = 1e-5
RMS_EPS = 1e-6

_SWQ_OFF = 4 * HG_WIDTH
_SGU_OFF = _SWQ_OFF + SW_WIDTH
_SWK_OFF = _SGU_OFF + 2 * SG_WIDTH


def _regroup_columns(a):
    k0 = _SWQ_OFF + SW_WIDTH
    k1 = k0 + 2 * SW_KV_WIDTH
    return jnp.concatenate([a[..., :k0], a[..., k1:], a[..., k0:k1]], axis=-1)


MOE_ROWS = 256
ROUTER_TILE = 512
COMBINE_TILE = 128
VMEM_LIMIT = 56 * 1024 * 1024

_LEVEL_SIZES = (64, 32, 16, 8, 4, 2, 1)
_N_LEVELS = len(_LEVEL_SIZES)


def _hgrn_constants():
    t = np.arange(CHUNK)[:, None]
    u = np.arange(CHUNK)[None, :]
    mats = []
    level = np.full((CHUNK, CHUNK), -1, np.int32)
    for li, s in enumerate(_LEVEL_SIZES):
        blk = t // s
        odd = (blk % 2) == 1
        m_query = (u >= blk * s) & (u <= t)
        m_key = (u > t) & (u <= (blk + 1) * s - 1)
        mats.append(np.where(odd, m_query, m_key))
        pair = ((t // s) % 2 == 1) & ((u // s) == (t // s) - 1)
        level[pair] = li
    mats.append(u <= t)
    mats.append(u > t)
    level[np.arange(CHUNK), np.arange(CHUNK)] = _N_LEVELS
    return np.concatenate(mats, 0).astype(np.float32), level


_HGRN_MSTACK, _HGRN_LEVEL = _hgrn_constants()


def _layer_norm(x, g, b):
    mu = jnp.mean(x, axis=-1, keepdims=True)
    xc = x - mu
    var = jnp.mean(xc * xc, axis=-1, keepdims=True)
    return xc * lax.rsqrt(var + LN_EPS) * g + b


def _silu(x):
    return x / (1.0 + jnp.exp(-x))


def _gelu(x):
    return 0.5 * x * (1.0 + lax.erf(x * (2.0 ** -0.5)))


def _dot(a, b):
    return jnp.dot(a, b, preferred_element_type=F32)


def _dot_nt(a, b):
    return lax.dot_general(a, b, (((1,), (1,)), ((), ())), preferred_element_type=F32)


def _dot_tn(a, b):
    return lax.dot_general(a, b, (((0,), (0,)), ((), ())), preferred_element_type=F32)


def _ln_kernel(x_ref, g_ref, b_ref, of_ref, ob_ref):
    y = _layer_norm(x_ref[...], g_ref[...], b_ref[...])
    of_ref[...] = y
    ob_ref[...] = y.astype(BF16)


def _ln(x, g, b, tm=256):
    t, d = x.shape
    row = pl.BlockSpec((tm, d), lambda i: (i, 0))
    vec = pl.BlockSpec((1, d), lambda i: (0, 0))
    return pl.pallas_call(
        _ln_kernel,
        grid=(t // tm,),
        in_specs=[row, vec, vec],
        out_specs=[row, row],
        out_shape=[jax.ShapeDtypeStruct((t, d), F32), jax.ShapeDtypeStruct((t, d), BF16)],
        name="ln0",
    )(x, g.reshape(1, d), b.reshape(1, d))


def _mm_bias_kernel(a_ref, w_ref, b_ref, o_ref):
    o_ref[...] = _dot(a_ref[...], w_ref[...]) + b_ref[...]


def _in_proj(a, w, b, tm=2048, tn=256):
    t, k = a.shape
    n = w.shape[1]
    tm = min(tm, t)
    return pl.pallas_call(
        _mm_bias_kernel,
        grid=(t // tm, n // tn),
        in_specs=[pl.BlockSpec((tm, k), lambda i, j: (i, 0)),
                  pl.BlockSpec((k, tn), lambda i, j: (0, j)),
                  pl.BlockSpec((1, tn), lambda i, j: (0, j))],
        out_specs=pl.BlockSpec((tm, tn), lambda i, j: (i, j)),
        out_shape=jax.ShapeDtypeStruct((t, n), F32),
        compiler_params=pltpu.CompilerParams(vmem_limit_bytes=VMEM_LIMIT),
        name="in_proj",
    )(a, w, b.reshape(1, n))


def _hgrn_kernel(q_ref, f_ref, i_ref, g_ref, llb_ref, l1lb_ref, oml_ref, gain_ref,
                 mstack_ref, level_ref, o_ref, state_ref):
    c = pl.program_id(1)

    @pl.when(c == 0)
    def _():
        state_ref[...] = jnp.zeros_like(state_ref)

    q = _silu(q_ref[...])
    z = f_ref[...]
    ez = jnp.exp(-jnp.abs(z))
    log_sig = jnp.minimum(z, 0.0) - jnp.log(1.0 + ez)
    sig_neg = jnp.where(z >= 0.0, ez, 1.0) / (1.0 + ez)
    k = oml_ref[...] * sig_neg
    a = llb_ref[...]
    bb = l1lb_ref[...] + log_sig
    log_f = jnp.maximum(a, bb) + jnp.log(1.0 + jnp.exp(-jnp.abs(a - bb)))

    lf_hi = log_f.astype(BF16)
    lf_lo = (log_f - lf_hi.astype(F32)).astype(BF16)
    m = mstack_ref[...]
    sums = _dot(m, lf_hi) + _dot(m, lf_lo)
    decay = jnp.exp(sums)

    rows = lax.broadcasted_iota(jnp.int32, (CHUNK, 1), 0)
    factors = []
    for li, s in enumerate(_LEVEL_SIZES):
        is_query = ((rows // s) % 2) == 1
        factors.append((jnp.where(is_query, q, k) * decay[li * CHUNK:(li + 1) * CHUNK]).astype(BF16))
    cum = _N_LEVELS * CHUNK
    q_in = (q * decay[cum:cum + CHUNK]).astype(BF16)
    k_dec = (k * decay[cum + CHUNK:cum + 2 * CHUNK]).astype(BF16)
    end_decay = decay[cum + CHUNK - 1:cum + CHUNK]
    qb = q.astype(BF16)
    kb = k.astype(BF16)
    vb = i_ref[...].astype(BF16)
    level = level_ref[...]
    gate = _silu(g_ref[...])
    gain = gain_ref[...]

    for h in range(HG_HEADS):
        hs = slice(h * HG_DK, (h + 1) * HG_DK)
        scores = jnp.zeros((CHUNK, CHUNK), F32)
        for li in range(_N_LEVELS):
            fl = factors[li][:, hs]
            scores = jnp.where(level == li, _dot_nt(fl, fl), scores)
        scores = jnp.where(level == _N_LEVELS, _dot_nt(qb[:, hs], kb[:, hs]), scores)
        st = state_ref[h]
        o = _dot(scores.astype(BF16), vb[:, hs]) + _dot_nt(q_in[:, hs], st.astype(BF16))
        state_ref[h] = st * end_decay[:, hs] + _dot_tn(vb[:, hs], k_dec[:, hs])
        ms = jnp.mean(o * o, axis=-1, keepdims=True)
        o = o * lax.rsqrt(ms + RMS_EPS) * gain[:, hs] * gate[:, hs]
        o_ref[:, hs] = o.astype(o_ref.dtype)


def _hgrn(proj3, llb, l1lb, oml, gain):
    b, s, _ = proj3.shape
    w = HG_WIDTH

    def col(j):
        return pl.BlockSpec((None, CHUNK, w), lambda bi, ci, j=j: (bi, ci, j))

    vec = pl.BlockSpec((1, w), lambda bi, ci: (0, 0))
    nm = _HGRN_MSTACK.shape[0]
    return pl.pallas_call(
        _hgrn_kernel,
        grid=(b, s // CHUNK),
        in_specs=[col(0), col(1), col(2), col(3), vec, vec, vec, vec,
                  pl.BlockSpec((nm, CHUNK), lambda bi, ci: (0, 0)),
                  pl.BlockSpec((CHUNK, CHUNK), lambda bi, ci: (0, 0))],
        out_specs=pl.BlockSpec((None, CHUNK, w), lambda bi, ci: (bi, ci, 0)),
        out_shape=jax.ShapeDtypeStruct((b, s, w), BF16),
        scratch_shapes=[pltpu.VMEM((HG_HEADS, HG_DK, HG_DK), F32)],
        compiler_params=pltpu.CompilerParams(dimension_semantics=("parallel", "arbitrary"),
                                             vmem_limit_bytes=VMEM_LIMIT),
        name="hgrn2",
    )(proj3, proj3, proj3, proj3, llb.reshape(1, w), l1lb.reshape(1, w), oml.reshape(1, w),
      gain.reshape(1, w), jnp.asarray(_HGRN_MSTACK, BF16), jnp.asarray(_HGRN_LEVEL))


def _swa_kernel(q_ref, kp_ref, kc_ref, vp_ref, vc_ref, sink_ref, gain_ref, o_ref):
    n = pl.program_id(1)
    g = SW_Q_HEADS // SW_KV_HEADS
    hd = SW_HEAD_DIM
    q = q_ref[...]
    kband = jnp.concatenate([kp_ref[...], kc_ref[...]], axis=0).astype(BF16)
    vband = jnp.concatenate([vp_ref[...], vc_ref[...]], axis=0).astype(BF16)
    t = lax.broadcasted_iota(jnp.int32, (g * CHUNK, 2 * CHUNK), 0) % CHUNK
    s = lax.broadcasted_iota(jnp.int32, (g * CHUNK, 2 * CHUNK), 1)
    rel = t + CHUNK - s
    mask = (rel >= 0) & (rel < CHUNK) & ((s >= CHUNK) | (n > 0))
    outs = []
    for kv in range(SW_KV_HEADS):
        qs = jnp.concatenate([q[:, (kv * g + j) * hd:(kv * g + j + 1) * hd] for j in range(g)], axis=0)
        sc = _dot_nt(qs.astype(BF16), kband[:, kv * hd:(kv + 1) * hd]) * (hd ** -0.5)
        sc = jnp.where(mask, sc, -jnp.inf)
        sink = sink_ref[kv]
        mx = jnp.maximum(jnp.max(sc, axis=-1, keepdims=True), sink)
        p = jnp.exp(sc - mx)
        denom = jnp.sum(p, axis=-1, keepdims=True) + jnp.exp(sink - mx)
        o = _dot((p / denom).astype(BF16), vband[:, kv * hd:(kv + 1) * hd])
        outs.extend(o[j * CHUNK:(j + 1) * CHUNK] for j in range(g))
    o = jnp.concatenate(outs, axis=-1)
    ms = jnp.mean(o * o, axis=-1, keepdims=True)
    o_ref[...] = (o * lax.rsqrt(ms + RMS_EPS) * gain_ref[...]).astype(o_ref.dtype)


def _swa(proj3, sinks, gain):
    b, s, _ = proj3.shape
    q_col = _SWQ_OFF // SW_WIDTH
    k_col = _SWK_OFF // SW_KV_WIDTH
    v_col = k_col + 1
    g = SW_Q_HEADS // SW_KV_HEADS
    sink_rows = jnp.repeat(sinks.astype(F32).reshape(SW_KV_HEADS, g), CHUNK, axis=1)[..., None]

    def kv_spec(col, prev):
        if prev:
            return pl.BlockSpec((None, CHUNK, SW_KV_WIDTH), lambda bi, ni: (bi, jnp.maximum(ni - 1, 0), col))
        return pl.BlockSpec((None, CHUNK, SW_KV_WIDTH), lambda bi, ni: (bi, ni, col))

    return pl.pallas_call(
        _swa_kernel,
        grid=(b, s // CHUNK),
        in_specs=[pl.BlockSpec((None, CHUNK, SW_WIDTH), lambda bi, ni: (bi, ni, q_col)),
                  kv_spec(k_col, True), kv_spec(k_col, False),
                  kv_spec(v_col, True), kv_spec(v_col, False),
                  pl.BlockSpec((SW_KV_HEADS, g * CHUNK, 1), lambda bi, ni: (0, 0, 0)),
                  pl.BlockSpec((1, SW_WIDTH), lambda bi, ni: (0, 0))],
        out_specs=pl.BlockSpec((None, CHUNK, SW_WIDTH), lambda bi, ni: (bi, ni, 0)),
        out_shape=jax.ShapeDtypeStruct((b, s, SW_WIDTH), BF16),
        compiler_params=pltpu.CompilerParams(dimension_semantics=("parallel", "parallel"),
                                             vmem_limit_bytes=VMEM_LIMIT),
        name="swa",
    )(proj3, proj3, proj3, proj3, proj3, sink_rows, gain.reshape(1, SW_WIDTH))


def _sgu_kernel(u_ref, v_ref, lng_ref, lnb_ref, w_ref, bs_ref, gain_ref, o_ref):
    u = _gelu(u_ref[...])
    v = _layer_norm(_gelu(v_ref[...]), lng_ref[...], lnb_ref[...]).astype(BF16)
    r = lax.broadcasted_iota(jnp.int32, (CHUNK, CHUNK), 0)
    c = lax.broadcasted_iota(jnp.int32, (CHUNK, CHUNK), 1)
    tril = c <= r
    bs = bs_ref[...]
    parts = []
    for gi in range(SG_GROUPS):
        gs = slice(gi * CHUNK, (gi + 1) * CHUNK)
        w = jnp.where(tril, w_ref[gi], 0.0).astype(BF16)
        parts.append(_dot(w, v[:, gs]) + bs[:, gi:gi + 1])
    o = u * jnp.concatenate(parts, axis=-1)
    ms = jnp.mean(o * o, axis=-1, keepdims=True)
    o_ref[...] = (o * lax.rsqrt(ms + RMS_EPS) * gain_ref[...]).astype(o_ref.dtype)


def _sgu(proj3, ln_g, ln_b, w_s, b_s, gain):
    b, s, _ = proj3.shape
    w = SG_WIDTH
    u_col = _SGU_OFF // SG_WIDTH
    vec = pl.BlockSpec((1, w), lambda bi, ci: (0, 0))
    return pl.pallas_call(
        _sgu_kernel,
        grid=(b, s // CHUNK),
        in_specs=[pl.BlockSpec((None, CHUNK, w), lambda bi, ci: (bi, ci, u_col)),
                  pl.BlockSpec((None, CHUNK, w), lambda bi, ci: (bi, ci, u_col + 1)),
                  vec, vec,
                  pl.BlockSpec((SG_GROUPS, CHUNK, CHUNK), lambda bi, ci: (0, 0, 0)),
                  pl.BlockSpec((CHUNK, SG_GROUPS), lambda bi, ci: (0, 0)),
                  vec],
        out_specs=pl.BlockSpec((None, CHUNK, w), lambda bi, ci: (bi, ci, 0)),
        out_shape=jax.ShapeDtypeStruct((b, s, w), BF16),
        compiler_params=pltpu.CompilerParams(dimension_semantics=("parallel", "parallel"),
                                             vmem_limit_bytes=VMEM_LIMIT),
        name="sgu",
    )(proj3, proj3, ln_g.reshape(1, w), ln_b.reshape(1, w), w_s, b_s.T, gain.reshape(1, w))


LANES = 128
SLAB = D_MODEL // 2 // LANES


def _pack_pair(low, high):
    def bits(v):
        return lax.bitcast_convert_type(v.astype(BF16).astype(F32), jnp.uint32)

    return (bits(high) & jnp.uint32(0xFFFF0000)) | (bits(low) >> 16)


def _pack_rows(x):
    half = x.shape[1] // 2
    return _pack_pair(x[:, :half], x[:, half:])


def _unpack_words(w):
    return (lax.bitcast_convert_type(w << 16, F32),
            lax.bitcast_convert_type(w & jnp.uint32(0xFFFF0000), F32))


def _store_slabs(ref, x):
    n = x.shape[0]
    for j in range(SLAB):
        ref[pl.ds(j, n, stride=SLAB), :] = x[:, j * LANES:(j + 1) * LANES]


def _load_packed_rows(ref, n, first=0, stride=SLAB):
    lows, highs = [], []
    for j in range(SLAB):
        lo, hi = _unpack_words(ref[pl.ds(first + j, n, stride=stride), :])
        lows.append(lo.astype(BF16))
        highs.append(hi.astype(BF16))
    return jnp.concatenate(lows + highs, axis=1)


def _out_proj_kernel(oa_ref, ob_ref, oc_ref, w_ref, h_ref, g_ref, b_ref, of_ref, os_ref):
    mixed = jnp.concatenate([oa_ref[...], ob_ref[...], oc_ref[...]], axis=-1)
    y = ALPHA * h_ref[...] + _dot(mixed, w_ref[...])
    out = _layer_norm(y, g_ref[...], b_ref[...])
    of_ref[...] = out
    _store_slabs(os_ref, _pack_rows(out))


def _out_proj(oa, ob, oc, w, h, g, b, tm=256):
    t, d = h.shape
    vec = pl.BlockSpec((1, d), lambda i: (0, 0))
    return pl.pallas_call(
        _out_proj_kernel,
        grid=(t // tm,),
        in_specs=[pl.BlockSpec((tm, oa.shape[1]), lambda i: (i, 0)),
                  pl.BlockSpec((tm, ob.shape[1]), lambda i: (i, 0)),
                  pl.BlockSpec((tm, oc.shape[1]), lambda i: (i, 0)),
                  pl.BlockSpec(w.shape, lambda i: (0, 0)),
                  pl.BlockSpec((tm, d), lambda i: (i, 0)), vec, vec],
        out_specs=[pl.BlockSpec((tm, d), lambda i: (i, 0)),
                   pl.BlockSpec((tm * SLAB, LANES), lambda i: (i, 0))],
        out_shape=[jax.ShapeDtypeStruct((t, d), F32), jax.ShapeDtypeStruct((t * SLAB, LANES), jnp.uint32)],
        compiler_params=pltpu.CompilerParams(vmem_limit_bytes=VMEM_LIMIT),
        name="out_proj_ln1",
    )(oa, ob, oc, w, h, g.reshape(1, d), b.reshape(1, d))


def _first_index_of_max(x, iota, size, axis):
    mx = jnp.max(x, axis=axis, keepdims=True)
    idx = jnp.min(jnp.where(x == mx, iota, size), axis=axis, keepdims=True)
    return mx, idx


def _router_kernel(h_ref, wt_ref, bias_ref, eidx_ref, gate_ref, count_ref, run_ref):
    i = pl.program_id(0)

    @pl.when(i == 0)
    def _():
        run_ref[...] = jnp.zeros_like(run_ref)

    tm = h_ref.shape[0]
    per_group = N_EXPERTS // N_EXPERT_GROUPS
    logits = lax.dot_general(wt_ref[...], h_ref[...], (((1,), (1,)), ((), ())),
                             precision=lax.Precision.HIGHEST, preferred_element_type=F32)
    scores = 1.0 / (1.0 + jnp.exp(-logits))
    sel = scores + bias_ref[...]
    sel3 = sel.reshape(N_EXPERT_GROUPS, per_group, tm)
    io3 = lax.broadcasted_iota(jnp.int32, sel3.shape, 1)
    m1, i1 = _first_index_of_max(sel3, io3, per_group, 1)
    m2 = jnp.max(jnp.where(io3 == i1, -jnp.inf, sel3), axis=1, keepdims=True)
    grp = (m1 + m2).reshape(N_EXPERT_GROUPS, tm)
    iog = lax.broadcasted_iota(jnp.int32, grp.shape, 0)
    keep = jnp.zeros(grp.shape, jnp.bool_)
    for _ in range(TOPK_GROUPS):
        _, gi = _first_index_of_max(grp, iog, N_EXPERT_GROUPS, 0)
        hit = iog == gi
        keep = keep | hit
        grp = jnp.where(hit, -jnp.inf, grp)
    keep3 = jnp.broadcast_to(keep.reshape(N_EXPERT_GROUPS, 1, tm), sel3.shape)
    cand = jnp.where(keep3, sel3, -jnp.inf).reshape(N_EXPERTS, tm)
    ioe = lax.broadcasted_iota(jnp.int32, cand.shape, 0)
    chosen = jnp.zeros(cand.shape, F32)
    idxs, gvals = [], []
    for _ in range(TOP_K):
        _, ei = _first_index_of_max(cand, ioe, N_EXPERTS, 0)
        hit = ioe == ei
        idxs.append(ei)
        gvals.append(jnp.sum(jnp.where(hit, scores, 0.0), axis=0, keepdims=True))
        chosen = jnp.where(hit, 1.0, chosen)
        cand = jnp.where(hit, -jnp.inf, cand)
    gsum = functools.reduce(lambda a, b: a + b, gvals)
    for kk in range(TOP_K):
        eidx_ref[kk:kk + 1, :] = idxs[kk]
        gate_ref[kk:kk + 1, :] = gvals[kk] / gsum * ROUTED_SCALE
    run = run_ref[...] + jnp.sum(chosen, axis=1, keepdims=True)
    run_ref[...] = run
    count_ref[...] = run.astype(jnp.int32)


def _router(h, router_w, router_bias, tm=ROUTER_TILE):
    t, d = h.shape
    tm = min(tm, t)
    slot = pl.BlockSpec((TOP_K, tm), lambda i: (0, i))
    return pl.pallas_call(
        _router_kernel,
        grid=(t // tm,),
        in_specs=[pl.BlockSpec((tm, d), lambda i: (i, 0)),
                  pl.BlockSpec((N_EXPERTS, d), lambda i: (0, 0)),
                  pl.BlockSpec((N_EXPERTS, 1), lambda i: (0, 0))],
        out_specs=[slot, slot, pl.BlockSpec((N_EXPERTS, 1), lambda i: (0, 0))],
        out_shape=[jax.ShapeDtypeStruct((TOP_K, t), jnp.int32),
                   jax.ShapeDtypeStruct((TOP_K, t), F32),
                   jax.ShapeDtypeStruct((N_EXPERTS, 1), jnp.int32)],
        scratch_shapes=[pltpu.VMEM((N_EXPERTS, 1), F32)],
        compiler_params=pltpu.CompilerParams(dimension_semantics=("arbitrary",),
                                             vmem_limit_bytes=VMEM_LIMIT),
        name="router",
    )(h, router_w.T, router_bias.reshape(N_EXPERTS, 1))


_ISSUE_UNROLL = 8
_TOP_K_BITS = TOP_K.bit_length() - 1
assert 1 << _TOP_K_BITS == TOP_K


def _experts_kernel(be_ref, nu_ref, cs_ref, nv_ref, order_ref, h_hbm, wg_ref, wu_ref, wd_ref,
                    y_hbm, xbuf0, xbuf1, obuf0, obuf1, gsem, ssem):
    b = pl.program_id(0)
    nu = nu_ref[0]
    xbuf = (xbuf0, xbuf1)
    obuf = (obuf0, obuf1)
    rows = xbuf0.shape[0] // SLAB
    d = wg_ref.shape[0]
    f = wg_ref.shape[1]
    pad0 = y_hbm.shape[0] - 2 * rows * SLAB

    def gather_row(first, r, dst_slot, priority):
        slot_id = order_ref[first + r]
        src = pl.multiple_of((slot_id >> _TOP_K_BITS) * SLAB, SLAB)
        buf_rows = pl.ds(pl.multiple_of(r * SLAB, SLAB), SLAB)
        pltpu.make_async_copy(h_hbm.at[pl.ds(src, SLAB)], xbuf[dst_slot].at[buf_rows],
                              gsem.at[dst_slot]).start(priority=priority)

    def scatter_row(first, n_real, r, src_slot, priority):
        slot_id = order_ref[first + r]
        dst = jnp.where(r < n_real, slot_id * SLAB, pad0 + (src_slot * rows + r) * SLAB)
        buf_rows = pl.ds(pl.multiple_of(r * SLAB, SLAB), SLAB)
        pltpu.make_async_copy(obuf[src_slot].at[buf_rows], y_hbm.at[pl.ds(pl.multiple_of(dst, SLAB), SLAB)],
                              ssem.at[src_slot]).start(priority=priority)

    def wait_gather(s):
        pltpu.make_async_copy(h_hbm.at[pl.ds(0, rows * SLAB)], xbuf[s], gsem.at[s]).wait()

    def wait_scatter(s):
        pltpu.make_async_copy(obuf[s], y_hbm.at[pl.ds(0, rows * SLAB)], ssem.at[s]).wait()

    @pl.when(b == 0)
    def _():
        for s in range(2):
            obuf[s][...] = jnp.zeros_like(obuf[s])
            pltpu.make_async_copy(obuf[s], y_hbm.at[pl.ds(pad0 + s * rows * SLAB, rows * SLAB)],
                                  ssem.at[s]).start()
        for s in range(2):
            wait_scatter(s)
        first = cs_ref[0]

        def body(r, carry):
            gather_row(first, r, 0, 0)
            return carry
        lax.fori_loop(0, rows, body, 0)

    def block(slot, with_scatter):
        other = 1 - slot
        wait_gather(slot)

        nxt = jnp.minimum(b + 1, nu - 1)
        prv = jnp.maximum(b - 1, 0)
        g_first, s_first, s_real = cs_ref[nxt], cs_ref[prv], nv_ref[prv]

        def issue(i, carry):
            for u in range(_ISSUE_UNROLL):
                r = i * _ISSUE_UNROLL + u
                gather_row(g_first, r, other, 0)
                if with_scatter:
                    scatter_row(s_first, s_real, r, other, 1)
            return carry
        lax.fori_loop(0, rows // _ISSUE_UNROLL, issue, 0)

        x = _load_packed_rows(xbuf[slot], rows)
        acts = []
        fh = f // 2
        for j in range(2):
            gate = _dot(x, wg_ref[:, j * fh:(j + 1) * fh])
            up = _dot(x, wu_ref[:, j * fh:(j + 1) * fh])
            acts.append((_silu(gate) * up).astype(BF16))
        act = jnp.concatenate(acts, axis=1)
        half = d // 2
        cw = 2 * LANES
        for q in range(half // cw):
            words = _pack_pair(_dot(act, wd_ref[:, q * cw:(q + 1) * cw]),
                               _dot(act, wd_ref[:, half + q * cw:half + (q + 1) * cw]))
            for jj in range(cw // LANES):
                obuf[slot][pl.ds(q * (cw // LANES) + jj, rows, stride=SLAB), :] = words[:, jj * LANES:(jj + 1) * LANES]

    def tail(slot):
        other = 1 - slot
        wait_gather(other)

        @pl.when(b >= 1)
        def _():
            wait_scatter(other)
        first, n_real = cs_ref[b], nv_ref[b]

        def body(r, carry):
            scatter_row(first, n_real, r, slot, 0)
            return carry
        lax.fori_loop(0, rows, body, 0)
        wait_scatter(slot)

    @pl.when(b == 0)
    def _():
        block(0, False)

    for parity in range(2):
        @pl.when((b >= 1) & (b < nu) & (b % 2 == parity))
        def _(parity=parity):
            @pl.when(b >= 2)
            def _():
                wait_scatter(parity)
            block(parity, True)

        @pl.when((b == nu - 1) & (b % 2 == parity))
        def _(parity=parity):
            tail(parity)


def _experts(h_slab, order, block_first, block_real, block_e, n_used, wg, wu, wd, layer, rows=MOE_ROWS):
    nb = block_e.shape[0]
    n_slots = order.shape[0] - rows
    d, f = wg.shape[-2], wg.shape[-1]

    def weight(shape):
        def index(b, be, nu, *_):
            return (layer, be[jnp.maximum(jnp.minimum(b, nu[0] - 1), 0)], 0, 0)
        return pl.BlockSpec((None, None) + shape, index)

    grid_spec = pltpu.PrefetchScalarGridSpec(
        num_scalar_prefetch=5,
        grid=(nb,),
        in_specs=[pl.BlockSpec(memory_space=pl.ANY),
                  weight((d, f)), weight((d, f)), weight((f, d))],
        out_specs=pl.BlockSpec(memory_space=pl.ANY),
        scratch_shapes=[pltpu.VMEM((rows * SLAB, LANES), jnp.uint32)] * 4
        + [pltpu.SemaphoreType.DMA((2,)), pltpu.SemaphoreType.DMA((2,))],
    )
    return pl.pallas_call(
        _experts_kernel,
        grid_spec=grid_spec,
        out_shape=jax.ShapeDtypeStruct(((n_slots + 2 * rows) * SLAB, LANES), jnp.uint32),
        compiler_params=pltpu.CompilerParams(dimension_semantics=("arbitrary",),
                                             vmem_limit_bytes=VMEM_LIMIT),
        name="experts",
    )(block_e, n_used, block_first, block_real, order, h_slab, wg, wu, wd)


def _combine_kernel(h_ref, gate_ref, y_ref, sg_ref, su_ref, sd_ref, g_ref, b_ref, of_ref, ob_ref):
    h = h_ref[...]
    tm = h.shape[0]
    hb = h.astype(BF16)
    act = _silu(_dot(hb, sg_ref[...])) * _dot(hb, su_ref[...])
    y = ALPHA * h + _dot(act.astype(BF16), sd_ref[...])
    gates = gate_ref[...]
    lows = [None] * SLAB
    highs = [None] * SLAB
    for kk in range(TOP_K):
        gk = gates[:, kk:kk + 1]
        for j in range(SLAB):
            lo, hi = _unpack_words(y_ref[pl.ds(kk * SLAB + j, tm, stride=TOP_K * SLAB), :])
            lows[j] = gk * lo if kk == 0 else lows[j] + gk * lo
            highs[j] = gk * hi if kk == 0 else highs[j] + gk * hi
    y = y + jnp.concatenate(lows + highs, axis=1)
    out = _layer_norm(y, g_ref[...], b_ref[...])
    of_ref[...] = out
    ob_ref[...] = out.astype(BF16)


def _combine(h, gates_t, y_slots, sg, su, sd, g, b, tm=COMBINE_TILE):
    t, d = h.shape
    n = t // tm
    f = sg.shape[1]
    vec = pl.BlockSpec((1, d), lambda i: (0, 0))
    row = pl.BlockSpec((tm, d), lambda i: (i, 0))
    return pl.pallas_call(
        _combine_kernel,
        grid=(n,),
        in_specs=[row, pl.BlockSpec((tm, TOP_K), lambda i: (i, 0)),
                  pl.BlockSpec((tm * TOP_K * SLAB, LANES), lambda i: (i, 0)),
                  pl.BlockSpec((d, f), lambda i: (0, 0)),
            pl.BlockSpec((d, f), lambda i: (0, 0)),
            pl.BlockSpec((f, d), lambda i: (0, 0)),
            vec, vec],
        out_specs=[row, row],
        out_shape=[jax.ShapeDtypeStruct((t, d), F32), jax.ShapeDtypeStruct((t, d), BF16)],
        compiler_params=pltpu.CompilerParams(dimension_semantics=("parallel",),
                                             vmem_limit_bytes=VMEM_LIMIT),
        name="combine_ln2",
    )(h, gates_t, y_slots, sg, su, sd, g.reshape(1, d), b.reshape(1, d))


def _moe(h, h_slab, router_w, router_bias, wg, wu, wd, layer, sg, su, sd, ln_g, ln_b):
    t, d = h.shape
    tk = t * TOP_K
    n_blocks = tk // MOE_ROWS + N_EXPERTS
    eidx, gates, counts = _router(h, router_w, router_bias)
    counts = counts.reshape(N_EXPERTS)
    slot_ids = jnp.arange(tk, dtype=jnp.int32)
    _, order = lax.sort((eidx.T.reshape(tk), slot_ids), num_keys=1, is_stable=True)
    order = jnp.concatenate([order, jnp.zeros((MOE_ROWS,), jnp.int32)])
    starts = jnp.cumsum(counts) - counts
    padded = (counts + MOE_ROWS - 1) // MOE_ROWS * MOE_ROWS
    pend = jnp.cumsum(padded)
    pstart = pend - padded
    block_start = jnp.arange(n_blocks, dtype=jnp.int32) * MOE_ROWS
    block_e = jnp.minimum(jnp.sum(pend[None, :] <= block_start[:, None], axis=1), N_EXPERTS - 1).astype(jnp.int32)
    owner = block_e[:, None] == jnp.arange(N_EXPERTS, dtype=jnp.int32)[None, :]

    def of_owner(v):
        return jnp.sum(jnp.where(owner, v[None, :], 0), axis=1)

    offset = block_start - of_owner(pstart)
    block_first = jnp.clip(of_owner(starts) + offset, 0, tk).astype(jnp.int32)
    block_real = jnp.clip(of_owner(counts) - offset, 0, MOE_ROWS).astype(jnp.int32)
    n_used = (pend[-1] // MOE_ROWS).astype(jnp.int32).reshape(1)
    y_slots = _experts(h_slab, order, block_first, block_real, block_e, n_used, wg, wu, wd, layer)
    return _combine(h, gates.T, y_slots, sg, su, sd, ln_g, ln_b, tm=min(COMBINE_TILE, t))


def kernel(x, ln0_gain, ln0_bias, w_in, b_in, hg_lb_logits, sw_sinks, sg_ln_gain, sg_ln_bias, sg_w_s, sg_b_s, mix_gain, w_out, ln1_gain, ln1_bias, router_w, router_bias, exp_w_gate, exp_w_up, exp_w_down, sh_w_gate, sh_w_up, sh_w_down, ln2_gain, ln2_bias):
    b, s, d = x.shape
    t = b * s
    p = jax.nn.softmax(hg_lb_logits.astype(F32), axis=0)
    cs = jnp.cumsum(p, axis=0)
    lbs = cs - cs[0:1]
    a0, a1 = HG_WIDTH, HG_WIDTH + SW_WIDTH

    wg_all, wu_all, wd_all = exp_w_gate.astype(BF16), exp_w_up.astype(BF16), exp_w_down.astype(BF16)
    h, hb = _ln(x.reshape(t, d), ln0_gain, ln0_bias)
    for l in range(DEPTH):
        proj = _in_proj(hb, _regroup_columns(w_in[l]).astype(BF16), _regroup_columns(b_in[l]))
        proj3 = proj.reshape(b, s, D_IN)
        lb = lbs[l]
        o_a = _hgrn(proj3, jnp.log(lb), jnp.log1p(-lb), 1.0 - lb, mix_gain[l, :a0])
        o_b = _swa(proj3, sw_sinks[l], mix_gain[l, a0:a1])
        o_c = _sgu(proj3, sg_ln_gain[l], sg_ln_bias[l], sg_w_s[l], sg_b_s[l], mix_gain[l, a1:])
        h, h_slab = _out_proj(o_a.reshape(t, -1), o_b.reshape(t, -1), o_c.reshape(t, -1),
                              w_out[l].astype(BF16), h, ln1_gain[l], ln1_bias[l])
        h, hb = _moe(h, h_slab, router_w[l], router_bias[l], wg_all, wu_all, wd_all, l,
                     sh_w_gate[l].astype(BF16), sh_w_up[l].astype(BF16), sh_w_down[l].astype(BF16),
                     ln2_gain[l], ln2_bias[l])
    return h.reshape(b, s, d)
```

```python
import functools

import numpy as np
import jax
import jax.numpy as jnp
from jax import lax
from jax.experimental import pallas as pl
from jax.experimental.pallas import tpu as pltpu

F32 = jnp.float32
BF16 = jnp.bfloat16

D_MODEL = 2048
DEPTH = 2
HG_HEADS = 4
HG_DK = 128
HG_WIDTH = HG_HEADS * HG_DK
CHUNK = 128
SW_Q_HEADS = 16
SW_KV_HEADS = 2
SW_HEAD_DIM = 64
SW_WIDTH = SW_Q_HEADS * SW_HEAD_DIM
SW_KV_WIDTH = SW_KV_HEADS * SW_HEAD_DIM
SG_GROUPS = 4
SG_WIDTH = SG_GROUPS * CHUNK
D_IN = 4 * HG_WIDTH + SW_WIDTH + 2 * SW_KV_WIDTH + 2 * SG_WIDTH
N_EXPERTS = 64
TOP_K = 8
D_EXPERT = 512
N_EXPERT_GROUPS = 8
TOPK_GROUPS = 4
ROUTED_SCALE = 2.5
ALPHA = (2 * DEPTH) ** 0.25
LN_EPS = 1e-5
RMS_EPS = 1e-6

_SWQ_OFF = 4 * HG_WIDTH
_SGU_OFF = _SWQ_OFF + SW_WIDTH
_SWK_OFF = _SGU_OFF + 2 * SG_WIDTH


def _regroup_columns(a):
    k0 = _SWQ_OFF + SW_WIDTH
    k1 = k0 + 2 * SW_KV_WIDTH
    return jnp.concatenate([a[..., :k0], a[..., k1:], a[..., k0:k1]], axis=-1)


MOE_ROWS = 256
ROUTER_TILE = 512
COMBINE_TILE = 128
VMEM_LIMIT = 56 * 1024 * 1024

_LEVEL_SIZES = (64, 32, 16, 8, 4, 2, 1)
_N_LEVELS = len(_LEVEL_SIZES)


def _hgrn_constants():
    t = np.arange(CHUNK)[:, None]
    u = np.arange(CHUNK)[None, :]
    mats = []
    level = np.full((CHUNK, CHUNK), -1, np.int32)
    for li, s in enumerate(_LEVEL_SIZES):
        blk = t // s
        odd = (blk % 2) == 1
        m_query = (u >= blk * s) & (u <= t)
        m_key = (u > t) & (u <= (blk + 1) * s - 1)
        mats.append(np.where(odd, m_query, m_key))
        pair = ((t // s) % 2 == 1) & ((u // s) == (t // s) - 1)
        level[pair] = li
    mats.append(u <= t)
    mats.append(u > t)
    level[np.arange(CHUNK), np.arange(CHUNK)] = _N_LEVELS
    return np.concatenate(mats, 0).astype(np.float32), level


_HGRN_MSTACK, _HGRN_LEVEL = _hgrn_constants()


def _layer_norm(x, g, b):
    mu = jnp.mean(x, axis=-1, keepdims=True)
    xc = x - mu
    var = jnp.mean(xc * xc, axis=-1, keepdims=True)
    return xc * lax.rsqrt(var + LN_EPS) * g + b


def _silu(x):
    return x / (1.0 + jnp.exp(-x))


def _gelu(x):
    return 0.5 * x * (1.0 + lax.erf(x * (2.0 ** -0.5)))


def _dot(a, b):
    return jnp.dot(a, b, preferred_element_type=F32)


def _dot_nt(a, b):
    return lax.dot_general(a, b, (((1,), (1,)), ((), ())), preferred_element_type=F32)


def _dot_tn(a, b):
    return lax.dot_general(a, b, (((0,), (0,)), ((), ())), preferred_element_type=F32)


def _ln_kernel(x_ref, g_ref, b_ref, of_ref, ob_ref):
    y = _layer_norm(x_ref[...], g_ref[...], b_ref[...])
    of_ref[...] = y
    ob_ref[...] = y.astype(BF16)


def _ln(x, g, b, tm=256):
    t, d = x.shape
    row = pl.BlockSpec((tm, d), lambda i: (i, 0))
    vec = pl.BlockSpec((1, d), lambda i: (0, 0))
    return pl.pallas_call(
        _ln_kernel,
        grid=(t // tm,),
        in_specs=[row, vec, vec],
        out_specs=[row, row],
        out_shape=[jax.ShapeDtypeStruct((t, d), F32), jax.ShapeDtypeStruct((t, d), BF16)],
        name="ln0",
    )(x, g.reshape(1, d), b.reshape(1, d))


def _mm_bias_kernel(a_ref, w_ref, b_ref, o_ref):
    o_ref[...] = _dot(a_ref[...], w_ref[...]) + b_ref[...]


def _in_proj(a, w, b, tm=2048, tn=256):
    t, k = a.shape
    n = w.shape[1]
    tm = min(tm, t)
    return pl.pallas_call(
        _mm_bias_kernel,
        grid=(t // tm, n // tn),
        in_specs=[pl.BlockSpec((tm, k), lambda i, j: (i, 0)),
                  pl.BlockSpec((k, tn), lambda i, j: (0, j)),
                  pl.BlockSpec((1, tn), lambda i, j: (0, j))],
        out_specs=pl.BlockSpec((tm, tn), lambda i, j: (i, j)),
        out_shape=jax.ShapeDtypeStruct((t, n), F32),
        compiler_params=pltpu.CompilerParams(vmem_limit_bytes=VMEM_LIMIT),
        name="in_proj",
    )(a, w, b.reshape(1, n))


def _hgrn_kernel(q_ref, f_ref, i_ref, g_ref, llb_ref, l1lb_ref, oml_ref, gain_ref,
                 mstack_ref, level_ref, o_ref, state_ref):
    c = pl.program_id(1)

    @pl.when(c == 0)
    def _():
        state_ref[...] = jnp.zeros_like(state_ref)

    q = _silu(q_ref[...])
    z = f_ref[...]
    ez = jnp.exp(-jnp.abs(z))
    log_sig = jnp.minimum(z, 0.0) - jnp.log(1.0 + ez)
    sig_neg = jnp.where(z >= 0.0, ez, 1.0) / (1.0 + ez)
    k = oml_ref[...] * sig_neg
    a = llb_ref[...]
    bb = l1lb_ref[...] + log_sig
    log_f = jnp.maximum(a, bb) + jnp.log(1.0 + jnp.exp(-jnp.abs(a - bb)))

    lf_hi = log_f.astype(BF16)
    lf_lo = (log_f - lf_hi.astype(F32)).astype(BF16)
    m = mstack_ref[...]
    sums = _dot(m, lf_hi) + _dot(m, lf_lo)
    decay = jnp.exp(sums)

    rows = lax.broadcasted_iota(jnp.int32, (CHUNK, 1), 0)
    factors = []
    for li, s in enumerate(_LEVEL_SIZES):
        is_query = ((rows // s) % 2) == 1
        factors.append((jnp.where(is_query, q, k) * decay[li * CHUNK:(li + 1) * CHUNK]).astype(BF16))
    cum = _N_LEVELS * CHUNK
    q_in = (q * decay[cum:cum + CHUNK]).astype(BF16)
    k_dec = (k * decay[cum + CHUNK:cum + 2 * CHUNK]).astype(BF16)
    end_decay = decay[cum + CHUNK - 1:cum + CHUNK]
    qb = q.astype(BF16)
    kb = k.astype(BF16)
    vb = i_ref[...].astype(BF16)
    level = level_ref[...]
    gate = _silu(g_ref[...])
    gain = gain_ref[...]

    for h in range(HG_HEADS):
        hs = slice(h * HG_DK, (h + 1) * HG_DK)
        scores = jnp.zeros((CHUNK, CHUNK), F32)
        for li in range(_N_LEVELS):
            fl = factors[li][:, hs]
            scores = jnp.where(level == li, _dot_nt(fl, fl), scores)
        scores = jnp.where(level == _N_LEVELS, _dot_nt(qb[:, hs], kb[:, hs]), scores)
        st = state_ref[h]
        o = _dot(scores.astype(BF16), vb[:, hs]) + _dot_nt(q_in[:, hs], st.astype(BF16))
        state_ref[h] = st * end_decay[:, hs] + _dot_tn(vb[:, hs], k_dec[:, hs])
        ms = jnp.mean(o * o, axis=-1, keepdims=True)
        o = o * lax.rsqrt(ms + RMS_EPS) * gain[:, hs] * gate[:, hs]
        o_ref[:, hs] = o.astype(o_ref.dtype)


def _hgrn(proj3, llb, l1lb, oml, gain):
    b, s, _ = proj3.shape
    w = HG_WIDTH

    def col(j):
        return pl.BlockSpec((None, CHUNK, w), lambda bi, ci, j=j: (bi, ci, j))

    vec = pl.BlockSpec((1, w), lambda bi, ci: (0, 0))
    nm = _HGRN_MSTACK.shape[0]
    return pl.pallas_call(
        _hgrn_kernel,
        grid=(b, s // CHUNK),
        in_specs=[col(0), col(1), col(2), col(3), vec, vec, vec, vec,
                  pl.BlockSpec((nm, CHUNK), lambda bi, ci: (0, 0)),
                  pl.BlockSpec((CHUNK, CHUNK), lambda bi, ci: (0, 0))],
        out_specs=pl.BlockSpec((None, CHUNK, w), lambda bi, ci: (bi, ci, 0)),
        out_shape=jax.ShapeDtypeStruct((b, s, w), BF16),
        scratch_shapes=[pltpu.VMEM((HG_HEADS, HG_DK, HG_DK), F32)],
        compiler_params=pltpu.CompilerParams(dimension_semantics=("parallel", "arbitrary"),
                                             vmem_limit_bytes=VMEM_LIMIT),
        name="hgrn2",
    )(proj3, proj3, proj3, proj3, llb.reshape(1, w), l1lb.reshape(1, w), oml.reshape(1, w),
      gain.reshape(1, w), jnp.asarray(_HGRN_MSTACK, BF16), jnp.asarray(_HGRN_LEVEL))


def _swa_kernel(q_ref, kp_ref, kc_ref, vp_ref, vc_ref, sink_ref, gain_ref, o_ref):
    n = pl.program_id(1)
    g = SW_Q_HEADS // SW_KV_HEADS
    hd = SW_HEAD_DIM
    q = q_ref[...]
    kband = jnp.concatenate([kp_ref[...], kc_ref[...]], axis=0).astype(BF16)
    vband = jnp.concatenate([vp_ref[...], vc_ref[...]], axis=0).astype(BF16)
    t = lax.broadcasted_iota(jnp.int32, (g * CHUNK, 2 * CHUNK), 0) % CHUNK
    s = lax.broadcasted_iota(jnp.int32, (g * CHUNK, 2 * CHUNK), 1)
    rel = t + CHUNK - s
    mask = (rel >= 0) & (rel < CHUNK) & ((s >= CHUNK) | (n > 0))
    outs = []
    for kv in range(SW_KV_HEADS):
        qs = jnp.concatenate([q[:, (kv * g + j) * hd:(kv * g + j + 1) * hd] for j in range(g)], axis=0)
        sc = _dot_nt(qs.astype(BF16), kband[:, kv * hd:(kv + 1) * hd]) * (hd ** -0.5)
        sc = jnp.where(mask, sc, -jnp.inf)
        sink = sink_ref[kv]
        mx = jnp.maximum(jnp.max(sc, axis=-1, keepdims=True), sink)
        p = jnp.exp(sc - mx)
        denom = jnp.sum(p, axis=-1, keepdims=True) + jnp.exp(sink - mx)
        o = _dot((p / denom).astype(BF16), vband[:, kv * hd:(kv + 1) * hd])
        outs.extend(o[j * CHUNK:(j + 1) * CHUNK] for j in range(g))
    o = jnp.concatenate(outs, axis=-1)
    ms = jnp.mean(o * o, axis=-1, keepdims=True)
    o_ref[...] = (o * lax.rsqrt(ms + RMS_EPS) * gain_ref[...]).astype(o_ref.dtype)


def _swa(proj3, sinks, gain):
    b, s, _ = proj3.shape
    q_col = _SWQ_OFF // SW_WIDTH
    k_col = _SWK_OFF // SW_KV_WIDTH
    v_col = k_col + 1
    g = SW_Q_HEADS // SW_KV_HEADS
    sink_rows = jnp.repeat(sinks.astype(F32).reshape(SW_KV_HEADS, g), CHUNK, axis=1)[..., None]

    def kv_spec(col, prev):
        if prev:
            return pl.BlockSpec((None, CHUNK, SW_KV_WIDTH), lambda bi, ni: (bi, jnp.maximum(ni - 1, 0), col))
        return pl.BlockSpec((None, CHUNK, SW_KV_WIDTH), lambda bi, ni: (bi, ni, col))

    return pl.pallas_call(
        _swa_kernel,
        grid=(b, s // CHUNK),
        in_specs=[pl.BlockSpec((None, CHUNK, SW_WIDTH), lambda bi, ni: (bi, ni, q_col)),
                  kv_spec(k_col, True), kv_spec(k_col, False),
                  kv_spec(v_col, True), kv_spec(v_col, False),
                  pl.BlockSpec((SW_KV_HEADS, g * CHUNK, 1), lambda bi, ni: (0, 0, 0)),
                  pl.BlockSpec((1, SW_WIDTH), lambda bi, ni: (0, 0))],
        out_specs=pl.BlockSpec((None, CHUNK, SW_WIDTH), lambda bi, ni: (bi, ni, 0)),
        out_shape=jax.ShapeDtypeStruct((b, s, SW_WIDTH), BF16),
        compiler_params=pltpu.CompilerParams(dimension_semantics=("parallel", "parallel"),
                                             vmem_limit_bytes=VMEM_LIMIT),
        name="swa",
    )(proj3, proj3, proj3, proj3, proj3, sink_rows, gain.reshape(1, SW_WIDTH))


def _sgu_kernel(u_ref, v_ref, lng_ref, lnb_ref, w_ref, bs_ref, gain_ref, o_ref):
    u = _gelu(u_ref[...])
    v = _layer_norm(_gelu(v_ref[...]), lng_ref[...], lnb_ref[...]).astype(BF16)
    r = lax.broadcasted_iota(jnp.int32, (CHUNK, CHUNK), 0)
    c = lax.broadcasted_iota(jnp.int32, (CHUNK, CHUNK), 1)
    tril = c <= r
    bs = bs_ref[...]
    parts = []
    for gi in range(SG_GROUPS):
        gs = slice(gi * CHUNK, (gi + 1) * CHUNK)
        w = jnp.where(tril, w_ref[gi], 0.0).astype(BF16)
        parts.append(_dot(w, v[:, gs]) + bs[:, gi:gi + 1])
    o = u * jnp.concatenate(parts, axis=-1)
    ms = jnp.mean(o * o, axis=-1, keepdims=True)
    o_ref[...] = (o * lax.rsqrt(ms + RMS_EPS) * gain_ref[...]).astype(o_ref.dtype)


def _sgu(proj3, ln_g, ln_b, w_s, b_s, gain):
    b, s, _ = proj3.shape
    w = SG_WIDTH
    u_col = _SGU_OFF // SG_WIDTH
    vec = pl.BlockSpec((1, w), lambda bi, ci: (0, 0))
    return pl.pallas_call(
        _sgu_kernel,
        grid=(b, s // CHUNK),
        in_specs=[pl.BlockSpec((None, CHUNK, w), lambda bi, ci: (bi, ci, u_col)),
                  pl.BlockSpec((None, CHUNK, w), lambda bi, ci: (bi, ci, u_col + 1)),
                  vec, vec,
                  pl.BlockSpec((SG_GROUPS, CHUNK, CHUNK), lambda bi, ci: (0, 0, 0)),
                  pl.BlockSpec((CHUNK, SG_GROUPS), lambda bi, ci: (0, 0)),
                  vec],
        out_specs=pl.BlockSpec((None, CHUNK, w), lambda bi, ci: (bi, ci, 0)),
        out_shape=jax.ShapeDtypeStruct((b, s, w), BF16),
        compiler_params=pltpu.CompilerParams(dimension_semantics=("parallel", "parallel"),
                                             vmem_limit_bytes=VMEM_LIMIT),
        name="sgu",
    )(proj3, proj3, ln_g.reshape(1, w), ln_b.reshape(1, w), w_s, b_s.T, gain.reshape(1, w))


LANES = 128
SLAB = D_MODEL // 2 // LANES


def _pack_pair(low, high):
    def bits(v):
        return lax.bitcast_convert_type(v.astype(BF16).astype(F32), jnp.uint32)

    return (bits(high) & jnp.uint32(0xFFFF0000)) | (bits(low) >> 16)


def _pack_rows(x):
    half = x.shape[1] // 2
    return _pack_pair(x[:, :half], x[:, half:])


def _unpack_words(w):
    return (lax.bitcast_convert_type(w << 16, F32),
            lax.bitcast_convert_type(w & jnp.uint32(0xFFFF0000), F32))


def _store_slabs(ref, x):
    n = x.shape[0]
    for j in range(SLAB):
        ref[pl.ds(j, n, stride=SLAB), :] = x[:, j * LANES:(j + 1) * LANES]


def _load_packed_rows(ref, n, first=0, stride=SLAB):
    lows, highs = [], []
    for j in range(SLAB):
        lo, hi = _unpack_words(ref[pl.ds(first + j, n, stride=stride), :])
        lows.append(lo.astype(BF16))
        highs.append(hi.astype(BF16))
    return jnp.concatenate(lows + highs, axis=1)


def _out_proj_kernel(oa_ref, ob_ref, oc_ref, w_ref, h_ref, g_ref, b_ref, of_ref, os_ref):
    mixed = jnp.concatenate([oa_ref[...], ob_ref[...], oc_ref[...]], axis=-1)
    y = ALPHA * h_ref[...] + _dot(mixed, w_ref[...])
    out = _layer_norm(y, g_ref[...], b_ref[...])
    of_ref[...] = out
    _store_slabs(os_ref, _pack_rows(out))


def _out_proj(oa, ob, oc, w, h, g, b, tm=256):
    t, d = h.shape
    vec = pl.BlockSpec((1, d), lambda i: (0, 0))
    return pl.pallas_call(
        _out_proj_kernel,
        grid=(t // tm,),
        in_specs=[pl.BlockSpec((tm, oa.shape[1]), lambda i: (i, 0)),
                  pl.BlockSpec((tm, ob.shape[1]), lambda i: (i, 0)),
                  pl.BlockSpec((tm, oc.shape[1]), lambda i: (i, 0)),
                  pl.BlockSpec(w.shape, lambda i: (0, 0)),
                  pl.BlockSpec((tm, d), lambda i: (i, 0)), vec, vec],
        out_specs=[pl.BlockSpec((tm, d), lambda i: (i, 0)),
                   pl.BlockSpec((tm * SLAB, LANES), lambda i: (i, 0))],
        out_shape=[jax.ShapeDtypeStruct((t, d), F32), jax.ShapeDtypeStruct((t * SLAB, LANES), jnp.uint32)],
        compiler_params=pltpu.CompilerParams(vmem_limit_bytes=VMEM_LIMIT),
        name="out_proj_ln1",
    )(oa, ob, oc, w, h, g.reshape(1, d), b.reshape(1, d))


def _first_index_of_max(x, iota, size, axis):
    mx = jnp.max(x, axis=axis, keepdims=True)
    idx = jnp.min(jnp.where(x == mx, iota, size), axis=axis, keepdims=True)
    return mx, idx


def _router_kernel(h_ref, wt_ref, bias_ref, eidx_ref, gate_ref, count_ref, run_ref):
    i = pl.program_id(0)

    @pl.when(i == 0)
    def _():
        run_ref[...] = jnp.zeros_like(run_ref)

    tm = h_ref.shape[0]
    per_group = N_EXPERTS // N_EXPERT_GROUPS
    logits = lax.dot_general(wt_ref[...], h_ref[...], (((1,), (1,)), ((), ())),
                             precision=lax.Precision.HIGHEST, preferred_element_type=F32)
    scores = 1.0 / (1.0 + jnp.exp(-logits))
    sel = scores + bias_ref[...]
    sel3 = sel.reshape(N_EXPERT_GROUPS, per_group, tm)
    io3 = lax.broadcasted_iota(jnp.int32, sel3.shape, 1)
    m1, i1 = _first_index_of_max(sel3, io3, per_group, 1)
    m2 = jnp.max(jnp.where(io3 == i1, -jnp.inf, sel3), axis=1, keepdims=True)
    grp = (m1 + m2).reshape(N_EXPERT_GROUPS, tm)
    iog = lax.broadcasted_iota(jnp.int32, grp.shape, 0)
    keep = jnp.zeros(grp.shape, jnp.bool_)
    for _ in range(TOPK_GROUPS):
        _, gi = _first_index_of_max(grp, iog, N_EXPERT_GROUPS, 0)
        hit = iog == gi
        keep = keep | hit
        grp = jnp.where(hit, -jnp.inf, grp)
    keep3 = jnp.broadcast_to(keep.reshape(N_EXPERT_GROUPS, 1, tm), sel3.shape)
    cand = jnp.where(keep3, sel3, -jnp.inf).reshape(N_EXPERTS, tm)
    ioe = lax.broadcasted_iota(jnp.int32, cand.shape, 0)
    chosen = jnp.zeros(cand.shape, F32)
    idxs, gvals = [], []
    for _ in range(TOP_K):
        _, ei = _first_index_of_max(cand, ioe, N_EXPERTS, 0)
        hit = ioe == ei
        idxs.append(ei)
        gvals.append(jnp.sum(jnp.where(hit, scores, 0.0), axis=0, keepdims=True))
        chosen = jnp.where(hit, 1.0, chosen)
        cand = jnp.where(hit, -jnp.inf, cand)
    gsum = functools.reduce(lambda a, b: a + b, gvals)
    for kk in range(TOP_K):
        eidx_ref[kk:kk + 1, :] = idxs[kk]
        gate_ref[kk:kk + 1, :] = gvals[kk] / gsum * ROUTED_SCALE
    run = run_ref[...] + jnp.sum(chosen, axis=1, keepdims=True)
    run_ref[...] = run
    count_ref[...] = run.astype(jnp.int32)


def _router(h, router_w, router_bias, tm=ROUTER_TILE):
    t, d = h.shape
    tm = min(tm, t)
    slot = pl.BlockSpec((TOP_K, tm), lambda i: (0, i))
    return pl.pallas_call(
        _router_kernel,
        grid=(t // tm,),
        in_specs=[pl.BlockSpec((tm, d), lambda i: (i, 0)),
                  pl.BlockSpec((N_EXPERTS, d), lambda i: (0, 0)),
                  pl.BlockSpec((N_EXPERTS, 1), lambda i: (0, 0))],
        out_specs=[slot, slot, pl.BlockSpec((N_EXPERTS, 1), lambda i: (0, 0))],
        out_shape=[jax.ShapeDtypeStruct((TOP_K, t), jnp.int32),
                   jax.ShapeDtypeStruct((TOP_K, t), F32),
                   jax.ShapeDtypeStruct((N_EXPERTS, 1), jnp.int32)],
        scratch_shapes=[pltpu.VMEM((N_EXPERTS, 1), F32)],
        compiler_params=pltpu.CompilerParams(dimension_semantics=("arbitrary",),
                                             vmem_limit_bytes=VMEM_LIMIT),
        name="router",
    )(h, router_w.T, router_bias.reshape(N_EXPERTS, 1))


_ISSUE_UNROLL = 8
_TOP_K_BITS = TOP_K.bit_length() - 1
assert 1 << _TOP_K_BITS == TOP_K


def _experts_kernel(be_ref, nu_ref, srcc_ref, srcn_ref, dstp_ref, dstc_ref, h_hbm, wg_ref, wu_ref, wd_ref,
                    y_hbm, xbuf0, xbuf1, obuf0, obuf1, gsem, ssem):
    b = pl.program_id(0)
    nu = nu_ref[0]
    xbuf = (xbuf0, xbuf1)
    obuf = (obuf0, obuf1)
    rows = xbuf0.shape[0] // SLAB
    d = wg_ref.shape[0]
    f = wg_ref.shape[1]
    pad0 = y_hbm.shape[0] - 2 * rows * SLAB

    def gather_row(src_ref, r, dst_slot, priority):
        src = pl.multiple_of(src_ref[0, 0, r], SLAB)
        buf_rows = pl.ds(pl.multiple_of(r * SLAB, SLAB), SLAB)
        pltpu.make_async_copy(h_hbm.at[pl.ds(src, SLAB)], xbuf[dst_slot].at[buf_rows],
                              gsem.at[dst_slot]).start(priority=priority)

    def scatter_row(dst_ref, r, src_slot, priority):
        dst = pl.multiple_of(dst_ref[0, 0, r], SLAB)
        buf_rows = pl.ds(pl.multiple_of(r * SLAB, SLAB), SLAB)
        pltpu.make_async_copy(obuf[src_slot].at[buf_rows], y_hbm.at[pl.ds(dst, SLAB)],
                              ssem.at[src_slot]).start(priority=priority)

    def wait_gather(s):
        pltpu.make_async_copy(h_hbm.at[pl.ds(0, rows * SLAB)], xbuf[s], gsem.at[s]).wait()

    def wait_scatter(s):
        pltpu.make_async_copy(obuf[s], y_hbm.at[pl.ds(0, rows * SLAB)], ssem.at[s]).wait()

    @pl.when(b == 0)
    def _():
        for s in range(2):
            obuf[s][...] = jnp.zeros_like(obuf[s])
            pltpu.make_async_copy(obuf[s], y_hbm.at[pl.ds(pad0 + s * rows * SLAB, rows * SLAB)],
                                  ssem.at[s]).start()
        for s in range(2):
            wait_scatter(s)

        def body(r, carry):
            gather_row(srcc_ref, r, 0, 0)
            return carry
        lax.fori_loop(0, rows, body, 0)

    def block(slot, with_scatter):
        other = 1 - slot
        wait_gather(slot)

        def issue(i, carry):
            for u in range(_ISSUE_UNROLL):
                r = i * _ISSUE_UNROLL + u
                gather_row(srcn_ref, r, other, 0)
                if with_scatter:
                    scatter_row(dstp_ref, r, other, 1)
            return carry
        lax.fori_loop(0, rows // _ISSUE_UNROLL, issue, 0)

        x = _load_packed_rows(xbuf[slot], rows)
        acts = []
        fh = f // 2
        for j in range(2):
            gate = _dot(x, wg_ref[:, j * fh:(j + 1) * fh])
            up = _dot(x, wu_ref[:, j * fh:(j + 1) * fh])
            acts.append((_silu(gate) * up).astype(BF16))
        act = jnp.concatenate(acts, axis=1)
        half = d // 2
        cw = 2 * LANES
        for q in range(half // cw):
            words = _pack_pair(_dot(act, wd_ref[:, q * cw:(q + 1) * cw]),
                               _dot(act, wd_ref[:, half + q * cw:half + (q + 1) * cw]))
            for jj in range(cw // LANES):
                obuf[slot][pl.ds(q * (cw // LANES) + jj, rows, stride=SLAB), :] = words[:, jj * LANES:(jj + 1) * LANES]

    def tail(slot):
        other = 1 - slot
        wait_gather(other)

        @pl.when(b >= 1)
        def _():
            wait_scatter(other)

        def body(r, carry):
            scatter_row(dstc_ref, r, slot, 0)
            return carry
        lax.fori_loop(0, rows, body, 0)
        wait_scatter(slot)

    @pl.when(b == 0)
    def _():
        block(0, False)

    for parity in range(2):
        @pl.when((b >= 1) & (b < nu) & (b % 2 == parity))
        def _(parity=parity):
            @pl.when(b >= 2)
            def _():
                wait_scatter(parity)
            block(parity, True)

        @pl.when((b == nu - 1) & (b % 2 == parity))
        def _(parity=parity):
            tail(parity)


def _experts(h_slab, row_src, row_dst, block_e, n_used, wg, wu, wd, layer, n_slots):
    nb, _, rows = row_src.shape
    d, f = wg.shape[-2], wg.shape[-1]

    def clamp(b, nu):
        return jnp.maximum(jnp.minimum(b, nu[0] - 1), 0)

    def smem(shift):
        return pl.BlockSpec((1, 1, rows), lambda b, be, nu: (clamp(b + shift, nu), 0, 0), memory_space=pltpu.SMEM)

    def weight(shape):
        return pl.BlockSpec((None, None) + shape, lambda b, be, nu: (layer, be[clamp(b, nu)], 0, 0))

    grid_spec = pltpu.PrefetchScalarGridSpec(
        num_scalar_prefetch=2,
        grid=(nb,),
        in_specs=[smem(0), smem(1), smem(-1), smem(0),
                  pl.BlockSpec(memory_space=pl.ANY),
                  weight((d, f)), weight((d, f)), weight((f, d))],
        out_specs=pl.BlockSpec(memory_space=pl.ANY),
        scratch_shapes=[pltpu.VMEM((rows * SLAB, LANES), jnp.uint32)] * 4
        + [pltpu.SemaphoreType.DMA((2,)), pltpu.SemaphoreType.DMA((2,))],
    )
    return pl.pallas_call(
        _experts_kernel,
        grid_spec=grid_spec,
        out_shape=jax.ShapeDtypeStruct(((n_slots + 2 * rows) * SLAB, LANES), jnp.uint32),
        compiler_params=pltpu.CompilerParams(dimension_semantics=("arbitrary",),
                                             vmem_limit_bytes=VMEM_LIMIT),
        name="experts",
    )(block_e, n_used, row_src, row_src, row_dst, row_dst, h_slab, wg, wu, wd)


_COMBINE_SUB = 32


def _combine_kernel(h_ref, gate_ref, *refs):
    y_refs = refs[:TOP_K]
    sg_ref, su_ref, sd_ref, g_ref, b_ref, of_ref, ob_ref, moe_ref = refs[TOP_K:]
    h = h_ref[...]
    tm = h.shape[0]
    half = SLAB * LANES
    for s in range(tm // _COMBINE_SUB):
        rs = slice(s * _COMBINE_SUB, (s + 1) * _COMBINE_SUB)
        gates = gate_ref[rs, :]
        first = s * _COMBINE_SUB * SLAB
        for j in range(SLAB):
            lo_acc = hi_acc = None
            for kk in range(TOP_K):
                lo, hi = _unpack_words(y_refs[kk][pl.ds(first + j, _COMBINE_SUB, stride=SLAB), :])
                gk = gates[:, kk:kk + 1]
                lo_acc = gk * lo if kk == 0 else lo_acc + gk * lo
                hi_acc = gk * hi if kk == 0 else hi_acc + gk * hi
            moe_ref[rs, j * LANES:(j + 1) * LANES] = lo_acc
            moe_ref[rs, half + j * LANES:half + (j + 1) * LANES] = hi_acc
    hb = h.astype(BF16)
    act = _silu(_dot(hb, sg_ref[...])) * _dot(hb, su_ref[...])
    y = ALPHA * h + _dot(act.astype(BF16), sd_ref[...]) + moe_ref[...]
    out = _layer_norm(y, g_ref[...], b_ref[...])
    of_ref[...] = out
    ob_ref[...] = out.astype(BF16)


def _combine(h, gates_t, y_slots, sg, su, sd, g, b, tm=COMBINE_TILE):
    t, d = h.shape
    n = t // tm
    f = sg.shape[1]
    vec = pl.BlockSpec((1, d), lambda i: (0, 0))
    row = pl.BlockSpec((tm, d), lambda i: (i, 0))
    return pl.pallas_call(
        _combine_kernel,
        grid=(n,),
        in_specs=[row, pl.BlockSpec((tm, TOP_K), lambda i: (i, 0))]
        + [pl.BlockSpec((tm * SLAB, LANES), lambda i, kk=kk: (kk * n + i, 0)) for kk in range(TOP_K)]
        + [pl.BlockSpec((d, f), lambda i: (0, 0)),
            pl.BlockSpec((d, f), lambda i: (0, 0)),
            pl.BlockSpec((f, d), lambda i: (0, 0)),
            vec, vec],
        out_specs=[row, row],
        out_shape=[jax.ShapeDtypeStruct((t, d), F32), jax.ShapeDtypeStruct((t, d), BF16)],
        scratch_shapes=[pltpu.VMEM((tm, d), F32)],
        compiler_params=pltpu.CompilerParams(dimension_semantics=("parallel",),
                                             vmem_limit_bytes=VMEM_LIMIT),
        name="combine_ln2",
    )(h, gates_t, *([y_slots] * TOP_K), sg, su, sd, g.reshape(1, d), b.reshape(1, d))


def _moe(h, h_slab, router_w, router_bias, wg, wu, wd, layer, sg, su, sd, ln_g, ln_b):
    t, d = h.shape
    tk = t * TOP_K
    n_blocks = tk // MOE_ROWS + N_EXPERTS
    eidx, gates, counts = _router(h, router_w, router_bias)
    counts = counts.reshape(N_EXPERTS)
    n_rows = n_blocks * MOE_ROWS
    padded = (counts + MOE_ROWS - 1) // MOE_ROWS * MOE_ROWS
    pend = jnp.cumsum(padded)
    pad_end = jnp.cumsum(padded - counts)
    pad_key = jnp.sum(pad_end[None, :] <= jnp.arange(n_rows - tk, dtype=jnp.int32)[:, None], axis=1)
    keys = jnp.concatenate([eidx.T.reshape(tk), pad_key.astype(jnp.int32)])
    _, row_slot = lax.sort((keys, jnp.arange(n_rows, dtype=jnp.int32)), num_keys=1, is_stable=True)
    real = row_slot < tk
    row = jnp.arange(n_rows, dtype=jnp.int32)
    row_src = jnp.where(real, (row_slot >> _TOP_K_BITS) * SLAB, 0)
    plane_row = (row_slot & (TOP_K - 1)) * t + (row_slot >> _TOP_K_BITS)
    row_dst = jnp.where(real, plane_row, tk + row % (2 * MOE_ROWS)) * SLAB
    block_start = jnp.arange(n_blocks, dtype=jnp.int32) * MOE_ROWS
    block_e = jnp.minimum(jnp.sum(pend[None, :] <= block_start[:, None], axis=1), N_EXPERTS - 1).astype(jnp.int32)
    n_used = (pend[-1] // MOE_ROWS).astype(jnp.int32).reshape(1)
    y_slots = _experts(h_slab, row_src.reshape(n_blocks, 1, MOE_ROWS), row_dst.reshape(n_blocks, 1, MOE_ROWS),
                       block_e, n_used, wg, wu, wd, layer, tk)
    return _combine(h, gates.T, y_slots, sg, su, sd, ln_g, ln_b, tm=min(COMBINE_TILE, t))


def kernel(x, ln0_gain, ln0_bias, w_in, b_in, hg_lb_logits, sw_sinks, sg_ln_gain, sg_ln_bias, sg_w_s, sg_b_s, mix_gain, w_out, ln1_gain, ln1_bias, router_w, router_bias, exp_w_gate, exp_w_up, exp_w_down, sh_w_gate, sh_w_up, sh_w_down, ln2_gain, ln2_bias):
    b, s, d = x.shape
    t = b * s
    p = jax.nn.softmax(hg_lb_logits.astype(F32), axis=0)
    cs = jnp.cumsum(p, axis=0)
    lbs = cs - cs[0:1]
    a0, a1 = HG_WIDTH, HG_WIDTH + SW_WIDTH

    wg_all, wu_all, wd_all = exp_w_gate.astype(BF16), exp_w_up.astype(BF16), exp_w_down.astype(BF16)
    h, hb = _ln(x.reshape(t, d), ln0_gain, ln0_bias)
    for l in range(DEPTH):
        proj = _in_proj(hb, _regroup_columns(w_in[l]).astype(BF16), _regroup_columns(b_in[l]))
        proj3 = proj.reshape(b, s, D_IN)
        lb = lbs[l]
        o_a = _hgrn(proj3, jnp.log(lb), jnp.log1p(-lb), 1.0 - lb, mix_gain[l, :a0])
        o_b = _swa(proj3, sw_sinks[l], mix_gain[l, a0:a1])
        o_c = _sgu(proj3, sg_ln_gain[l], sg_ln_bias[l], sg_w_s[l], sg_b_s[l], mix_gain[l, a1:])
        h, h_slab = _out_proj(o_a.reshape(t, -1), o_b.reshape(t, -1), o_c.reshape(t, -1),
                              w_out[l].astype(BF16), h, ln1_gain[l], ln1_bias[l])
        h, hb = _moe(h, h_slab, router_w[l], router_bias[l], wg_all, wu_all, wd_all, l,
                     sh_w_gate[l].astype(BF16), sh_w_up[l].astype(BF16), sh_w_down[l].astype(BF16),
                     ln2_gain[l], ln2_bias[l])
    return h.reshape(b, s, d)
```

```python
import functools

import numpy as np
import jax
import jax.numpy as jnp
from jax import lax
from jax.experimental import pallas as pl
from jax.experimental.pallas import tpu as pltpu

F32 = jnp.float32
BF16 = jnp.bfloat16

D_MODEL = 2048
DEPTH = 2
HG_HEADS = 4
HG_DK = 128
HG_WIDTH = HG_HEADS * HG_DK
CHUNK = 128
SW_Q_HEADS = 16
SW_KV_HEADS = 2
SW_HEAD_DIM = 64
SW_WIDTH = SW_Q_HEADS * SW_HEAD_DIM
SW_KV_WIDTH = SW_KV_HEADS * SW_HEAD_DIM
SG_GROUPS = 4
SG_WIDTH = SG_GROUPS * CHUNK
D_IN = 4 * HG_WIDTH + SW_WIDTH + 2 * SW_KV_WIDTH + 2 * SG_WIDTH
N_EXPERTS = 64
TOP_K = 8
D_EXPERT = 512
N_EXPERT_GROUPS = 8
TOPK_GROUPS = 4
ROUTED_SCALE = 2.5
ALPHA = (2 * DEPTH) ** 0.25
LN_EPS = 1e-5
RMS_EPS = 1e-6

_SWQ_OFF = 4 * HG_WIDTH
_SGU_OFF = _SWQ_OFF + SW_WIDTH
_SWK_OFF = _SGU_OFF + 2 * SG_WIDTH


def _regroup_columns(a):
    k0 = _SWQ_OFF + SW_WIDTH
    k1 = k0 + 2 * SW_KV_WIDTH
    return jnp.concatenate([a[..., :k0], a[..., k1:], a[..., k0:k1]], axis=-1)


MOE_ROWS = 256
ROUTER_TILE = 512
COMBINE_TILE = 128
VMEM_LIMIT = 56 * 1024 * 1024

_LEVEL_SIZES = (64, 32, 16, 8, 4, 2, 1)
_N_LEVELS = len(_LEVEL_SIZES)


def _hgrn_constants():
    t = np.arange(CHUNK)[:, None]
    u = np.arange(CHUNK)[None, :]
    mats = []
    level = np.full((CHUNK, CHUNK), -1, np.int32)
    for li, s in enumerate(_LEVEL_SIZES):
        blk = t // s
        odd = (blk % 2) == 1
        m_query = (u >= blk * s) & (u <= t)
        m_key = (u > t) & (u <= (blk + 1) * s - 1)
        mats.append(np.where(odd, m_query, m_key))
        pair = ((t // s) % 2 == 1) & ((u // s) == (t // s) - 1)
        level[pair] = li
    mats.append(u <= t)
    mats.append(u > t)
    level[np.arange(CHUNK), np.arange(CHUNK)] = _N_LEVELS
    return np.concatenate(mats, 0).astype(np.float32), level


_HGRN_MSTACK, _HGRN_LEVEL = _hgrn_constants()


def _layer_norm(x, g, b):
    mu = jnp.mean(x, axis=-1, keepdims=True)
    xc = x - mu
    var = jnp.mean(xc * xc, axis=-1, keepdims=True)
    return xc * lax.rsqrt(var + LN_EPS) * g + b


def _silu(x):
    return x / (1.0 + jnp.exp(-x))


def _gelu(x):
    return 0.5 * x * (1.0 + lax.erf(x * (2.0 ** -0.5)))


def _dot(a, b):
    return jnp.dot(a, b, preferred_element_type=F32)


def _dot_nt(a, b):
    return lax.dot_general(a, b, (((1,), (1,)), ((), ())), preferred_element_type=F32)


def _dot_tn(a, b):
    return lax.dot_general(a, b, (((0,), (0,)), ((), ())), preferred_element_type=F32)


def _ln_kernel(x_ref, g_ref, b_ref, of_ref, ob_ref):
    y = _layer_norm(x_ref[...], g_ref[...], b_ref[...])
    of_ref[...] = y
    ob_ref[...] = y.astype(BF16)


def _ln(x, g, b, tm=256):
    t, d = x.shape
    row = pl.BlockSpec((tm, d), lambda i: (i, 0))
    vec = pl.BlockSpec((1, d), lambda i: (0, 0))
    return pl.pallas_call(
        _ln_kernel,
        grid=(t // tm,),
        in_specs=[row, vec, vec],
        out_specs=[row, row],
        out_shape=[jax.ShapeDtypeStruct((t, d), F32), jax.ShapeDtypeStruct((t, d), BF16)],
        name="ln0",
    )(x, g.reshape(1, d), b.reshape(1, d))


def _mm_bias_kernel(a_ref, w_ref, b_ref, o_ref):
    o_ref[...] = _dot(a_ref[...], w_ref[...]) + b_ref[...]


def _in_proj(a, w, b, tm=2048, tn=256):
    t, k = a.shape
    n = w.shape[1]
    tm = min(tm, t)
    return pl.pallas_call(
        _mm_bias_kernel,
        grid=(t // tm, n // tn),
        in_specs=[pl.BlockSpec((tm, k), lambda i, j: (i, 0)),
                  pl.BlockSpec((k, tn), lambda i, j: (0, j)),
                  pl.BlockSpec((1, tn), lambda i, j: (0, j))],
        out_specs=pl.BlockSpec((tm, tn), lambda i, j: (i, j)),
        out_shape=jax.ShapeDtypeStruct((t, n), F32),
        compiler_params=pltpu.CompilerParams(vmem_limit_bytes=VMEM_LIMIT),
        name="in_proj",
    )(a, w, b.reshape(1, n))


def _hgrn_kernel(q_ref, f_ref, i_ref, g_ref, llb_ref, l1lb_ref, oml_ref, gain_ref,
                 mstack_ref, level_ref, o_ref, state_ref):
    c = pl.program_id(1)

    @pl.when(c == 0)
    def _():
        state_ref[...] = jnp.zeros_like(state_ref)

    q = _silu(q_ref[...])
    z = f_ref[...]
    ez = jnp.exp(-jnp.abs(z))
    log_sig = jnp.minimum(z, 0.0) - jnp.log(1.0 + ez)
    sig_neg = jnp.where(z >= 0.0, ez, 1.0) / (1.0 + ez)
    k = oml_ref[...] * sig_neg
    a = llb_ref[...]
    bb = l1lb_ref[...] + log_sig
    log_f = jnp.maximum(a, bb) + jnp.log(1.0 + jnp.exp(-jnp.abs(a - bb)))

    lf_hi = log_f.astype(BF16)
    lf_lo = (log_f - lf_hi.astype(F32)).astype(BF16)
    m = mstack_ref[...]
    sums = _dot(m, lf_hi) + _dot(m, lf_lo)
    decay = jnp.exp(sums)

    rows = lax.broadcasted_iota(jnp.int32, (CHUNK, 1), 0)
    factors = []
    for li, s in enumerate(_LEVEL_SIZES):
        is_query = ((rows // s) % 2) == 1
        factors.append((jnp.where(is_query, q, k) * decay[li * CHUNK:(li + 1) * CHUNK]).astype(BF16))
    cum = _N_LEVELS * CHUNK
    q_in = (q * decay[cum:cum + CHUNK]).astype(BF16)
    k_dec = (k * decay[cum + CHUNK:cum + 2 * CHUNK]).astype(BF16)
    end_decay = decay[cum + CHUNK - 1:cum + CHUNK]
    qb = q.astype(BF16)
    kb = k.astype(BF16)
    vb = i_ref[...].astype(BF16)
    level = level_ref[...]
    gate = _silu(g_ref[...])
    gain = gain_ref[...]

    for h in range(HG_HEADS):
        hs = slice(h * HG_DK, (h + 1) * HG_DK)
        scores = jnp.zeros((CHUNK, CHUNK), F32)
        for li in range(_N_LEVELS):
            fl = factors[li][:, hs]
            scores = jnp.where(level == li, _dot_nt(fl, fl), scores)
        scores = jnp.where(level == _N_LEVELS, _dot_nt(qb[:, hs], kb[:, hs]), scores)
        st = state_ref[h]
        o = _dot(scores.astype(BF16), vb[:, hs]) + _dot_nt(q_in[:, hs], st.astype(BF16))
        state_ref[h] = st * end_decay[:, hs] + _dot_tn(vb[:, hs], k_dec[:, hs])
        ms = jnp.mean(o * o, axis=-1, keepdims=True)
        o = o * lax.rsqrt(ms + RMS_EPS) * gain[:, hs] * gate[:, hs]
        o_ref[:, hs] = o.astype(o_ref.dtype)


def _hgrn(proj3, llb, l1lb, oml, gain):
    b, s, _ = proj3.shape
    w = HG_WIDTH

    def col(j):
        return pl.BlockSpec((None, CHUNK, w), lambda bi, ci, j=j: (bi, ci, j))

    vec = pl.BlockSpec((1, w), lambda bi, ci: (0, 0))
    nm = _HGRN_MSTACK.shape[0]
    return pl.pallas_call(
        _hgrn_kernel,
        grid=(b, s // CHUNK),
        in_specs=[col(0), col(1), col(2), col(3), vec, vec, vec, vec,
                  pl.BlockSpec((nm, CHUNK), lambda bi, ci: (0, 0)),
                  pl.BlockSpec((CHUNK, CHUNK), lambda bi, ci: (0, 0))],
        out_specs=pl.BlockSpec((None, CHUNK, w), lambda bi, ci: (bi, ci, 0)),
        out_shape=jax.ShapeDtypeStruct((b, s, w), BF16),
        scratch_shapes=[pltpu.VMEM((HG_HEADS, HG_DK, HG_DK), F32)],
        compiler_params=pltpu.CompilerParams(dimension_semantics=("parallel", "arbitrary"),
                                             vmem_limit_bytes=VMEM_LIMIT),
        name="hgrn2",
    )(proj3, proj3, proj3, proj3, llb.reshape(1, w), l1lb.reshape(1, w), oml.reshape(1, w),
      gain.reshape(1, w), jnp.asarray(_HGRN_MSTACK, BF16), jnp.asarray(_HGRN_LEVEL))


def _swa_kernel(q_ref, kp_ref, kc_ref, vp_ref, vc_ref, sink_ref, gain_ref, o_ref):
    n = pl.program_id(1)
    g = SW_Q_HEADS // SW_KV_HEADS
    hd = SW_HEAD_DIM
    q = q_ref[...]
    kband = jnp.concatenate([kp_ref[...], kc_ref[...]], axis=0).astype(BF16)
    vband = jnp.concatenate([vp_ref[...], vc_ref[...]], axis=0).astype(BF16)
    t = lax.broadcasted_iota(jnp.int32, (g * CHUNK, 2 * CHUNK), 0) % CHUNK
    s = lax.broadcasted_iota(jnp.int32, (g * CHUNK, 2 * CHUNK), 1)
    rel = t + CHUNK - s
    mask = (rel >= 0) & (rel < CHUNK) & ((s >= CHUNK) | (n > 0))
    outs = []
    for kv in range(SW_KV_HEADS):
        qs = jnp.concatenate([q[:, (kv * g + j) * hd:(kv * g + j + 1) * hd] for j in range(g)], axis=0)
        sc = _dot_nt(qs.astype(BF16), kband[:, kv * hd:(kv + 1) * hd]) * (hd ** -0.5)
        sc = jnp.where(mask, sc, -jnp.inf)
        sink = sink_ref[kv]
        mx = jnp.maximum(jnp.max(sc, axis=-1, keepdims=True), sink)
        p = jnp.exp(sc - mx)
        denom = jnp.sum(p, axis=-1, keepdims=True) + jnp.exp(sink - mx)
        o = _dot((p / denom).astype(BF16), vband[:, kv * hd:(kv + 1) * hd])
        outs.extend(o[j * CHUNK:(j + 1) * CHUNK] for j in range(g))
    o = jnp.concatenate(outs, axis=-1)
    ms = jnp.mean(o * o, axis=-1, keepdims=True)
    o_ref[...] = (o * lax.rsqrt(ms + RMS_EPS) * gain_ref[...]).astype(o_ref.dtype)


def _swa(proj3, sinks, gain):
    b, s, _ = proj3.shape
    q_col = _SWQ_OFF // SW_WIDTH
    k_col = _SWK_OFF // SW_KV_WIDTH
    v_col = k_col + 1
    g = SW_Q_HEADS // SW_KV_HEADS
    sink_rows = jnp.repeat(sinks.astype(F32).reshape(SW_KV_HEADS, g), CHUNK, axis=1)[..., None]

    def kv_spec(col, prev):
        if prev:
            return pl.BlockSpec((None, CHUNK, SW_KV_WIDTH), lambda bi, ni: (bi, jnp.maximum(ni - 1, 0), col))
        return pl.BlockSpec((None, CHUNK, SW_KV_WIDTH), lambda bi, ni: (bi, ni, col))

    return pl.pallas_call(
        _swa_kernel,
        grid=(b, s // CHUNK),
        in_specs=[pl.BlockSpec((None, CHUNK, SW_WIDTH), lambda bi, ni: (bi, ni, q_col)),
                  kv_spec(k_col, True), kv_spec(k_col, False),
                  kv_spec(v_col, True), kv_spec(v_col, False),
                  pl.BlockSpec((SW_KV_HEADS, g * CHUNK, 1), lambda bi, ni: (0, 0, 0)),
                  pl.BlockSpec((1, SW_WIDTH), lambda bi, ni: (0, 0))],
        out_specs=pl.BlockSpec((None, CHUNK, SW_WIDTH), lambda bi, ni: (bi, ni, 0)),
        out_shape=jax.ShapeDtypeStruct((b, s, SW_WIDTH), BF16),
        compiler_params=pltpu.CompilerParams(dimension_semantics=("parallel", "parallel"),
                                             vmem_limit_bytes=VMEM_LIMIT),
        name="swa",
    )(proj3, proj3, proj3, proj3, proj3, sink_rows, gain.reshape(1, SW_WIDTH))


def _sgu_kernel(u_ref, v_ref, lng_ref, lnb_ref, w_ref, bs_ref, gain_ref, o_ref):
    u = _gelu(u_ref[...])
    v = _layer_norm(_gelu(v_ref[...]), lng_ref[...], lnb_ref[...]).astype(BF16)
    r = lax.broadcasted_iota(jnp.int32, (CHUNK, CHUNK), 0)
    c = lax.broadcasted_iota(jnp.int32, (CHUNK, CHUNK), 1)
    tril = c <= r
    bs = bs_ref[...]
    parts = []
    for gi in range(SG_GROUPS):
        gs = slice(gi * CHUNK, (gi + 1) * CHUNK)
        w = jnp.where(tril, w_ref[gi], 0.0).astype(BF16)
        parts.append(_dot(w, v[:, gs]) + bs[:, gi:gi + 1])
    o = u * jnp.concatenate(parts, axis=-1)
    ms = jnp.mean(o * o, axis=-1, keepdims=True)
    o_ref[...] = (o * lax.rsqrt(ms + RMS_EPS) * gain_ref[...]).astype(o_ref.dtype)


def _sgu(proj3, ln_g, ln_b, w_s, b_s, gain):
    b, s, _ = proj3.shape
    w = SG_WIDTH
    u_col = _SGU_OFF // SG_WIDTH
    vec = pl.BlockSpec((1, w), lambda bi, ci: (0, 0))
    return pl.pallas_call(
        _sgu_kernel,
        grid=(b, s // CHUNK),
        in_specs=[pl.BlockSpec((None, CHUNK, w), lambda bi, ci: (bi, ci, u_col)),
                  pl.BlockSpec((None, CHUNK, w), lambda bi, ci: (bi, ci, u_col + 1)),
                  vec, vec,
                  pl.BlockSpec((SG_GROUPS, CHUNK, CHUNK), lambda bi, ci: (0, 0, 0)),
                  pl.BlockSpec((CHUNK, SG_GROUPS), lambda bi, ci: (0, 0)),
                  vec],
        out_specs=pl.BlockSpec((None, CHUNK, w), lambda bi, ci: (bi, ci, 0)),
        out_shape=jax.ShapeDtypeStruct((b, s, w), BF16),
        compiler_params=pltpu.CompilerParams(dimension_semantics=("parallel", "parallel"),
                                             vmem_limit_bytes=VMEM_LIMIT),
        name="sgu",
    )(proj3, proj3, ln_g.reshape(1, w), ln_b.reshape(1, w), w_s, b_s.T, gain.reshape(1, w))


LANES = 128
SLAB = D_MODEL // 2 // LANES


def _pack_pair(low, high):
    def bits(v):
        return lax.bitcast_convert_type(v.astype(BF16).astype(F32), jnp.uint32)

    return (bits(high) & jnp.uint32(0xFFFF0000)) | (bits(low) >> 16)


def _pack_rows(x):
    half = x.shape[1] // 2
    return _pack_pair(x[:, :half], x[:, half:])


def _unpack_words(w):
    return (lax.bitcast_convert_type(w << 16, F32),
            lax.bitcast_convert_type(w & jnp.uint32(0xFFFF0000), F32))


def _store_slabs(ref, x):
    n = x.shape[0]
    for j in range(SLAB):
        ref[pl.ds(j, n, stride=SLAB), :] = x[:, j * LANES:(j + 1) * LANES]


def _load_packed_rows(ref, n, first=0, stride=SLAB):
    lows, highs = [], []
    for j in range(SLAB):
        lo, hi = _unpack_words(ref[pl.ds(first + j, n, stride=stride), :])
        lows.append(lo.astype(BF16))
        highs.append(hi.astype(BF16))
    return jnp.concatenate(lows + highs, axis=1)


def _out_proj_kernel(oa_ref, ob_ref, oc_ref, w_ref, h_ref, g_ref, b_ref, of_ref, os_ref):
    mixed = jnp.concatenate([oa_ref[...], ob_ref[...], oc_ref[...]], axis=-1)
    y = ALPHA * h_ref[...] + _dot(mixed, w_ref[...])
    out = _layer_norm(y, g_ref[...], b_ref[...])
    of_ref[...] = out
    _store_slabs(os_ref, _pack_rows(out))


def _out_proj(oa, ob, oc, w, h, g, b, tm=256):
    t, d = h.shape
    vec = pl.BlockSpec((1, d), lambda i: (0, 0))
    return pl.pallas_call(
        _out_proj_kernel,
        grid=(t // tm,),
        in_specs=[pl.BlockSpec((tm, oa.shape[1]), lambda i: (i, 0)),
                  pl.BlockSpec((tm, ob.shape[1]), lambda i: (i, 0)),
                  pl.BlockSpec((tm, oc.shape[1]), lambda i: (i, 0)),
                  pl.BlockSpec(w.shape, lambda i: (0, 0)),
                  pl.BlockSpec((tm, d), lambda i: (i, 0)), vec, vec],
        out_specs=[pl.BlockSpec((tm, d), lambda i: (i, 0)),
                   pl.BlockSpec((tm * SLAB, LANES), lambda i: (i, 0))],
        out_shape=[jax.ShapeDtypeStruct((t, d), F32), jax.ShapeDtypeStruct((t * SLAB, LANES), jnp.uint32)],
        compiler_params=pltpu.CompilerParams(vmem_limit_bytes=VMEM_LIMIT),
        name="out_proj_ln1",
    )(oa, ob, oc, w, h, g.reshape(1, d), b.reshape(1, d))


def _first_index_of_max(x, iota, size, axis):
    mx = jnp.max(x, axis=axis, keepdims=True)
    idx = jnp.min(jnp.where(x == mx, iota, size), axis=axis, keepdims=True)
    return mx, idx


def _router_kernel(h_ref, wt_ref, bias_ref, eidx_ref, gate_ref, count_ref, run_ref):
    i = pl.program_id(0)

    @pl.when(i == 0)
    def _():
        run_ref[...] = jnp.zeros_like(run_ref)

    tm = h_ref.shape[0]
    per_group = N_EXPERTS // N_EXPERT_GROUPS
    logits = lax.dot_general(wt_ref[...], h_ref[...], (((1,), (1,)), ((), ())),
                             precision=lax.Precision.HIGHEST, preferred_element_type=F32)
    scores = 1.0 / (1.0 + jnp.exp(-logits))
    sel = scores + bias_ref[...]
    sel3 = sel.reshape(N_EXPERT_GROUPS, per_group, tm)
    io3 = lax.broadcasted_iota(jnp.int32, sel3.shape, 1)
    m1, i1 = _first_index_of_max(sel3, io3, per_group, 1)
    m2 = jnp.max(jnp.where(io3 == i1, -jnp.inf, sel3), axis=1, keepdims=True)
    grp = (m1 + m2).reshape(N_EXPERT_GROUPS, tm)
    iog = lax.broadcasted_iota(jnp.int32, grp.shape, 0)
    keep = jnp.zeros(grp.shape, jnp.bool_)
    for _ in range(TOPK_GROUPS):
        _, gi = _first_index_of_max(grp, iog, N_EXPERT_GROUPS, 0)
        hit = iog == gi
        keep = keep | hit
        grp = jnp.where(hit, -jnp.inf, grp)
    keep3 = jnp.broadcast_to(keep.reshape(N_EXPERT_GROUPS, 1, tm), sel3.shape)
    cand = jnp.where(keep3, sel3, -jnp.inf).reshape(N_EXPERTS, tm)
    ioe = lax.broadcasted_iota(jnp.int32, cand.shape, 0)
    chosen = jnp.zeros(cand.shape, F32)
    idxs, gvals = [], []
    for _ in range(TOP_K):
        _, ei = _first_index_of_max(cand, ioe, N_EXPERTS, 0)
        hit = ioe == ei
        idxs.append(ei)
        gvals.append(jnp.sum(jnp.where(hit, scores, 0.0), axis=0, keepdims=True))
        chosen = jnp.where(hit, 1.0, chosen)
        cand = jnp.where(hit, -jnp.inf, cand)
    gsum = functools.reduce(lambda a, b: a + b, gvals)
    for kk in range(TOP_K):
        eidx_ref[kk:kk + 1, :] = idxs[kk]
        gate_ref[kk:kk + 1, :] = gvals[kk] / gsum * ROUTED_SCALE
    run = run_ref[...] + jnp.sum(chosen, axis=1, keepdims=True)
    run_ref[...] = run
    count_ref[...] = run.astype(jnp.int32)


def _router(h, router_w, router_bias, tm=ROUTER_TILE):
    t, d = h.shape
    tm = min(tm, t)
    slot = pl.BlockSpec((TOP_K, tm), lambda i: (0, i))
    return pl.pallas_call(
        _router_kernel,
        grid=(t // tm,),
        in_specs=[pl.BlockSpec((tm, d), lambda i: (i, 0)),
                  pl.BlockSpec((N_EXPERTS, d), lambda i: (0, 0)),
                  pl.BlockSpec((N_EXPERTS, 1), lambda i: (0, 0))],
        out_specs=[slot, slot, pl.BlockSpec((N_EXPERTS, 1), lambda i: (0, 0))],
        out_shape=[jax.ShapeDtypeStruct((TOP_K, t), jnp.int32),
                   jax.ShapeDtypeStruct((TOP_K, t), F32),
                   jax.ShapeDtypeStruct((N_EXPERTS, 1), jnp.int32)],
        scratch_shapes=[pltpu.VMEM((N_EXPERTS, 1), F32)],
        compiler_params=pltpu.CompilerParams(dimension_semantics=("arbitrary",),
                                             vmem_limit_bytes=VMEM_LIMIT),
        name="router",
    )(h, router_w.T, router_bias.reshape(N_EXPERTS, 1))


_ISSUE_UNROLL = 8
_TOP_K_BITS = TOP_K.bit_length() - 1
assert 1 << _TOP_K_BITS == TOP_K


def _experts_kernel(be_ref, nu_ref, srcc_ref, srcn_ref, dstp_ref, dstc_ref, h_hbm, wg_ref, wu_ref, wd_ref,
                    y_hbm, xbuf0, xbuf1, obuf0, obuf1, gsem, ssem):
    b = pl.program_id(0)
    nu = nu_ref[0]
    xbuf = (xbuf0, xbuf1)
    obuf = (obuf0, obuf1)
    rows = xbuf0.shape[0] // SLAB
    d = wg_ref.shape[0]
    f = wg_ref.shape[1]
    pad0 = y_hbm.shape[0] - 2 * rows * SLAB

    def gather_row(src_ref, r, dst_slot, priority):
        src = pl.multiple_of(src_ref[0, 0, r], SLAB)
        buf_rows = pl.ds(pl.multiple_of(r * SLAB, SLAB), SLAB)
        pltpu.make_async_copy(h_hbm.at[pl.ds(src, SLAB)], xbuf[dst_slot].at[buf_rows],
                              gsem.at[dst_slot]).start(priority=priority)

    def scatter_row(dst_ref, r, src_slot, priority):
        dst = pl.multiple_of(dst_ref[0, 0, r], SLAB)
        buf_rows = pl.ds(pl.multiple_of(r * SLAB, SLAB), SLAB)
        pltpu.make_async_copy(obuf[src_slot].at[buf_rows], y_hbm.at[pl.ds(dst, SLAB)],
                              ssem.at[src_slot]).start(priority=priority)

    def wait_gather(s):
        pltpu.make_async_copy(h_hbm.at[pl.ds(0, rows * SLAB)], xbuf[s], gsem.at[s]).wait()

    def wait_scatter(s):
        pltpu.make_async_copy(obuf[s], y_hbm.at[pl.ds(0, rows * SLAB)], ssem.at[s]).wait()

    @pl.when(b == 0)
    def _():
        for s in range(2):
            obuf[s][...] = jnp.zeros_like(obuf[s])
            pltpu.make_async_copy(obuf[s], y_hbm.at[pl.ds(pad0 + s * rows * SLAB, rows * SLAB)],
                                  ssem.at[s]).start()
        for s in range(2):
            wait_scatter(s)

        def body(r, carry):
            gather_row(srcc_ref, r, 0, 0)
            return carry
        lax.fori_loop(0, rows, body, 0)

    def block(slot, with_scatter):
        other = 1 - slot
        wait_gather(slot)

        def issue(i, carry):
            for u in range(_ISSUE_UNROLL):
                r = i * _ISSUE_UNROLL + u
                gather_row(srcn_ref, r, other, u % 2)
                if with_scatter:
                    scatter_row(dstp_ref, r, other, (u + 1) % 2)
            return carry
        lax.fori_loop(0, rows // _ISSUE_UNROLL, issue, 0)

        x = _load_packed_rows(xbuf[slot], rows)
        acts = []
        fh = f // 2
        for j in range(2):
            gate = _dot(x, wg_ref[:, j * fh:(j + 1) * fh])
            up = _dot(x, wu_ref[:, j * fh:(j + 1) * fh])
            acts.append((_silu(gate) * up).astype(BF16))
        act = jnp.concatenate(acts, axis=1)
        half = d // 2
        cw = 2 * LANES
        for q in range(half // cw):
            words = _pack_pair(_dot(act, wd_ref[:, q * cw:(q + 1) * cw]),
                               _dot(act, wd_ref[:, half + q * cw:half + (q + 1) * cw]))
            for jj in range(cw // LANES):
                obuf[slot][pl.ds(q * (cw // LANES) + jj, rows, stride=SLAB), :] = words[:, jj * LANES:(jj + 1) * LANES]

    def tail(slot):
        other = 1 - slot
        wait_gather(other)

        @pl.when(b >= 1)
        def _():
            wait_scatter(other)

        def body(r, carry):
            scatter_row(dstc_ref, r, slot, 0)
            return carry
        lax.fori_loop(0, rows, body, 0)
        wait_scatter(slot)

    @pl.when(b == 0)
    def _():
        block(0, False)

    for parity in range(2):
        @pl.when((b >= 1) & (b < nu) & (b % 2 == parity))
        def _(parity=parity):
            @pl.when(b >= 2)
            def _():
                wait_scatter(parity)
            block(parity, True)

        @pl.when((b == nu - 1) & (b % 2 == parity))
        def _(parity=parity):
            tail(parity)


def _experts(h_slab, row_src, row_dst, block_e, n_used, wg, wu, wd, layer, n_slots):
    nb, _, rows = row_src.shape
    d, f = wg.shape[-2], wg.shape[-1]

    def clamp(b, nu):
        return jnp.maximum(jnp.minimum(b, nu[0] - 1), 0)

    def smem(shift):
        return pl.BlockSpec((1, 1, rows), lambda b, be, nu: (clamp(b + shift, nu), 0, 0), memory_space=pltpu.SMEM)

    def weight(shape):
        return pl.BlockSpec((None, None) + shape, lambda b, be, nu: (layer, be[clamp(b, nu)], 0, 0))

    grid_spec = pltpu.PrefetchScalarGridSpec(
        num_scalar_prefetch=2,
        grid=(nb,),
        in_specs=[smem(0), smem(1), smem(-1), smem(0),
                  pl.BlockSpec(memory_space=pl.ANY),
                  weight((d, f)), weight((d, f)), weight((f, d))],
        out_specs=pl.BlockSpec(memory_space=pl.ANY),
        scratch_shapes=[pltpu.VMEM((rows * SLAB, LANES), jnp.uint32)] * 4
        + [pltpu.SemaphoreType.DMA((2,)), pltpu.SemaphoreType.DMA((2,))],
    )
    return pl.pallas_call(
        _experts_kernel,
        grid_spec=grid_spec,
        out_shape=jax.ShapeDtypeStruct(((n_slots + 2 * rows) * SLAB, LANES), jnp.uint32),
        compiler_params=pltpu.CompilerParams(dimension_semantics=("arbitrary",),
                                             vmem_limit_bytes=VMEM_LIMIT),
        name="experts",
    )(block_e, n_used, row_src, row_src, row_dst, row_dst, h_slab, wg, wu, wd)


_COMBINE_SUB = 32


def _combine_kernel(h_ref, gate_ref, *refs):
    y_refs = refs[:TOP_K]
    sg_ref, su_ref, sd_ref, g_ref, b_ref, of_ref, ob_ref, moe_ref = refs[TOP_K:]
    h = h_ref[...]
    tm = h.shape[0]
    half = SLAB * LANES
    for s in range(tm // _COMBINE_SUB):
        rs = slice(s * _COMBINE_SUB, (s + 1) * _COMBINE_SUB)
        gates = gate_ref[rs, :]
        first = s * _COMBINE_SUB * SLAB
        for j in range(SLAB):
            lo_acc = hi_acc = None
            for kk in range(TOP_K):
                lo, hi = _unpack_words(y_refs[kk][pl.ds(first + j, _COMBINE_SUB, stride=SLAB), :])
                gk = gates[:, kk:kk + 1]
                lo_acc = gk * lo if kk == 0 else lo_acc + gk * lo
                hi_acc = gk * hi if kk == 0 else hi_acc + gk * hi
            moe_ref[rs, j * LANES:(j + 1) * LANES] = lo_acc
            moe_ref[rs, half + j * LANES:half + (j + 1) * LANES] = hi_acc
    hb = h.astype(BF16)
    act = _silu(_dot(hb, sg_ref[...])) * _dot(hb, su_ref[...])
    y = ALPHA * h + _dot(act.astype(BF16), sd_ref[...]) + moe_ref[...]
    out = _layer_norm(y, g_ref[...], b_ref[...])
    of_ref[...] = out
    ob_ref[...] = out.astype(BF16)


def _combine(h, gates_t, y_slots, sg, su, sd, g, b, tm=COMBINE_TILE):
    t, d = h.shape
    n = t // tm
    f = sg.shape[1]
    vec = pl.BlockSpec((1, d), lambda i: (0, 0))
    row = pl.BlockSpec((tm, d), lambda i: (i, 0))
    return pl.pallas_call(
        _combine_kernel,
        grid=(n,),
        in_specs=[row, pl.BlockSpec((tm, TOP_K), lambda i: (i, 0))]
        + [pl.BlockSpec((tm * SLAB, LANES), lambda i, kk=kk: (kk * n + i, 0)) for kk in range(TOP_K)]
        + [pl.BlockSpec((d, f), lambda i: (0, 0)),
            pl.BlockSpec((d, f), lambda i: (0, 0)),
            pl.BlockSpec((f, d), lambda i: (0, 0)),
            vec, vec],
        out_specs=[row, row],
        out_shape=[jax.ShapeDtypeStruct((t, d), F32), jax.ShapeDtypeStruct((t, d), BF16)],
        scratch_shapes=[pltpu.VMEM((tm, d), F32)],
        compiler_params=pltpu.CompilerParams(dimension_semantics=("parallel",),
                                             vmem_limit_bytes=VMEM_LIMIT),
        name="combine_ln2",
    )(h, gates_t, *([y_slots] * TOP_K), sg, su, sd, g.reshape(1, d), b.reshape(1, d))


def _moe(h, h_slab, router_w, router_bias, wg, wu, wd, layer, sg, su, sd, ln_g, ln_b):
    t, d = h.shape
    tk = t * TOP_K
    n_blocks = tk // MOE_ROWS + N_EXPERTS
    eidx, gates, counts = _router(h, router_w, router_bias)
    counts = counts.reshape(N_EXPERTS)
    n_rows = n_blocks * MOE_ROWS
    padded = (counts + MOE_ROWS - 1) // MOE_ROWS * MOE_ROWS
    pend = jnp.cumsum(padded)
    pad_end = jnp.cumsum(padded - counts)
    pad_key = jnp.sum(pad_end[None, :] <= jnp.arange(n_rows - tk, dtype=jnp.int32)[:, None], axis=1)
    keys = jnp.concatenate([eidx.T.reshape(tk), pad_key.astype(jnp.int32)])
    _, row_slot = lax.sort((keys, jnp.arange(n_rows, dtype=jnp.int32)), num_keys=1, is_stable=True)
    real = row_slot < tk
    row = jnp.arange(n_rows, dtype=jnp.int32)
    row_src = jnp.where(real, (row_slot >> _TOP_K_BITS) * SLAB, 0)
    plane_row = (row_slot & (TOP_K - 1)) * t + (row_slot >> _TOP_K_BITS)
    row_dst = jnp.where(real, plane_row, tk + row % (2 * MOE_ROWS)) * SLAB
    block_start = jnp.arange(n_blocks, dtype=jnp.int32) * MOE_ROWS
    block_e = jnp.minimum(jnp.sum(pend[None, :] <= block_start[:, None], axis=1), N_EXPERTS - 1).astype(jnp.int32)
    n_used = (pend[-1] // MOE_ROWS).astype(jnp.int32).reshape(1)
    y_slots = _experts(h_slab, row_src.reshape(n_blocks, 1, MOE_ROWS), row_dst.reshape(n_blocks, 1, MOE_ROWS),
                       block_e, n_used, wg, wu, wd, layer, tk)
    return _combine(h, gates.T, y_slots, sg, su, sd, ln_g, ln_b, tm=min(COMBINE_TILE, t))


def kernel(x, ln0_gain, ln0_bias, w_in, b_in, hg_lb_logits, sw_sinks, sg_ln_gain, sg_ln_bias, sg_w_s, sg_b_s, mix_gain, w_out, ln1_gain, ln1_bias, router_w, router_bias, exp_w_gate, exp_w_up, exp_w_down, sh_w_gate, sh_w_up, sh_w_down, ln2_gain, ln2_bias):
    b, s, d = x.shape
    t = b * s
    p = jax.nn.softmax(hg_lb_logits.astype(F32), axis=0)
    cs = jnp.cumsum(p, axis=0)
    lbs = cs - cs[0:1]
    a0, a1 = HG_WIDTH, HG_WIDTH + SW_WIDTH

    wg_all, wu_all, wd_all = exp_w_gate.astype(BF16), exp_w_up.astype(BF16), exp_w_down.astype(BF16)
    h, hb = _ln(x.reshape(t, d), ln0_gain, ln0_bias)
    for l in range(DEPTH):
        proj = _in_proj(hb, _regroup_columns(w_in[l]).astype(BF16), _regroup_columns(b_in[l]))
        proj3 = proj.reshape(b, s, D_IN)
        lb = lbs[l]
        o_a = _hgrn(proj3, jnp.log(lb), jnp.log1p(-lb), 1.0 - lb, mix_gain[l, :a0])
        o_b = _swa(proj3, sw_sinks[l], mix_gain[l, a0:a1])
        o_c = _sgu(proj3, sg_ln_gain[l], sg_ln_bias[l], sg_w_s[l], sg_b_s[l], mix_gain[l, a1:])
        h, h_slab = _out_proj(o_a.reshape(t, -1), o_b.reshape(t, -1), o_c.reshape(t, -1),
                              w_out[l].astype(BF16), h, ln1_gain[l], ln1_bias[l])
        h, hb = _moe(h, h_slab, router_w[l], router_bias[l], wg_all, wu_all, wd_all, l,
                     sh_w_gate[l].astype(BF16), sh_w_up[l].astype(BF16), sh_w_down[l].astype(BF16),
                     ln2_gain[l], ln2_bias[l])
    return h.reshape(b, s, d)
```

```python
import functools

import numpy as np
import jax
import jax.numpy as jnp
from jax import lax
from jax.experimental import pallas as pl
from jax.experimental.pallas import tpu as pltpu

F32 = jnp.float32
BF16 = jnp.bfloat16

D_MODEL = 2048
DEPTH = 2
HG_HEADS = 4
HG_DK = 128
HG_WIDTH = HG_HEADS * HG_DK
CHUNK = 128
SW_Q_HEADS = 16
SW_KV_HEADS = 2
SW_HEAD_DIM = 64
SW_WIDTH = SW_Q_HEADS * SW_HEAD_DIM
SW_KV_WIDTH = SW_KV_HEADS * SW_HEAD_DIM
SG_GROUPS = 4
SG_WIDTH = SG_GROUPS * CHUNK
D_IN = 4 * HG_WIDTH + SW_WIDTH + 2 * SW_KV_WIDTH + 2 * SG_WIDTH
N_EXPERTS = 64
TOP_K = 8
D_EXPERT = 512
N_EXPERT_GROUPS = 8
TOPK_GROUPS = 4
ROUTED_SCALE = 2.5
ALPHA = (2 * DEPTH) ** 0.25
LN_EPS = 1e-5
RMS_EPS = 1e-6

_SWQ_OFF = 4 * HG_WIDTH
_SGU_OFF = _SWQ_OFF + SW_WIDTH
_SWK_OFF = _SGU_OFF + 2 * SG_WIDTH


def _regroup_columns(a):
    k0 = _SWQ_OFF + SW_WIDTH
    k1 = k0 + 2 * SW_KV_WIDTH
    return jnp.concatenate([a[..., :k0], a[..., k1:], a[..., k0:k1]], axis=-1)


MOE_ROWS = 256
TOKEN_GROUPS = 2
ROUTER_TILE = 512
COMBINE_TILE = 128
VMEM_LIMIT = 56 * 1024 * 1024

_LEVEL_SIZES = (64, 32, 16, 8, 4, 2, 1)
_N_LEVELS = len(_LEVEL_SIZES)


def _hgrn_constants():
    t = np.arange(CHUNK)[:, None]
    u = np.arange(CHUNK)[None, :]
    mats = []
    level = np.full((CHUNK, CHUNK), -1, np.int32)
    for li, s in enumerate(_LEVEL_SIZES):
        blk = t // s
        odd = (blk % 2) == 1
        m_query = (u >= blk * s) & (u <= t)
        m_key = (u > t) & (u <= (blk + 1) * s - 1)
        mats.append(np.where(odd, m_query, m_key))
        pair = ((t // s) % 2 == 1) & ((u // s) == (t // s) - 1)
        level[pair] = li
    mats.append(u <= t)
    mats.append(u > t)
    level[np.arange(CHUNK), np.arange(CHUNK)] = _N_LEVELS
    return np.concatenate(mats, 0).astype(np.float32), level


_HGRN_MSTACK, _HGRN_LEVEL = _hgrn_constants()


def _layer_norm(x, g, b):
    mu = jnp.mean(x, axis=-1, keepdims=True)
    xc = x - mu
    var = jnp.mean(xc * xc, axis=-1, keepdims=True)
    return xc * lax.rsqrt(var + LN_EPS) * g + b


def _silu(x):
    return x / (1.0 + jnp.exp(-x))


def _gelu(x):
    return 0.5 * x * (1.0 + lax.erf(x * (2.0 ** -0.5)))


def _dot(a, b):
    return jnp.dot(a, b, preferred_element_type=F32)


def _dot_nt(a, b):
    return lax.dot_general(a, b, (((1,), (1,)), ((), ())), preferred_element_type=F32)


def _dot_tn(a, b):
    return lax.dot_general(a, b, (((0,), (0,)), ((), ())), preferred_element_type=F32)


def _ln_kernel(x_ref, g_ref, b_ref, of_ref, ob_ref):
    y = _layer_norm(x_ref[...], g_ref[...], b_ref[...])
    of_ref[...] = y
    ob_ref[...] = y.astype(BF16)


def _ln(x, g, b, tm=256):
    t, d = x.shape
    row = pl.BlockSpec((tm, d), lambda i: (i, 0))
    vec = pl.BlockSpec((1, d), lambda i: (0, 0))
    return pl.pallas_call(
        _ln_kernel,
        grid=(t // tm,),
        in_specs=[row, vec, vec],
        out_specs=[row, row],
        out_shape=[jax.ShapeDtypeStruct((t, d), F32), jax.ShapeDtypeStruct((t, d), BF16)],
        name="ln0",
    )(x, g.reshape(1, d), b.reshape(1, d))


def _mm_bias_kernel(a_ref, w_ref, b_ref, o_ref):
    o_ref[...] = _dot(a_ref[...], w_ref[...]) + b_ref[...]


def _in_proj(a, w, b, tm=2048, tn=256):
    t, k = a.shape
    n = w.shape[1]
    tm = min(tm, t)
    return pl.pallas_call(
        _mm_bias_kernel,
        grid=(t // tm, n // tn),
        in_specs=[pl.BlockSpec((tm, k), lambda i, j: (i, 0)),
                  pl.BlockSpec((k, tn), lambda i, j: (0, j)),
                  pl.BlockSpec((1, tn), lambda i, j: (0, j))],
        out_specs=pl.BlockSpec((tm, tn), lambda i, j: (i, j)),
        out_shape=jax.ShapeDtypeStruct((t, n), F32),
        compiler_params=pltpu.CompilerParams(vmem_limit_bytes=VMEM_LIMIT),
        name="in_proj",
    )(a, w, b.reshape(1, n))


def _hgrn_kernel(q_ref, f_ref, i_ref, g_ref, llb_ref, l1lb_ref, oml_ref, gain_ref,
                 mstack_ref, level_ref, o_ref, state_ref):
    c = pl.program_id(1)

    @pl.when(c == 0)
    def _():
        state_ref[...] = jnp.zeros_like(state_ref)

    q = _silu(q_ref[...])
    z = f_ref[...]
    ez = jnp.exp(-jnp.abs(z))
    log_sig = jnp.minimum(z, 0.0) - jnp.log(1.0 + ez)
    sig_neg = jnp.where(z >= 0.0, ez, 1.0) / (1.0 + ez)
    k = oml_ref[...] * sig_neg
    a = llb_ref[...]
    bb = l1lb_ref[...] + log_sig
    log_f = jnp.maximum(a, bb) + jnp.log(1.0 + jnp.exp(-jnp.abs(a - bb)))

    lf_hi = log_f.astype(BF16)
    lf_lo = (log_f - lf_hi.astype(F32)).astype(BF16)
    m = mstack_ref[...]
    sums = _dot(m, lf_hi) + _dot(m, lf_lo)
    decay = jnp.exp(sums)

    rows = lax.broadcasted_iota(jnp.int32, (CHUNK, 1), 0)
    factors = []
    for li, s in enumerate(_LEVEL_SIZES):
        is_query = ((rows // s) % 2) == 1
        factors.append((jnp.where(is_query, q, k) * decay[li * CHUNK:(li + 1) * CHUNK]).astype(BF16))
    cum = _N_LEVELS * CHUNK
    q_in = (q * decay[cum:cum + CHUNK]).astype(BF16)
    k_dec = (k * decay[cum + CHUNK:cum + 2 * CHUNK]).astype(BF16)
    end_decay = decay[cum + CHUNK - 1:cum + CHUNK]
    qb = q.astype(BF16)
    kb = k.astype(BF16)
    vb = i_ref[...].astype(BF16)
    level = level_ref[...]
    gate = _silu(g_ref[...])
    gain = gain_ref[...]

    for h in range(HG_HEADS):
        hs = slice(h * HG_DK, (h + 1) * HG_DK)
        scores = jnp.zeros((CHUNK, CHUNK), F32)
        for li in range(_N_LEVELS):
            fl = factors[li][:, hs]
            scores = jnp.where(level == li, _dot_nt(fl, fl), scores)
        scores = jnp.where(level == _N_LEVELS, _dot_nt(qb[:, hs], kb[:, hs]), scores)
        st = state_ref[h]
        o = _dot(scores.astype(BF16), vb[:, hs]) + _dot_nt(q_in[:, hs], st.astype(BF16))
        state_ref[h] = st * end_decay[:, hs] + _dot_tn(vb[:, hs], k_dec[:, hs])
        ms = jnp.mean(o * o, axis=-1, keepdims=True)
        o = o * lax.rsqrt(ms + RMS_EPS) * gain[:, hs] * gate[:, hs]
        o_ref[:, hs] = o.astype(o_ref.dtype)


def _hgrn(proj3, llb, l1lb, oml, gain):
    b, s, _ = proj3.shape
    w = HG_WIDTH

    def col(j):
        return pl.BlockSpec((None, CHUNK, w), lambda bi, ci, j=j: (bi, ci, j))

    vec = pl.BlockSpec((1, w), lambda bi, ci: (0, 0))
    nm = _HGRN_MSTACK.shape[0]
    return pl.pallas_call(
        _hgrn_kernel,
        grid=(b, s // CHUNK),
        in_specs=[col(0), col(1), col(2), col(3), vec, vec, vec, vec,
                  pl.BlockSpec((nm, CHUNK), lambda bi, ci: (0, 0)),
                  pl.BlockSpec((CHUNK, CHUNK), lambda bi, ci: (0, 0))],
        out_specs=pl.BlockSpec((None, CHUNK, w), lambda bi, ci: (bi, ci, 0)),
        out_shape=jax.ShapeDtypeStruct((b, s, w), BF16),
        scratch_shapes=[pltpu.VMEM((HG_HEADS, HG_DK, HG_DK), F32)],
        compiler_params=pltpu.CompilerParams(dimension_semantics=("parallel", "arbitrary"),
                                             vmem_limit_bytes=VMEM_LIMIT),
        name="hgrn2",
    )(proj3, proj3, proj3, proj3, llb.reshape(1, w), l1lb.reshape(1, w), oml.reshape(1, w),
      gain.reshape(1, w), jnp.asarray(_HGRN_MSTACK, BF16), jnp.asarray(_HGRN_LEVEL))


def _swa_kernel(q_ref, kp_ref, kc_ref, vp_ref, vc_ref, sink_ref, gain_ref, o_ref):
    n = pl.program_id(1)
    g = SW_Q_HEADS // SW_KV_HEADS
    hd = SW_HEAD_DIM
    q = q_ref[...]
    kband = jnp.concatenate([kp_ref[...], kc_ref[...]], axis=0).astype(BF16)
    vband = jnp.concatenate([vp_ref[...], vc_ref[...]], axis=0).astype(BF16)
    t = lax.broadcasted_iota(jnp.int32, (g * CHUNK, 2 * CHUNK), 0) % CHUNK
    s = lax.broadcasted_iota(jnp.int32, (g * CHUNK, 2 * CHUNK), 1)
    rel = t + CHUNK - s
    mask = (rel >= 0) & (rel < CHUNK) & ((s >= CHUNK) | (n > 0))
    outs = []
    for kv in range(SW_KV_HEADS):
        qs = jnp.concatenate([q[:, (kv * g + j) * hd:(kv * g + j + 1) * hd] for j in range(g)], axis=0)
        sc = _dot_nt(qs.astype(BF16), kband[:, kv * hd:(kv + 1) * hd]) * (hd ** -0.5)
        sc = jnp.where(mask, sc, -jnp.inf)
        sink = sink_ref[kv]
        mx = jnp.maximum(jnp.max(sc, axis=-1, keepdims=True), sink)
        p = jnp.exp(sc - mx)
        denom = jnp.sum(p, axis=-1, keepdims=True) + jnp.exp(sink - mx)
        o = _dot((p / denom).astype(BF16), vband[:, kv * hd:(kv + 1) * hd])
        outs.extend(o[j * CHUNK:(j + 1) * CHUNK] for j in range(g))
    o = jnp.concatenate(outs, axis=-1)
    ms = jnp.mean(o * o, axis=-1, keepdims=True)
    o_ref[...] = (o * lax.rsqrt(ms + RMS_EPS) * gain_ref[...]).astype(o_ref.dtype)


def _swa(proj3, sinks, gain):
    b, s, _ = proj3.shape
    q_col = _SWQ_OFF // SW_WIDTH
    k_col = _SWK_OFF // SW_KV_WIDTH
    v_col = k_col + 1
    g = SW_Q_HEADS // SW_KV_HEADS
    sink_rows = jnp.repeat(sinks.astype(F32).reshape(SW_KV_HEADS, g), CHUNK, axis=1)[..., None]

    def kv_spec(col, prev):
        if prev:
            return pl.BlockSpec((None, CHUNK, SW_KV_WIDTH), lambda bi, ni: (bi, jnp.maximum(ni - 1, 0), col))
        return pl.BlockSpec((None, CHUNK, SW_KV_WIDTH), lambda bi, ni: (bi, ni, col))

    return pl.pallas_call(
        _swa_kernel,
        grid=(b, s // CHUNK),
        in_specs=[pl.BlockSpec((None, CHUNK, SW_WIDTH), lambda bi, ni: (bi, ni, q_col)),
                  kv_spec(k_col, True), kv_spec(k_col, False),
                  kv_spec(v_col, True), kv_spec(v_col, False),
                  pl.BlockSpec((SW_KV_HEADS, g * CHUNK, 1), lambda bi, ni: (0, 0, 0)),
                  pl.BlockSpec((1, SW_WIDTH), lambda bi, ni: (0, 0))],
        out_specs=pl.BlockSpec((None, CHUNK, SW_WIDTH), lambda bi, ni: (bi, ni, 0)),
        out_shape=jax.ShapeDtypeStruct((b, s, SW_WIDTH), BF16),
        compiler_params=pltpu.CompilerParams(dimension_semantics=("parallel", "parallel"),
                                             vmem_limit_bytes=VMEM_LIMIT),
        name="swa",
    )(proj3, proj3, proj3, proj3, proj3, sink_rows, gain.reshape(1, SW_WIDTH))


def _sgu_kernel(u_ref, v_ref, lng_ref, lnb_ref, w_ref, bs_ref, gain_ref, o_ref):
    u = _gelu(u_ref[...])
    v = _layer_norm(_gelu(v_ref[...]), lng_ref[...], lnb_ref[...]).astype(BF16)
    r = lax.broadcasted_iota(jnp.int32, (CHUNK, CHUNK), 0)
    c = lax.broadcasted_iota(jnp.int32, (CHUNK, CHUNK), 1)
    tril = c <= r
    bs = bs_ref[...]
    parts = []
    for gi in range(SG_GROUPS):
        gs = slice(gi * CHUNK, (gi + 1) * CHUNK)
        w = jnp.where(tril, w_ref[gi], 0.0).astype(BF16)
        parts.append(_dot(w, v[:, gs]) + bs[:, gi:gi + 1])
    o = u * jnp.concatenate(parts, axis=-1)
    ms = jnp.mean(o * o, axis=-1, keepdims=True)
    o_ref[...] = (o * lax.rsqrt(ms + RMS_EPS) * gain_ref[...]).astype(o_ref.dtype)


def _sgu(proj3, ln_g, ln_b, w_s, b_s, gain):
    b, s, _ = proj3.shape
    w = SG_WIDTH
    u_col = _SGU_OFF // SG_WIDTH
    vec = pl.BlockSpec((1, w), lambda bi, ci: (0, 0))
    return pl.pallas_call(
        _sgu_kernel,
        grid=(b, s // CHUNK),
        in_specs=[pl.BlockSpec((None, CHUNK, w), lambda bi, ci: (bi, ci, u_col)),
                  pl.BlockSpec((None, CHUNK, w), lambda bi, ci: (bi, ci, u_col + 1)),
                  vec, vec,
                  pl.BlockSpec((SG_GROUPS, CHUNK, CHUNK), lambda bi, ci: (0, 0, 0)),
                  pl.BlockSpec((CHUNK, SG_GROUPS), lambda bi, ci: (0, 0)),
                  vec],
        out_specs=pl.BlockSpec((None, CHUNK, w), lambda bi, ci: (bi, ci, 0)),
        out_shape=jax.ShapeDtypeStruct((b, s, w), BF16),
        compiler_params=pltpu.CompilerParams(dimension_semantics=("parallel", "parallel"),
                                             vmem_limit_bytes=VMEM_LIMIT),
        name="sgu",
    )(proj3, proj3, ln_g.reshape(1, w), ln_b.reshape(1, w), w_s, b_s.T, gain.reshape(1, w))


LANES = 128
SLAB = D_MODEL // 2 // LANES


def _pack_pair(low, high):
    def bits(v):
        return lax.bitcast_convert_type(v.astype(BF16).astype(F32), jnp.uint32)

    return (bits(high) & jnp.uint32(0xFFFF0000)) | (bits(low) >> 16)


def _pack_rows(x):
    half = x.shape[1] // 2
    return _pack_pair(x[:, :half], x[:, half:])


def _unpack_words(w):
    return (lax.bitcast_convert_type(w << 16, F32),
            lax.bitcast_convert_type(w & jnp.uint32(0xFFFF0000), F32))


def _store_slabs(ref, x):
    n = x.shape[0]
    for j in range(SLAB):
        ref[pl.ds(j, n, stride=SLAB), :] = x[:, j * LANES:(j + 1) * LANES]


def _load_packed_rows(ref, n, first=0, stride=SLAB):
    lows, highs = [], []
    for j in range(SLAB):
        lo, hi = _unpack_words(ref[pl.ds(first + j, n, stride=stride), :])
        lows.append(lo.astype(BF16))
        highs.append(hi.astype(BF16))
    return jnp.concatenate(lows + highs, axis=1)


def _out_proj_kernel(oa_ref, ob_ref, oc_ref, w_ref, h_ref, g_ref, b_ref, of_ref, os_ref):
    mixed = jnp.concatenate([oa_ref[...], ob_ref[...], oc_ref[...]], axis=-1)
    y = ALPHA * h_ref[...] + _dot(mixed, w_ref[...])
    out = _layer_norm(y, g_ref[...], b_ref[...])
    of_ref[...] = out
    _store_slabs(os_ref, _pack_rows(out))


def _out_proj(oa, ob, oc, w, h, g, b, tm=256):
    t, d = h.shape
    vec = pl.BlockSpec((1, d), lambda i: (0, 0))
    return pl.pallas_call(
        _out_proj_kernel,
        grid=(t // tm,),
        in_specs=[pl.BlockSpec((tm, oa.shape[1]), lambda i: (i, 0)),
                  pl.BlockSpec((tm, ob.shape[1]), lambda i: (i, 0)),
                  pl.BlockSpec((tm, oc.shape[1]), lambda i: (i, 0)),
                  pl.BlockSpec(w.shape, lambda i: (0, 0)),
                  pl.BlockSpec((tm, d), lambda i: (i, 0)), vec, vec],
        out_specs=[pl.BlockSpec((tm, d), lambda i: (i, 0)),
                   pl.BlockSpec((tm * SLAB, LANES), lambda i: (i, 0))],
        out_shape=[jax.ShapeDtypeStruct((t, d), F32), jax.ShapeDtypeStruct((t * SLAB, LANES), jnp.uint32)],
        compiler_params=pltpu.CompilerParams(vmem_limit_bytes=VMEM_LIMIT),
        name="out_proj_ln1",
    )(oa, ob, oc, w, h, g.reshape(1, d), b.reshape(1, d))


def _first_index_of_max(x, iota, size, axis):
    mx = jnp.max(x, axis=axis, keepdims=True)
    idx = jnp.min(jnp.where(x == mx, iota, size), axis=axis, keepdims=True)
    return mx, idx


def _router_kernel(tiles_per_group, h_ref, wt_ref, bias_ref, eidx_ref, gate_ref, count_ref, run_ref):
    i = pl.program_id(0)

    @pl.when(i % tiles_per_group == 0)
    def _():
        run_ref[...] = jnp.zeros_like(run_ref)

    tm = h_ref.shape[0]
    per_group = N_EXPERTS // N_EXPERT_GROUPS
    logits = lax.dot_general(wt_ref[...], h_ref[...], (((1,), (1,)), ((), ())),
                             precision=lax.Precision.HIGHEST, preferred_element_type=F32)
    scores = 1.0 / (1.0 + jnp.exp(-logits))
    sel = scores + bias_ref[...]
    sel3 = sel.reshape(N_EXPERT_GROUPS, per_group, tm)
    io3 = lax.broadcasted_iota(jnp.int32, sel3.shape, 1)
    m1, i1 = _first_index_of_max(sel3, io3, per_group, 1)
    m2 = jnp.max(jnp.where(io3 == i1, -jnp.inf, sel3), axis=1, keepdims=True)
    grp = (m1 + m2).reshape(N_EXPERT_GROUPS, tm)
    iog = lax.broadcasted_iota(jnp.int32, grp.shape, 0)
    keep = jnp.zeros(grp.shape, jnp.bool_)
    for _ in range(TOPK_GROUPS):
        _, gi = _first_index_of_max(grp, iog, N_EXPERT_GROUPS, 0)
        hit = iog == gi
        keep = keep | hit
        grp = jnp.where(hit, -jnp.inf, grp)
    keep3 = jnp.broadcast_to(keep.reshape(N_EXPERT_GROUPS, 1, tm), sel3.shape)
    cand = jnp.where(keep3, sel3, -jnp.inf).reshape(N_EXPERTS, tm)
    ioe = lax.broadcasted_iota(jnp.int32, cand.shape, 0)
    chosen = jnp.zeros(cand.shape, F32)
    idxs, gvals = [], []
    for _ in range(TOP_K):
        _, ei = _first_index_of_max(cand, ioe, N_EXPERTS, 0)
        hit = ioe == ei
        idxs.append(ei)
        gvals.append(jnp.sum(jnp.where(hit, scores, 0.0), axis=0, keepdims=True))
        chosen = jnp.where(hit, 1.0, chosen)
        cand = jnp.where(hit, -jnp.inf, cand)
    gsum = functools.reduce(lambda a, b: a + b, gvals)
    for kk in range(TOP_K):
        eidx_ref[kk:kk + 1, :] = idxs[kk]
        gate_ref[kk:kk + 1, :] = gvals[kk] / gsum * ROUTED_SCALE
    run = run_ref[...] + jnp.sum(chosen, axis=1, keepdims=True)
    run_ref[...] = run
    count_ref[...] = run.astype(jnp.int32)


def _router(h, router_w, router_bias, n_groups, tm=ROUTER_TILE):
    t, d = h.shape
    tm = min(tm, t // n_groups)
    tiles_per_group = t // n_groups // tm
    slot = pl.BlockSpec((TOP_K, tm), lambda i: (0, i))
    return pl.pallas_call(
        functools.partial(_router_kernel, tiles_per_group),
        grid=(t // tm,),
        in_specs=[pl.BlockSpec((tm, d), lambda i: (i, 0)),
                  pl.BlockSpec((N_EXPERTS, d), lambda i: (0, 0)),
                  pl.BlockSpec((N_EXPERTS, 1), lambda i: (0, 0))],
        out_specs=[slot, slot, pl.BlockSpec((None, N_EXPERTS, 1), lambda i: (i // tiles_per_group, 0, 0))],
        out_shape=[jax.ShapeDtypeStruct((TOP_K, t), jnp.int32),
                   jax.ShapeDtypeStruct((TOP_K, t), F32),
                   jax.ShapeDtypeStruct((n_groups, N_EXPERTS, 1), jnp.int32)],
        scratch_shapes=[pltpu.VMEM((N_EXPERTS, 1), F32)],
        compiler_params=pltpu.CompilerParams(dimension_semantics=("arbitrary",),
                                             vmem_limit_bytes=VMEM_LIMIT),
        name="router",
    )(h, router_w.T, router_bias.reshape(N_EXPERTS, 1))


_ISSUE_UNROLL = 8
_TOP_K_BITS = TOP_K.bit_length() - 1
assert 1 << _TOP_K_BITS == TOP_K


def _experts_kernel(be_ref, bg_ref, nu_ref, src_ref, dstp_ref, dstc_ref, h_hbm, wg_ref, wu_ref, wd_ref,
                    y_hbm, tok_ref, xbuf, obuf0, obuf1, lsem, ssem):
    b = pl.program_id(0)
    nu = nu_ref[0]
    obuf = (obuf0, obuf1)
    rows = xbuf.shape[0] // SLAB
    d = wg_ref.shape[0]
    f = wg_ref.shape[1]
    pad0 = y_hbm.shape[0] - 2 * rows * SLAB

    def scatter_row(dst_ref, r, src_slot, priority):
        dst = pl.multiple_of(dst_ref[0, 0, r], SLAB)
        buf_rows = pl.ds(pl.multiple_of(r * SLAB, SLAB), SLAB)
        pltpu.make_async_copy(obuf[src_slot].at[buf_rows], y_hbm.at[pl.ds(dst, SLAB)],
                              ssem.at[src_slot]).start(priority=priority)

    def wait_scatter(s):
        pltpu.make_async_copy(obuf[s], y_hbm.at[pl.ds(0, rows * SLAB)], ssem.at[s]).wait()

    @pl.when(b == 0)
    def _():
        for s in range(2):
            obuf[s][...] = jnp.zeros_like(obuf[s])
            pltpu.make_async_copy(obuf[s], y_hbm.at[pl.ds(pad0 + s * rows * SLAB, rows * SLAB)],
                                  ssem.at[s]).start()
        for s in range(2):
            wait_scatter(s)

    group = bg_ref[b]

    @pl.when((b < nu) & ((b == 0) | (group != bg_ref[jnp.maximum(b - 1, 0)])))
    def _():
        load = pltpu.make_async_copy(h_hbm.at[group], tok_ref, lsem.at[0])
        load.start()
        load.wait()

    def block(slot, with_scatter):
        other = 1 - slot

        if with_scatter:
            def issue(i, carry):
                for u in range(_ISSUE_UNROLL):
                    scatter_row(dstp_ref, i * _ISSUE_UNROLL + u, other, u % 2)
                return carry
            lax.fori_loop(0, rows // _ISSUE_UNROLL, issue, 0)

        def gather(i, carry):
            for u in range(_ISSUE_UNROLL):
                r = i * _ISSUE_UNROLL + u
                src = pl.multiple_of(src_ref[0, 0, r], SLAB)
                xbuf[pl.ds(pl.multiple_of(r * SLAB, SLAB), SLAB), :] = tok_ref[pl.ds(src, SLAB), :]
            return carry
        lax.fori_loop(0, rows // _ISSUE_UNROLL, gather, 0)

        x = _load_packed_rows(xbuf, rows)
        acts = []
        fh = f // 2
        for j in range(2):
            gate = _dot(x, wg_ref[:, j * fh:(j + 1) * fh])
            up = _dot(x, wu_ref[:, j * fh:(j + 1) * fh])
            acts.append((_silu(gate) * up).astype(BF16))
        act = jnp.concatenate(acts, axis=1)
        half = d // 2
        cw = 2 * LANES
        for q in range(half // cw):
            words = _pack_pair(_dot(act, wd_ref[:, q * cw:(q + 1) * cw]),
                               _dot(act, wd_ref[:, half + q * cw:half + (q + 1) * cw]))
            for jj in range(cw // LANES):
                obuf[slot][pl.ds(q * (cw // LANES) + jj, rows, stride=SLAB), :] = words[:, jj * LANES:(jj + 1) * LANES]

    def tail(slot):
        other = 1 - slot

        @pl.when(b >= 1)
        def _():
            wait_scatter(other)

        def body(r, carry):
            scatter_row(dstc_ref, r, slot, 0)
            return carry
        lax.fori_loop(0, rows, body, 0)
        wait_scatter(slot)

    @pl.when(b == 0)
    def _():
        block(0, False)

    for parity in range(2):
        @pl.when((b >= 1) & (b < nu) & (b % 2 == parity))
        def _(parity=parity):
            @pl.when(b >= 2)
            def _():
                wait_scatter(parity)
            block(parity, True)

        @pl.when((b == nu - 1) & (b % 2 == parity))
        def _(parity=parity):
            tail(parity)


def _experts(h_groups, row_src, row_dst, block_e, block_g, n_used, wg, wu, wd, layer, n_slots):
    nb, _, rows = row_src.shape
    d, f = wg.shape[-2], wg.shape[-1]
    group_rows = h_groups.shape[1]

    def clamp(b, nu):
        return jnp.maximum(jnp.minimum(b, nu[0] - 1), 0)

    def smem(shift):
        return pl.BlockSpec((1, 1, rows), lambda b, be, bg, nu: (clamp(b + shift, nu), 0, 0),
                            memory_space=pltpu.SMEM)

    def weight(shape):
        return pl.BlockSpec((None, None) + shape, lambda b, be, bg, nu: (layer, be[clamp(b, nu)], 0, 0))

    grid_spec = pltpu.PrefetchScalarGridSpec(
        num_scalar_prefetch=3,
        grid=(nb,),
        in_specs=[smem(0), smem(-1), smem(0),
                  pl.BlockSpec(memory_space=pl.ANY),
                  weight((d, f)), weight((d, f)), weight((f, d))],
        out_specs=pl.BlockSpec(memory_space=pl.ANY),
        scratch_shapes=[pltpu.VMEM((group_rows, LANES), jnp.uint32)]
        + [pltpu.VMEM((rows * SLAB, LANES), jnp.uint32)] * 3
        + [pltpu.SemaphoreType.DMA((1,)), pltpu.SemaphoreType.DMA((2,))],
    )
    return pl.pallas_call(
        _experts_kernel,
        grid_spec=grid_spec,
        out_shape=jax.ShapeDtypeStruct(((n_slots + 2 * rows) * SLAB, LANES), jnp.uint32),
        compiler_params=pltpu.CompilerParams(dimension_semantics=("arbitrary",),
                                             vmem_limit_bytes=VMEM_LIMIT),
        name="experts",
    )(block_e, block_g, n_used, row_src, row_dst, row_dst, h_groups, wg, wu, wd)


_COMBINE_SUB = 32


def _combine_kernel(h_ref, gate_ref, *refs):
    y_refs = refs[:TOP_K]
    sg_ref, su_ref, sd_ref, g_ref, b_ref, of_ref, ob_ref, moe_ref = refs[TOP_K:]
    h = h_ref[...]
    tm = h.shape[0]
    half = SLAB * LANES
    for s in range(tm // _COMBINE_SUB):
        rs = slice(s * _COMBINE_SUB, (s + 1) * _COMBINE_SUB)
        gates = gate_ref[rs, :]
        first = s * _COMBINE_SUB * SLAB
        for j in range(SLAB):
            lo_acc = hi_acc = None
            for kk in range(TOP_K):
                lo, hi = _unpack_words(y_refs[kk][pl.ds(first + j, _COMBINE_SUB, stride=SLAB), :])
                gk = gates[:, kk:kk + 1]
                lo_acc = gk * lo if kk == 0 else lo_acc + gk * lo
                hi_acc = gk * hi if kk == 0 else hi_acc + gk * hi
            moe_ref[rs, j * LANES:(j + 1) * LANES] = lo_acc
            moe_ref[rs, half + j * LANES:half + (j + 1) * LANES] = hi_acc
    hb = h.astype(BF16)
    act = _silu(_dot(hb, sg_ref[...])) * _dot(hb, su_ref[...])
    y = ALPHA * h + _dot(act.astype(BF16), sd_ref[...]) + moe_ref[...]
    out = _layer_norm(y, g_ref[...], b_ref[...])
    of_ref[...] = out
    ob_ref[...] = out.astype(BF16)


def _combine(h, gates_t, y_slots, sg, su, sd, g, b, tm=COMBINE_TILE):
    t, d = h.shape
    n = t // tm
    f = sg.shape[1]
    vec = pl.BlockSpec((1, d), lambda i: (0, 0))
    row = pl.BlockSpec((tm, d), lambda i: (i, 0))
    return pl.pallas_call(
        _combine_kernel,
        grid=(n,),
        in_specs=[row, pl.BlockSpec((tm, TOP_K), lambda i: (i, 0))]
        + [pl.BlockSpec((tm * SLAB, LANES), lambda i, kk=kk: (kk * n + i, 0)) for kk in range(TOP_K)]
        + [pl.BlockSpec((d, f), lambda i: (0, 0)),
            pl.BlockSpec((d, f), lambda i: (0, 0)),
            pl.BlockSpec((f, d), lambda i: (0, 0)),
            vec, vec],
        out_specs=[row, row],
        out_shape=[jax.ShapeDtypeStruct((t, d), F32), jax.ShapeDtypeStruct((t, d), BF16)],
        scratch_shapes=[pltpu.VMEM((tm, d), F32)],
        compiler_params=pltpu.CompilerParams(dimension_semantics=("parallel",),
                                             vmem_limit_bytes=VMEM_LIMIT),
        name="combine_ln2",
    )(h, gates_t, *([y_slots] * TOP_K), sg, su, sd, g.reshape(1, d), b.reshape(1, d))


def _moe(h, h_slab, router_w, router_bias, wg, wu, wd, layer, sg, su, sd, ln_g, ln_b):
    t, d = h.shape
    tk = t * TOP_K
    n_groups = TOKEN_GROUPS if t % (TOKEN_GROUPS * ROUTER_TILE) == 0 else 1
    tg = t // n_groups
    n_buckets = n_groups * N_EXPERTS
    n_blocks = tk // MOE_ROWS + n_buckets
    eidx, gates, counts = _router(h, router_w, router_bias, n_groups)
    counts = counts.reshape(n_buckets)
    n_rows = n_blocks * MOE_ROWS
    padded = (counts + MOE_ROWS - 1) // MOE_ROWS * MOE_ROWS
    pend = jnp.cumsum(padded)
    pad_end = jnp.cumsum(padded - counts)
    pad_key = jnp.sum(pad_end[None, :] <= jnp.arange(n_rows - tk, dtype=jnp.int32)[:, None], axis=1)
    slot_group = (jnp.arange(tk, dtype=jnp.int32) >> _TOP_K_BITS) // tg
    keys = jnp.concatenate([slot_group * N_EXPERTS + eidx.T.reshape(tk), pad_key.astype(jnp.int32)])
    _, row_slot = lax.sort((keys, jnp.arange(n_rows, dtype=jnp.int32)), num_keys=1, is_stable=True)
    real = row_slot < tk
    row = jnp.arange(n_rows, dtype=jnp.int32)
    row_src = jnp.where(real, ((row_slot >> _TOP_K_BITS) % tg) * SLAB, 0)
    plane_row = (row_slot & (TOP_K - 1)) * t + (row_slot >> _TOP_K_BITS)
    row_dst = jnp.where(real, plane_row, tk + row % (2 * MOE_ROWS)) * SLAB
    block_start = jnp.arange(n_blocks, dtype=jnp.int32) * MOE_ROWS
    block_bucket = jnp.minimum(jnp.sum(pend[None, :] <= block_start[:, None], axis=1), n_buckets - 1).astype(jnp.int32)
    n_used = (pend[-1] // MOE_ROWS).astype(jnp.int32).reshape(1)
    y_slots = _experts(h_slab.reshape(n_groups, tg * SLAB, LANES),
                       row_src.reshape(n_blocks, 1, MOE_ROWS), row_dst.reshape(n_blocks, 1, MOE_ROWS),
                       block_bucket % N_EXPERTS, block_bucket // N_EXPERTS, n_used, wg, wu, wd, layer, tk)
    return _combine(h, gates.T, y_slots, sg, su, sd, ln_g, ln_b, tm=min(COMBINE_TILE, t))


def kernel(x, ln0_gain, ln0_bias, w_in, b_in, hg_lb_logits, sw_sinks, sg_ln_gain, sg_ln_bias, sg_w_s, sg_b_s, mix_gain, w_out, ln1_gain, ln1_bias, router_w, router_bias, exp_w_gate, exp_w_up, exp_w_down, sh_w_gate, sh_w_up, sh_w_down, ln2_gain, ln2_bias):
    b, s, d = x.shape
    t = b * s
    p = jax.nn.softmax(hg_lb_logits.astype(F32), axis=0)
    cs = jnp.cumsum(p, axis=0)
    lbs = cs - cs[0:1]
    a0, a1 = HG_WIDTH, HG_WIDTH + SW_WIDTH

    wg_all, wu_all, wd_all = exp_w_gate.astype(BF16), exp_w_up.astype(BF16), exp_w_down.astype(BF16)
    h, hb = _ln(x.reshape(t, d), ln0_gain, ln0_bias)
    for l in range(DEPTH):
        proj = _in_proj(hb, _regroup_columns(w_in[l]).astype(BF16), _regroup_columns(b_in[l]))
        proj3 = proj.reshape(b, s, D_IN)
        lb = lbs[l]
        o_a = _hgrn(proj3, jnp.log(lb), jnp.log1p(-lb), 1.0 - lb, mix_gain[l, :a0])
        o_b = _swa(proj3, sw_sinks[l], mix_gain[l, a0:a1])
        o_c = _sgu(proj3, sg_ln_gain[l], sg_ln_bias[l], sg_w_s[l], sg_b_s[l], mix_gain[l, a1:])
        h, h_slab = _out_proj(o_a.reshape(t, -1), o_b.reshape(t, -1), o_c.reshape(t, -1),
                              w_out[l].astype(BF16), h, ln1_gain[l], ln1_bias[l])
        h, hb = _moe(h, h_slab, router_w[l], router_bias[l], wg_all, wu_all, wd_all, l,
                     sh_w_gate[l].astype(BF16), sh_w_up[l].astype(BF16), sh_w_down[l].astype(BF16),
                     ln2_gain[l], ln2_bias[l])
    return h.reshape(b, s, d)
```

```python
import functools

import numpy as np
import jax
import jax.numpy as jnp
from jax import lax
from jax.experimental import pallas as pl
from jax.experimental.pallas import tpu as pltpu

F32 = jnp.float32
BF16 = jnp.bfloat16

D_MODEL = 2048
DEPTH = 2
HG_HEADS = 4
HG_DK = 128
HG_WIDTH = HG_HEADS * HG_DK
CHUNK = 128
SW_Q_HEADS = 16
SW_KV_HEADS = 2
SW_HEAD_DIM = 64
SW_WIDTH = SW_Q_HEADS * SW_HEAD_DIM
SW_KV_WIDTH = SW_KV_HEADS * SW_HEAD_DIM
SG_GROUPS = 4
SG_WIDTH = SG_GROUPS * CHUNK
D_IN = 4 * HG_WIDTH + SW_WIDTH + 2 * SW_KV_WIDTH + 2 * SG_WIDTH
N_EXPERTS = 64
TOP_K = 8
D_EXPERT = 512
N_EXPERT_GROUPS = 8
TOPK_GROUPS = 4
ROUTED_SCALE = 2.5
ALPHA = (2 * DEPTH) ** 0.25
LN_EPS = 1e-5
RMS_EPS = 1e-6

_SWQ_OFF = 4 * HG_WIDTH
_SGU_OFF = _SWQ_OFF + SW_WIDTH
_SWK_OFF = _SGU_OFF + 2 * SG_WIDTH


def _regroup_columns(a):
    k0 = _SWQ_OFF + SW_WIDTH
    k1 = k0 + 2 * SW_KV_WIDTH
    return jnp.concatenate([a[..., :k0], a[..., k1:], a[..., k0:k1]], axis=-1)


MOE_ROWS = 256
TOKEN_GROUPS = 2
ROUTER_TILE = 512
COMBINE_TILE = 256
VMEM_LIMIT = 56 * 1024 * 1024

_LEVEL_SIZES = (64, 32, 16, 8, 4, 2, 1)
_N_LEVELS = len(_LEVEL_SIZES)


def _hgrn_constants():
    t = np.arange(CHUNK)[:, None]
    u = np.arange(CHUNK)[None, :]
    mats = []
    level = np.full((CHUNK, CHUNK), -1, np.int32)
    for li, s in enumerate(_LEVEL_SIZES):
        blk = t // s
        odd = (blk % 2) == 1
        m_query = (u >= blk * s) & (u <= t)
        m_key = (u > t) & (u <= (blk + 1) * s - 1)
        mats.append(np.where(odd, m_query, m_key))
        pair = ((t // s) % 2 == 1) & ((u // s) == (t // s) - 1)
        level[pair] = li
    mats.append(u <= t)
    mats.append(u > t)
    level[np.arange(CHUNK), np.arange(CHUNK)] = _N_LEVELS
    return np.concatenate(mats, 0).astype(np.float32), level


_HGRN_MSTACK, _HGRN_LEVEL = _hgrn_constants()


def _layer_norm(x, g, b):
    mu = jnp.mean(x, axis=-1, keepdims=True)
    xc = x - mu
    var = jnp.mean(xc * xc, axis=-1, keepdims=True)
    return xc * lax.rsqrt(var + LN_EPS) * g + b


def _silu(x):
    return x / (1.0 + jnp.exp(-x))


def _gelu(x):
    return 0.5 * x * (1.0 + lax.erf(x * (2.0 ** -0.5)))


def _dot(a, b):
    return jnp.dot(a, b, preferred_element_type=F32)


def _dot_nt(a, b):
    return lax.dot_general(a, b, (((1,), (1,)), ((), ())), preferred_element_type=F32)


def _dot_tn(a, b):
    return lax.dot_general(a, b, (((0,), (0,)), ((), ())), preferred_element_type=F32)


def _ln_kernel(x_ref, g_ref, b_ref, of_ref, ob_ref):
    y = _layer_norm(x_ref[...], g_ref[...], b_ref[...])
    of_ref[...] = y
    ob_ref[...] = y.astype(BF16)


def _ln(x, g, b, tm=256):
    t, d = x.shape
    row = pl.BlockSpec((tm, d), lambda i: (i, 0))
    vec = pl.BlockSpec((1, d), lambda i: (0, 0))
    return pl.pallas_call(
        _ln_kernel,
        grid=(t // tm,),
        in_specs=[row, vec, vec],
        out_specs=[row, row],
        out_shape=[jax.ShapeDtypeStruct((t, d), F32), jax.ShapeDtypeStruct((t, d), BF16)],
        name="ln0",
    )(x, g.reshape(1, d), b.reshape(1, d))


def _mm_bias_kernel(a_ref, w_ref, b_ref, o_ref):
    o_ref[...] = _dot(a_ref[...], w_ref[...]) + b_ref[...]


def _in_proj(a, w, b, tm=2048, tn=256):
    t, k = a.shape
    n = w.shape[1]
    tm = min(tm, t)
    return pl.pallas_call(
        _mm_bias_kernel,
        grid=(t // tm, n // tn),
        in_specs=[pl.BlockSpec((tm, k), lambda i, j: (i, 0)),
                  pl.BlockSpec((k, tn), lambda i, j: (0, j)),
                  pl.BlockSpec((1, tn), lambda i, j: (0, j))],
        out_specs=pl.BlockSpec((tm, tn), lambda i, j: (i, j)),
        out_shape=jax.ShapeDtypeStruct((t, n), F32),
        compiler_params=pltpu.CompilerParams(vmem_limit_bytes=VMEM_LIMIT),
        name="in_proj",
    )(a, w, b.reshape(1, n))


def _hgrn_kernel(q_ref, f_ref, i_ref, g_ref, llb_ref, l1lb_ref, oml_ref, gain_ref,
                 mstack_ref, level_ref, o_ref, state_ref):
    c = pl.program_id(1)

    @pl.when(c == 0)
    def _():
        state_ref[...] = jnp.zeros_like(state_ref)

    q = _silu(q_ref[...])
    z = f_ref[...]
    ez = jnp.exp(-jnp.abs(z))
    log_sig = jnp.minimum(z, 0.0) - jnp.log(1.0 + ez)
    sig_neg = jnp.where(z >= 0.0, ez, 1.0) / (1.0 + ez)
    k = oml_ref[...] * sig_neg
    a = llb_ref[...]
    bb = l1lb_ref[...] + log_sig
    log_f = jnp.maximum(a, bb) + jnp.log(1.0 + jnp.exp(-jnp.abs(a - bb)))

    lf_hi = log_f.astype(BF16)
    lf_lo = (log_f - lf_hi.astype(F32)).astype(BF16)
    m = mstack_ref[...]
    sums = _dot(m, lf_hi) + _dot(m, lf_lo)
    decay = jnp.exp(sums)

    rows = lax.broadcasted_iota(jnp.int32, (CHUNK, 1), 0)
    factors = []
    for li, s in enumerate(_LEVEL_SIZES):
        is_query = ((rows // s) % 2) == 1
        factors.append((jnp.where(is_query, q, k) * decay[li * CHUNK:(li + 1) * CHUNK]).astype(BF16))
    cum = _N_LEVELS * CHUNK
    q_in = (q * decay[cum:cum + CHUNK]).astype(BF16)
    k_dec = (k * decay[cum + CHUNK:cum + 2 * CHUNK]).astype(BF16)
    end_decay = decay[cum + CHUNK - 1:cum + CHUNK]
    qb = q.astype(BF16)
    kb = k.astype(BF16)
    vb = i_ref[...].astype(BF16)
    level = level_ref[...]
    gate = _silu(g_ref[...])
    gain = gain_ref[...]

    for h in range(HG_HEADS):
        hs = slice(h * HG_DK, (h + 1) * HG_DK)
        scores = jnp.zeros((CHUNK, CHUNK), F32)
        for li in range(_N_LEVELS):
            fl = factors[li][:, hs]
            scores = jnp.where(level == li, _dot_nt(fl, fl), scores)
        scores = jnp.where(level == _N_LEVELS, _dot_nt(qb[:, hs], kb[:, hs]), scores)
        st = state_ref[h]
        o = _dot(scores.astype(BF16), vb[:, hs]) + _dot_nt(q_in[:, hs], st.astype(BF16))
        state_ref[h] = st * end_decay[:, hs] + _dot_tn(vb[:, hs], k_dec[:, hs])
        ms = jnp.mean(o * o, axis=-1, keepdims=True)
        o = o * lax.rsqrt(ms + RMS_EPS) * gain[:, hs] * gate[:, hs]
        o_ref[:, hs] = o.astype(o_ref.dtype)


def _hgrn(proj3, llb, l1lb, oml, gain):
    b, s, _ = proj3.shape
    w = HG_WIDTH

    def col(j):
        return pl.BlockSpec((None, CHUNK, w), lambda bi, ci, j=j: (bi, ci, j))

    vec = pl.BlockSpec((1, w), lambda bi, ci: (0, 0))
    nm = _HGRN_MSTACK.shape[0]
    return pl.pallas_call(
        _hgrn_kernel,
        grid=(b, s // CHUNK),
        in_specs=[col(0), col(1), col(2), col(3), vec, vec, vec, vec,
                  pl.BlockSpec((nm, CHUNK), lambda bi, ci: (0, 0)),
                  pl.BlockSpec((CHUNK, CHUNK), lambda bi, ci: (0, 0))],
        out_specs=pl.BlockSpec((None, CHUNK, w), lambda bi, ci: (bi, ci, 0)),
        out_shape=jax.ShapeDtypeStruct((b, s, w), BF16),
        scratch_shapes=[pltpu.VMEM((HG_HEADS, HG_DK, HG_DK), F32)],
        compiler_params=pltpu.CompilerParams(dimension_semantics=("parallel", "arbitrary"),
                                             vmem_limit_bytes=VMEM_LIMIT),
        name="hgrn2",
    )(proj3, proj3, proj3, proj3, llb.reshape(1, w), l1lb.reshape(1, w), oml.reshape(1, w),
      gain.reshape(1, w), jnp.asarray(_HGRN_MSTACK, BF16), jnp.asarray(_HGRN_LEVEL))


def _swa_kernel(q_ref, kp_ref, kc_ref, vp_ref, vc_ref, sink_ref, gain_ref, o_ref):
    n = pl.program_id(1)
    g = SW_Q_HEADS // SW_KV_HEADS
    hd = SW_HEAD_DIM
    q = q_ref[...]
    kband = jnp.concatenate([kp_ref[...], kc_ref[...]], axis=0).astype(BF16)
    vband = jnp.concatenate([vp_ref[...], vc_ref[...]], axis=0).astype(BF16)
    t = lax.broadcasted_iota(jnp.int32, (g * CHUNK, 2 * CHUNK), 0) % CHUNK
    s = lax.broadcasted_iota(jnp.int32, (g * CHUNK, 2 * CHUNK), 1)
    rel = t + CHUNK - s
    mask = (rel >= 0) & (rel < CHUNK) & ((s >= CHUNK) | (n > 0))
    outs = []
    for kv in range(SW_KV_HEADS):
        qs = jnp.concatenate([q[:, (kv * g + j) * hd:(kv * g + j + 1) * hd] for j in range(g)], axis=0)
        sc = _dot_nt(qs.astype(BF16), kband[:, kv * hd:(kv + 1) * hd]) * (hd ** -0.5)
        sc = jnp.where(mask, sc, -jnp.inf)
        sink = sink_ref[kv]
        mx = jnp.maximum(jnp.max(sc, axis=-1, keepdims=True), sink)
        p = jnp.exp(sc - mx)
        denom = jnp.sum(p, axis=-1, keepdims=True) + jnp.exp(sink - mx)
        o = _dot((p / denom).astype(BF16), vband[:, kv * hd:(kv + 1) * hd])
        outs.extend(o[j * CHUNK:(j + 1) * CHUNK] for j in range(g))
    o = jnp.concatenate(outs, axis=-1)
    ms = jnp.mean(o * o, axis=-1, keepdims=True)
    o_ref[...] = (o * lax.rsqrt(ms + RMS_EPS) * gain_ref[...]).astype(o_ref.dtype)


def _swa(proj3, sinks, gain):
    b, s, _ = proj3.shape
    q_col = _SWQ_OFF // SW_WIDTH
    k_col = _SWK_OFF // SW_KV_WIDTH
    v_col = k_col + 1
    g = SW_Q_HEADS // SW_KV_HEADS
    sink_rows = jnp.repeat(sinks.astype(F32).reshape(SW_KV_HEADS, g), CHUNK, axis=1)[..., None]

    def kv_spec(col, prev):
        if prev:
            return pl.BlockSpec((None, CHUNK, SW_KV_WIDTH), lambda bi, ni: (bi, jnp.maximum(ni - 1, 0), col))
        return pl.BlockSpec((None, CHUNK, SW_KV_WIDTH), lambda bi, ni: (bi, ni, col))

    return pl.pallas_call(
        _swa_kernel,
        grid=(b, s // CHUNK),
        in_specs=[pl.BlockSpec((None, CHUNK, SW_WIDTH), lambda bi, ni: (bi, ni, q_col)),
                  kv_spec(k_col, True), kv_spec(k_col, False),
                  kv_spec(v_col, True), kv_spec(v_col, False),
                  pl.BlockSpec((SW_KV_HEADS, g * CHUNK, 1), lambda bi, ni: (0, 0, 0)),
                  pl.BlockSpec((1, SW_WIDTH), lambda bi, ni: (0, 0))],
        out_specs=pl.BlockSpec((None, CHUNK, SW_WIDTH), lambda bi, ni: (bi, ni, 0)),
        out_shape=jax.ShapeDtypeStruct((b, s, SW_WIDTH), BF16),
        compiler_params=pltpu.CompilerParams(dimension_semantics=("parallel", "parallel"),
                                             vmem_limit_bytes=VMEM_LIMIT),
        name="swa",
    )(proj3, proj3, proj3, proj3, proj3, sink_rows, gain.reshape(1, SW_WIDTH))


def _sgu_kernel(u_ref, v_ref, lng_ref, lnb_ref, w_ref, bs_ref, gain_ref, o_ref):
    u = _gelu(u_ref[...])
    v = _layer_norm(_gelu(v_ref[...]), lng_ref[...], lnb_ref[...]).astype(BF16)
    r = lax.broadcasted_iota(jnp.int32, (CHUNK, CHUNK), 0)
    c = lax.broadcasted_iota(jnp.int32, (CHUNK, CHUNK), 1)
    tril = c <= r
    bs = bs_ref[...]
    parts = []
    for gi in range(SG_GROUPS):
        gs = slice(gi * CHUNK, (gi + 1) * CHUNK)
        w = jnp.where(tril, w_ref[gi], 0.0).astype(BF16)
        parts.append(_dot(w, v[:, gs]) + bs[:, gi:gi + 1])
    o = u * jnp.concatenate(parts, axis=-1)
    ms = jnp.mean(o * o, axis=-1, keepdims=True)
    o_ref[...] = (o * lax.rsqrt(ms + RMS_EPS) * gain_ref[...]).astype(o_ref.dtype)


def _sgu(proj3, ln_g, ln_b, w_s, b_s, gain):
    b, s, _ = proj3.shape
    w = SG_WIDTH
    u_col = _SGU_OFF // SG_WIDTH
    vec = pl.BlockSpec((1, w), lambda bi, ci: (0, 0))
    return pl.pallas_call(
        _sgu_kernel,
        grid=(b, s // CHUNK),
        in_specs=[pl.BlockSpec((None, CHUNK, w), lambda bi, ci: (bi, ci, u_col)),
                  pl.BlockSpec((None, CHUNK, w), lambda bi, ci: (bi, ci, u_col + 1)),
                  vec, vec,
                  pl.BlockSpec((SG_GROUPS, CHUNK, CHUNK), lambda bi, ci: (0, 0, 0)),
                  pl.BlockSpec((CHUNK, SG_GROUPS), lambda bi, ci: (0, 0)),
                  vec],
        out_specs=pl.BlockSpec((None, CHUNK, w), lambda bi, ci: (bi, ci, 0)),
        out_shape=jax.ShapeDtypeStruct((b, s, w), BF16),
        compiler_params=pltpu.CompilerParams(dimension_semantics=("parallel", "parallel"),
                                             vmem_limit_bytes=VMEM_LIMIT),
        name="sgu",
    )(proj3, proj3, ln_g.reshape(1, w), ln_b.reshape(1, w), w_s, b_s.T, gain.reshape(1, w))


LANES = 128
SLAB = D_MODEL // 2 // LANES


def _pack_pair(low, high):
    def bits(v):
        return lax.bitcast_convert_type(v.astype(BF16).astype(F32), jnp.uint32)

    return (bits(high) & jnp.uint32(0xFFFF0000)) | (bits(low) >> 16)


def _pack_rows(x):
    half = x.shape[1] // 2
    return _pack_pair(x[:, :half], x[:, half:])


def _unpack_words(w):
    return (lax.bitcast_convert_type(w << 16, F32),
            lax.bitcast_convert_type(w & jnp.uint32(0xFFFF0000), F32))


def _store_slabs(ref, x):
    n = x.shape[0]
    for j in range(SLAB):
        ref[pl.ds(j, n, stride=SLAB), :] = x[:, j * LANES:(j + 1) * LANES]


def _load_packed_rows(ref, n, first=0, stride=SLAB):
    lows, highs = [], []
    for j in range(SLAB):
        lo, hi = _unpack_words(ref[pl.ds(first + j, n, stride=stride), :])
        lows.append(lo.astype(BF16))
        highs.append(hi.astype(BF16))
    return jnp.concatenate(lows + highs, axis=1)


def _out_proj_kernel(oa_ref, ob_ref, oc_ref, w_ref, h_ref, g_ref, b_ref, of_ref, os_ref):
    mixed = jnp.concatenate([oa_ref[...], ob_ref[...], oc_ref[...]], axis=-1)
    y = ALPHA * h_ref[...] + _dot(mixed, w_ref[...])
    out = _layer_norm(y, g_ref[...], b_ref[...])
    of_ref[...] = out
    _store_slabs(os_ref, _pack_rows(out))


def _out_proj(oa, ob, oc, w, h, g, b, tm=256):
    t, d = h.shape
    vec = pl.BlockSpec((1, d), lambda i: (0, 0))
    return pl.pallas_call(
        _out_proj_kernel,
        grid=(t // tm,),
        in_specs=[pl.BlockSpec((tm, oa.shape[1]), lambda i: (i, 0)),
                  pl.BlockSpec((tm, ob.shape[1]), lambda i: (i, 0)),
                  pl.BlockSpec((tm, oc.shape[1]), lambda i: (i, 0)),
                  pl.BlockSpec(w.shape, lambda i: (0, 0)),
                  pl.BlockSpec((tm, d), lambda i: (i, 0)), vec, vec],
        out_specs=[pl.BlockSpec((tm, d), lambda i: (i, 0)),
                   pl.BlockSpec((tm * SLAB, LANES), lambda i: (i, 0))],
        out_shape=[jax.ShapeDtypeStruct((t, d), F32), jax.ShapeDtypeStruct((t * SLAB, LANES), jnp.uint32)],
        compiler_params=pltpu.CompilerParams(vmem_limit_bytes=VMEM_LIMIT),
        name="out_proj_ln1",
    )(oa, ob, oc, w, h, g.reshape(1, d), b.reshape(1, d))


def _first_index_of_max(x, iota, size, axis):
    mx = jnp.max(x, axis=axis, keepdims=True)
    idx = jnp.min(jnp.where(x == mx, iota, size), axis=axis, keepdims=True)
    return mx, idx


def _router_kernel(tiles_per_group, h_ref, wt_ref, bias_ref, eidx_ref, gate_ref, count_ref, run_ref):
    i = pl.program_id(0)

    @pl.when(i % tiles_per_group == 0)
    def _():
        run_ref[...] = jnp.zeros_like(run_ref)

    tm = h_ref.shape[0]
    per_group = N_EXPERTS // N_EXPERT_GROUPS
    def split(v):
        hi = v.astype(BF16)
        return hi, (v - hi.astype(F32)).astype(BF16)

    w_hi, w_lo = split(wt_ref[...])
    h_hi, h_lo = split(h_ref[...])
    logits = _dot_nt(w_hi, h_hi) + (_dot_nt(w_hi, h_lo) + _dot_nt(w_lo, h_hi))
    scores = 1.0 / (1.0 + jnp.exp(-logits))
    sel = scores + bias_ref[...]
    sel3 = sel.reshape(N_EXPERT_GROUPS, per_group, tm)
    io3 = lax.broadcasted_iota(jnp.int32, sel3.shape, 1)
    m1, i1 = _first_index_of_max(sel3, io3, per_group, 1)
    m2 = jnp.max(jnp.where(io3 == i1, -jnp.inf, sel3), axis=1, keepdims=True)
    grp = (m1 + m2).reshape(N_EXPERT_GROUPS, tm)
    iog = lax.broadcasted_iota(jnp.int32, grp.shape, 0)
    keep = jnp.zeros(grp.shape, jnp.bool_)
    for _ in range(TOPK_GROUPS):
        _, gi = _first_index_of_max(grp, iog, N_EXPERT_GROUPS, 0)
        hit = iog == gi
        keep = keep | hit
        grp = jnp.where(hit, -jnp.inf, grp)
    keep3 = jnp.broadcast_to(keep.reshape(N_EXPERT_GROUPS, 1, tm), sel3.shape)
    cand = jnp.where(keep3, sel3, -jnp.inf).reshape(N_EXPERTS, tm)
    ioe = lax.broadcasted_iota(jnp.int32, cand.shape, 0)
    chosen = jnp.zeros(cand.shape, F32)
    idxs, gvals = [], []
    for _ in range(TOP_K):
        _, ei = _first_index_of_max(cand, ioe, N_EXPERTS, 0)
        hit = ioe == ei
        idxs.append(ei)
        gvals.append(jnp.sum(jnp.where(hit, scores, 0.0), axis=0, keepdims=True))
        chosen = jnp.where(hit, 1.0, chosen)
        cand = jnp.where(hit, -jnp.inf, cand)
    gsum = functools.reduce(lambda a, b: a + b, gvals)
    for kk in range(TOP_K):
        eidx_ref[kk:kk + 1, :] = idxs[kk]
        gate_ref[kk:kk + 1, :] = gvals[kk] / gsum * ROUTED_SCALE
    run = run_ref[...] + jnp.sum(chosen, axis=1, keepdims=True)
    run_ref[...] = run
    count_ref[...] = run.astype(jnp.int32)


def _router(h, router_w, router_bias, n_groups, tm=ROUTER_TILE):
    t, d = h.shape
    tm = min(tm, t // n_groups)
    tiles_per_group = t // n_groups // tm
    slot = pl.BlockSpec((TOP_K, tm), lambda i: (0, i))
    return pl.pallas_call(
        functools.partial(_router_kernel, tiles_per_group),
        grid=(t // tm,),
        in_specs=[pl.BlockSpec((tm, d), lambda i: (i, 0)),
                  pl.BlockSpec((N_EXPERTS, d), lambda i: (0, 0)),
                  pl.BlockSpec((N_EXPERTS, 1), lambda i: (0, 0))],
        out_specs=[slot, slot, pl.BlockSpec((None, N_EXPERTS, 1), lambda i: (i // tiles_per_group, 0, 0))],
        out_shape=[jax.ShapeDtypeStruct((TOP_K, t), jnp.int32),
                   jax.ShapeDtypeStruct((TOP_K, t), F32),
                   jax.ShapeDtypeStruct((n_groups, N_EXPERTS, 1), jnp.int32)],
        scratch_shapes=[pltpu.VMEM((N_EXPERTS, 1), F32)],
        compiler_params=pltpu.CompilerParams(dimension_semantics=("arbitrary",),
                                             vmem_limit_bytes=VMEM_LIMIT),
        name="router",
    )(h, router_w.T, router_bias.reshape(N_EXPERTS, 1))


_ISSUE_UNROLL = 8
_TOP_K_BITS = TOP_K.bit_length() - 1
assert 1 << _TOP_K_BITS == TOP_K


def _experts_kernel(layer, be_ref, bg_ref, wplan_ref, nu_ref, src_ref, dstp_ref, dstc_ref, h_hbm,
                    wg_hbm, wu_hbm, wd_hbm, y_hbm, tok_ref, xbuf, obuf0, obuf1, wg_buf, wu_buf, wd_buf,
                    lsem, ssem, wsem):
    b = pl.program_id(0)
    nu = nu_ref[0]
    obuf = (obuf0, obuf1)
    rows = xbuf.shape[0] // SLAB
    d = wg_buf.shape[1]
    f = wg_buf.shape[2]
    wslot = wplan_ref[1, b]

    def weight_copies(expert, slot_):
        return [pltpu.make_async_copy(src.at[layer, expert], dst.at[slot_], wsem.at[slot_])
                for src, dst in ((wg_hbm, wg_buf), (wu_hbm, wu_buf), (wd_hbm, wd_buf))]

    @pl.when((b < nu) & (wplan_ref[0, b] == 1))
    def _():
        @pl.when(b == 0)
        def _():
            for c in weight_copies(be_ref[0], wslot):
                c.start()
        for c in weight_copies(be_ref[b], wslot):
            c.wait()
        nxt = wplan_ref[2, b]

        @pl.when(nxt >= 0)
        def _():
            for c in weight_copies(nxt, 1 - wslot):
                c.start()

    wg_ref, wu_ref, wd_ref = wg_buf.at[wslot], wu_buf.at[wslot], wd_buf.at[wslot]
    pad0 = y_hbm.shape[0] - 2 * rows * SLAB

    def scatter_row(dst_ref, r, src_slot, priority):
        dst = pl.multiple_of(dst_ref[0, 0, r], SLAB)
        buf_rows = pl.ds(pl.multiple_of(r * SLAB, SLAB), SLAB)
        pltpu.make_async_copy(obuf[src_slot].at[buf_rows], y_hbm.at[pl.ds(dst, SLAB)],
                              ssem.at[src_slot]).start(priority=priority)

    def wait_scatter(s):
        pltpu.make_async_copy(obuf[s], y_hbm.at[pl.ds(0, rows * SLAB)], ssem.at[s]).wait()

    @pl.when(b == 0)
    def _():
        for s in range(2):
            obuf[s][...] = jnp.zeros_like(obuf[s])
            pltpu.make_async_copy(obuf[s], y_hbm.at[pl.ds(pad0 + s * rows * SLAB, rows * SLAB)],
                                  ssem.at[s]).start()
        for s in range(2):
            wait_scatter(s)

    group = bg_ref[b]

    @pl.when((b < nu) & ((b == 0) | (group != bg_ref[jnp.maximum(b - 1, 0)])))
    def _():
        load = pltpu.make_async_copy(h_hbm.at[group], tok_ref, lsem.at[0])
        load.start()
        load.wait()

    def block(slot, with_scatter):
        other = 1 - slot

        if with_scatter:
            def issue(i, carry):
                for u in range(_ISSUE_UNROLL):
                    scatter_row(dstp_ref, i * _ISSUE_UNROLL + u, other, u % 2)
                return carry
            lax.fori_loop(0, rows // _ISSUE_UNROLL, issue, 0)

        def gather(i, carry):
            for u in range(_ISSUE_UNROLL):
                r = i * _ISSUE_UNROLL + u
                src = pl.multiple_of(src_ref[0, 0, r], SLAB)
                xbuf[pl.ds(pl.multiple_of(r * SLAB, SLAB), SLAB), :] = tok_ref[pl.ds(src, SLAB), :]
            return carry
        lax.fori_loop(0, rows // _ISSUE_UNROLL, gather, 0)

        x = _load_packed_rows(xbuf, rows)
        acts = []
        fh = f // 2
        for j in range(2):
            gate = _dot(x, wg_ref[:, j * fh:(j + 1) * fh])
            up = _dot(x, wu_ref[:, j * fh:(j + 1) * fh])
            acts.append((_silu(gate) * up).astype(BF16))
        act = jnp.concatenate(acts, axis=1)
        half = d // 2
        cw = 2 * LANES
        for q in range(half // cw):
            words = _pack_pair(_dot(act, wd_ref[:, q * cw:(q + 1) * cw]),
                               _dot(act, wd_ref[:, half + q * cw:half + (q + 1) * cw]))
            for jj in range(cw // LANES):
                obuf[slot][pl.ds(q * (cw // LANES) + jj, rows, stride=SLAB), :] = words[:, jj * LANES:(jj + 1) * LANES]

    def tail(slot):
        other = 1 - slot

        @pl.when(b >= 1)
        def _():
            wait_scatter(other)

        def body(r, carry):
            scatter_row(dstc_ref, r, slot, 0)
            return carry
        lax.fori_loop(0, rows, body, 0)
        wait_scatter(slot)

    @pl.when(b == 0)
    def _():
        block(0, False)

    for parity in range(2):
        @pl.when((b >= 1) & (b < nu) & (b % 2 == parity))
        def _(parity=parity):
            @pl.when(b >= 2)
            def _():
                wait_scatter(parity)
            block(parity, True)

        @pl.when((b == nu - 1) & (b % 2 == parity))
        def _(parity=parity):
            tail(parity)


def _experts(h_groups, row_src, row_dst, block_e, block_g, weight_plan, n_used, wg, wu, wd, layer, n_slots):
    nb, _, rows = row_src.shape
    d, f = wg.shape[-2], wg.shape[-1]
    group_rows = h_groups.shape[1]

    def smem(shift):
        def index(b, be, bg, wplan, nu):
            return (jnp.maximum(jnp.minimum(b + shift, nu[0] - 1), 0), 0, 0)
        return pl.BlockSpec((1, 1, rows), index, memory_space=pltpu.SMEM)

    hbm = pl.BlockSpec(memory_space=pl.ANY)
    grid_spec = pltpu.PrefetchScalarGridSpec(
        num_scalar_prefetch=4,
        grid=(nb,),
        in_specs=[smem(0), smem(-1), smem(0), hbm, hbm, hbm, hbm],
        out_specs=hbm,
        scratch_shapes=[pltpu.VMEM((group_rows, LANES), jnp.uint32)]
        + [pltpu.VMEM((rows * SLAB, LANES), jnp.uint32)] * 3
        + [pltpu.VMEM((2, d, f), BF16), pltpu.VMEM((2, d, f), BF16), pltpu.VMEM((2, f, d), BF16)]
        + [pltpu.SemaphoreType.DMA((1,)), pltpu.SemaphoreType.DMA((2,)), pltpu.SemaphoreType.DMA((2,))],
    )
    return pl.pallas_call(
        functools.partial(_experts_kernel, layer),
        grid_spec=grid_spec,
        out_shape=jax.ShapeDtypeStruct(((n_slots + 2 * rows) * SLAB, LANES), jnp.uint32),
        compiler_params=pltpu.CompilerParams(dimension_semantics=("arbitrary",),
                                             vmem_limit_bytes=VMEM_LIMIT),
        name="experts",
    )(block_e, block_g, weight_plan, n_used, row_src, row_dst, row_dst, h_groups, wg, wu, wd)


_COMBINE_SUB = 32


def _combine_kernel(h_ref, gate_ref, *refs):
    y_refs = refs[:TOP_K]
    sg_ref, su_ref, sd_ref, g_ref, b_ref, of_ref, ob_ref, moe_ref = refs[TOP_K:]
    h = h_ref[...]
    tm = h.shape[0]
    half = SLAB * LANES
    for s in range(tm // _COMBINE_SUB):
        rs = slice(s * _COMBINE_SUB, (s + 1) * _COMBINE_SUB)
        gates = gate_ref[rs, :]
        first = s * _COMBINE_SUB * SLAB
        for j in range(SLAB):
            lo_acc = hi_acc = None
            for kk in range(TOP_K):
                lo, hi = _unpack_words(y_refs[kk][pl.ds(first + j, _COMBINE_SUB, stride=SLAB), :])
                gk = gates[:, kk:kk + 1]
                lo_acc = gk * lo if kk == 0 else lo_acc + gk * lo
                hi_acc = gk * hi if kk == 0 else hi_acc + gk * hi
            moe_ref[rs, j * LANES:(j + 1) * LANES] = lo_acc
            moe_ref[rs, half + j * LANES:half + (j + 1) * LANES] = hi_acc
    hb = h.astype(BF16)
    act = _silu(_dot(hb, sg_ref[...])) * _dot(hb, su_ref[...])
    y = ALPHA * h + _dot(act.astype(BF16), sd_ref[...]) + moe_ref[...]
    out = _layer_norm(y, g_ref[...], b_ref[...])
    of_ref[...] = out
    ob_ref[...] = out.astype(BF16)


def _combine(h, gates_t, y_slots, sg, su, sd, g, b, tm=COMBINE_TILE):
    t, d = h.shape
    n = t // tm
    f = sg.shape[1]
    vec = pl.BlockSpec((1, d), lambda i: (0, 0))
    row = pl.BlockSpec((tm, d), lambda i: (i, 0))
    return pl.pallas_call(
        _combine_kernel,
        grid=(n,),
        in_specs=[row, pl.BlockSpec((tm, TOP_K), lambda i: (i, 0))]
        + [pl.BlockSpec((tm * SLAB, LANES), lambda i, kk=kk: (kk * n + i, 0)) for kk in range(TOP_K)]
        + [pl.BlockSpec((d, f), lambda i: (0, 0)),
            pl.BlockSpec((d, f), lambda i: (0, 0)),
            pl.BlockSpec((f, d), lambda i: (0, 0)),
            vec, vec],
        out_specs=[row, row],
        out_shape=[jax.ShapeDtypeStruct((t, d), F32), jax.ShapeDtypeStruct((t, d), BF16)],
        scratch_shapes=[pltpu.VMEM((tm, d), F32)],
        compiler_params=pltpu.CompilerParams(dimension_semantics=("parallel",),
                                             vmem_limit_bytes=VMEM_LIMIT),
        name="combine_ln2",
    )(h, gates_t, *([y_slots] * TOP_K), sg, su, sd, g.reshape(1, d), b.reshape(1, d))


def _moe(h, h_slab, router_w, router_bias, wg, wu, wd, layer, sg, su, sd, ln_g, ln_b):
    t, d = h.shape
    tk = t * TOP_K
    n_groups = TOKEN_GROUPS if t % (TOKEN_GROUPS * ROUTER_TILE) == 0 else 1
    tg = t // n_groups
    n_buckets = n_groups * N_EXPERTS
    n_blocks = tk // MOE_ROWS + n_buckets
    eidx, gates, counts = _router(h, router_w, router_bias, n_groups)
    counts = counts.reshape(n_buckets)
    n_rows = n_blocks * MOE_ROWS
    padded = (counts + MOE_ROWS - 1) // MOE_ROWS * MOE_ROWS
    pend = jnp.cumsum(padded)
    pad_end = jnp.cumsum(padded - counts)
    pad_key = jnp.sum(pad_end[None, :] <= jnp.arange(n_rows - tk, dtype=jnp.int32)[:, None], axis=1)
    slot_group = (jnp.arange(tk, dtype=jnp.int32) >> _TOP_K_BITS) // tg
    keys = jnp.concatenate([slot_group * N_EXPERTS + eidx.T.reshape(tk), pad_key.astype(jnp.int32)])
    _, row_slot = lax.sort((keys, jnp.arange(n_rows, dtype=jnp.int32)), num_keys=1, is_stable=True)
    real = row_slot < tk
    row = jnp.arange(n_rows, dtype=jnp.int32)
    row_src = jnp.where(real, ((row_slot >> _TOP_K_BITS) % tg) * SLAB, 0)
    plane_row = (row_slot & (TOP_K - 1)) * t + (row_slot >> _TOP_K_BITS)
    row_dst = jnp.where(real, plane_row, tk + row % (2 * MOE_ROWS)) * SLAB
    block_start = jnp.arange(n_blocks, dtype=jnp.int32) * MOE_ROWS
    block_bucket = jnp.minimum(jnp.sum(pend[None, :] <= block_start[:, None], axis=1), n_buckets - 1).astype(jnp.int32)
    n_used = (pend[-1] // MOE_ROWS).astype(jnp.int32)
    block_e = block_bucket % N_EXPERTS
    block = jnp.arange(n_blocks, dtype=jnp.int32)
    first = (block == 0) | (block_bucket != jnp.roll(block_bucket, 1))
    seq = jnp.cumsum(first.astype(jnp.int32)) - 1
    later_first = jnp.where(first & (block < n_used), block, n_blocks)
    next_first = jnp.concatenate([lax.cummin(later_first, reverse=True)[1:], jnp.full((1,), n_blocks, jnp.int32)])
    next_e = jnp.where(next_first < n_blocks, block_e[jnp.minimum(next_first, n_blocks - 1)], -1)
    weight_plan = jnp.stack([first.astype(jnp.int32), seq % 2, next_e]).astype(jnp.int32)
    y_slots = _experts(h_slab.reshape(n_groups, tg * SLAB, LANES),
                       row_src.reshape(n_blocks, 1, MOE_ROWS), row_dst.reshape(n_blocks, 1, MOE_ROWS),
                       block_e, block_bucket // N_EXPERTS, weight_plan, n_used.reshape(1),
                       wg, wu, wd, layer, tk)
    return _combine(h, gates.T, y_slots, sg, su, sd, ln_g, ln_b, tm=min(COMBINE_TILE, t))


def kernel(x, ln0_gain, ln0_bias, w_in, b_in, hg_lb_logits, sw_sinks, sg_ln_gain, sg_ln_bias, sg_w_s, sg_b_s, mix_gain, w_out, ln1_gain, ln1_bias, router_w, router_bias, exp_w_gate, exp_w_up, exp_w_down, sh_w_gate, sh_w_up, sh_w_down, ln2_gain, ln2_bias):
    b, s, d = x.shape
    t = b * s
    p = jax.nn.softmax(hg_lb_logits.astype(F32), axis=0)
    cs = jnp.cumsum(p, axis=0)
    lbs = cs - cs[0:1]
    a0, a1 = HG_WIDTH, HG_WIDTH + SW_WIDTH

    wg_all, wu_all, wd_all = exp_w_gate.astype(BF16), exp_w_up.astype(BF16), exp_w_down.astype(BF16)
    h, hb = _ln(x.reshape(t, d), ln0_gain, ln0_bias)
    for l in range(DEPTH):
        proj = _in_proj(hb, _regroup_columns(w_in[l]).astype(BF16), _regroup_columns(b_in[l]))
        proj3 = proj.reshape(b, s, D_IN)
        lb = lbs[l]
        o_a = _hgrn(proj3, jnp.log(lb), jnp.log1p(-lb), 1.0 - lb, mix_gain[l, :a0])
        o_b = _swa(proj3, sw_sinks[l], mix_gain[l, a0:a1])
        o_c = _sgu(proj3, sg_ln_gain[l], sg_ln_bias[l], sg_w_s[l], sg_b_s[l], mix_gain[l, a1:])
        h, h_slab = _out_proj(o_a.reshape(t, -1), o_b.reshape(t, -1), o_c.reshape(t, -1),
                              w_out[l].astype(BF16), h, ln1_gain[l], ln1_bias[l])
        h, hb = _moe(h, h_slab, router_w[l], router_bias[l], wg_all, wu_all, wd_all, l,
                     sh_w_gate[l].astype(BF16), sh_w_up[l].astype(BF16), sh_w_down[l].astype(BF16),
                     ln2_gain[l], ln2_bias[l])
    return h.reshape(b, s, d)
```

```python
import functools

import numpy as np
import jax
import jax.numpy as jnp
from jax import lax
from jax.experimental import pallas as pl
from jax.experimental.pallas import tpu as pltpu

F32 = jnp.float32
BF16 = jnp.bfloat16

D_MODEL = 2048
DEPTH = 2
HG_HEADS = 4
HG_DK = 128
HG_WIDTH = HG_HEADS * HG_DK
CHUNK = 128
SW_Q_HEADS = 16
SW_KV_HEADS = 2
SW_HEAD_DIM = 64
SW_WIDTH = SW_Q_HEADS * SW_HEAD_DIM
SW_KV_WIDTH = SW_KV_HEADS * SW_HEAD_DIM
SG_GROUPS = 4
SG_WIDTH = SG_GROUPS * CHUNK
D_IN = 4 * HG_WIDTH + SW_WIDTH + 2 * SW_KV_WIDTH + 2 * SG_WIDTH
N_EXPERTS = 64
TOP_K = 8
D_EXPERT = 512
N_EXPERT_GROUPS = 8
TOPK_GROUPS = 4
ROUTED_SCALE = 2.5
ALPHA = (2 * DEPTH) ** 0.25
LN_EPS = 1e-5
RMS_EPS = 1e-6

_SWQ_OFF = 4 * HG_WIDTH
_SGU_OFF = _SWQ_OFF + SW_WIDTH
_SWK_OFF = _SGU_OFF + 2 * SG_WIDTH


def _regroup_columns(a):
    k0 = _SWQ_OFF + SW_WIDTH
    k1 = k0 + 2 * SW_KV_WIDTH
    return jnp.concatenate([a[..., :k0], a[..., k1:], a[..., k0:k1]], axis=-1)


MOE_ROWS = 256
TOKEN_GROUPS = 2
ROUTER_TILE = 512
COMBINE_TILE = 256
VMEM_LIMIT = 56 * 1024 * 1024

_LEVEL_SIZES = (64, 32, 16, 8, 4, 2, 1)
_N_LEVELS = len(_LEVEL_SIZES)


def _hgrn_constants():
    t = np.arange(CHUNK)[:, None]
    u = np.arange(CHUNK)[None, :]
    mats = []
    level = np.full((CHUNK, CHUNK), -1, np.int32)
    for li, s in enumerate(_LEVEL_SIZES):
        blk = t // s
        odd = (blk % 2) == 1
        m_query = (u >= blk * s) & (u <= t)
        m_key = (u > t) & (u <= (blk + 1) * s - 1)
        mats.append(np.where(odd, m_query, m_key))
        pair = ((t // s) % 2 == 1) & ((u // s) == (t // s) - 1)
        level[pair] = li
    mats.append(u <= t)
    mats.append(u > t)
    level[np.arange(CHUNK), np.arange(CHUNK)] = _N_LEVELS
    return np.concatenate(mats, 0).astype(np.float32), level


_HGRN_MSTACK, _HGRN_LEVEL = _hgrn_constants()


def _layer_norm(x, g, b):
    mu = jnp.mean(x, axis=-1, keepdims=True)
    xc = x - mu
    var = jnp.mean(xc * xc, axis=-1, keepdims=True)
    return xc * lax.rsqrt(var + LN_EPS) * g + b


def _silu(x):
    return x / (1.0 + jnp.exp(-x))


def _gelu(x):
    return 0.5 * x * (1.0 + lax.erf(x * (2.0 ** -0.5)))


def _dot(a, b):
    return jnp.dot(a, b, preferred_element_type=F32)


def _dot_nt(a, b):
    return lax.dot_general(a, b, (((1,), (1,)), ((), ())), preferred_element_type=F32)


def _dot_tn(a, b):
    return lax.dot_general(a, b, (((0,), (0,)), ((), ())), preferred_element_type=F32)


def _ln_kernel(x_ref, g_ref, b_ref, of_ref, ob_ref):
    y = _layer_norm(x_ref[...], g_ref[...], b_ref[...])
    of_ref[...] = y
    ob_ref[...] = y.astype(BF16)


def _ln(x, g, b, tm=256):
    t, d = x.shape
    row = pl.BlockSpec((tm, d), lambda i: (i, 0))
    vec = pl.BlockSpec((1, d), lambda i: (0, 0))
    return pl.pallas_call(
        _ln_kernel,
        grid=(t // tm,),
        in_specs=[row, vec, vec],
        out_specs=[row, row],
        out_shape=[jax.ShapeDtypeStruct((t, d), F32), jax.ShapeDtypeStruct((t, d), BF16)],
        name="ln0",
    )(x, g.reshape(1, d), b.reshape(1, d))


def _mm_bias_kernel(a_ref, w_ref, b_ref, o_ref):
    o_ref[...] = _dot(a_ref[...], w_ref[...]) + b_ref[...]


def _in_proj(a, w, b, tm=512):
    t, k = a.shape
    n = w.shape[1]
    tm = min(tm, t)
    tn = n // 2
    return pl.pallas_call(
        _mm_bias_kernel,
        grid=(n // tn, t // tm),
        in_specs=[pl.BlockSpec((tm, k), lambda j, i: (i, 0)),
                  pl.BlockSpec((k, tn), lambda j, i: (0, j)),
                  pl.BlockSpec((1, tn), lambda j, i: (0, j))],
        out_specs=pl.BlockSpec((tm, tn), lambda j, i: (i, j)),
        out_shape=jax.ShapeDtypeStruct((t, n), F32),
        compiler_params=pltpu.CompilerParams(vmem_limit_bytes=VMEM_LIMIT),
        name="in_proj",
    )(a, w, b.reshape(1, n))


def _hgrn_kernel(q_ref, f_ref, i_ref, g_ref, llb_ref, l1lb_ref, oml_ref, gain_ref,
                 mstack_ref, level_ref, o_ref, state_ref):
    c = pl.program_id(1)

    @pl.when(c == 0)
    def _():
        state_ref[...] = jnp.zeros_like(state_ref)

    q = _silu(q_ref[...])
    z = f_ref[...]
    ez = jnp.exp(-jnp.abs(z))
    log_sig = jnp.minimum(z, 0.0) - jnp.log(1.0 + ez)
    sig_neg = jnp.where(z >= 0.0, ez, 1.0) / (1.0 + ez)
    k = oml_ref[...] * sig_neg
    a = llb_ref[...]
    bb = l1lb_ref[...] + log_sig
    log_f = jnp.maximum(a, bb) + jnp.log(1.0 + jnp.exp(-jnp.abs(a - bb)))

    lf_hi = log_f.astype(BF16)
    lf_lo = (log_f - lf_hi.astype(F32)).astype(BF16)
    m = mstack_ref[...]
    sums = _dot(m, lf_hi) + _dot(m, lf_lo)
    decay = jnp.exp(sums)

    rows = lax.broadcasted_iota(jnp.int32, (CHUNK, 1), 0)
    factors = []
    for li, s in enumerate(_LEVEL_SIZES):
        is_query = ((rows // s) % 2) == 1
        factors.append((jnp.where(is_query, q, k) * decay[li * CHUNK:(li + 1) * CHUNK]).astype(BF16))
    cum = _N_LEVELS * CHUNK
    q_in = (q * decay[cum:cum + CHUNK]).astype(BF16)
    k_dec = (k * decay[cum + CHUNK:cum + 2 * CHUNK]).astype(BF16)
    end_decay = decay[cum + CHUNK - 1:cum + CHUNK]
    qb = q.astype(BF16)
    kb = k.astype(BF16)
    vb = i_ref[...].astype(BF16)
    level = level_ref[...]
    gate = _silu(g_ref[...])
    gain = gain_ref[...]

    for h in range(HG_HEADS):
        hs = slice(h * HG_DK, (h + 1) * HG_DK)
        scores = jnp.zeros((CHUNK, CHUNK), F32)
        for li in range(_N_LEVELS):
            fl = factors[li][:, hs]
            scores = jnp.where(level == li, _dot_nt(fl, fl), scores)
        scores = jnp.where(level == _N_LEVELS, _dot_nt(qb[:, hs], kb[:, hs]), scores)
        st = state_ref[h]
        o = _dot(scores.astype(BF16), vb[:, hs]) + _dot_nt(q_in[:, hs], st.astype(BF16))
        state_ref[h] = st * end_decay[:, hs] + _dot_tn(vb[:, hs], k_dec[:, hs])
        ms = jnp.mean(o * o, axis=-1, keepdims=True)
        o = o * lax.rsqrt(ms + RMS_EPS) * gain[:, hs] * gate[:, hs]
        o_ref[:, hs] = o.astype(o_ref.dtype)


def _hgrn(proj3, llb, l1lb, oml, gain):
    b, s, _ = proj3.shape
    w = HG_WIDTH

    def col(j):
        return pl.BlockSpec((None, CHUNK, w), lambda bi, ci, j=j: (bi, ci, j))

    vec = pl.BlockSpec((1, w), lambda bi, ci: (0, 0))
    nm = _HGRN_MSTACK.shape[0]
    return pl.pallas_call(
        _hgrn_kernel,
        grid=(b, s // CHUNK),
        in_specs=[col(0), col(1), col(2), col(3), vec, vec, vec, vec,
                  pl.BlockSpec((nm, CHUNK), lambda bi, ci: (0, 0)),
                  pl.BlockSpec((CHUNK, CHUNK), lambda bi, ci: (0, 0))],
        out_specs=pl.BlockSpec((None, CHUNK, w), lambda bi, ci: (bi, ci, 0)),
        out_shape=jax.ShapeDtypeStruct((b, s, w), BF16),
        scratch_shapes=[pltpu.VMEM((HG_HEADS, HG_DK, HG_DK), F32)],
        compiler_params=pltpu.CompilerParams(dimension_semantics=("parallel", "arbitrary"),
                                             vmem_limit_bytes=VMEM_LIMIT),
        name="hgrn2",
    )(proj3, proj3, proj3, proj3, llb.reshape(1, w), l1lb.reshape(1, w), oml.reshape(1, w),
      gain.reshape(1, w), jnp.asarray(_HGRN_MSTACK, BF16), jnp.asarray(_HGRN_LEVEL))


def _swa_kernel(q_ref, kp_ref, kc_ref, vp_ref, vc_ref, sink_ref, gain_ref, o_ref):
    n = pl.program_id(1)
    g = SW_Q_HEADS // SW_KV_HEADS
    hd = SW_HEAD_DIM
    assert (hd ** -0.5) == 2.0 ** round(np.log2(hd ** -0.5))
    q = (q_ref[...] * (hd ** -0.5)).astype(BF16)
    kband = jnp.concatenate([kp_ref[...], kc_ref[...]], axis=0).astype(BF16)
    vband = jnp.concatenate([vp_ref[...], vc_ref[...]], axis=0).astype(BF16)
    t = lax.broadcasted_iota(jnp.int32, (CHUNK, 2 * CHUNK), 0)
    s = lax.broadcasted_iota(jnp.int32, (CHUNK, 2 * CHUNK), 1)
    rel = t + CHUNK - s
    mask = (rel >= 0) & (rel < CHUNK) & ((s >= CHUNK) | (n > 0))
    bias = jnp.where(mask, 0.0, -jnp.inf)
    kvs = [(kband[:, kv * hd:(kv + 1) * hd], vband[:, kv * hd:(kv + 1) * hd]) for kv in range(SW_KV_HEADS)]
    outs = []
    for h in range(SW_Q_HEADS):
        k_h, v_h = kvs[h // g]
        sc = _dot_nt(q[:, h * hd:(h + 1) * hd], k_h) + bias
        sink = sink_ref[h]
        mx = jnp.maximum(jnp.max(sc, axis=-1, keepdims=True), sink)
        p = jnp.exp(sc - mx)
        denom = jnp.sum(p, axis=-1, keepdims=True) + jnp.exp(sink - mx)
        outs.append(_dot(p.astype(BF16), v_h) / denom)
    o = jnp.concatenate(outs, axis=-1)
    ms = jnp.mean(o * o, axis=-1, keepdims=True)
    o_ref[...] = (o * lax.rsqrt(ms + RMS_EPS) * gain_ref[...]).astype(o_ref.dtype)


def _swa(proj3, sinks, gain):
    b, s, _ = proj3.shape
    q_col = _SWQ_OFF // SW_WIDTH
    k_col = _SWK_OFF // SW_KV_WIDTH
    v_col = k_col + 1

    def kv_spec(col, prev):
        if prev:
            return pl.BlockSpec((None, CHUNK, SW_KV_WIDTH), lambda bi, ni: (bi, jnp.maximum(ni - 1, 0), col))
        return pl.BlockSpec((None, CHUNK, SW_KV_WIDTH), lambda bi, ni: (bi, ni, col))

    return pl.pallas_call(
        _swa_kernel,
        grid=(b, s // CHUNK),
        in_specs=[pl.BlockSpec((None, CHUNK, SW_WIDTH), lambda bi, ni: (bi, ni, q_col)),
                  kv_spec(k_col, True), kv_spec(k_col, False),
                  kv_spec(v_col, True), kv_spec(v_col, False),
                  pl.BlockSpec(memory_space=pltpu.SMEM),
                  pl.BlockSpec((1, SW_WIDTH), lambda bi, ni: (0, 0))],
        out_specs=pl.BlockSpec((None, CHUNK, SW_WIDTH), lambda bi, ni: (bi, ni, 0)),
        out_shape=jax.ShapeDtypeStruct((b, s, SW_WIDTH), BF16),
        compiler_params=pltpu.CompilerParams(dimension_semantics=("parallel", "parallel"),
                                             vmem_limit_bytes=VMEM_LIMIT),
        name="swa",
    )(proj3, proj3, proj3, proj3, proj3, sinks.astype(F32), gain.reshape(1, SW_WIDTH))


def _sgu_kernel(u_ref, v_ref, lng_ref, lnb_ref, w_ref, bs_ref, gain_ref, o_ref):
    u = _gelu(u_ref[...])
    v = _layer_norm(_gelu(v_ref[...]), lng_ref[...], lnb_ref[...]).astype(BF16)
    r = lax.broadcasted_iota(jnp.int32, (CHUNK, CHUNK), 0)
    c = lax.broadcasted_iota(jnp.int32, (CHUNK, CHUNK), 1)
    tril = c <= r
    bs = bs_ref[...]
    parts = []
    for gi in range(SG_GROUPS):
        gs = slice(gi * CHUNK, (gi + 1) * CHUNK)
        w = jnp.where(tril, w_ref[gi], 0.0).astype(BF16)
        parts.append(_dot(w, v[:, gs]) + bs[:, gi:gi + 1])
    o = u * jnp.concatenate(parts, axis=-1)
    ms = jnp.mean(o * o, axis=-1, keepdims=True)
    o_ref[...] = (o * lax.rsqrt(ms + RMS_EPS) * gain_ref[...]).astype(o_ref.dtype)


def _sgu(proj3, ln_g, ln_b, w_s, b_s, gain):
    b, s, _ = proj3.shape
    w = SG_WIDTH
    u_col = _SGU_OFF // SG_WIDTH
    vec = pl.BlockSpec((1, w), lambda bi, ci: (0, 0))
    return pl.pallas_call(
        _sgu_kernel,
        grid=(b, s // CHUNK),
        in_specs=[pl.BlockSpec((None, CHUNK, w), lambda bi, ci: (bi, ci, u_col)),
                  pl.BlockSpec((None, CHUNK, w), lambda bi, ci: (bi, ci, u_col + 1)),
                  vec, vec,
                  pl.BlockSpec((SG_GROUPS, CHUNK, CHUNK), lambda bi, ci: (0, 0, 0)),
                  pl.BlockSpec((CHUNK, SG_GROUPS), lambda bi, ci: (0, 0)),
                  vec],
        out_specs=pl.BlockSpec((None, CHUNK, w), lambda bi, ci: (bi, ci, 0)),
        out_shape=jax.ShapeDtypeStruct((b, s, w), BF16),
        compiler_params=pltpu.CompilerParams(dimension_semantics=("parallel", "parallel"),
                                             vmem_limit_bytes=VMEM_LIMIT),
        name="sgu",
    )(proj3, proj3, ln_g.reshape(1, w), ln_b.reshape(1, w), w_s, b_s.T, gain.reshape(1, w))


LANES = 128
SLAB = D_MODEL // 2 // LANES


def _pack_pair(low, high):
    def bits(v):
        return lax.bitcast_convert_type(v.astype(BF16).astype(F32), jnp.uint32)

    return (bits(high) & jnp.uint32(0xFFFF0000)) | (bits(low) >> 16)


def _pack_rows(x):
    half = x.shape[1] // 2
    return _pack_pair(x[:, :half], x[:, half:])


def _unpack_words(w):
    return (lax.bitcast_convert_type(w << 16, F32),
            lax.bitcast_convert_type(w & jnp.uint32(0xFFFF0000), F32))


def _store_slabs(ref, x):
    n = x.shape[0]
    for j in range(SLAB):
        ref[pl.ds(j, n, stride=SLAB), :] = x[:, j * LANES:(j + 1) * LANES]


def _load_packed_rows(ref, n, first=0, stride=SLAB):
    lows, highs = [], []
    for j in range(SLAB):
        lo, hi = _unpack_words(ref[pl.ds(first + j, n, stride=stride), :])
        lows.append(lo.astype(BF16))
        highs.append(hi.astype(BF16))
    return jnp.concatenate(lows + highs, axis=1)


def _out_proj_kernel(oa_ref, ob_ref, oc_ref, w_ref, h_ref, g_ref, b_ref, of_ref, os_ref):
    mixed = jnp.concatenate([oa_ref[...], ob_ref[...], oc_ref[...]], axis=-1)
    y = ALPHA * h_ref[...] + _dot(mixed, w_ref[...])
    out = _layer_norm(y, g_ref[...], b_ref[...])
    of_ref[...] = out
    _store_slabs(os_ref, _pack_rows(out))


def _out_proj(oa, ob, oc, w, h, g, b, tm=256):
    t, d = h.shape
    vec = pl.BlockSpec((1, d), lambda i: (0, 0))
    return pl.pallas_call(
        _out_proj_kernel,
        grid=(t // tm,),
        in_specs=[pl.BlockSpec((tm, oa.shape[1]), lambda i: (i, 0)),
                  pl.BlockSpec((tm, ob.shape[1]), lambda i: (i, 0)),
                  pl.BlockSpec((tm, oc.shape[1]), lambda i: (i, 0)),
                  pl.BlockSpec(w.shape, lambda i: (0, 0)),
                  pl.BlockSpec((tm, d), lambda i: (i, 0)), vec, vec],
        out_specs=[pl.BlockSpec((tm, d), lambda i: (i, 0)),
                   pl.BlockSpec((tm * SLAB, LANES), lambda i: (i, 0))],
        out_shape=[jax.ShapeDtypeStruct((t, d), F32), jax.ShapeDtypeStruct((t * SLAB, LANES), jnp.uint32)],
        compiler_params=pltpu.CompilerParams(vmem_limit_bytes=VMEM_LIMIT),
        name="out_proj_ln1",
    )(oa, ob, oc, w, h, g.reshape(1, d), b.reshape(1, d))


def _first_index_of_max(x, iota, size, axis):
    mx = jnp.max(x, axis=axis, keepdims=True)
    idx = jnp.min(jnp.where(x == mx, iota, size), axis=axis, keepdims=True)
    return mx, idx


def _router_kernel(tiles_per_group, h_ref, wt_ref, bias_ref, eidx_ref, gate_ref, count_ref, run_ref):
    i = pl.program_id(0)

    @pl.when(i % tiles_per_group == 0)
    def _():
        run_ref[...] = jnp.zeros_like(run_ref)

    tm = h_ref.shape[0]
    per_group = N_EXPERTS // N_EXPERT_GROUPS
    def split(v):
        hi = v.astype(BF16)
        return hi, (v - hi.astype(F32)).astype(BF16)

    w_hi, w_lo = split(wt_ref[...])
    h_hi, h_lo = split(h_ref[...])
    logits = _dot_nt(w_hi, h_hi) + (_dot_nt(w_hi, h_lo) + _dot_nt(w_lo, h_hi))
    scores = 1.0 / (1.0 + jnp.exp(-logits))
    sel = scores + bias_ref[...]
    sel3 = sel.reshape(N_EXPERT_GROUPS, per_group, tm)
    io3 = lax.broadcasted_iota(jnp.int32, sel3.shape, 1)
    m1, i1 = _first_index_of_max(sel3, io3, per_group, 1)
    m2 = jnp.max(jnp.where(io3 == i1, -jnp.inf, sel3), axis=1, keepdims=True)
    grp = (m1 + m2).reshape(N_EXPERT_GROUPS, tm)
    iog = lax.broadcasted_iota(jnp.int32, grp.shape, 0)
    keep = jnp.zeros(grp.shape, jnp.bool_)
    for _ in range(TOPK_GROUPS):
        _, gi = _first_index_of_max(grp, iog, N_EXPERT_GROUPS, 0)
        hit = iog == gi
        keep = keep | hit
        grp = jnp.where(hit, -jnp.inf, grp)
    keep3 = jnp.broadcast_to(keep.reshape(N_EXPERT_GROUPS, 1, tm), sel3.shape)
    cand = jnp.where(keep3, sel3, -jnp.inf).reshape(N_EXPERTS, tm)
    ioe = lax.broadcasted_iota(jnp.int32, cand.shape, 0)
    chosen = jnp.zeros(cand.shape, F32)
    idxs, gvals = [], []
    for _ in range(TOP_K):
        _, ei = _first_index_of_max(cand, ioe, N_EXPERTS, 0)
        hit = ioe == ei
        idxs.append(ei)
        gvals.append(jnp.sum(jnp.where(hit, scores, 0.0), axis=0, keepdims=True))
        chosen = jnp.where(hit, 1.0, chosen)
        cand = jnp.where(hit, -jnp.inf, cand)
    gsum = functools.reduce(lambda a, b: a + b, gvals)
    for kk in range(TOP_K):
        eidx_ref[kk:kk + 1, :] = idxs[kk]
        gate_ref[kk:kk + 1, :] = gvals[kk] / gsum * ROUTED_SCALE
    run = run_ref[...] + jnp.sum(chosen, axis=1, keepdims=True)
    run_ref[...] = run
    count_ref[...] = run.astype(jnp.int32)


def _router(h, router_w, router_bias, n_groups, tm=ROUTER_TILE):
    t, d = h.shape
    tm = min(tm, t // n_groups)
    tiles_per_group = t // n_groups // tm
    slot = pl.BlockSpec((TOP_K, tm), lambda i: (0, i))
    return pl.pallas_call(
        functools.partial(_router_kernel, tiles_per_group),
        grid=(t // tm,),
        in_specs=[pl.BlockSpec((tm, d), lambda i: (i, 0)),
                  pl.BlockSpec((N_EXPERTS, d), lambda i: (0, 0)),
                  pl.BlockSpec((N_EXPERTS, 1), lambda i: (0, 0))],
        out_specs=[slot, slot, pl.BlockSpec((None, N_EXPERTS, 1), lambda i: (i // tiles_per_group, 0, 0))],
        out_shape=[jax.ShapeDtypeStruct((TOP_K, t), jnp.int32),
                   jax.ShapeDtypeStruct((TOP_K, t), F32),
                   jax.ShapeDtypeStruct((n_groups, N_EXPERTS, 1), jnp.int32)],
        scratch_shapes=[pltpu.VMEM((N_EXPERTS, 1), F32)],
        compiler_params=pltpu.CompilerParams(dimension_semantics=("arbitrary",),
                                             vmem_limit_bytes=VMEM_LIMIT),
        name="router",
    )(h, router_w.T, router_bias.reshape(N_EXPERTS, 1))


_ISSUE_UNROLL = 8
_TOP_K_BITS = TOP_K.bit_length() - 1
assert 1 << _TOP_K_BITS == TOP_K


def _experts_kernel(layer, be_ref, bg_ref, wplan_ref, nu_ref, src_ref, dstp_ref, dstc_ref, h_hbm,
                    wg_hbm, wu_hbm, wd_hbm, y_hbm, tok_ref, xbuf, obuf0, obuf1, wg_buf, wu_buf, wd_buf,
                    lsem, ssem, wsem):
    b = pl.program_id(0)
    nu = nu_ref[0]
    obuf = (obuf0, obuf1)
    rows = xbuf.shape[0] // SLAB
    d = wg_buf.shape[1]
    f = wg_buf.shape[2]
    wslot = wplan_ref[1, b]

    def weight_copies(expert, slot_):
        return [pltpu.make_async_copy(src.at[layer, expert], dst.at[slot_], wsem.at[slot_])
                for src, dst in ((wg_hbm, wg_buf), (wu_hbm, wu_buf), (wd_hbm, wd_buf))]

    @pl.when((b < nu) & (wplan_ref[0, b] == 1))
    def _():
        @pl.when(b == 0)
        def _():
            for c in weight_copies(be_ref[0], wslot):
                c.start()
        for c in weight_copies(be_ref[b], wslot):
            c.wait()
        nxt = wplan_ref[2, b]

        @pl.when(nxt >= 0)
        def _():
            for c in weight_copies(nxt, 1 - wslot):
                c.start()

    wg_ref, wu_ref, wd_ref = wg_buf.at[wslot], wu_buf.at[wslot], wd_buf.at[wslot]
    pad0 = y_hbm.shape[0] - 2 * rows * SLAB

    def scatter_row(dst_ref, r, src_slot, priority):
        dst = pl.multiple_of(dst_ref[0, 0, r], SLAB)
        buf_rows = pl.ds(pl.multiple_of(r * SLAB, SLAB), SLAB)
        pltpu.make_async_copy(obuf[src_slot].at[buf_rows], y_hbm.at[pl.ds(dst, SLAB)],
                              ssem.at[src_slot]).start(priority=priority)

    def wait_scatter(s):
        pltpu.make_async_copy(obuf[s], y_hbm.at[pl.ds(0, rows * SLAB)], ssem.at[s]).wait()

    @pl.when(b == 0)
    def _():
        for s in range(2):
            obuf[s][...] = jnp.zeros_like(obuf[s])
            pltpu.make_async_copy(obuf[s], y_hbm.at[pl.ds(pad0 + s * rows * SLAB, rows * SLAB)],
                                  ssem.at[s]).start()
        for s in range(2):
            wait_scatter(s)

    group = bg_ref[b]

    @pl.when((b < nu) & ((b == 0) | (group != bg_ref[jnp.maximum(b - 1, 0)])))
    def _():
        load = pltpu.make_async_copy(h_hbm.at[group], tok_ref, lsem.at[0])
        load.start()
        load.wait()

    def block(slot, with_scatter):
        other = 1 - slot

        if with_scatter:
            def issue(i, carry):
                for u in range(_ISSUE_UNROLL):
                    scatter_row(dstp_ref, i * _ISSUE_UNROLL + u, other, u % 2)
                return carry
            lax.fori_loop(0, rows // _ISSUE_UNROLL, issue, 0)

        def gather(i, carry):
            for u in range(_ISSUE_UNROLL):
                r = i * _ISSUE_UNROLL + u
                src = pl.multiple_of(src_ref[0, 0, r], SLAB)
                xbuf[pl.ds(pl.multiple_of(r * SLAB, SLAB), SLAB), :] = tok_ref[pl.ds(src, SLAB), :]
            return carry
        lax.fori_loop(0, rows // _ISSUE_UNROLL, gather, 0)

        x = _load_packed_rows(xbuf, rows)
        acts = []
        fh = f // 2
        for j in range(2):
            gate = _dot(x, wg_ref[:, j * fh:(j + 1) * fh])
            up = _dot(x, wu_ref[:, j * fh:(j + 1) * fh])
            acts.append((_silu(gate) * up).astype(BF16))
        act = jnp.concatenate(acts, axis=1)
        half = d // 2
        cw = 2 * LANES
        for q in range(half // cw):
            words = _pack_pair(_dot(act, wd_ref[:, q * cw:(q + 1) * cw]),
                               _dot(act, wd_ref[:, half + q * cw:half + (q + 1) * cw]))
            for jj in range(cw // LANES):
                obuf[slot][pl.ds(q * (cw // LANES) + jj, rows, stride=SLAB), :] = words[:, jj * LANES:(jj + 1) * LANES]

    def tail(slot):
        other = 1 - slot

        @pl.when(b >= 1)
        def _():
            wait_scatter(other)

        def body(r, carry):
            scatter_row(dstc_ref, r, slot, 0)
            return carry
        lax.fori_loop(0, rows, body, 0)
        wait_scatter(slot)

    @pl.when(b == 0)
    def _():
        block(0, False)

    for parity in range(2):
        @pl.when((b >= 1) & (b < nu) & (b % 2 == parity))
        def _(parity=parity):
            @pl.when(b >= 2)
            def _():
                wait_scatter(parity)
            block(parity, True)

        @pl.when((b == nu - 1) & (b % 2 == parity))
        def _(parity=parity):
            tail(parity)


def _experts(h_groups, row_src, row_dst, block_e, block_g, weight_plan, n_used, wg, wu, wd, layer, n_slots):
    nb, _, rows = row_src.shape
    d, f = wg.shape[-2], wg.shape[-1]
    group_rows = h_groups.shape[1]

    def smem(shift):
        def index(b, be, bg, wplan, nu):
            return (jnp.maximum(jnp.minimum(b + shift, nu[0] - 1), 0), 0, 0)
        return pl.BlockSpec((1, 1, rows), index, memory_space=pltpu.SMEM)

    hbm = pl.BlockSpec(memory_space=pl.ANY)
    grid_spec = pltpu.PrefetchScalarGridSpec(
        num_scalar_prefetch=4,
        grid=(nb,),
        in_specs=[smem(0), smem(-1), smem(0), hbm, hbm, hbm, hbm],
        out_specs=hbm,
        scratch_shapes=[pltpu.VMEM((group_rows, LANES), jnp.uint32)]
        + [pltpu.VMEM((rows * SLAB, LANES), jnp.uint32)] * 3
        + [pltpu.VMEM((2, d, f), BF16), pltpu.VMEM((2, d, f), BF16), pltpu.VMEM((2, f, d), BF16)]
        + [pltpu.SemaphoreType.DMA((1,)), pltpu.SemaphoreType.DMA((2,)), pltpu.SemaphoreType.DMA((2,))],
    )
    return pl.pallas_call(
        functools.partial(_experts_kernel, layer),
        grid_spec=grid_spec,
        out_shape=jax.ShapeDtypeStruct(((n_slots + 2 * rows) * SLAB, LANES), jnp.uint32),
        compiler_params=pltpu.CompilerParams(dimension_semantics=("arbitrary",),
                                             vmem_limit_bytes=VMEM_LIMIT),
        name="experts",
    )(block_e, block_g, weight_plan, n_used, row_src, row_dst, row_dst, h_groups, wg, wu, wd)


_COMBINE_SUB = 32


def _combine_kernel(h_ref, gate_ref, *refs):
    y_refs = refs[:TOP_K]
    sg_ref, su_ref, sd_ref, g_ref, b_ref, of_ref, ob_ref, moe_ref = refs[TOP_K:]
    h = h_ref[...]
    tm = h.shape[0]
    half = SLAB * LANES
    for s in range(tm // _COMBINE_SUB):
        rs = slice(s * _COMBINE_SUB, (s + 1) * _COMBINE_SUB)
        gates = gate_ref[rs, :]
        first = s * _COMBINE_SUB * SLAB
        for j in range(SLAB):
            lo_acc = hi_acc = None
            for kk in range(TOP_K):
                lo, hi = _unpack_words(y_refs[kk][pl.ds(first + j, _COMBINE_SUB, stride=SLAB), :])
                gk = gates[:, kk:kk + 1]
                lo_acc = gk * lo if kk == 0 else lo_acc + gk * lo
                hi_acc = gk * hi if kk == 0 else hi_acc + gk * hi
            moe_ref[rs, j * LANES:(j + 1) * LANES] = lo_acc
            moe_ref[rs, half + j * LANES:half + (j + 1) * LANES] = hi_acc
    hb = h.astype(BF16)
    act = _silu(_dot(hb, sg_ref[...])) * _dot(hb, su_ref[...])
    y = ALPHA * h + _dot(act.astype(BF16), sd_ref[...]) + moe_ref[...]
    out = _layer_norm(y, g_ref[...], b_ref[...])
    of_ref[...] = out
    ob_ref[...] = out.astype(BF16)


def _combine(h, gates_t, y_slots, sg, su, sd, g, b, tm=COMBINE_TILE):
    t, d = h.shape
    n = t // tm
    f = sg.shape[1]
    vec = pl.BlockSpec((1, d), lambda i: (0, 0))
    row = pl.BlockSpec((tm, d), lambda i: (i, 0))
    return pl.pallas_call(
        _combine_kernel,
        grid=(n,),
        in_specs=[row, pl.BlockSpec((tm, TOP_K), lambda i: (i, 0))]
        + [pl.BlockSpec((tm * SLAB, LANES), lambda i, kk=kk: (kk * n + i, 0)) for kk in range(TOP_K)]
        + [pl.BlockSpec((d, f), lambda i: (0, 0)),
            pl.BlockSpec((d, f), lambda i: (0, 0)),
            pl.BlockSpec((f, d), lambda i: (0, 0)),
            vec, vec],
        out_specs=[row, row],
        out_shape=[jax.ShapeDtypeStruct((t, d), F32), jax.ShapeDtypeStruct((t, d), BF16)],
        scratch_shapes=[pltpu.VMEM((tm, d), F32)],
        compiler_params=pltpu.CompilerParams(dimension_semantics=("parallel",),
                                             vmem_limit_bytes=VMEM_LIMIT),
        name="combine_ln2",
    )(h, gates_t, *([y_slots] * TOP_K), sg, su, sd, g.reshape(1, d), b.reshape(1, d))


def _moe(h, h_slab, router_w, router_bias, wg, wu, wd, layer, sg, su, sd, ln_g, ln_b):
    t, d = h.shape
    tk = t * TOP_K
    n_groups = TOKEN_GROUPS if t % (TOKEN_GROUPS * ROUTER_TILE) == 0 else 1
    tg = t // n_groups
    n_buckets = n_groups * N_EXPERTS
    n_blocks = tk // MOE_ROWS + n_buckets
    eidx, gates, counts = _router(h, router_w, router_bias, n_groups)
    counts = counts.reshape(n_buckets)
    n_rows = n_blocks * MOE_ROWS
    padded = (counts + MOE_ROWS - 1) // MOE_ROWS * MOE_ROWS
    pend = jnp.cumsum(padded)
    pad_end = jnp.cumsum(padded - counts)
    pad_key = jnp.sum(pad_end[None, :] <= jnp.arange(n_rows - tk, dtype=jnp.int32)[:, None], axis=1)
    slot_group = (jnp.arange(tk, dtype=jnp.int32) >> _TOP_K_BITS) // tg
    keys = jnp.concatenate([slot_group * N_EXPERTS + eidx.T.reshape(tk), pad_key.astype(jnp.int32)])
    _, row_slot = lax.sort((keys, jnp.arange(n_rows, dtype=jnp.int32)), num_keys=1, is_stable=True)
    real = row_slot < tk
    row = jnp.arange(n_rows, dtype=jnp.int32)
    row_src = jnp.where(real, ((row_slot >> _TOP_K_BITS) % tg) * SLAB, 0)
    plane_row = (row_slot & (TOP_K - 1)) * t + (row_slot >> _TOP_K_BITS)
    row_dst = jnp.where(real, plane_row, tk + row % (2 * MOE_ROWS)) * SLAB
    block_start = jnp.arange(n_blocks, dtype=jnp.int32) * MOE_ROWS
    block_bucket = jnp.minimum(jnp.sum(pend[None, :] <= block_start[:, None], axis=1), n_buckets - 1).astype(jnp.int32)
    n_used = (pend[-1] // MOE_ROWS).astype(jnp.int32)
    block_e = block_bucket % N_EXPERTS
    block = jnp.arange(n_blocks, dtype=jnp.int32)
    first = (block == 0) | (block_bucket != jnp.roll(block_bucket, 1))
    seq = jnp.cumsum(first.astype(jnp.int32)) - 1
    later_first = jnp.where(first & (block < n_used), block, n_blocks)
    next_first = jnp.concatenate([lax.cummin(later_first, reverse=True)[1:], jnp.full((1,), n_blocks, jnp.int32)])
    next_e = jnp.where(next_first < n_blocks, block_e[jnp.minimum(next_first, n_blocks - 1)], -1)
    weight_plan = jnp.stack([first.astype(jnp.int32), seq % 2, next_e]).astype(jnp.int32)
    y_slots = _experts(h_slab.reshape(n_groups, tg * SLAB, LANES),
                       row_src.reshape(n_blocks, 1, MOE_ROWS), row_dst.reshape(n_blocks, 1, MOE_ROWS),
                       block_e, block_bucket // N_EXPERTS, weight_plan, n_used.reshape(1),
                       wg, wu, wd, layer, tk)
    return _combine(h, gates.T, y_slots, sg, su, sd, ln_g, ln_b, tm=min(COMBINE_TILE, t))


def kernel(x, ln0_gain, ln0_bias, w_in, b_in, hg_lb_logits, sw_sinks, sg_ln_gain, sg_ln_bias, sg_w_s, sg_b_s, mix_gain, w_out, ln1_gain, ln1_bias, router_w, router_bias, exp_w_gate, exp_w_up, exp_w_down, sh_w_gate, sh_w_up, sh_w_down, ln2_gain, ln2_bias):
    b, s, d = x.shape
    t = b * s
    p = jax.nn.softmax(hg_lb_logits.astype(F32), axis=0)
    cs = jnp.cumsum(p, axis=0)
    lbs = cs - cs[0:1]
    a0, a1 = HG_WIDTH, HG_WIDTH + SW_WIDTH

    wg_all, wu_all, wd_all = exp_w_gate.astype(BF16), exp_w_up.astype(BF16), exp_w_down.astype(BF16)
    h, hb = _ln(x.reshape(t, d), ln0_gain, ln0_bias)
    for l in range(DEPTH):
        proj = _in_proj(hb, _regroup_columns(w_in[l]).astype(BF16), _regroup_columns(b_in[l]))
        proj3 = proj.reshape(b, s, D_IN)
        lb = lbs[l]
        o_a = _hgrn(proj3, jnp.log(lb), jnp.log1p(-lb), 1.0 - lb, mix_gain[l, :a0])
        o_b = _swa(proj3, sw_sinks[l], mix_gain[l, a0:a1])
        o_c = _sgu(proj3, sg_ln_gain[l], sg_ln_bias[l], sg_w_s[l], sg_b_s[l], mix_gain[l, a1:])
        h, h_slab = _out_proj(o_a.reshape(t, -1), o_b.reshape(t, -1), o_c.reshape(t, -1),
                              w_out[l].astype(BF16), h, ln1_gain[l], ln1_bias[l])
        h, hb = _moe(h, h_slab, router_w[l], router_bias[l], wg_all, wu_all, wd_all, l,
                     sh_w_gate[l].astype(BF16), sh_w_up[l].astype(BF16), sh_w_down[l].astype(BF16),
                     ln2_gain[l], ln2_bias[l])
    return h.reshape(b, s, d)
```

```python
import functools

import numpy as np
import jax
import jax.numpy as jnp
from jax import lax
from jax.experimental import pallas as pl
from jax.experimental.pallas import tpu as pltpu

F32 = jnp.float32
BF16 = jnp.bfloat16

D_MODEL = 2048
DEPTH = 2
HG_HEADS = 4
HG_DK = 128
HG_WIDTH = HG_HEADS * HG_DK
CHUNK = 128
SW_Q_HEADS = 16
SW_KV_HEADS = 2
SW_HEAD_DIM = 64
SW_WIDTH = SW_Q_HEADS * SW_HEAD_DIM
SW_KV_WIDTH = SW_KV_HEADS * SW_HEAD_DIM
SG_GROUPS = 4
SG_WIDTH = SG_GROUPS * CHUNK
D_IN = 4 * HG_WIDTH + SW_WIDTH + 2 * SW_KV_WIDTH + 2 * SG_WIDTH
N_EXPERTS = 64
TOP_K = 8
D_EXPERT = 512
N_EXPERT_GROUPS = 8
TOPK_GROUPS = 4
ROUTED_SCALE = 2.5
ALPHA = (2 * DEPTH) ** 0.25
LN_EPS = 1e-5
RMS_EPS = 1e-6

_SWQ_OFF = 4 * HG_WIDTH
_SGU_OFF = _SWQ_OFF + SW_WIDTH
_SWK_OFF = _SGU_OFF + 2 * SG_WIDTH


def _regroup_columns(a):
    k0 = _SWQ_OFF + SW_WIDTH
    k1 = k0 + 2 * SW_KV_WIDTH
    return jnp.concatenate([a[..., :k0], a[..., k1:], a[..., k0:k1]], axis=-1)


MOE_ROWS = 256
TOKEN_GROUPS = 2
ROUTER_TILE = 512
COMBINE_TILE = 256
VMEM_LIMIT = 56 * 1024 * 1024

_LEVEL_SIZES = (64, 32, 16, 8, 4, 2, 1)
_N_LEVELS = len(_LEVEL_SIZES)


def _hgrn_constants():
    t = np.arange(CHUNK)[:, None]
    u = np.arange(CHUNK)[None, :]
    mats = []
    level = np.full((CHUNK, CHUNK), -1, np.int32)
    for li, s in enumerate(_LEVEL_SIZES):
        blk = t // s
        odd = (blk % 2) == 1
        m_query = (u >= blk * s) & (u <= t)
        m_key = (u > t) & (u <= (blk + 1) * s - 1)
        mats.append(np.where(odd, m_query, m_key))
        pair = ((t // s) % 2 == 1) & ((u // s) == (t // s) - 1)
        level[pair] = li
    mats.append(u <= t)
    mats.append(u > t)
    level[np.arange(CHUNK), np.arange(CHUNK)] = _N_LEVELS
    return np.concatenate(mats, 0).astype(np.float32), level


_HGRN_MSTACK, _HGRN_LEVEL = _hgrn_constants()


def _layer_norm(x, g, b):
    mu = jnp.mean(x, axis=-1, keepdims=True)
    xc = x - mu
    var = jnp.mean(xc * xc, axis=-1, keepdims=True)
    return xc * lax.rsqrt(var + LN_EPS) * g + b


def _silu(x):
    return x / (1.0 + jnp.exp(-x))


def _gelu(x):
    return 0.5 * x * (1.0 + lax.erf(x * (2.0 ** -0.5)))


def _dot(a, b):
    return jnp.dot(a, b, preferred_element_type=F32)


def _dot_nt(a, b):
    return lax.dot_general(a, b, (((1,), (1,)), ((), ())), preferred_element_type=F32)


def _dot_tn(a, b):
    return lax.dot_general(a, b, (((0,), (0,)), ((), ())), preferred_element_type=F32)


def _ln_kernel(x_ref, g_ref, b_ref, of_ref, ob_ref):
    y = _layer_norm(x_ref[...], g_ref[...], b_ref[...])
    of_ref[...] = y
    ob_ref[...] = y.astype(BF16)


def _ln(x, g, b, tm=256):
    t, d = x.shape
    row = pl.BlockSpec((tm, d), lambda i: (i, 0))
    vec = pl.BlockSpec((1, d), lambda i: (0, 0))
    return pl.pallas_call(
        _ln_kernel,
        grid=(t // tm,),
        in_specs=[row, vec, vec],
        out_specs=[row, row],
        out_shape=[jax.ShapeDtypeStruct((t, d), F32), jax.ShapeDtypeStruct((t, d), BF16)],
        name="ln0",
    )(x, g.reshape(1, d), b.reshape(1, d))


def _mm_bias_kernel(a_ref, w_ref, b_ref, o_ref):
    o_ref[...] = _dot(a_ref[...], w_ref[...]) + b_ref[...]


def _in_proj(a, w, b, tm=512):
    t, k = a.shape
    n = w.shape[1]
    tm = min(tm, t)
    tn = n // 2
    return pl.pallas_call(
        _mm_bias_kernel,
        grid=(n // tn, t // tm),
        in_specs=[pl.BlockSpec((tm, k), lambda j, i: (i, 0)),
                  pl.BlockSpec((k, tn), lambda j, i: (0, j)),
                  pl.BlockSpec((1, tn), lambda j, i: (0, j))],
        out_specs=pl.BlockSpec((tm, tn), lambda j, i: (i, j)),
        out_shape=jax.ShapeDtypeStruct((t, n), F32),
        compiler_params=pltpu.CompilerParams(vmem_limit_bytes=VMEM_LIMIT),
        name="in_proj",
    )(a, w, b.reshape(1, n))


def _hgrn_kernel(q_ref, f_ref, i_ref, g_ref, llb_ref, l1lb_ref, oml_ref, gain_ref,
                 mstack_ref, level_ref, o_ref, state_ref):
    c = pl.program_id(1)

    @pl.when(c == 0)
    def _():
        state_ref[...] = jnp.zeros_like(state_ref)

    q = _silu(q_ref[...])
    z = f_ref[...]
    ez = jnp.exp(-jnp.abs(z))
    log_sig = jnp.minimum(z, 0.0) - jnp.log(1.0 + ez)
    sig_neg = jnp.where(z >= 0.0, ez, 1.0) / (1.0 + ez)
    k = oml_ref[...] * sig_neg
    a = llb_ref[...]
    bb = l1lb_ref[...] + log_sig
    log_f = jnp.maximum(a, bb) + jnp.log(1.0 + jnp.exp(-jnp.abs(a - bb)))

    lf_hi = log_f.astype(BF16)
    lf_lo = (log_f - lf_hi.astype(F32)).astype(BF16)
    m = mstack_ref[...]
    sums = _dot(m, lf_hi) + _dot(m, lf_lo)
    decay = jnp.exp(sums)

    rows = lax.broadcasted_iota(jnp.int32, (CHUNK, 1), 0)
    factors = []
    for li, s in enumerate(_LEVEL_SIZES):
        is_query = ((rows // s) % 2) == 1
        factors.append((jnp.where(is_query, q, k) * decay[li * CHUNK:(li + 1) * CHUNK]).astype(BF16))
    cum = _N_LEVELS * CHUNK
    q_in = (q * decay[cum:cum + CHUNK]).astype(BF16)
    k_dec = (k * decay[cum + CHUNK:cum + 2 * CHUNK]).astype(BF16)
    end_decay = decay[cum + CHUNK - 1:cum + CHUNK]
    qb = q.astype(BF16)
    kb = k.astype(BF16)
    vb = i_ref[...].astype(BF16)
    level = level_ref[...]
    gate = _silu(g_ref[...])
    gain = gain_ref[...]

    for h in range(HG_HEADS):
        hs = slice(h * HG_DK, (h + 1) * HG_DK)
        scores = jnp.zeros((CHUNK, CHUNK), F32)
        for li in range(_N_LEVELS):
            fl = factors[li][:, hs]
            scores = jnp.where(level == li, _dot_nt(fl, fl), scores)
        scores = jnp.where(level == _N_LEVELS, _dot_nt(qb[:, hs], kb[:, hs]), scores)
        st = state_ref[h]
        o = _dot(scores.astype(BF16), vb[:, hs]) + _dot_nt(q_in[:, hs], st.astype(BF16))
        state_ref[h] = st * end_decay[:, hs] + _dot_tn(vb[:, hs], k_dec[:, hs])
        ms = jnp.mean(o * o, axis=-1, keepdims=True)
        o = o * lax.rsqrt(ms + RMS_EPS) * gain[:, hs] * gate[:, hs]
        o_ref[:, hs] = o.astype(o_ref.dtype)


def _hgrn(proj3, llb, l1lb, oml, gain):
    b, s, _ = proj3.shape
    w = HG_WIDTH

    def col(j):
        return pl.BlockSpec((None, CHUNK, w), lambda bi, ci, j=j: (bi, ci, j))

    vec = pl.BlockSpec((1, w), lambda bi, ci: (0, 0))
    nm = _HGRN_MSTACK.shape[0]
    return pl.pallas_call(
        _hgrn_kernel,
        grid=(b, s // CHUNK),
        in_specs=[col(0), col(1), col(2), col(3), vec, vec, vec, vec,
                  pl.BlockSpec((nm, CHUNK), lambda bi, ci: (0, 0)),
                  pl.BlockSpec((CHUNK, CHUNK), lambda bi, ci: (0, 0))],
        out_specs=pl.BlockSpec((None, CHUNK, w), lambda bi, ci: (bi, ci, 0)),
        out_shape=jax.ShapeDtypeStruct((b, s, w), BF16),
        scratch_shapes=[pltpu.VMEM((HG_HEADS, HG_DK, HG_DK), F32)],
        compiler_params=pltpu.CompilerParams(dimension_semantics=("parallel", "arbitrary"),
                                             vmem_limit_bytes=VMEM_LIMIT),
        name="hgrn2",
    )(proj3, proj3, proj3, proj3, llb.reshape(1, w), l1lb.reshape(1, w), oml.reshape(1, w),
      gain.reshape(1, w), jnp.asarray(_HGRN_MSTACK, BF16), jnp.asarray(_HGRN_LEVEL))


def _swa_kernel(q_ref, kp_ref, kc_ref, vp_ref, vc_ref, sink_ref, gain_ref, o_ref):
    n = pl.program_id(1)
    g = SW_Q_HEADS // SW_KV_HEADS
    hd = SW_HEAD_DIM
    assert (hd ** -0.5) == 2.0 ** round(np.log2(hd ** -0.5))
    q = (q_ref[...] * (hd ** -0.5)).astype(BF16)
    kband = jnp.concatenate([kp_ref[...], kc_ref[...]], axis=0).astype(BF16)
    vband = jnp.concatenate([vp_ref[...], vc_ref[...]], axis=0).astype(BF16)
    t = lax.broadcasted_iota(jnp.int32, (CHUNK, 2 * CHUNK), 0)
    s = lax.broadcasted_iota(jnp.int32, (CHUNK, 2 * CHUNK), 1)
    rel = t + CHUNK - s
    mask = (rel >= 0) & (rel < CHUNK) & ((s >= CHUNK) | (n > 0))
    bias = jnp.where(mask, 0.0, -jnp.inf)
    kvs = [(kband[:, kv * hd:(kv + 1) * hd], vband[:, kv * hd:(kv + 1) * hd]) for kv in range(SW_KV_HEADS)]
    outs = []
    for h in range(SW_Q_HEADS):
        k_h, v_h = kvs[h // g]
        sc = _dot_nt(q[:, h * hd:(h + 1) * hd], k_h) + bias
        sink = sink_ref[h]
        mx = jnp.maximum(jnp.max(sc, axis=-1, keepdims=True), sink)
        p = jnp.exp(sc - mx)
        denom = jnp.sum(p, axis=-1, keepdims=True) + jnp.exp(sink - mx)
        outs.append(_dot(p.astype(BF16), v_h) / denom)
    o = jnp.concatenate(outs, axis=-1)
    ms = jnp.mean(o * o, axis=-1, keepdims=True)
    o_ref[...] = (o * lax.rsqrt(ms + RMS_EPS) * gain_ref[...]).astype(o_ref.dtype)


def _swa(proj3, sinks, gain):
    b, s, _ = proj3.shape
    q_col = _SWQ_OFF // SW_WIDTH
    k_col = _SWK_OFF // SW_KV_WIDTH
    v_col = k_col + 1

    def kv_spec(col, prev):
        if prev:
            return pl.BlockSpec((None, CHUNK, SW_KV_WIDTH), lambda bi, ni: (bi, jnp.maximum(ni - 1, 0), col))
        return pl.BlockSpec((None, CHUNK, SW_KV_WIDTH), lambda bi, ni: (bi, ni, col))

    return pl.pallas_call(
        _swa_kernel,
        grid=(b, s // CHUNK),
        in_specs=[pl.BlockSpec((None, CHUNK, SW_WIDTH), lambda bi, ni: (bi, ni, q_col)),
                  kv_spec(k_col, True), kv_spec(k_col, False),
                  kv_spec(v_col, True), kv_spec(v_col, False),
                  pl.BlockSpec(memory_space=pltpu.SMEM),
                  pl.BlockSpec((1, SW_WIDTH), lambda bi, ni: (0, 0))],
        out_specs=pl.BlockSpec((None, CHUNK, SW_WIDTH), lambda bi, ni: (bi, ni, 0)),
        out_shape=jax.ShapeDtypeStruct((b, s, SW_WIDTH), BF16),
        compiler_params=pltpu.CompilerParams(dimension_semantics=("parallel", "parallel"),
                                             vmem_limit_bytes=VMEM_LIMIT),
        name="swa",
    )(proj3, proj3, proj3, proj3, proj3, sinks.astype(F32), gain.reshape(1, SW_WIDTH))


def _sgu_kernel(u_ref, v_ref, lng_ref, lnb_ref, w_ref, bs_ref, gain_ref, o_ref):
    u = _gelu(u_ref[...])
    v = _layer_norm(_gelu(v_ref[...]), lng_ref[...], lnb_ref[...]).astype(BF16)
    r = lax.broadcasted_iota(jnp.int32, (CHUNK, CHUNK), 0)
    c = lax.broadcasted_iota(jnp.int32, (CHUNK, CHUNK), 1)
    tril = c <= r
    bs = bs_ref[...]
    parts = []
    for gi in range(SG_GROUPS):
        gs = slice(gi * CHUNK, (gi + 1) * CHUNK)
        w = jnp.where(tril, w_ref[gi], 0.0).astype(BF16)
        parts.append(_dot(w, v[:, gs]) + bs[:, gi:gi + 1])
    o = u * jnp.concatenate(parts, axis=-1)
    ms = jnp.mean(o * o, axis=-1, keepdims=True)
    o_ref[...] = (o * lax.rsqrt(ms + RMS_EPS) * gain_ref[...]).astype(o_ref.dtype)


def _sgu(proj3, ln_g, ln_b, w_s, b_s, gain):
    b, s, _ = proj3.shape
    w = SG_WIDTH
    u_col = _SGU_OFF // SG_WIDTH
    vec = pl.BlockSpec((1, w), lambda bi, ci: (0, 0))
    return pl.pallas_call(
        _sgu_kernel,
        grid=(b, s // CHUNK),
        in_specs=[pl.BlockSpec((None, CHUNK, w), lambda bi, ci: (bi, ci, u_col)),
                  pl.BlockSpec((None, CHUNK, w), lambda bi, ci: (bi, ci, u_col + 1)),
                  vec, vec,
                  pl.BlockSpec((SG_GROUPS, CHUNK, CHUNK), lambda bi, ci: (0, 0, 0)),
                  pl.BlockSpec((CHUNK, SG_GROUPS), lambda bi, ci: (0, 0)),
                  vec],
        out_specs=pl.BlockSpec((None, CHUNK, w), lambda bi, ci: (bi, ci, 0)),
        out_shape=jax.ShapeDtypeStruct((b, s, w), BF16),
        compiler_params=pltpu.CompilerParams(dimension_semantics=("parallel", "parallel"),
                                             vmem_limit_bytes=VMEM_LIMIT),
        name="sgu",
    )(proj3, proj3, ln_g.reshape(1, w), ln_b.reshape(1, w), w_s, b_s.T, gain.reshape(1, w))


LANES = 128
SLAB = D_MODEL // 2 // LANES


def _pack_pair(low, high):
    def bits(v):
        return lax.bitcast_convert_type(v.astype(BF16).astype(F32), jnp.uint32)

    return (bits(high) & jnp.uint32(0xFFFF0000)) | (bits(low) >> 16)


def _pack_rows(x):
    half = x.shape[1] // 2
    return _pack_pair(x[:, :half], x[:, half:])


def _unpack_words(w):
    return (lax.bitcast_convert_type(w << 16, F32),
            lax.bitcast_convert_type(w & jnp.uint32(0xFFFF0000), F32))


def _store_slabs(ref, x):
    n = x.shape[0]
    for j in range(SLAB):
        ref[pl.ds(j, n, stride=SLAB), :] = x[:, j * LANES:(j + 1) * LANES]


def _load_packed_rows(ref, n, first=0, stride=SLAB):
    lows, highs = [], []
    for j in range(SLAB):
        lo, hi = _unpack_words(ref[pl.ds(first + j, n, stride=stride), :])
        lows.append(lo.astype(BF16))
        highs.append(hi.astype(BF16))
    return jnp.concatenate(lows + highs, axis=1)


def _out_proj_kernel(oa_ref, ob_ref, oc_ref, w_ref, h_ref, g_ref, b_ref, of_ref, os_ref):
    mixed = jnp.concatenate([oa_ref[...], ob_ref[...], oc_ref[...]], axis=-1)
    y = ALPHA * h_ref[...] + _dot(mixed, w_ref[...])
    out = _layer_norm(y, g_ref[...], b_ref[...])
    of_ref[...] = out
    _store_slabs(os_ref, _pack_rows(out))


def _out_proj(oa, ob, oc, w, h, g, b, tm=256):
    t, d = h.shape
    vec = pl.BlockSpec((1, d), lambda i: (0, 0))
    return pl.pallas_call(
        _out_proj_kernel,
        grid=(t // tm,),
        in_specs=[pl.BlockSpec((tm, oa.shape[1]), lambda i: (i, 0)),
                  pl.BlockSpec((tm, ob.shape[1]), lambda i: (i, 0)),
                  pl.BlockSpec((tm, oc.shape[1]), lambda i: (i, 0)),
                  pl.BlockSpec(w.shape, lambda i: (0, 0)),
                  pl.BlockSpec((tm, d), lambda i: (i, 0)), vec, vec],
        out_specs=[pl.BlockSpec((tm, d), lambda i: (i, 0)),
                   pl.BlockSpec((tm * SLAB, LANES), lambda i: (i, 0))],
        out_shape=[jax.ShapeDtypeStruct((t, d), F32), jax.ShapeDtypeStruct((t * SLAB, LANES), jnp.uint32)],
        compiler_params=pltpu.CompilerParams(vmem_limit_bytes=VMEM_LIMIT),
        name="out_proj_ln1",
    )(oa, ob, oc, w, h, g.reshape(1, d), b.reshape(1, d))


def _first_index_of_max(x, iota, size, axis):
    mx = jnp.max(x, axis=axis, keepdims=True)
    idx = jnp.min(jnp.where(x == mx, iota, size), axis=axis, keepdims=True)
    return mx, idx


def _router_kernel(tiles_per_group, h_ref, wt_ref, bias_ref, eidx_ref, gate_ref, count_ref, run_ref):
    i = pl.program_id(0)

    @pl.when(i % tiles_per_group == 0)
    def _():
        run_ref[...] = jnp.zeros_like(run_ref)

    tm = h_ref.shape[0]
    per_group = N_EXPERTS // N_EXPERT_GROUPS
    def split(v):
        hi = v.astype(BF16)
        return hi, (v - hi.astype(F32)).astype(BF16)

    w_hi, w_lo = split(wt_ref[...])
    h_hi, h_lo = split(h_ref[...])
    logits = _dot_nt(w_hi, h_hi) + (_dot_nt(w_hi, h_lo) + _dot_nt(w_lo, h_hi))
    scores = 1.0 / (1.0 + jnp.exp(-logits))
    sel = scores + bias_ref[...]
    sel3 = sel.reshape(N_EXPERT_GROUPS, per_group, tm)
    io3 = lax.broadcasted_iota(jnp.int32, sel3.shape, 1)
    m1, i1 = _first_index_of_max(sel3, io3, per_group, 1)
    m2 = jnp.max(jnp.where(io3 == i1, -jnp.inf, sel3), axis=1, keepdims=True)
    grp = (m1 + m2).reshape(N_EXPERT_GROUPS, tm)
    iog = lax.broadcasted_iota(jnp.int32, grp.shape, 0)
    keep = jnp.zeros(grp.shape, jnp.bool_)
    for _ in range(TOPK_GROUPS):
        _, gi = _first_index_of_max(grp, iog, N_EXPERT_GROUPS, 0)
        hit = iog == gi
        keep = keep | hit
        grp = jnp.where(hit, -jnp.inf, grp)
    keep3 = jnp.broadcast_to(keep.reshape(N_EXPERT_GROUPS, 1, tm), sel3.shape)
    cand = jnp.where(keep3, sel3, -jnp.inf).reshape(N_EXPERTS, tm)
    ioe = lax.broadcasted_iota(jnp.int32, cand.shape, 0)
    chosen = jnp.zeros(cand.shape, F32)
    idxs, gvals = [], []
    for _ in range(TOP_K):
        _, ei = _first_index_of_max(cand, ioe, N_EXPERTS, 0)
        hit = ioe == ei
        idxs.append(ei)
        gvals.append(jnp.sum(jnp.where(hit, scores, 0.0), axis=0, keepdims=True))
        chosen = jnp.where(hit, 1.0, chosen)
        cand = jnp.where(hit, -jnp.inf, cand)
    gsum = functools.reduce(lambda a, b: a + b, gvals)
    for kk in range(TOP_K):
        eidx_ref[kk:kk + 1, :] = idxs[kk]
        gate_ref[kk:kk + 1, :] = gvals[kk] / gsum * ROUTED_SCALE
    run = run_ref[...] + jnp.sum(chosen, axis=1, keepdims=True)
    run_ref[...] = run
    count_ref[...] = run.astype(jnp.int32)


def _router(h, router_w, router_bias, n_groups, tm=ROUTER_TILE):
    t, d = h.shape
    tm = min(tm, t // n_groups)
    tiles_per_group = t // n_groups // tm
    slot = pl.BlockSpec((TOP_K, tm), lambda i: (0, i))
    return pl.pallas_call(
        functools.partial(_router_kernel, tiles_per_group),
        grid=(t // tm,),
        in_specs=[pl.BlockSpec((tm, d), lambda i: (i, 0)),
                  pl.BlockSpec((N_EXPERTS, d), lambda i: (0, 0)),
                  pl.BlockSpec((N_EXPERTS, 1), lambda i: (0, 0))],
        out_specs=[slot, slot, pl.BlockSpec((None, N_EXPERTS, 1), lambda i: (i // tiles_per_group, 0, 0))],
        out_shape=[jax.ShapeDtypeStruct((TOP_K, t), jnp.int32),
                   jax.ShapeDtypeStruct((TOP_K, t), F32),
                   jax.ShapeDtypeStruct((n_groups, N_EXPERTS, 1), jnp.int32)],
        scratch_shapes=[pltpu.VMEM((N_EXPERTS, 1), F32)],
        compiler_params=pltpu.CompilerParams(dimension_semantics=("arbitrary",),
                                             vmem_limit_bytes=VMEM_LIMIT),
        name="router",
    )(h, router_w.T, router_bias.reshape(N_EXPERTS, 1))


_ISSUE_UNROLL = 8
_TOP_K_BITS = TOP_K.bit_length() - 1
assert 1 << _TOP_K_BITS == TOP_K


_OUT_BUFFERS = 3


def _experts_kernel(layer, be_ref, bg_ref, wplan_ref, nu_ref, src_ref, dstp_ref, dstc_ref, h_hbm,
                    wg_hbm, wu_hbm, wd_hbm, y_hbm, tok_ref, xbuf, obuf0, obuf1, obuf2, wg_buf, wu_buf, wd_buf,
                    lsem, ssem, wsem):
    b = pl.program_id(0)
    nu = nu_ref[0]
    obuf = (obuf0, obuf1, obuf2)
    rows = xbuf.shape[0] // SLAB
    d = wg_buf.shape[1]
    f = wg_buf.shape[2]
    wslot = wplan_ref[1, b]

    def weight_copies(expert, slot_):
        return [pltpu.make_async_copy(src.at[layer, expert], dst.at[slot_], wsem.at[slot_])
                for src, dst in ((wg_hbm, wg_buf), (wu_hbm, wu_buf), (wd_hbm, wd_buf))]

    @pl.when((b < nu) & (wplan_ref[0, b] == 1))
    def _():
        @pl.when(b == 0)
        def _():
            for c in weight_copies(be_ref[0], wslot):
                c.start()
        for c in weight_copies(be_ref[b], wslot):
            c.wait()
        nxt = wplan_ref[2, b]

        @pl.when(nxt >= 0)
        def _():
            for c in weight_copies(nxt, 1 - wslot):
                c.start()

    wg_ref, wu_ref, wd_ref = wg_buf.at[wslot], wu_buf.at[wslot], wd_buf.at[wslot]
    pad0 = y_hbm.shape[0] - 2 * rows * SLAB

    def scatter_row(dst_ref, r, src_slot, priority):
        dst = pl.multiple_of(dst_ref[0, 0, r], SLAB)
        buf_rows = pl.ds(pl.multiple_of(r * SLAB, SLAB), SLAB)
        pltpu.make_async_copy(obuf[src_slot].at[buf_rows], y_hbm.at[pl.ds(dst, SLAB)],
                              ssem.at[src_slot]).start(priority=priority)

    def wait_scatter(s):
        pltpu.make_async_copy(obuf[s], y_hbm.at[pl.ds(0, rows * SLAB)], ssem.at[s]).wait()

    @pl.when(b == 0)
    def _():
        for s in range(2):
            obuf[s][...] = jnp.zeros_like(obuf[s])
            pltpu.make_async_copy(obuf[s], y_hbm.at[pl.ds(pad0 + s * rows * SLAB, rows * SLAB)],
                                  ssem.at[s]).start()
        for s in range(2):
            wait_scatter(s)

    group = bg_ref[b]

    @pl.when((b < nu) & ((b == 0) | (group != bg_ref[jnp.maximum(b - 1, 0)])))
    def _():
        load = pltpu.make_async_copy(h_hbm.at[group], tok_ref, lsem.at[0])
        load.start()
        load.wait()

    def block(slot, with_scatter):
        other = (slot - 1) % _OUT_BUFFERS

        def gather(i, carry):
            for u in range(_ISSUE_UNROLL):
                r = i * _ISSUE_UNROLL + u
                src = pl.multiple_of(src_ref[0, 0, r], SLAB)
                xbuf[pl.ds(pl.multiple_of(r * SLAB, SLAB), SLAB), :] = tok_ref[pl.ds(src, SLAB), :]
            return carry
        lax.fori_loop(0, rows // _ISSUE_UNROLL, gather, 0)

        if with_scatter:
            for r in range(rows):
                scatter_row(dstp_ref, r, other, r % 2)

        x = _load_packed_rows(xbuf, rows)
        acts = []
        fh = f // 2
        for j in range(2):
            gate = _dot(x, wg_ref[:, j * fh:(j + 1) * fh])
            up = _dot(x, wu_ref[:, j * fh:(j + 1) * fh])
            acts.append((_silu(gate) * up).astype(BF16))
        act = jnp.concatenate(acts, axis=1)
        half = d // 2
        cw = 2 * LANES
        for q in range(half // cw):
            words = _pack_pair(_dot(act, wd_ref[:, q * cw:(q + 1) * cw]),
                               _dot(act, wd_ref[:, half + q * cw:half + (q + 1) * cw]))
            for jj in range(cw // LANES):
                obuf[slot][pl.ds(q * (cw // LANES) + jj, rows, stride=SLAB), :] = words[:, jj * LANES:(jj + 1) * LANES]

    def tail(slot):
        for back in (1, 2):
            @pl.when(b >= back)
            def _(back=back):
                wait_scatter((slot - back) % _OUT_BUFFERS)

        def body(r, carry):
            scatter_row(dstc_ref, r, slot, 0)
            return carry
        lax.fori_loop(0, rows, body, 0)
        wait_scatter(slot)

    @pl.when(b == 0)
    def _():
        block(0, False)

    for s in range(_OUT_BUFFERS):
        @pl.when((b >= 1) & (b < nu) & (b % _OUT_BUFFERS == s))
        def _(s=s):
            @pl.when(b >= _OUT_BUFFERS)
            def _():
                wait_scatter(s)
            block(s, True)

        @pl.when((b == nu - 1) & (b % _OUT_BUFFERS == s))
        def _(s=s):
            tail(s)


def _experts(h_groups, row_src, row_dst, block_e, block_g, weight_plan, n_used, wg, wu, wd, layer, n_slots):
    nb, _, rows = row_src.shape
    d, f = wg.shape[-2], wg.shape[-1]
    group_rows = h_groups.shape[1]

    def smem(shift):
        def index(b, be, bg, wplan, nu):
            return (jnp.maximum(jnp.minimum(b + shift, nu[0] - 1), 0), 0, 0)
        return pl.BlockSpec((1, 1, rows), index, memory_space=pltpu.SMEM)

    hbm = pl.BlockSpec(memory_space=pl.ANY)
    grid_spec = pltpu.PrefetchScalarGridSpec(
        num_scalar_prefetch=4,
        grid=(nb,),
        in_specs=[smem(0), smem(-1), smem(0), hbm, hbm, hbm, hbm],
        out_specs=hbm,
        scratch_shapes=[pltpu.VMEM((group_rows, LANES), jnp.uint32)]
        + [pltpu.VMEM((rows * SLAB, LANES), jnp.uint32)] * (1 + _OUT_BUFFERS)
        + [pltpu.VMEM((2, d, f), BF16), pltpu.VMEM((2, d, f), BF16), pltpu.VMEM((2, f, d), BF16)]
        + [pltpu.SemaphoreType.DMA((1,)), pltpu.SemaphoreType.DMA((_OUT_BUFFERS,)),
           pltpu.SemaphoreType.DMA((2,))],
    )
    return pl.pallas_call(
        functools.partial(_experts_kernel, layer),
        grid_spec=grid_spec,
        out_shape=jax.ShapeDtypeStruct(((n_slots + 2 * rows) * SLAB, LANES), jnp.uint32),
        compiler_params=pltpu.CompilerParams(dimension_semantics=("arbitrary",),
                                             vmem_limit_bytes=VMEM_LIMIT),
        name="experts",
    )(block_e, block_g, weight_plan, n_used, row_src, row_dst, row_dst, h_groups, wg, wu, wd)


_COMBINE_SUB = 32


def _combine_kernel(h_ref, gate_ref, *refs):
    y_refs = refs[:TOP_K]
    sg_ref, su_ref, sd_ref, g_ref, b_ref, of_ref, ob_ref, moe_ref = refs[TOP_K:]
    h = h_ref[...]
    tm = h.shape[0]
    half = SLAB * LANES
    for s in range(tm // _COMBINE_SUB):
        rs = slice(s * _COMBINE_SUB, (s + 1) * _COMBINE_SUB)
        gates = gate_ref[rs, :]
        first = s * _COMBINE_SUB * SLAB
        for j in range(SLAB):
            lo_acc = hi_acc = None
            for kk in range(TOP_K):
                lo, hi = _unpack_words(y_refs[kk][pl.ds(first + j, _COMBINE_SUB, stride=SLAB), :])
                gk = gates[:, kk:kk + 1]
                lo_acc = gk * lo if kk == 0 else lo_acc + gk * lo
                hi_acc = gk * hi if kk == 0 else hi_acc + gk * hi
            moe_ref[rs, j * LANES:(j + 1) * LANES] = lo_acc
            moe_ref[rs, half + j * LANES:half + (j + 1) * LANES] = hi_acc
    hb = h.astype(BF16)
    act = _silu(_dot(hb, sg_ref[...])) * _dot(hb, su_ref[...])
    y = ALPHA * h + _dot(act.astype(BF16), sd_ref[...]) + moe_ref[...]
    out = _layer_norm(y, g_ref[...], b_ref[...])
    of_ref[...] = out
    ob_ref[...] = out.astype(BF16)


def _combine(h, gates_t, y_slots, sg, su, sd, g, b, tm=COMBINE_TILE):
    t, d = h.shape
    n = t // tm
    f = sg.shape[1]
    vec = pl.BlockSpec((1, d), lambda i: (0, 0))
    row = pl.BlockSpec((tm, d), lambda i: (i, 0))
    return pl.pallas_call(
        _combine_kernel,
        grid=(n,),
        in_specs=[row, pl.BlockSpec((tm, TOP_K), lambda i: (i, 0))]
        + [pl.BlockSpec((tm * SLAB, LANES), lambda i, kk=kk: (kk * n + i, 0)) for kk in range(TOP_K)]
        + [pl.BlockSpec((d, f), lambda i: (0, 0)),
            pl.BlockSpec((d, f), lambda i: (0, 0)),
            pl.BlockSpec((f, d), lambda i: (0, 0)),
            vec, vec],
        out_specs=[row, row],
        out_shape=[jax.ShapeDtypeStruct((t, d), F32), jax.ShapeDtypeStruct((t, d), BF16)],
        scratch_shapes=[pltpu.VMEM((tm, d), F32)],
        compiler_params=pltpu.CompilerParams(dimension_semantics=("parallel",),
                                             vmem_limit_bytes=VMEM_LIMIT),
        name="combine_ln2",
    )(h, gates_t, *([y_slots] * TOP_K), sg, su, sd, g.reshape(1, d), b.reshape(1, d))


def _moe(h, h_slab, router_w, router_bias, wg, wu, wd, layer, sg, su, sd, ln_g, ln_b):
    t, d = h.shape
    tk = t * TOP_K
    n_groups = TOKEN_GROUPS if t % (TOKEN_GROUPS * ROUTER_TILE) == 0 else 1
    tg = t // n_groups
    n_buckets = n_groups * N_EXPERTS
    n_blocks = tk // MOE_ROWS + n_buckets
    eidx, gates, counts = _router(h, router_w, router_bias, n_groups)
    counts = counts.reshape(n_buckets)
    n_rows = n_blocks * MOE_ROWS
    padded = (counts + MOE_ROWS - 1) // MOE_ROWS * MOE_ROWS
    pend = jnp.cumsum(padded)
    pad_end = jnp.cumsum(padded - counts)
    pad_key = jnp.sum(pad_end[None, :] <= jnp.arange(n_rows - tk, dtype=jnp.int32)[:, None], axis=1)
    slot_group = (jnp.arange(tk, dtype=jnp.int32) >> _TOP_K_BITS) // tg
    keys = jnp.concatenate([slot_group * N_EXPERTS + eidx.T.reshape(tk), pad_key.astype(jnp.int32)])
    _, row_slot = lax.sort((keys, jnp.arange(n_rows, dtype=jnp.int32)), num_keys=1, is_stable=True)
    real = row_slot < tk
    row = jnp.arange(n_rows, dtype=jnp.int32)
    row_src = jnp.where(real, ((row_slot >> _TOP_K_BITS) % tg) * SLAB, 0)
    plane_row = (row_slot & (TOP_K - 1)) * t + (row_slot >> _TOP_K_BITS)
    row_dst = jnp.where(real, plane_row, tk + row % (2 * MOE_ROWS)) * SLAB
    block_start = jnp.arange(n_blocks, dtype=jnp.int32) * MOE_ROWS
    block_bucket = jnp.minimum(jnp.sum(pend[None, :] <= block_start[:, None], axis=1), n_buckets - 1).astype(jnp.int32)
    n_used = (pend[-1] // MOE_ROWS).astype(jnp.int32)
    block_e = block_bucket % N_EXPERTS
    block = jnp.arange(n_blocks, dtype=jnp.int32)
    first = (block == 0) | (block_bucket != jnp.roll(block_bucket, 1))
    seq = jnp.cumsum(first.astype(jnp.int32)) - 1
    later_first = jnp.where(first & (block < n_used), block, n_blocks)
    next_first = jnp.concatenate([lax.cummin(later_first, reverse=True)[1:], jnp.full((1,), n_blocks, jnp.int32)])
    next_e = jnp.where(next_first < n_blocks, block_e[jnp.minimum(next_first, n_blocks - 1)], -1)
    weight_plan = jnp.stack([first.astype(jnp.int32), seq % 2, next_e]).astype(jnp.int32)
    y_slots = _experts(h_slab.reshape(n_groups, tg * SLAB, LANES),
                       row_src.reshape(n_blocks, 1, MOE_ROWS), row_dst.reshape(n_blocks, 1, MOE_ROWS),
                       block_e, block_bucket // N_EXPERTS, weight_plan, n_used.reshape(1),
                       wg, wu, wd, layer, tk)
    return _combine(h, gates.T, y_slots, sg, su, sd, ln_g, ln_b, tm=min(COMBINE_TILE, t))


def kernel(x, ln0_gain, ln0_bias, w_in, b_in, hg_lb_logits, sw_sinks, sg_ln_gain, sg_ln_bias, sg_w_s, sg_b_s, mix_gain, w_out, ln1_gain, ln1_bias, router_w, router_bias, exp_w_gate, exp_w_up, exp_w_down, sh_w_gate, sh_w_up, sh_w_down, ln2_gain, ln2_bias):
    b, s, d = x.shape
    t = b * s
    p = jax.nn.softmax(hg_lb_logits.astype(F32), axis=0)
    cs = jnp.cumsum(p, axis=0)
    lbs = cs - cs[0:1]
    a0, a1 = HG_WIDTH, HG_WIDTH + SW_WIDTH

    wg_all, wu_all, wd_all = exp_w_gate.astype(BF16), exp_w_up.astype(BF16), exp_w_down.astype(BF16)
    h, hb = _ln(x.reshape(t, d), ln0_gain, ln0_bias)
    for l in range(DEPTH):
        proj = _in_proj(hb, _regroup_columns(w_in[l]).astype(BF16), _regroup_columns(b_in[l]))
        proj3 = proj.reshape(b, s, D_IN)
        lb = lbs[l]
        o_a = _hgrn(proj3, jnp.log(lb), jnp.log1p(-lb), 1.0 - lb, mix_gain[l, :a0])
        o_b = _swa(proj3, sw_sinks[l], mix_gain[l, a0:a1])
        o_c = _sgu(proj3, sg_ln_gain[l], sg_ln_bias[l], sg_w_s[l], sg_b_s[l], mix_gain[l, a1:])
        h, h_slab = _out_proj(o_a.reshape(t, -1), o_b.reshape(t, -1), o_c.reshape(t, -1),
                              w_out[l].astype(BF16), h, ln1_gain[l], ln1_bias[l])
        h, hb = _moe(h, h_slab, router_w[l], router_bias[l], wg_all, wu_all, wd_all, l,
                     sh_w_gate[l].astype(BF16), sh_w_up[l].astype(BF16), sh_w_down[l].astype(BF16),
                     ln2_gain[l], ln2_bias[l])
    return h.reshape(b, s, d)
```

```python
import functools

import numpy as np
import jax
import jax.numpy as jnp
from jax import lax
from jax.experimental import pallas as pl
from jax.experimental.pallas import tpu as pltpu

F32 = jnp.float32
BF16 = jnp.bfloat16

D_MODEL = 2048
DEPTH = 2
HG_HEADS = 4
HG_DK = 128
HG_WIDTH = HG_HEADS * HG_DK
CHUNK = 128
SW_Q_HEADS = 16
SW_KV_HEADS = 2
SW_HEAD_DIM = 64
SW_WIDTH = SW_Q_HEADS * SW_HEAD_DIM
SW_KV_WIDTH = SW_KV_HEADS * SW_HEAD_DIM
SG_GROUPS = 4
SG_WIDTH = SG_GROUPS * CHUNK
D_IN = 4 * HG_WIDTH + SW_WIDTH + 2 * SW_KV_WIDTH + 2 * SG_WIDTH
N_EXPERTS = 64
TOP_K = 8
D_EXPERT = 512
N_EXPERT_GROUPS = 8
TOPK_GROUPS = 4
ROUTED_SCALE = 2.5
ALPHA = (2 * DEPTH) ** 0.25
LN_EPS = 1e-5
RMS_EPS = 1e-6

_SWQ_OFF = 4 * HG_WIDTH
_SGU_OFF = _SWQ_OFF + SW_WIDTH
_SWK_OFF = _SGU_OFF + 2 * SG_WIDTH


def _regroup_columns(a):
    k0 = _SWQ_OFF + SW_WIDTH
    k1 = k0 + 2 * SW_KV_WIDTH
    return jnp.concatenate([a[..., :k0], a[..., k1:], a[..., k0:k1]], axis=-1)


MOE_ROWS = 256
TOKEN_GROUPS = 2
ROUTER_TILE = 512
COMBINE_TILE = 256
VMEM_LIMIT = 56 * 1024 * 1024

_LEVEL_SIZES = (64, 32, 16, 8, 4, 2, 1)
_N_LEVELS = len(_LEVEL_SIZES)


def _hgrn_constants():
    t = np.arange(CHUNK)[:, None]
    u = np.arange(CHUNK)[None, :]
    mats = []
    level = np.full((CHUNK, CHUNK), -1, np.int32)
    for li, s in enumerate(_LEVEL_SIZES):
        blk = t // s
        odd = (blk % 2) == 1
        m_query = (u >= blk * s) & (u <= t)
        m_key = (u > t) & (u <= (blk + 1) * s - 1)
        mats.append(np.where(odd, m_query, m_key))
        pair = ((t // s) % 2 == 1) & ((u // s) == (t // s) - 1)
        level[pair] = li
    mats.append(u <= t)
    mats.append(u > t)
    level[np.arange(CHUNK), np.arange(CHUNK)] = _N_LEVELS
    return np.concatenate(mats, 0).astype(np.float32), level


_HGRN_MSTACK, _HGRN_LEVEL = _hgrn_constants()


def _layer_norm(x, g, b):
    mu = jnp.mean(x, axis=-1, keepdims=True)
    xc = x - mu
    var = jnp.mean(xc * xc, axis=-1, keepdims=True)
    return xc * lax.rsqrt(var + LN_EPS) * g + b


def _silu(x):
    return x / (1.0 + jnp.exp(-x))


def _gelu(x):
    return 0.5 * x * (1.0 + lax.erf(x * (2.0 ** -0.5)))


def _dot(a, b):
    return jnp.dot(a, b, preferred_element_type=F32)


def _dot_nt(a, b):
    return lax.dot_general(a, b, (((1,), (1,)), ((), ())), preferred_element_type=F32)


def _dot_tn(a, b):
    return lax.dot_general(a, b, (((0,), (0,)), ((), ())), preferred_element_type=F32)


def _ln_kernel(x_ref, g_ref, b_ref, of_ref, ob_ref):
    y = _layer_norm(x_ref[...], g_ref[...], b_ref[...])
    of_ref[...] = y
    ob_ref[...] = y.astype(BF16)


def _ln(x, g, b, tm=256):
    t, d = x.shape
    row = pl.BlockSpec((tm, d), lambda i: (i, 0))
    vec = pl.BlockSpec((1, d), lambda i: (0, 0))
    return pl.pallas_call(
        _ln_kernel,
        grid=(t // tm,),
        in_specs=[row, vec, vec],
        out_specs=[row, row],
        out_shape=[jax.ShapeDtypeStruct((t, d), F32), jax.ShapeDtypeStruct((t, d), BF16)],
        name="ln0",
    )(x, g.reshape(1, d), b.reshape(1, d))


def _mm_bias_kernel(a_ref, w_ref, b_ref, o_ref):
    o_ref[...] = _dot(a_ref[...], w_ref[...]) + b_ref[...]


def _in_proj(a, w, b, tm=512):
    t, k = a.shape
    n = w.shape[1]
    tm = min(tm, t)
    tn = n // 2
    return pl.pallas_call(
        _mm_bias_kernel,
        grid=(n // tn, t // tm),
        in_specs=[pl.BlockSpec((tm, k), lambda j, i: (i, 0)),
                  pl.BlockSpec((k, tn), lambda j, i: (0, j)),
                  pl.BlockSpec((1, tn), lambda j, i: (0, j))],
        out_specs=pl.BlockSpec((tm, tn), lambda j, i: (i, j)),
        out_shape=jax.ShapeDtypeStruct((t, n), F32),
        compiler_params=pltpu.CompilerParams(vmem_limit_bytes=VMEM_LIMIT),
        name="in_proj",
    )(a, w, b.reshape(1, n))


def _hgrn_kernel(q_ref, f_ref, i_ref, g_ref, llb_ref, l1lb_ref, oml_ref, gain_ref,
                 mstack_ref, level_ref, o_ref, state_ref):
    c = pl.program_id(1)

    @pl.when(c == 0)
    def _():
        state_ref[...] = jnp.zeros_like(state_ref)

    q = _silu(q_ref[...])
    z = f_ref[...]
    ez = jnp.exp(-jnp.abs(z))
    log_sig = jnp.minimum(z, 0.0) - jnp.log(1.0 + ez)
    sig_neg = jnp.where(z >= 0.0, ez, 1.0) / (1.0 + ez)
    k = oml_ref[...] * sig_neg
    a = llb_ref[...]
    bb = l1lb_ref[...] + log_sig
    log_f = jnp.maximum(a, bb) + jnp.log(1.0 + jnp.exp(-jnp.abs(a - bb)))

    lf_hi = log_f.astype(BF16)
    lf_lo = (log_f - lf_hi.astype(F32)).astype(BF16)
    m = mstack_ref[...]
    sums = _dot(m, lf_hi) + _dot(m, lf_lo)
    decay = jnp.exp(sums)

    rows = lax.broadcasted_iota(jnp.int32, (CHUNK, 1), 0)
    factors = []
    for li, s in enumerate(_LEVEL_SIZES):
        is_query = ((rows // s) % 2) == 1
        factors.append((jnp.where(is_query, q, k) * decay[li * CHUNK:(li + 1) * CHUNK]).astype(BF16))
    cum = _N_LEVELS * CHUNK
    q_in = (q * decay[cum:cum + CHUNK]).astype(BF16)
    k_dec = (k * decay[cum + CHUNK:cum + 2 * CHUNK]).astype(BF16)
    end_decay = decay[cum + CHUNK - 1:cum + CHUNK]
    qb = q.astype(BF16)
    kb = k.astype(BF16)
    vb = i_ref[...].astype(BF16)
    level = level_ref[...]
    gate = _silu(g_ref[...])
    gain = gain_ref[...]

    for h in range(HG_HEADS):
        hs = slice(h * HG_DK, (h + 1) * HG_DK)
        scores = jnp.zeros((CHUNK, CHUNK), F32)
        for li in range(_N_LEVELS):
            fl = factors[li][:, hs]
            scores = jnp.where(level == li, _dot_nt(fl, fl), scores)
        scores = jnp.where(level == _N_LEVELS, _dot_nt(qb[:, hs], kb[:, hs]), scores)
        st = state_ref[h]
        o = _dot(scores.astype(BF16), vb[:, hs]) + _dot_nt(q_in[:, hs], st.astype(BF16))
        state_ref[h] = st * end_decay[:, hs] + _dot_tn(vb[:, hs], k_dec[:, hs])
        ms = jnp.mean(o * o, axis=-1, keepdims=True)
        o = o * lax.rsqrt(ms + RMS_EPS) * gain[:, hs] * gate[:, hs]
        o_ref[:, hs] = o.astype(o_ref.dtype)


def _hgrn(proj3, llb, l1lb, oml, gain):
    b, s, _ = proj3.shape
    w = HG_WIDTH

    def col(j):
        return pl.BlockSpec((None, CHUNK, w), lambda bi, ci, j=j: (bi, ci, j))

    vec = pl.BlockSpec((1, w), lambda bi, ci: (0, 0))
    nm = _HGRN_MSTACK.shape[0]
    return pl.pallas_call(
        _hgrn_kernel,
        grid=(b, s // CHUNK),
        in_specs=[col(0), col(1), col(2), col(3), vec, vec, vec, vec,
                  pl.BlockSpec((nm, CHUNK), lambda bi, ci: (0, 0)),
                  pl.BlockSpec((CHUNK, CHUNK), lambda bi, ci: (0, 0))],
        out_specs=pl.BlockSpec((None, CHUNK, w), lambda bi, ci: (bi, ci, 0)),
        out_shape=jax.ShapeDtypeStruct((b, s, w), BF16),
        scratch_shapes=[pltpu.VMEM((HG_HEADS, HG_DK, HG_DK), F32)],
        compiler_params=pltpu.CompilerParams(dimension_semantics=("parallel", "arbitrary"),
                                             vmem_limit_bytes=VMEM_LIMIT),
        name="hgrn2",
    )(proj3, proj3, proj3, proj3, llb.reshape(1, w), l1lb.reshape(1, w), oml.reshape(1, w),
      gain.reshape(1, w), jnp.asarray(_HGRN_MSTACK, BF16), jnp.asarray(_HGRN_LEVEL))


def _swa_kernel(q_ref, kp_ref, kc_ref, vp_ref, vc_ref, sink_ref, gain_ref, o_ref):
    n = pl.program_id(1)
    g = SW_Q_HEADS // SW_KV_HEADS
    hd = SW_HEAD_DIM
    assert (hd ** -0.5) == 2.0 ** round(np.log2(hd ** -0.5))
    q = (q_ref[...] * (hd ** -0.5)).astype(BF16)
    kband = jnp.concatenate([kp_ref[...], kc_ref[...]], axis=0).astype(BF16)
    vband = jnp.concatenate([vp_ref[...], vc_ref[...]], axis=0).astype(BF16)
    t = lax.broadcasted_iota(jnp.int32, (CHUNK, 2 * CHUNK), 0)
    s = lax.broadcasted_iota(jnp.int32, (CHUNK, 2 * CHUNK), 1)
    rel = t + CHUNK - s
    mask = (rel >= 0) & (rel < CHUNK) & ((s >= CHUNK) | (n > 0))
    bias = jnp.where(mask, 0.0, -jnp.inf)
    kvs = [(kband[:, kv * hd:(kv + 1) * hd], vband[:, kv * hd:(kv + 1) * hd]) for kv in range(SW_KV_HEADS)]
    outs = []
    for h in range(SW_Q_HEADS):
        k_h, v_h = kvs[h // g]
        sc = _dot_nt(q[:, h * hd:(h + 1) * hd], k_h) + bias
        sink = sink_ref[h]
        mx = jnp.maximum(jnp.max(sc, axis=-1, keepdims=True), sink)
        p = jnp.exp(sc - mx)
        denom = jnp.sum(p, axis=-1, keepdims=True) + jnp.exp(sink - mx)
        outs.append(_dot(p.astype(BF16), v_h) / denom)
    o = jnp.concatenate(outs, axis=-1)
    ms = jnp.mean(o * o, axis=-1, keepdims=True)
    o_ref[...] = (o * lax.rsqrt(ms + RMS_EPS) * gain_ref[...]).astype(o_ref.dtype)


def _swa(proj3, sinks, gain):
    b, s, _ = proj3.shape
    q_col = _SWQ_OFF // SW_WIDTH
    k_col = _SWK_OFF // SW_KV_WIDTH
    v_col = k_col + 1

    def kv_spec(col, prev):
        if prev:
            return pl.BlockSpec((None, CHUNK, SW_KV_WIDTH), lambda bi, ni: (bi, jnp.maximum(ni - 1, 0), col))
        return pl.BlockSpec((None, CHUNK, SW_KV_WIDTH), lambda bi, ni: (bi, ni, col))

    return pl.pallas_call(
        _swa_kernel,
        grid=(b, s // CHUNK),
        in_specs=[pl.BlockSpec((None, CHUNK, SW_WIDTH), lambda bi, ni: (bi, ni, q_col)),
                  kv_spec(k_col, True), kv_spec(k_col, False),
                  kv_spec(v_col, True), kv_spec(v_col, False),
                  pl.BlockSpec(memory_space=pltpu.SMEM),
                  pl.BlockSpec((1, SW_WIDTH), lambda bi, ni: (0, 0))],
        out_specs=pl.BlockSpec((None, CHUNK, SW_WIDTH), lambda bi, ni: (bi, ni, 0)),
        out_shape=jax.ShapeDtypeStruct((b, s, SW_WIDTH), BF16),
        compiler_params=pltpu.CompilerParams(dimension_semantics=("parallel", "parallel"),
                                             vmem_limit_bytes=VMEM_LIMIT),
        name="swa",
    )(proj3, proj3, proj3, proj3, proj3, sinks.astype(F32), gain.reshape(1, SW_WIDTH))


def _sgu_kernel(u_ref, v_ref, lng_ref, lnb_ref, w_ref, bs_ref, gain_ref, o_ref):
    u = _gelu(u_ref[...])
    v = _layer_norm(_gelu(v_ref[...]), lng_ref[...], lnb_ref[...]).astype(BF16)
    r = lax.broadcasted_iota(jnp.int32, (CHUNK, CHUNK), 0)
    c = lax.broadcasted_iota(jnp.int32, (CHUNK, CHUNK), 1)
    tril = c <= r
    bs = bs_ref[...]
    parts = []
    for gi in range(SG_GROUPS):
        gs = slice(gi * CHUNK, (gi + 1) * CHUNK)
        w = jnp.where(tril, w_ref[gi], 0.0).astype(BF16)
        parts.append(_dot(w, v[:, gs]) + bs[:, gi:gi + 1])
    o = u * jnp.concatenate(parts, axis=-1)
    ms = jnp.mean(o * o, axis=-1, keepdims=True)
    o_ref[...] = (o * lax.rsqrt(ms + RMS_EPS) * gain_ref[...]).astype(o_ref.dtype)


def _sgu(proj3, ln_g, ln_b, w_s, b_s, gain):
    b, s, _ = proj3.shape
    w = SG_WIDTH
    u_col = _SGU_OFF // SG_WIDTH
    vec = pl.BlockSpec((1, w), lambda bi, ci: (0, 0))
    return pl.pallas_call(
        _sgu_kernel,
        grid=(b, s // CHUNK),
        in_specs=[pl.BlockSpec((None, CHUNK, w), lambda bi, ci: (bi, ci, u_col)),
                  pl.BlockSpec((None, CHUNK, w), lambda bi, ci: (bi, ci, u_col + 1)),
                  vec, vec,
                  pl.BlockSpec((SG_GROUPS, CHUNK, CHUNK), lambda bi, ci: (0, 0, 0)),
                  pl.BlockSpec((CHUNK, SG_GROUPS), lambda bi, ci: (0, 0)),
                  vec],
        out_specs=pl.BlockSpec((None, CHUNK, w), lambda bi, ci: (bi, ci, 0)),
        out_shape=jax.ShapeDtypeStruct((b, s, w), BF16),
        compiler_params=pltpu.CompilerParams(dimension_semantics=("parallel", "parallel"),
                                             vmem_limit_bytes=VMEM_LIMIT),
        name="sgu",
    )(proj3, proj3, ln_g.reshape(1, w), ln_b.reshape(1, w), w_s, b_s.T, gain.reshape(1, w))


LANES = 128
SLAB = D_MODEL // 2 // LANES


def _pack_pair(low, high):
    def bits(v):
        return lax.bitcast_convert_type(v.astype(BF16).astype(F32), jnp.uint32)

    return (bits(high) & jnp.uint32(0xFFFF0000)) | (bits(low) >> 16)


def _pack_rows(x):
    half = x.shape[1] // 2
    return _pack_pair(x[:, :half], x[:, half:])


def _unpack_words(w):
    return (lax.bitcast_convert_type(w << 16, F32),
            lax.bitcast_convert_type(w & jnp.uint32(0xFFFF0000), F32))


def _store_slabs(ref, x):
    n = x.shape[0]
    for j in range(SLAB):
        ref[pl.ds(j, n, stride=SLAB), :] = x[:, j * LANES:(j + 1) * LANES]


def _load_packed_rows(ref, n, first=0, stride=SLAB):
    lows, highs = [], []
    for j in range(SLAB):
        lo, hi = _unpack_words(ref[pl.ds(first + j, n, stride=stride), :])
        lows.append(lo.astype(BF16))
        highs.append(hi.astype(BF16))
    return jnp.concatenate(lows + highs, axis=1)


def _out_proj_kernel(oa_ref, ob_ref, oc_ref, w_ref, h_ref, g_ref, b_ref, of_ref, os_ref):
    mixed = jnp.concatenate([oa_ref[...], ob_ref[...], oc_ref[...]], axis=-1)
    y = ALPHA * h_ref[...] + _dot(mixed, w_ref[...])
    out = _layer_norm(y, g_ref[...], b_ref[...])
    of_ref[...] = out
    _store_slabs(os_ref, _pack_rows(out))


def _out_proj(oa, ob, oc, w, h, g, b, tm=256):
    t, d = h.shape
    vec = pl.BlockSpec((1, d), lambda i: (0, 0))
    return pl.pallas_call(
        _out_proj_kernel,
        grid=(t // tm,),
        in_specs=[pl.BlockSpec((tm, oa.shape[1]), lambda i: (i, 0)),
                  pl.BlockSpec((tm, ob.shape[1]), lambda i: (i, 0)),
                  pl.BlockSpec((tm, oc.shape[1]), lambda i: (i, 0)),
                  pl.BlockSpec(w.shape, lambda i: (0, 0)),
                  pl.BlockSpec((tm, d), lambda i: (i, 0)), vec, vec],
        out_specs=[pl.BlockSpec((tm, d), lambda i: (i, 0)),
                   pl.BlockSpec((tm * SLAB, LANES), lambda i: (i, 0))],
        out_shape=[jax.ShapeDtypeStruct((t, d), F32), jax.ShapeDtypeStruct((t * SLAB, LANES), jnp.uint32)],
        compiler_params=pltpu.CompilerParams(vmem_limit_bytes=VMEM_LIMIT),
        name="out_proj_ln1",
    )(oa, ob, oc, w, h, g.reshape(1, d), b.reshape(1, d))


def _first_index_of_max(x, iota, size, axis):
    mx = jnp.max(x, axis=axis, keepdims=True)
    idx = jnp.min(jnp.where(x == mx, iota, size), axis=axis, keepdims=True)
    return mx, idx


def _router_kernel(tiles_per_group, h_ref, wt_ref, bias_ref, eidx_ref, gate_ref, count_ref, run_ref):
    i = pl.program_id(0)

    @pl.when(i % tiles_per_group == 0)
    def _():
        run_ref[...] = jnp.zeros_like(run_ref)

    tm = h_ref.shape[0]
    per_group = N_EXPERTS // N_EXPERT_GROUPS
    def split(v):
        hi = v.astype(BF16)
        return hi, (v - hi.astype(F32)).astype(BF16)

    w_hi, w_lo = split(wt_ref[...])
    h_hi, h_lo = split(h_ref[...])
    logits = _dot_nt(w_hi, h_hi) + (_dot_nt(w_hi, h_lo) + _dot_nt(w_lo, h_hi))
    scores = 1.0 / (1.0 + jnp.exp(-logits))
    sel = scores + bias_ref[...]
    sel3 = sel.reshape(N_EXPERT_GROUPS, per_group, tm)
    io3 = lax.broadcasted_iota(jnp.int32, sel3.shape, 1)
    m1, i1 = _first_index_of_max(sel3, io3, per_group, 1)
    m2 = jnp.max(jnp.where(io3 == i1, -jnp.inf, sel3), axis=1, keepdims=True)
    grp = (m1 + m2).reshape(N_EXPERT_GROUPS, tm)
    iog = lax.broadcasted_iota(jnp.int32, grp.shape, 0)
    keep = jnp.zeros(grp.shape, jnp.bool_)
    for _ in range(TOPK_GROUPS):
        _, gi = _first_index_of_max(grp, iog, N_EXPERT_GROUPS, 0)
        hit = iog == gi
        keep = keep | hit
        grp = jnp.where(hit, -jnp.inf, grp)
    keep3 = jnp.broadcast_to(keep.reshape(N_EXPERT_GROUPS, 1, tm), sel3.shape)
    cand = jnp.where(keep3, sel3, -jnp.inf).reshape(N_EXPERTS, tm)
    ioe = lax.broadcasted_iota(jnp.int32, cand.shape, 0)
    chosen = jnp.zeros(cand.shape, F32)
    idxs, gvals = [], []
    for _ in range(TOP_K):
        _, ei = _first_index_of_max(cand, ioe, N_EXPERTS, 0)
        hit = ioe == ei
        idxs.append(ei)
        gvals.append(jnp.sum(jnp.where(hit, scores, 0.0), axis=0, keepdims=True))
        chosen = jnp.where(hit, 1.0, chosen)
        cand = jnp.where(hit, -jnp.inf, cand)
    gsum = functools.reduce(lambda a, b: a + b, gvals)
    for kk in range(TOP_K):
        eidx_ref[kk:kk + 1, :] = idxs[kk]
        gate_ref[kk:kk + 1, :] = gvals[kk] / gsum * ROUTED_SCALE
    run = run_ref[...] + jnp.sum(chosen, axis=1, keepdims=True)
    run_ref[...] = run
    count_ref[...] = run.astype(jnp.int32)


def _router(h, router_w, router_bias, n_groups, tm=ROUTER_TILE):
    t, d = h.shape
    tm = min(tm, t // n_groups)
    tiles_per_group = t // n_groups // tm
    slot = pl.BlockSpec((TOP_K, tm), lambda i: (0, i))
    return pl.pallas_call(
        functools.partial(_router_kernel, tiles_per_group),
        grid=(t // tm,),
        in_specs=[pl.BlockSpec((tm, d), lambda i: (i, 0)),
                  pl.BlockSpec((N_EXPERTS, d), lambda i: (0, 0)),
                  pl.BlockSpec((N_EXPERTS, 1), lambda i: (0, 0))],
        out_specs=[slot, slot, pl.BlockSpec((None, N_EXPERTS, 1), lambda i: (i // tiles_per_group, 0, 0))],
        out_shape=[jax.ShapeDtypeStruct((TOP_K, t), jnp.int32),
                   jax.ShapeDtypeStruct((TOP_K, t), F32),
                   jax.ShapeDtypeStruct((n_groups, N_EXPERTS, 1), jnp.int32)],
        scratch_shapes=[pltpu.VMEM((N_EXPERTS, 1), F32)],
        compiler_params=pltpu.CompilerParams(dimension_semantics=("arbitrary",),
                                             vmem_limit_bytes=VMEM_LIMIT),
        name="router",
    )(h, router_w.T, router_bias.reshape(N_EXPERTS, 1))


_ISSUE_UNROLL = 8
_TOP_K_BITS = TOP_K.bit_length() - 1
assert 1 << _TOP_K_BITS == TOP_K


_OUT_BUFFERS = 3


def _experts_kernel(layer, be_ref, bg_ref, wplan_ref, nu_ref, srcc_ref, srcn_ref, dstp_ref, dstc_ref, h_hbm,
                    wg_hbm, wu_hbm, wd_hbm, y_hbm, tok_ref, xbuf, obuf0, obuf1, obuf2, wg_buf, wu_buf, wd_buf,
                    lsem, ssem, wsem):
    b = pl.program_id(0)
    nu = nu_ref[0]
    obuf = (obuf0, obuf1, obuf2)
    rows = xbuf.shape[1] // SLAB
    d = wg_buf.shape[1]
    f = wg_buf.shape[2]
    wslot = wplan_ref[1, b]

    def weight_copies(expert, slot_):
        return [pltpu.make_async_copy(src.at[layer, expert], dst.at[slot_], wsem.at[slot_])
                for src, dst in ((wg_hbm, wg_buf), (wu_hbm, wu_buf), (wd_hbm, wd_buf))]

    @pl.when((b < nu) & (wplan_ref[0, b] == 1))
    def _():
        @pl.when(b == 0)
        def _():
            for c in weight_copies(be_ref[0], wslot):
                c.start()
        for c in weight_copies(be_ref[b], wslot):
            c.wait()
        nxt = wplan_ref[2, b]

        @pl.when(nxt >= 0)
        def _():
            for c in weight_copies(nxt, 1 - wslot):
                c.start()

    wg_ref, wu_ref, wd_ref = wg_buf.at[wslot], wu_buf.at[wslot], wd_buf.at[wslot]
    pad0 = y_hbm.shape[0] - 2 * rows * SLAB

    def scatter_row(dst_ref, r, src_slot, priority):
        dst = pl.multiple_of(dst_ref[0, 0, r], SLAB)
        buf_rows = pl.ds(pl.multiple_of(r * SLAB, SLAB), SLAB)
        pltpu.make_async_copy(obuf[src_slot].at[buf_rows], y_hbm.at[pl.ds(dst, SLAB)],
                              ssem.at[src_slot]).start(priority=priority)

    def wait_scatter(s):
        pltpu.make_async_copy(obuf[s], y_hbm.at[pl.ds(0, rows * SLAB)], ssem.at[s]).wait()

    @pl.when(b == 0)
    def _():
        for s in range(2):
            obuf[s][...] = jnp.zeros_like(obuf[s])
            pltpu.make_async_copy(obuf[s], y_hbm.at[pl.ds(pad0 + s * rows * SLAB, rows * SLAB)],
                                  ssem.at[s]).start()
        for s in range(2):
            wait_scatter(s)

    group = bg_ref[b]

    x_cur = xbuf.at[b % 2]
    x_next = xbuf.at[(b + 1) % 2]

    @pl.when((b < nu) & ((b == 0) | (group != bg_ref[jnp.maximum(b - 1, 0)])))
    def _():
        load = pltpu.make_async_copy(h_hbm.at[group], tok_ref, lsem.at[0])
        load.start()
        load.wait()

        def gather(i, carry):
            for u in range(_ISSUE_UNROLL):
                r = i * _ISSUE_UNROLL + u
                src = pl.multiple_of(srcc_ref[0, 0, r], SLAB)
                x_cur[pl.ds(pl.multiple_of(r * SLAB, SLAB), SLAB), :] = tok_ref[pl.ds(src, SLAB), :]
            return carry
        lax.fori_loop(0, rows // _ISSUE_UNROLL, gather, 0)

    def block(slot, with_scatter):
        other = (slot - 1) % _OUT_BUFFERS

        if with_scatter:
            for r in range(rows):
                scatter_row(dstp_ref, r, other, r % 2)

        for r in range(rows):
            src = pl.multiple_of(srcn_ref[0, 0, r], SLAB)
            x_next[pl.ds(r * SLAB, SLAB), :] = tok_ref[pl.ds(src, SLAB), :]

        x = _load_packed_rows(x_cur, rows)
        acts = []
        fh = f // 2
        for j in range(2):
            gate = _dot(x, wg_ref[:, j * fh:(j + 1) * fh])
            up = _dot(x, wu_ref[:, j * fh:(j + 1) * fh])
            acts.append((_silu(gate) * up).astype(BF16))
        act = jnp.concatenate(acts, axis=1)
        half = d // 2
        cw = 2 * LANES
        for q in range(half // cw):
            words = _pack_pair(_dot(act, wd_ref[:, q * cw:(q + 1) * cw]),
                               _dot(act, wd_ref[:, half + q * cw:half + (q + 1) * cw]))
            for jj in range(cw // LANES):
                obuf[slot][pl.ds(q * (cw // LANES) + jj, rows, stride=SLAB), :] = words[:, jj * LANES:(jj + 1) * LANES]

    def tail(slot):
        for back in (1, 2):
            @pl.when(b >= back)
            def _(back=back):
                wait_scatter((slot - back) % _OUT_BUFFERS)

        def body(r, carry):
            scatter_row(dstc_ref, r, slot, 0)
            return carry
        lax.fori_loop(0, rows, body, 0)
        wait_scatter(slot)

    @pl.when(b == 0)
    def _():
        block(0, False)

    for s in range(_OUT_BUFFERS):
        @pl.when((b >= 1) & (b < nu) & (b % _OUT_BUFFERS == s))
        def _(s=s):
            @pl.when(b >= _OUT_BUFFERS)
            def _():
                wait_scatter(s)
            block(s, True)

        @pl.when((b == nu - 1) & (b % _OUT_BUFFERS == s))
        def _(s=s):
            tail(s)


def _experts(h_groups, row_src, row_dst, block_e, block_g, weight_plan, n_used, wg, wu, wd, layer, n_slots):
    nb, _, rows = row_src.shape
    d, f = wg.shape[-2], wg.shape[-1]
    group_rows = h_groups.shape[1]

    def smem(shift):
        def index(b, be, bg, wplan, nu):
            return (jnp.maximum(jnp.minimum(b + shift, nu[0] - 1), 0), 0, 0)
        return pl.BlockSpec((1, 1, rows), index, memory_space=pltpu.SMEM)

    hbm = pl.BlockSpec(memory_space=pl.ANY)
    grid_spec = pltpu.PrefetchScalarGridSpec(
        num_scalar_prefetch=4,
        grid=(nb,),
        in_specs=[smem(0), smem(1), smem(-1), smem(0), hbm, hbm, hbm, hbm],
        out_specs=hbm,
        scratch_shapes=[pltpu.VMEM((group_rows, LANES), jnp.uint32),
                        pltpu.VMEM((2, rows * SLAB, LANES), jnp.uint32)]
        + [pltpu.VMEM((rows * SLAB, LANES), jnp.uint32)] * _OUT_BUFFERS
        + [pltpu.VMEM((2, d, f), BF16), pltpu.VMEM((2, d, f), BF16), pltpu.VMEM((2, f, d), BF16)]
        + [pltpu.SemaphoreType.DMA((1,)), pltpu.SemaphoreType.DMA((_OUT_BUFFERS,)),
           pltpu.SemaphoreType.DMA((2,))],
    )
    return pl.pallas_call(
        functools.partial(_experts_kernel, layer),
        grid_spec=grid_spec,
        out_shape=jax.ShapeDtypeStruct(((n_slots + 2 * rows) * SLAB, LANES), jnp.uint32),
        compiler_params=pltpu.CompilerParams(dimension_semantics=("arbitrary",),
                                             vmem_limit_bytes=VMEM_LIMIT),
        name="experts",
    )(block_e, block_g, weight_plan, n_used, row_src, row_src, row_dst, row_dst, h_groups, wg, wu, wd)


_COMBINE_SUB = 32


def _combine_kernel(h_ref, gate_ref, *refs):
    y_refs = refs[:TOP_K]
    sg_ref, su_ref, sd_ref, g_ref, b_ref, of_ref, ob_ref, moe_ref = refs[TOP_K:]
    h = h_ref[...]
    tm = h.shape[0]
    half = SLAB * LANES
    for s in range(tm // _COMBINE_SUB):
        rs = slice(s * _COMBINE_SUB, (s + 1) * _COMBINE_SUB)
        gates = gate_ref[rs, :]
        first = s * _COMBINE_SUB * SLAB
        for j in range(SLAB):
            lo_acc = hi_acc = None
            for kk in range(TOP_K):
                lo, hi = _unpack_words(y_refs[kk][pl.ds(first + j, _COMBINE_SUB, stride=SLAB), :])
                gk = gates[:, kk:kk + 1]
                lo_acc = gk * lo if kk == 0 else lo_acc + gk * lo
                hi_acc = gk * hi if kk == 0 else hi_acc + gk * hi
            moe_ref[rs, j * LANES:(j + 1) * LANES] = lo_acc
            moe_ref[rs, half + j * LANES:half + (j + 1) * LANES] = hi_acc
    hb = h.astype(BF16)
    act = _silu(_dot(hb, sg_ref[...])) * _dot(hb, su_ref[...])
    y = ALPHA * h + _dot(act.astype(BF16), sd_ref[...]) + moe_ref[...]
    out = _layer_norm(y, g_ref[...], b_ref[...])
    of_ref[...] = out
    ob_ref[...] = out.astype(BF16)


def _combine(h, gates_t, y_slots, sg, su, sd, g, b, tm=COMBINE_TILE):
    t, d = h.shape
    n = t // tm
    f = sg.shape[1]
    vec = pl.BlockSpec((1, d), lambda i: (0, 0))
    row = pl.BlockSpec((tm, d), lambda i: (i, 0))
    return pl.pallas_call(
        _combine_kernel,
        grid=(n,),
        in_specs=[row, pl.BlockSpec((tm, TOP_K), lambda i: (i, 0))]
        + [pl.BlockSpec((tm * SLAB, LANES), lambda i, kk=kk: (kk * n + i, 0)) for kk in range(TOP_K)]
        + [pl.BlockSpec((d, f), lambda i: (0, 0)),
            pl.BlockSpec((d, f), lambda i: (0, 0)),
            pl.BlockSpec((f, d), lambda i: (0, 0)),
            vec, vec],
        out_specs=[row, row],
        out_shape=[jax.ShapeDtypeStruct((t, d), F32), jax.ShapeDtypeStruct((t, d), BF16)],
        scratch_shapes=[pltpu.VMEM((tm, d), F32)],
        compiler_params=pltpu.CompilerParams(dimension_semantics=("parallel",),
                                             vmem_limit_bytes=VMEM_LIMIT),
        name="combine_ln2",
    )(h, gates_t, *([y_slots] * TOP_K), sg, su, sd, g.reshape(1, d), b.reshape(1, d))


def _moe(h, h_slab, router_w, router_bias, wg, wu, wd, layer, sg, su, sd, ln_g, ln_b):
    t, d = h.shape
    tk = t * TOP_K
    n_groups = TOKEN_GROUPS if t % (TOKEN_GROUPS * ROUTER_TILE) == 0 else 1
    tg = t // n_groups
    n_buckets = n_groups * N_EXPERTS
    n_blocks = tk // MOE_ROWS + n_buckets
    eidx, gates, counts = _router(h, router_w, router_bias, n_groups)
    counts = counts.reshape(n_buckets)
    n_rows = n_blocks * MOE_ROWS
    padded = (counts + MOE_ROWS - 1) // MOE_ROWS * MOE_ROWS
    pend = jnp.cumsum(padded)
    pad_end = jnp.cumsum(padded - counts)
    pad_key = jnp.sum(pad_end[None, :] <= jnp.arange(n_rows - tk, dtype=jnp.int32)[:, None], axis=1)
    slot_group = (jnp.arange(tk, dtype=jnp.int32) >> _TOP_K_BITS) // tg
    keys = jnp.concatenate([slot_group * N_EXPERTS + eidx.T.reshape(tk), pad_key.astype(jnp.int32)])
    _, row_slot = lax.sort((keys, jnp.arange(n_rows, dtype=jnp.int32)), num_keys=1, is_stable=True)
    real = row_slot < tk
    row = jnp.arange(n_rows, dtype=jnp.int32)
    row_src = jnp.where(real, ((row_slot >> _TOP_K_BITS) % tg) * SLAB, 0)
    plane_row = (row_slot & (TOP_K - 1)) * t + (row_slot >> _TOP_K_BITS)
    row_dst = jnp.where(real, plane_row, tk + row % (2 * MOE_ROWS)) * SLAB
    block_start = jnp.arange(n_blocks, dtype=jnp.int32) * MOE_ROWS
    block_bucket = jnp.minimum(jnp.sum(pend[None, :] <= block_start[:, None], axis=1), n_buckets - 1).astype(jnp.int32)
    n_used = (pend[-1] // MOE_ROWS).astype(jnp.int32)
    block_e = block_bucket % N_EXPERTS
    block = jnp.arange(n_blocks, dtype=jnp.int32)
    first = (block == 0) | (block_bucket != jnp.roll(block_bucket, 1))
    seq = jnp.cumsum(first.astype(jnp.int32)) - 1
    later_first = jnp.where(first & (block < n_used), block, n_blocks)
    next_first = jnp.concatenate([lax.cummin(later_first, reverse=True)[1:], jnp.full((1,), n_blocks, jnp.int32)])
    next_e = jnp.where(next_first < n_blocks, block_e[jnp.minimum(next_first, n_blocks - 1)], -1)
    weight_plan = jnp.stack([first.astype(jnp.int32), seq % 2, next_e]).astype(jnp.int32)
    y_slots = _experts(h_slab.reshape(n_groups, tg * SLAB, LANES),
                       row_src.reshape(n_blocks, 1, MOE_ROWS), row_dst.reshape(n_blocks, 1, MOE_ROWS),
                       block_e, block_bucket // N_EXPERTS, weight_plan, n_used.reshape(1),
                       wg, wu, wd, layer, tk)
    return _combine(h, gates.T, y_slots, sg, su, sd, ln_g, ln_b, tm=min(COMBINE_TILE, t))


def kernel(x, ln0_gain, ln0_bias, w_in, b_in, hg_lb_logits, sw_sinks, sg_ln_gain, sg_ln_bias, sg_w_s, sg_b_s, mix_gain, w_out, ln1_gain, ln1_bias, router_w, router_bias, exp_w_gate, exp_w_up, exp_w_down, sh_w_gate, sh_w_up, sh_w_down, ln2_gain, ln2_bias):
    b, s, d = x.shape
    t = b * s
    p = jax.nn.softmax(hg_lb_logits.astype(F32), axis=0)
    cs = jnp.cumsum(p, axis=0)
    lbs = cs - cs[0:1]
    a0, a1 = HG_WIDTH, HG_WIDTH + SW_WIDTH

    wg_all, wu_all, wd_all = exp_w_gate.astype(BF16), exp_w_up.astype(BF16), exp_w_down.astype(BF16)
    h, hb = _ln(x.reshape(t, d), ln0_gain, ln0_bias)
    for l in range(DEPTH):
        proj = _in_proj(hb, _regroup_columns(w_in[l]).astype(BF16), _regroup_columns(b_in[l]))
        proj3 = proj.reshape(b, s, D_IN)
        lb = lbs[l]
        o_a = _hgrn(proj3, jnp.log(lb), jnp.log1p(-lb), 1.0 - lb, mix_gain[l, :a0])
        o_b = _swa(proj3, sw_sinks[l], mix_gain[l, a0:a1])
        o_c = _sgu(proj3, sg_ln_gain[l], sg_ln_bias[l], sg_w_s[l], sg_b_s[l], mix_gain[l, a1:])
        h, h_slab = _out_proj(o_a.reshape(t, -1), o_b.reshape(t, -1), o_c.reshape(t, -1),
                              w_out[l].astype(BF16), h, ln1_gain[l], ln1_bias[l])
        h, hb = _moe(h, h_slab, router_w[l], router_bias[l], wg_all, wu_all, wd_all, l,
                     sh_w_gate[l].astype(BF16), sh_w_up[l].astype(BF16), sh_w_down[l].astype(BF16),
                     ln2_gain[l], ln2_bias[l])
    return h.reshape(b, s, d)
```

```python
import functools

import numpy as np
import jax
import jax.numpy as jnp
from jax import lax
from jax.experimental import pallas as pl
from jax.experimental.pallas import tpu as pltpu

F32 = jnp.float32
BF16 = jnp.bfloat16

D_MODEL = 2048
DEPTH = 2
HG_HEADS = 4
HG_DK = 128
HG_WIDTH = HG_HEADS * HG_DK
CHUNK = 128
SW_Q_HEADS = 16
SW_KV_HEADS = 2
SW_HEAD_DIM = 64
SW_WIDTH = SW_Q_HEADS * SW_HEAD_DIM
SW_KV_WIDTH = SW_KV_HEADS * SW_HEAD_DIM
SG_GROUPS = 4
SG_WIDTH = SG_GROUPS * CHUNK
D_IN = 4 * HG_WIDTH + SW_WIDTH + 2 * SW_KV_WIDTH + 2 * SG_WIDTH
N_EXPERTS = 64
TOP_K = 8
D_EXPERT = 512
N_EXPERT_GROUPS = 8
TOPK_GROUPS = 4
ROUTED_SCALE = 2.5
ALPHA = (2 * DEPTH) ** 0.25
LN_EPS = 1e-5
RMS_EPS = 1e-6

_SWQ_OFF = 4 * HG_WIDTH
_SGU_OFF = _SWQ_OFF + SW_WIDTH
_SWK_OFF = _SGU_OFF + 2 * SG_WIDTH


def _regroup_columns(a):
    k0 = _SWQ_OFF + SW_WIDTH
    k1 = k0 + 2 * SW_KV_WIDTH
    return jnp.concatenate([a[..., :k0], a[..., k1:], a[..., k0:k1]], axis=-1)


MOE_ROWS = 256
TOKEN_GROUPS = 2
ROUTER_TILE = 512
COMBINE_TILE = 256
VMEM_LIMIT = 56 * 1024 * 1024

_LEVEL_SIZES = (64, 32, 16, 8, 4, 2, 1)
_N_LEVELS = len(_LEVEL_SIZES)


def _hgrn_constants():
    t = np.arange(CHUNK)[:, None]
    u = np.arange(CHUNK)[None, :]
    mats = []
    level = np.full((CHUNK, CHUNK), -1, np.int32)
    for li, s in enumerate(_LEVEL_SIZES):
        blk = t // s
        odd = (blk % 2) == 1
        m_query = (u >= blk * s) & (u <= t)
        m_key = (u > t) & (u <= (blk + 1) * s - 1)
        mats.append(np.where(odd, m_query, m_key))
        pair = ((t // s) % 2 == 1) & ((u // s) == (t // s) - 1)
        level[pair] = li
    mats.append(u <= t)
    mats.append(u > t)
    level[np.arange(CHUNK), np.arange(CHUNK)] = _N_LEVELS
    return np.concatenate(mats, 0).astype(np.float32), level


_HGRN_MSTACK, _HGRN_LEVEL = _hgrn_constants()


def _layer_norm(x, g, b):
    mu = jnp.mean(x, axis=-1, keepdims=True)
    xc = x - mu
    var = jnp.mean(xc * xc, axis=-1, keepdims=True)
    return xc * lax.rsqrt(var + LN_EPS) * g + b


def _silu(x):
    return x / (1.0 + jnp.exp(-x))


def _gelu(x):
    return 0.5 * x * (1.0 + lax.erf(x * (2.0 ** -0.5)))


def _dot(a, b):
    return jnp.dot(a, b, preferred_element_type=F32)


def _dot_nt(a, b):
    return lax.dot_general(a, b, (((1,), (1,)), ((), ())), preferred_element_type=F32)


def _dot_tn(a, b):
    return lax.dot_general(a, b, (((0,), (0,)), ((), ())), preferred_element_type=F32)


def _ln_kernel(x_ref, g_ref, b_ref, of_ref, ob_ref):
    y = _layer_norm(x_ref[...], g_ref[...], b_ref[...])
    of_ref[...] = y
    ob_ref[...] = y.astype(BF16)


def _ln(x, g, b, tm=256):
    t, d = x.shape
    row = pl.BlockSpec((tm, d), lambda i: (i, 0))
    vec = pl.BlockSpec((1, d), lambda i: (0, 0))
    return pl.pallas_call(
        _ln_kernel,
        grid=(t // tm,),
        in_specs=[row, vec, vec],
        out_specs=[row, row],
        out_shape=[jax.ShapeDtypeStruct((t, d), F32), jax.ShapeDtypeStruct((t, d), BF16)],
        name="ln0",
    )(x, g.reshape(1, d), b.reshape(1, d))


def _mm_bias_kernel(a_ref, w_ref, b_ref, o_ref):
    o_ref[...] = _dot(a_ref[...], w_ref[...]) + b_ref[...]


def _in_proj(a, w, b, tm=512):
    t, k = a.shape
    n = w.shape[1]
    tm = min(tm, t)
    tn = n // 2
    return pl.pallas_call(
        _mm_bias_kernel,
        grid=(n // tn, t // tm),
        in_specs=[pl.BlockSpec((tm, k), lambda j, i: (i, 0)),
                  pl.BlockSpec((k, tn), lambda j, i: (0, j)),
                  pl.BlockSpec((1, tn), lambda j, i: (0, j))],
        out_specs=pl.BlockSpec((tm, tn), lambda j, i: (i, j)),
        out_shape=jax.ShapeDtypeStruct((t, n), F32),
        compiler_params=pltpu.CompilerParams(vmem_limit_bytes=VMEM_LIMIT),
        name="in_proj",
    )(a, w, b.reshape(1, n))


def _hgrn_kernel(q_ref, f_ref, i_ref, g_ref, llb_ref, l1lb_ref, oml_ref, gain_ref,
                 mstack_ref, level_ref, o_ref, state_ref):
    c = pl.program_id(1)

    @pl.when(c == 0)
    def _():
        state_ref[...] = jnp.zeros_like(state_ref)

    q = _silu(q_ref[...])
    z = f_ref[...]
    ez = jnp.exp(-jnp.abs(z))
    log_sig = jnp.minimum(z, 0.0) - jnp.log(1.0 + ez)
    sig_neg = jnp.where(z >= 0.0, ez, 1.0) / (1.0 + ez)
    k = oml_ref[...] * sig_neg
    a = llb_ref[...]
    bb = l1lb_ref[...] + log_sig
    log_f = jnp.maximum(a, bb) + jnp.log(1.0 + jnp.exp(-jnp.abs(a - bb)))

    lf_hi = log_f.astype(BF16)
    lf_lo = (log_f - lf_hi.astype(F32)).astype(BF16)
    m = mstack_ref[...]
    sums = _dot(m, lf_hi) + _dot(m, lf_lo)
    decay = jnp.exp(sums)

    rows = lax.broadcasted_iota(jnp.int32, (CHUNK, 1), 0)
    factors = []
    for li, s in enumerate(_LEVEL_SIZES):
        is_query = ((rows // s) % 2) == 1
        factors.append((jnp.where(is_query, q, k) * decay[li * CHUNK:(li + 1) * CHUNK]).astype(BF16))
    cum = _N_LEVELS * CHUNK
    q_in = (q * decay[cum:cum + CHUNK]).astype(BF16)
    k_dec = (k * decay[cum + CHUNK:cum + 2 * CHUNK]).astype(BF16)
    end_decay = decay[cum + CHUNK - 1:cum + CHUNK]
    qb = q.astype(BF16)
    kb = k.astype(BF16)
    vb = i_ref[...].astype(BF16)
    level = level_ref[...]
    gate = _silu(g_ref[...])
    gain = gain_ref[...]

    for h in range(HG_HEADS):
        hs = slice(h * HG_DK, (h + 1) * HG_DK)
        scores = jnp.zeros((CHUNK, CHUNK), F32)
        for li in range(_N_LEVELS):
            fl = factors[li][:, hs]
            scores = jnp.where(level == li, _dot_nt(fl, fl), scores)
        scores = jnp.where(level == _N_LEVELS, _dot_nt(qb[:, hs], kb[:, hs]), scores)
        st = state_ref[h]
        o = _dot(scores.astype(BF16), vb[:, hs]) + _dot_nt(q_in[:, hs], st.astype(BF16))
        state_ref[h] = st * end_decay[:, hs] + _dot_tn(vb[:, hs], k_dec[:, hs])
        ms = jnp.mean(o * o, axis=-1, keepdims=True)
        o = o * lax.rsqrt(ms + RMS_EPS) * gain[:, hs] * gate[:, hs]
        o_ref[:, hs] = o.astype(o_ref.dtype)


def _hgrn(proj3, llb, l1lb, oml, gain):
    b, s, _ = proj3.shape
    w = HG_WIDTH

    def col(j):
        return pl.BlockSpec((None, CHUNK, w), lambda bi, ci, j=j: (bi, ci, j))

    vec = pl.BlockSpec((1, w), lambda bi, ci: (0, 0))
    nm = _HGRN_MSTACK.shape[0]
    return pl.pallas_call(
        _hgrn_kernel,
        grid=(b, s // CHUNK),
        in_specs=[col(0), col(1), col(2), col(3), vec, vec, vec, vec,
                  pl.BlockSpec((nm, CHUNK), lambda bi, ci: (0, 0)),
                  pl.BlockSpec((CHUNK, CHUNK), lambda bi, ci: (0, 0))],
        out_specs=pl.BlockSpec((None, CHUNK, w), lambda bi, ci: (bi, ci, 0)),
        out_shape=jax.ShapeDtypeStruct((b, s, w), BF16),
        scratch_shapes=[pltpu.VMEM((HG_HEADS, HG_DK, HG_DK), F32)],
        compiler_params=pltpu.CompilerParams(dimension_semantics=("parallel", "arbitrary"),
                                             vmem_limit_bytes=VMEM_LIMIT),
        name="hgrn2",
    )(proj3, proj3, proj3, proj3, llb.reshape(1, w), l1lb.reshape(1, w), oml.reshape(1, w),
      gain.reshape(1, w), jnp.asarray(_HGRN_MSTACK, BF16), jnp.asarray(_HGRN_LEVEL))


def _swa_kernel(q_ref, kp_ref, kc_ref, vp_ref, vc_ref, sink_ref, gain_ref, o_ref):
    n = pl.program_id(1)
    g = SW_Q_HEADS // SW_KV_HEADS
    hd = SW_HEAD_DIM
    assert (hd ** -0.5) == 2.0 ** round(np.log2(hd ** -0.5))
    q = (q_ref[...] * (hd ** -0.5)).astype(BF16)
    kband = jnp.concatenate([kp_ref[...], kc_ref[...]], axis=0).astype(BF16)
    vband = jnp.concatenate([vp_ref[...], vc_ref[...]], axis=0).astype(BF16)
    t = lax.broadcasted_iota(jnp.int32, (CHUNK, 2 * CHUNK), 0)
    s = lax.broadcasted_iota(jnp.int32, (CHUNK, 2 * CHUNK), 1)
    rel = t + CHUNK - s
    mask = (rel >= 0) & (rel < CHUNK) & ((s >= CHUNK) | (n > 0))
    bias = jnp.where(mask, 0.0, -jnp.inf)
    kvs = [(kband[:, kv * hd:(kv + 1) * hd], vband[:, kv * hd:(kv + 1) * hd]) for kv in range(SW_KV_HEADS)]
    outs = []
    for h in range(SW_Q_HEADS):
        k_h, v_h = kvs[h // g]
        sc = _dot_nt(q[:, h * hd:(h + 1) * hd], k_h) + bias
        sink = sink_ref[h]
        mx = jnp.maximum(jnp.max(sc, axis=-1, keepdims=True), sink)
        p = jnp.exp(sc - mx)
        denom = jnp.sum(p, axis=-1, keepdims=True) + jnp.exp(sink - mx)
        outs.append(_dot(p.astype(BF16), v_h) / denom)
    o = jnp.concatenate(outs, axis=-1)
    ms = jnp.mean(o * o, axis=-1, keepdims=True)
    o_ref[...] = (o * lax.rsqrt(ms + RMS_EPS) * gain_ref[...]).astype(o_ref.dtype)


def _swa(proj3, sinks, gain):
    b, s, _ = proj3.shape
    q_col = _SWQ_OFF // SW_WIDTH
    k_col = _SWK_OFF // SW_KV_WIDTH
    v_col = k_col + 1

    def kv_spec(col, prev):
        if prev:
            return pl.BlockSpec((None, CHUNK, SW_KV_WIDTH), lambda bi, ni: (bi, jnp.maximum(ni - 1, 0), col))
        return pl.BlockSpec((None, CHUNK, SW_KV_WIDTH), lambda bi, ni: (bi, ni, col))

    return pl.pallas_call(
        _swa_kernel,
        grid=(b, s // CHUNK),
        in_specs=[pl.BlockSpec((None, CHUNK, SW_WIDTH), lambda bi, ni: (bi, ni, q_col)),
                  kv_spec(k_col, True), kv_spec(k_col, False),
                  kv_spec(v_col, True), kv_spec(v_col, False),
                  pl.BlockSpec(memory_space=pltpu.SMEM),
                  pl.BlockSpec((1, SW_WIDTH), lambda bi, ni: (0, 0))],
        out_specs=pl.BlockSpec((None, CHUNK, SW_WIDTH), lambda bi, ni: (bi, ni, 0)),
        out_shape=jax.ShapeDtypeStruct((b, s, SW_WIDTH), BF16),
        compiler_params=pltpu.CompilerParams(dimension_semantics=("parallel", "parallel"),
                                             vmem_limit_bytes=VMEM_LIMIT),
        name="swa",
    )(proj3, proj3, proj3, proj3, proj3, sinks.astype(F32), gain.reshape(1, SW_WIDTH))


def _sgu_kernel(u_ref, v_ref, lng_ref, lnb_ref, w_ref, bs_ref, gain_ref, o_ref):
    u = _gelu(u_ref[...])
    v = _layer_norm(_gelu(v_ref[...]), lng_ref[...], lnb_ref[...]).astype(BF16)
    r = lax.broadcasted_iota(jnp.int32, (CHUNK, CHUNK), 0)
    c = lax.broadcasted_iota(jnp.int32, (CHUNK, CHUNK), 1)
    tril = c <= r
    bs = bs_ref[...]
    parts = []
    for gi in range(SG_GROUPS):
        gs = slice(gi * CHUNK, (gi + 1) * CHUNK)
        w = jnp.where(tril, w_ref[gi], 0.0).astype(BF16)
        parts.append(_dot(w, v[:, gs]) + bs[:, gi:gi + 1])
    o = u * jnp.concatenate(parts, axis=-1)
    ms = jnp.mean(o * o, axis=-1, keepdims=True)
    o_ref[...] = (o * lax.rsqrt(ms + RMS_EPS) * gain_ref[...]).astype(o_ref.dtype)


def _sgu(proj3, ln_g, ln_b, w_s, b_s, gain):
    b, s, _ = proj3.shape
    w = SG_WIDTH
    u_col = _SGU_OFF // SG_WIDTH
    vec = pl.BlockSpec((1, w), lambda bi, ci: (0, 0))
    return pl.pallas_call(
        _sgu_kernel,
        grid=(b, s // CHUNK),
        in_specs=[pl.BlockSpec((None, CHUNK, w), lambda bi, ci: (bi, ci, u_col)),
                  pl.BlockSpec((None, CHUNK, w), lambda bi, ci: (bi, ci, u_col + 1)),
                  vec, vec,
                  pl.BlockSpec((SG_GROUPS, CHUNK, CHUNK), lambda bi, ci: (0, 0, 0)),
                  pl.BlockSpec((CHUNK, SG_GROUPS), lambda bi, ci: (0, 0)),
                  vec],
        out_specs=pl.BlockSpec((None, CHUNK, w), lambda bi, ci: (bi, ci, 0)),
        out_shape=jax.ShapeDtypeStruct((b, s, w), BF16),
        compiler_params=pltpu.CompilerParams(dimension_semantics=("parallel", "parallel"),
                                             vmem_limit_bytes=VMEM_LIMIT),
        name="sgu",
    )(proj3, proj3, ln_g.reshape(1, w), ln_b.reshape(1, w), w_s, b_s.T, gain.reshape(1, w))


LANES = 128
SLAB = D_MODEL // 2 // LANES


def _pack_pair(low, high):
    def bits(v):
        return lax.bitcast_convert_type(v.astype(BF16).astype(F32), jnp.uint32)

    return (bits(high) & jnp.uint32(0xFFFF0000)) | (bits(low) >> 16)


def _pack_rows(x):
    half = x.shape[1] // 2
    return _pack_pair(x[:, :half], x[:, half:])


def _unpack_words(w):
    return (lax.bitcast_convert_type(w << 16, F32),
            lax.bitcast_convert_type(w & jnp.uint32(0xFFFF0000), F32))


def _store_slabs(ref, x):
    n = x.shape[0]
    for j in range(SLAB):
        ref[pl.ds(j, n, stride=SLAB), :] = x[:, j * LANES:(j + 1) * LANES]


def _load_packed_rows(ref, n, first=0, stride=SLAB):
    lows, highs = [], []
    for j in range(SLAB):
        lo, hi = _unpack_words(ref[pl.ds(first + j, n, stride=stride), :])
        lows.append(lo.astype(BF16))
        highs.append(hi.astype(BF16))
    return jnp.concatenate(lows + highs, axis=1)


def _out_proj_kernel(oa_ref, ob_ref, oc_ref, w_ref, h_ref, g_ref, b_ref, of_ref, os_ref):
    mixed = jnp.concatenate([oa_ref[...], ob_ref[...], oc_ref[...]], axis=-1)
    y = ALPHA * h_ref[...] + _dot(mixed, w_ref[...])
    out = _layer_norm(y, g_ref[...], b_ref[...])
    of_ref[...] = out
    _store_slabs(os_ref, _pack_rows(out))


def _out_proj(oa, ob, oc, w, h, g, b, tm=512):
    t, d = h.shape
    tm = min(tm, t)
    vec = pl.BlockSpec((1, d), lambda i: (0, 0))
    return pl.pallas_call(
        _out_proj_kernel,
        grid=(t // tm,),
        in_specs=[pl.BlockSpec((tm, oa.shape[1]), lambda i: (i, 0)),
                  pl.BlockSpec((tm, ob.shape[1]), lambda i: (i, 0)),
                  pl.BlockSpec((tm, oc.shape[1]), lambda i: (i, 0)),
                  pl.BlockSpec(w.shape, lambda i: (0, 0)),
                  pl.BlockSpec((tm, d), lambda i: (i, 0)), vec, vec],
        out_specs=[pl.BlockSpec((tm, d), lambda i: (i, 0)),
                   pl.BlockSpec((tm * SLAB, LANES), lambda i: (i, 0))],
        out_shape=[jax.ShapeDtypeStruct((t, d), F32), jax.ShapeDtypeStruct((t * SLAB, LANES), jnp.uint32)],
        compiler_params=pltpu.CompilerParams(vmem_limit_bytes=VMEM_LIMIT),
        name="out_proj_ln1",
    )(oa, ob, oc, w, h, g.reshape(1, d), b.reshape(1, d))


def _first_index_of_max(x, iota, size, axis):
    mx = jnp.max(x, axis=axis, keepdims=True)
    idx = jnp.min(jnp.where(x == mx, iota, size), axis=axis, keepdims=True)
    return mx, idx


def _router_kernel(tiles_per_group, h_ref, wt_ref, bias_ref, eidx_ref, gate_ref, count_ref, run_ref):
    i = pl.program_id(0)

    @pl.when(i % tiles_per_group == 0)
    def _():
        run_ref[...] = jnp.zeros_like(run_ref)

    tm = h_ref.shape[0]
    per_group = N_EXPERTS // N_EXPERT_GROUPS
    def split(v):
        hi = v.astype(BF16)
        return hi, (v - hi.astype(F32)).astype(BF16)

    w_hi, w_lo = split(wt_ref[...])
    h_hi, h_lo = split(h_ref[...])
    logits = _dot_nt(w_hi, h_hi) + (_dot_nt(w_hi, h_lo) + _dot_nt(w_lo, h_hi))
    scores = 1.0 / (1.0 + jnp.exp(-logits))
    sel = scores + bias_ref[...]
    sel3 = sel.reshape(N_EXPERT_GROUPS, per_group, tm)
    io3 = lax.broadcasted_iota(jnp.int32, sel3.shape, 1)
    m1, i1 = _first_index_of_max(sel3, io3, per_group, 1)
    m2 = jnp.max(jnp.where(io3 == i1, -jnp.inf, sel3), axis=1, keepdims=True)
    grp = (m1 + m2).reshape(N_EXPERT_GROUPS, tm)
    iog = lax.broadcasted_iota(jnp.int32, grp.shape, 0)
    keep = jnp.zeros(grp.shape, jnp.bool_)
    for _ in range(TOPK_GROUPS):
        _, gi = _first_index_of_max(grp, iog, N_EXPERT_GROUPS, 0)
        hit = iog == gi
        keep = keep | hit
        grp = jnp.where(hit, -jnp.inf, grp)
    keep3 = jnp.broadcast_to(keep.reshape(N_EXPERT_GROUPS, 1, tm), sel3.shape)
    cand = jnp.where(keep3, sel3, -jnp.inf).reshape(N_EXPERTS, tm)
    ioe = lax.broadcasted_iota(jnp.int32, cand.shape, 0)
    chosen = jnp.zeros(cand.shape, F32)
    idxs, gvals = [], []
    for _ in range(TOP_K):
        _, ei = _first_index_of_max(cand, ioe, N_EXPERTS, 0)
        hit = ioe == ei
        idxs.append(ei)
        gvals.append(jnp.sum(jnp.where(hit, scores, 0.0), axis=0, keepdims=True))
        chosen = jnp.where(hit, 1.0, chosen)
        cand = jnp.where(hit, -jnp.inf, cand)
    gsum = functools.reduce(lambda a, b: a + b, gvals)
    for kk in range(TOP_K):
        eidx_ref[kk:kk + 1, :] = idxs[kk]
        gate_ref[kk:kk + 1, :] = gvals[kk] / gsum * ROUTED_SCALE
    run = run_ref[...] + jnp.sum(chosen, axis=1, keepdims=True)
    run_ref[...] = run
    count_ref[...] = run.astype(jnp.int32)


def _router(h, router_w, router_bias, n_groups, tm=ROUTER_TILE):
    t, d = h.shape
    tm = min(tm, t // n_groups)
    tiles_per_group = t // n_groups // tm
    slot = pl.BlockSpec((TOP_K, tm), lambda i: (0, i))
    return pl.pallas_call(
        functools.partial(_router_kernel, tiles_per_group),
        grid=(t // tm,),
        in_specs=[pl.BlockSpec((tm, d), lambda i: (i, 0)),
                  pl.BlockSpec((N_EXPERTS, d), lambda i: (0, 0)),
                  pl.BlockSpec((N_EXPERTS, 1), lambda i: (0, 0))],
        out_specs=[slot, slot, pl.BlockSpec((None, N_EXPERTS, 1), lambda i: (i // tiles_per_group, 0, 0))],
        out_shape=[jax.ShapeDtypeStruct((TOP_K, t), jnp.int32),
                   jax.ShapeDtypeStruct((TOP_K, t), F32),
                   jax.ShapeDtypeStruct((n_groups, N_EXPERTS, 1), jnp.int32)],
        scratch_shapes=[pltpu.VMEM((N_EXPERTS, 1), F32)],
        compiler_params=pltpu.CompilerParams(dimension_semantics=("arbitrary",),
                                             vmem_limit_bytes=VMEM_LIMIT),
        name="router",
    )(h, router_w.T, router_bias.reshape(N_EXPERTS, 1))


_ISSUE_UNROLL = 8
_TOP_K_BITS = TOP_K.bit_length() - 1
assert 1 << _TOP_K_BITS == TOP_K


_OUT_BUFFERS = 3


def _experts_kernel(layer, be_ref, bg_ref, wplan_ref, nu_ref, src_ref, dstp_ref, dstc_ref, h_hbm,
                    wg_hbm, wu_hbm, wd_hbm, y_hbm, tok_ref, xbuf, obuf0, obuf1, obuf2, wg_buf, wu_buf, wd_buf,
                    lsem, ssem, wsem):
    b = pl.program_id(0)
    nu = nu_ref[0]
    obuf = (obuf0, obuf1, obuf2)
    rows = xbuf.shape[0] // SLAB
    d = wg_buf.shape[1]
    f = wg_buf.shape[2]
    wslot = wplan_ref[1, b]

    def weight_copies(expert, slot_):
        return [pltpu.make_async_copy(src.at[layer, expert], dst.at[slot_], wsem.at[slot_])
                for src, dst in ((wg_hbm, wg_buf), (wu_hbm, wu_buf), (wd_hbm, wd_buf))]

    @pl.when((b < nu) & (wplan_ref[0, b] == 1))
    def _():
        @pl.when(b == 0)
        def _():
            for c in weight_copies(be_ref[0], wslot):
                c.start()
        for c in weight_copies(be_ref[b], wslot):
            c.wait()
        nxt = wplan_ref[2, b]

        @pl.when(nxt >= 0)
        def _():
            for c in weight_copies(nxt, 1 - wslot):
                c.start()

    wg_ref, wu_ref, wd_ref = wg_buf.at[wslot], wu_buf.at[wslot], wd_buf.at[wslot]
    pad0 = y_hbm.shape[0] - 2 * rows * SLAB

    def scatter_row(dst_ref, r, src_slot, priority):
        dst = pl.multiple_of(dst_ref[0, 0, r], SLAB)
        buf_rows = pl.ds(pl.multiple_of(r * SLAB, SLAB), SLAB)
        pltpu.make_async_copy(obuf[src_slot].at[buf_rows], y_hbm.at[pl.ds(dst, SLAB)],
                              ssem.at[src_slot]).start(priority=priority)

    def wait_scatter(s):
        pltpu.make_async_copy(obuf[s], y_hbm.at[pl.ds(0, rows * SLAB)], ssem.at[s]).wait()

    @pl.when(b == 0)
    def _():
        for s in range(2):
            obuf[s][...] = jnp.zeros_like(obuf[s])
            pltpu.make_async_copy(obuf[s], y_hbm.at[pl.ds(pad0 + s * rows * SLAB, rows * SLAB)],
                                  ssem.at[s]).start()
        for s in range(2):
            wait_scatter(s)

    group = bg_ref[b]

    @pl.when((b < nu) & ((b == 0) | (group != bg_ref[jnp.maximum(b - 1, 0)])))
    def _():
        load = pltpu.make_async_copy(h_hbm.at[group], tok_ref, lsem.at[0])
        load.start()
        load.wait()

    def block(slot, with_scatter):
        other = (slot - 1) % _OUT_BUFFERS

        def gather(i, carry):
            for u in range(_ISSUE_UNROLL):
                r = i * _ISSUE_UNROLL + u
                src = pl.multiple_of(src_ref[0, 0, r], SLAB)
                xbuf[pl.ds(pl.multiple_of(r * SLAB, SLAB), SLAB), :] = tok_ref[pl.ds(src, SLAB), :]
            return carry
        lax.fori_loop(0, rows // _ISSUE_UNROLL, gather, 0)

        if with_scatter:
            for r in range(rows):
                scatter_row(dstp_ref, r, other, r % 2)

        x = _load_packed_rows(xbuf, rows)
        acts = []
        fh = f // 2
        for j in range(2):
            gate = _dot(x, wg_ref[:, j * fh:(j + 1) * fh])
            up = _dot(x, wu_ref[:, j * fh:(j + 1) * fh])
            acts.append((_silu(gate) * up).astype(BF16))
        act = jnp.concatenate(acts, axis=1)
        half = d // 2
        cw = 2 * LANES
        for q in range(half // cw):
            words = _pack_pair(_dot(act, wd_ref[:, q * cw:(q + 1) * cw]),
                               _dot(act, wd_ref[:, half + q * cw:half + (q + 1) * cw]))
            for jj in range(cw // LANES):
                obuf[slot][pl.ds(q * (cw // LANES) + jj, rows, stride=SLAB), :] = words[:, jj * LANES:(jj + 1) * LANES]

    def tail(slot):
        for back in (1, 2):
            @pl.when(b >= back)
            def _(back=back):
                wait_scatter((slot - back) % _OUT_BUFFERS)

        def body(r, carry):
            scatter_row(dstc_ref, r, slot, 0)
            return carry
        lax.fori_loop(0, rows, body, 0)
        wait_scatter(slot)

    @pl.when(b == 0)
    def _():
        block(0, False)

    for s in range(_OUT_BUFFERS):
        @pl.when((b >= 1) & (b < nu) & (b % _OUT_BUFFERS == s))
        def _(s=s):
            @pl.when(b >= _OUT_BUFFERS)
            def _():
                wait_scatter(s)
            block(s, True)

        @pl.when((b == nu - 1) & (b % _OUT_BUFFERS == s))
        def _(s=s):
            tail(s)


def _experts(h_groups, row_src, row_dst, block_e, block_g, weight_plan, n_used, wg, wu, wd, layer, n_slots):
    nb, _, rows = row_src.shape
    d, f = wg.shape[-2], wg.shape[-1]
    group_rows = h_groups.shape[1]

    def smem(shift):
        def index(b, be, bg, wplan, nu):
            return (jnp.maximum(jnp.minimum(b + shift, nu[0] - 1), 0), 0, 0)
        return pl.BlockSpec((1, 1, rows), index, memory_space=pltpu.SMEM)

    hbm = pl.BlockSpec(memory_space=pl.ANY)
    grid_spec = pltpu.PrefetchScalarGridSpec(
        num_scalar_prefetch=4,
        grid=(nb,),
        in_specs=[smem(0), smem(-1), smem(0), hbm, hbm, hbm, hbm],
        out_specs=hbm,
        scratch_shapes=[pltpu.VMEM((group_rows, LANES), jnp.uint32)]
        + [pltpu.VMEM((rows * SLAB, LANES), jnp.uint32)] * (1 + _OUT_BUFFERS)
        + [pltpu.VMEM((2, d, f), BF16), pltpu.VMEM((2, d, f), BF16), pltpu.VMEM((2, f, d), BF16)]
        + [pltpu.SemaphoreType.DMA((1,)), pltpu.SemaphoreType.DMA((_OUT_BUFFERS,)),
           pltpu.SemaphoreType.DMA((2,))],
    )
    return pl.pallas_call(
        functools.partial(_experts_kernel, layer),
        grid_spec=grid_spec,
        out_shape=jax.ShapeDtypeStruct(((n_slots + 2 * rows) * SLAB, LANES), jnp.uint32),
        compiler_params=pltpu.CompilerParams(dimension_semantics=("arbitrary",),
                                             vmem_limit_bytes=VMEM_LIMIT),
        name="experts",
    )(block_e, block_g, weight_plan, n_used, row_src, row_dst, row_dst, h_groups, wg, wu, wd)


_COMBINE_SUB = 32


def _combine_kernel(h_ref, gate_ref, *refs):
    y_refs = refs[:TOP_K]
    sg_ref, su_ref, sd_ref, g_ref, b_ref, of_ref, ob_ref, moe_ref = refs[TOP_K:]
    h = h_ref[...]
    tm = h.shape[0]
    half = SLAB * LANES

    def weighted_sum(s, carry):
        row0 = pl.multiple_of(s * _COMBINE_SUB, _COMBINE_SUB)
        rs = pl.ds(row0, _COMBINE_SUB)
        gates = gate_ref[rs, :]
        for j in range(SLAB):
            lo_acc = hi_acc = None
            for kk in range(TOP_K):
                lo, hi = _unpack_words(y_refs[kk][pl.ds(row0 * SLAB + j, _COMBINE_SUB, stride=SLAB), :])
                gk = gates[:, kk:kk + 1]
                lo_acc = gk * lo if kk == 0 else lo_acc + gk * lo
                hi_acc = gk * hi if kk == 0 else hi_acc + gk * hi
            moe_ref[rs, j * LANES:(j + 1) * LANES] = lo_acc
            moe_ref[rs, half + j * LANES:half + (j + 1) * LANES] = hi_acc
        return carry
    lax.fori_loop(0, tm // _COMBINE_SUB, weighted_sum, 0)
    hb = h.astype(BF16)
    act = _silu(_dot(hb, sg_ref[...])) * _dot(hb, su_ref[...])
    y = ALPHA * h + _dot(act.astype(BF16), sd_ref[...]) + moe_ref[...]
    out = _layer_norm(y, g_ref[...], b_ref[...])
    of_ref[...] = out
    ob_ref[...] = out.astype(BF16)


def _combine(h, gates_t, y_slots, sg, su, sd, g, b, tm=COMBINE_TILE):
    t, d = h.shape
    n = t // tm
    f = sg.shape[1]
    vec = pl.BlockSpec((1, d), lambda i: (0, 0))
    row = pl.BlockSpec((tm, d), lambda i: (i, 0))
    return pl.pallas_call(
        _combine_kernel,
        grid=(n,),
        in_specs=[row, pl.BlockSpec((tm, TOP_K), lambda i: (i, 0))]
        + [pl.BlockSpec((tm * SLAB, LANES), lambda i, kk=kk: (kk * n + i, 0)) for kk in range(TOP_K)]
        + [pl.BlockSpec((d, f), lambda i: (0, 0)),
            pl.BlockSpec((d, f), lambda i: (0, 0)),
            pl.BlockSpec((f, d), lambda i: (0, 0)),
            vec, vec],
        out_specs=[row, row],
        out_shape=[jax.ShapeDtypeStruct((t, d), F32), jax.ShapeDtypeStruct((t, d), BF16)],
        scratch_shapes=[pltpu.VMEM((tm, d), F32)],
        compiler_params=pltpu.CompilerParams(dimension_semantics=("parallel",),
                                             vmem_limit_bytes=VMEM_LIMIT),
        name="combine_ln2",
    )(h, gates_t, *([y_slots] * TOP_K), sg, su, sd, g.reshape(1, d), b.reshape(1, d))


def _moe(h, h_slab, router_w, router_bias, wg, wu, wd, layer, sg, su, sd, ln_g, ln_b):
    t, d = h.shape
    tk = t * TOP_K
    n_groups = TOKEN_GROUPS if t % (TOKEN_GROUPS * ROUTER_TILE) == 0 else 1
    tg = t // n_groups
    n_buckets = n_groups * N_EXPERTS
    n_blocks = tk // MOE_ROWS + n_buckets
    eidx, gates, counts = _router(h, router_w, router_bias, n_groups)
    counts = counts.reshape(n_buckets)
    n_rows = n_blocks * MOE_ROWS
    padded = (counts + MOE_ROWS - 1) // MOE_ROWS * MOE_ROWS
    pend = jnp.cumsum(padded)
    pad_end = jnp.cumsum(padded - counts)
    pad_key = jnp.sum(pad_end[None, :] <= jnp.arange(n_rows - tk, dtype=jnp.int32)[:, None], axis=1)
    slot_group = (jnp.arange(tk, dtype=jnp.int32) >> _TOP_K_BITS) // tg
    keys = jnp.concatenate([slot_group * N_EXPERTS + eidx.T.reshape(tk), pad_key.astype(jnp.int32)])
    index_bits = (n_rows - 1).bit_length()
    assert (n_buckets + 1) << index_bits < 2 ** 31
    packed = lax.sort((keys << index_bits) | jnp.arange(n_rows, dtype=jnp.int32))
    row_slot = packed & ((1 << index_bits) - 1)
    real = row_slot < tk
    row = jnp.arange(n_rows, dtype=jnp.int32)
    row_src = jnp.where(real, ((row_slot >> _TOP_K_BITS) % tg) * SLAB, 0)
    plane_row = (row_slot & (TOP_K - 1)) * t + (row_slot >> _TOP_K_BITS)
    row_dst = jnp.where(real, plane_row, tk + row % (2 * MOE_ROWS)) * SLAB
    block_start = jnp.arange(n_blocks, dtype=jnp.int32) * MOE_ROWS
    block_bucket = jnp.minimum(jnp.sum(pend[None, :] <= block_start[:, None], axis=1), n_buckets - 1).astype(jnp.int32)
    n_used = (pend[-1] // MOE_ROWS).astype(jnp.int32)
    block_e = block_bucket % N_EXPERTS
    block = jnp.arange(n_blocks, dtype=jnp.int32)
    first = (block == 0) | (block_bucket != jnp.roll(block_bucket, 1))
    seq = jnp.cumsum(first.astype(jnp.int32)) - 1
    later_first = jnp.where(first & (block < n_used), block, n_blocks)
    next_first = jnp.concatenate([lax.cummin(later_first, reverse=True)[1:], jnp.full((1,), n_blocks, jnp.int32)])
    next_e = jnp.where(next_first < n_blocks, block_e[jnp.minimum(next_first, n_blocks - 1)], -1)
    weight_plan = jnp.stack([first.astype(jnp.int32), seq % 2, next_e]).astype(jnp.int32)
    y_slots = _experts(h_slab.reshape(n_groups, tg * SLAB, LANES),
                       row_src.reshape(n_blocks, 1, MOE_ROWS), row_dst.reshape(n_blocks, 1, MOE_ROWS),
                       block_e, block_bucket // N_EXPERTS, weight_plan, n_used.reshape(1),
                       wg, wu, wd, layer, tk)
    return _combine(h, gates.T, y_slots, sg, su, sd, ln_g, ln_b, tm=min(COMBINE_TILE, t))


def kernel(x, ln0_gain, ln0_bias, w_in, b_in, hg_lb_logits, sw_sinks, sg_ln_gain, sg_ln_bias, sg_w_s, sg_b_s, mix_gain, w_out, ln1_gain, ln1_bias, router_w, router_bias, exp_w_gate, exp_w_up, exp_w_down, sh_w_gate, sh_w_up, sh_w_down, ln2_gain, ln2_bias):
    b, s, d = x.shape
    t = b * s
    p = jax.nn.softmax(hg_lb_logits.astype(F32), axis=0)
    cs = jnp.cumsum(p, axis=0)
    lbs = cs - cs[0:1]
    a0, a1 = HG_WIDTH, HG_WIDTH + SW_WIDTH

    wg_all, wu_all, wd_all = exp_w_gate.astype(BF16), exp_w_up.astype(BF16), exp_w_down.astype(BF16)
    h, hb = _ln(x.reshape(t, d), ln0_gain, ln0_bias)
    for l in range(DEPTH):
        proj = _in_proj(hb, _regroup_columns(w_in[l]).astype(BF16), _regroup_columns(b_in[l]))
        proj3 = proj.reshape(b, s, D_IN)
        lb = lbs[l]
        o_a = _hgrn(proj3, jnp.log(lb), jnp.log1p(-lb), 1.0 - lb, mix_gain[l, :a0])
        o_b = _swa(proj3, sw_sinks[l], mix_gain[l, a0:a1])
        o_c = _sgu(proj3, sg_ln_gain[l], sg_ln_bias[l], sg_w_s[l], sg_b_s[l], mix_gain[l, a1:])
        h, h_slab = _out_proj(o_a.reshape(t, -1), o_b.reshape(t, -1), o_c.reshape(t, -1),
                              w_out[l].astype(BF16), h, ln1_gain[l], ln1_bias[l])
        h, hb = _moe(h, h_slab, router_w[l], router_bias[l], wg_all, wu_all, wd_all, l,
                     sh_w_gate[l].astype(BF16), sh_w_up[l].astype(BF16), sh_w_down[l].astype(BF16),
                     ln2_gain[l], ln2_bias[l])
    return h.reshape(b, s, d)
```

```python
import functools

import numpy as np
import jax
import jax.numpy as jnp
from jax import lax
from jax.experimental import pallas as pl
from jax.experimental.pallas import tpu as pltpu

F32 = jnp.float32
BF16 = jnp.bfloat16

D_MODEL = 2048
DEPTH = 2
HG_HEADS = 4
HG_DK = 128
HG_WIDTH = HG_HEADS * HG_DK
CHUNK = 128
SW_Q_HEADS = 16
SW_KV_HEADS = 2
SW_HEAD_DIM = 64
SW_WIDTH = SW_Q_HEADS * SW_HEAD_DIM
SW_KV_WIDTH = SW_KV_HEADS * SW_HEAD_DIM
SG_GROUPS = 4
SG_WIDTH = SG_GROUPS * CHUNK
D_IN = 4 * HG_WIDTH + SW_WIDTH + 2 * SW_KV_WIDTH + 2 * SG_WIDTH
N_EXPERTS = 64
TOP_K = 8
D_EXPERT = 512
N_EXPERT_GROUPS = 8
TOPK_GROUPS = 4
ROUTED_SCALE = 2.5
ALPHA = (2 * DEPTH) ** 0.25
LN_EPS = 1e-5
RMS_EPS = 1e-6

_SWQ_OFF = 4 * HG_WIDTH
_SGU_OFF = _SWQ_OFF + SW_WIDTH
_SWK_OFF = _SGU_OFF + 2 * SG_WIDTH


def _regroup_columns(a):
    k0 = _SWQ_OFF + SW_WIDTH
    k1 = k0 + 2 * SW_KV_WIDTH
    return jnp.concatenate([a[..., :k0], a[..., k1:], a[..., k0:k1]], axis=-1)


MOE_ROWS = 256
TOKEN_GROUPS = 2
ROUTER_TILE = 512
COMBINE_TILE = 256
VMEM_LIMIT = 56 * 1024 * 1024

_LEVEL_SIZES = (64, 32, 16, 8, 4, 2, 1)
_N_LEVELS = len(_LEVEL_SIZES)


def _hgrn_constants():
    t = np.arange(CHUNK)[:, None]
    u = np.arange(CHUNK)[None, :]
    mats = []
    level = np.full((CHUNK, CHUNK), -1, np.int32)
    for li, s in enumerate(_LEVEL_SIZES):
        blk = t // s
        odd = (blk % 2) == 1
        m_query = (u >= blk * s) & (u <= t)
        m_key = (u > t) & (u <= (blk + 1) * s - 1)
        mats.append(np.where(odd, m_query, m_key))
        pair = ((t // s) % 2 == 1) & ((u // s) == (t // s) - 1)
        level[pair] = li
    mats.append(u <= t)
    mats.append(u > t)
    level[np.arange(CHUNK), np.arange(CHUNK)] = _N_LEVELS
    return np.concatenate(mats, 0).astype(np.float32), level


_HGRN_MSTACK, _HGRN_LEVEL = _hgrn_constants()


def _layer_norm(x, g, b):
    mu = jnp.mean(x, axis=-1, keepdims=True)
    xc = x - mu
    var = jnp.mean(xc * xc, axis=-1, keepdims=True)
    return xc * lax.rsqrt(var + LN_EPS) * g + b


def _silu(x):
    return x / (1.0 + jnp.exp(-x))


def _gelu(x):
    return 0.5 * x * (1.0 + lax.erf(x * (2.0 ** -0.5)))


def _dot(a, b):
    return jnp.dot(a, b, preferred_element_type=F32)


def _dot_nt(a, b):
    return lax.dot_general(a, b, (((1,), (1,)), ((), ())), preferred_element_type=F32)


def _dot_tn(a, b):
    return lax.dot_general(a, b, (((0,), (0,)), ((), ())), preferred_element_type=F32)


def _ln_kernel(x_ref, g_ref, b_ref, of_ref, ob_ref):
    y = _layer_norm(x_ref[...], g_ref[...], b_ref[...])
    of_ref[...] = y
    ob_ref[...] = y.astype(BF16)


def _ln(x, g, b, tm=256):
    t, d = x.shape
    row = pl.BlockSpec((tm, d), lambda i: (i, 0))
    vec = pl.BlockSpec((1, d), lambda i: (0, 0))
    return pl.pallas_call(
        _ln_kernel,
        grid=(t // tm,),
        in_specs=[row, vec, vec],
        out_specs=[row, row],
        out_shape=[jax.ShapeDtypeStruct((t, d), F32), jax.ShapeDtypeStruct((t, d), BF16)],
        name="ln0",
    )(x, g.reshape(1, d), b.reshape(1, d))


def _mm_bias_kernel(a_ref, w_ref, b_ref, o_ref):
    o_ref[...] = _dot(a_ref[...], w_ref[...]) + b_ref[...]


def _in_proj(a, w, b, tm=512):
    t, k = a.shape
    n = w.shape[1]
    tm = min(tm, t)
    tn = n // 2
    return pl.pallas_call(
        _mm_bias_kernel,
        grid=(n // tn, t // tm),
        in_specs=[pl.BlockSpec((tm, k), lambda j, i: (i, 0)),
                  pl.BlockSpec((k, tn), lambda j, i: (0, j)),
                  pl.BlockSpec((1, tn), lambda j, i: (0, j))],
        out_specs=pl.BlockSpec((tm, tn), lambda j, i: (i, j)),
        out_shape=jax.ShapeDtypeStruct((t, n), F32),
        compiler_params=pltpu.CompilerParams(vmem_limit_bytes=VMEM_LIMIT),
        name="in_proj",
    )(a, w, b.reshape(1, n))


def _mixer_call(kernel_fn, name, proj3, in_specs, args, width, semantics, cast, scratch_shapes=()):
    b, s, _ = proj3.shape
    nc = s // CHUNK
    w_all, layer = cast
    cols = w_all.shape[1]
    rows = w_all.shape[0] // DEPTH
    per_step = rows // (b * nc)
    assert per_step * b * nc == rows and per_step % 16 == 0
    n_in = len(in_specs)

    def body(*refs):
        w_in, out, w_out = refs[n_in], refs[n_in + 1], refs[n_in + 2]
        kernel_fn(*refs[:n_in], out, *refs[n_in + 3:])
        w_out[...] = w_in[...].astype(BF16)

    return pl.pallas_call(
        body,
        grid=(b, nc),
        in_specs=list(in_specs) + [
            pl.BlockSpec((per_step, cols), lambda bi, ci: (layer * (b * nc) + bi * nc + ci, 0))],
        out_specs=[pl.BlockSpec((None, CHUNK, width), lambda bi, ci: (bi, ci, 0)),
                   pl.BlockSpec((per_step, cols), lambda bi, ci: (bi * nc + ci, 0))],
        out_shape=[jax.ShapeDtypeStruct((b, s, width), BF16), jax.ShapeDtypeStruct((rows, cols), BF16)],
        scratch_shapes=list(scratch_shapes),
        compiler_params=pltpu.CompilerParams(dimension_semantics=semantics, vmem_limit_bytes=VMEM_LIMIT),
        name=name,
    )(*args, w_all)


def _hgrn_kernel(q_ref, f_ref, i_ref, g_ref, llb_ref, l1lb_ref, oml_ref, gain_ref,
                 mstack_ref, level_ref, o_ref, state_ref):
    c = pl.program_id(1)

    @pl.when(c == 0)
    def _():
        state_ref[...] = jnp.zeros_like(state_ref)

    q = _silu(q_ref[...])
    z = f_ref[...]
    ez = jnp.exp(-jnp.abs(z))
    log_sig = jnp.minimum(z, 0.0) - jnp.log(1.0 + ez)
    sig_neg = jnp.where(z >= 0.0, ez, 1.0) / (1.0 + ez)
    k = oml_ref[...] * sig_neg
    a = llb_ref[...]
    bb = l1lb_ref[...] + log_sig
    log_f = jnp.maximum(a, bb) + jnp.log(1.0 + jnp.exp(-jnp.abs(a - bb)))

    lf_hi = log_f.astype(BF16)
    lf_lo = (log_f - lf_hi.astype(F32)).astype(BF16)
    m = mstack_ref[...]
    sums = _dot(m, lf_hi) + _dot(m, lf_lo)
    decay = jnp.exp(sums)

    rows = lax.broadcasted_iota(jnp.int32, (CHUNK, 1), 0)
    factors = []
    for li, s in enumerate(_LEVEL_SIZES):
        is_query = ((rows // s) % 2) == 1
        factors.append((jnp.where(is_query, q, k) * decay[li * CHUNK:(li + 1) * CHUNK]).astype(BF16))
    cum = _N_LEVELS * CHUNK
    q_in = (q * decay[cum:cum + CHUNK]).astype(BF16)
    k_dec = (k * decay[cum + CHUNK:cum + 2 * CHUNK]).astype(BF16)
    end_decay = decay[cum + CHUNK - 1:cum + CHUNK]
    qb = q.astype(BF16)
    kb = k.astype(BF16)
    vb = i_ref[...].astype(BF16)
    level = level_ref[...]
    gate = _silu(g_ref[...])
    gain = gain_ref[...]

    for h in range(HG_HEADS):
        hs = slice(h * HG_DK, (h + 1) * HG_DK)
        scores = jnp.zeros((CHUNK, CHUNK), F32)
        for li in range(_N_LEVELS):
            fl = factors[li][:, hs]
            scores = jnp.where(level == li, _dot_nt(fl, fl), scores)
        scores = jnp.where(level == _N_LEVELS, _dot_nt(qb[:, hs], kb[:, hs]), scores)
        st = state_ref[h]
        o = _dot(scores.astype(BF16), vb[:, hs]) + _dot_nt(q_in[:, hs], st.astype(BF16))
        state_ref[h] = st * end_decay[:, hs] + _dot_tn(vb[:, hs], k_dec[:, hs])
        ms = jnp.mean(o * o, axis=-1, keepdims=True)
        o = o * lax.rsqrt(ms + RMS_EPS) * gain[:, hs] * gate[:, hs]
        o_ref[:, hs] = o.astype(o_ref.dtype)


def _hgrn(proj3, llb, l1lb, oml, gain, cast):
    w = HG_WIDTH

    def col(j):
        return pl.BlockSpec((None, CHUNK, w), lambda bi, ci, j=j: (bi, ci, j))

    vec = pl.BlockSpec((1, w), lambda bi, ci: (0, 0))
    nm = _HGRN_MSTACK.shape[0]
    return _mixer_call(
        _hgrn_kernel, "hgrn2", proj3,
        [col(0), col(1), col(2), col(3), vec, vec, vec, vec,
         pl.BlockSpec((nm, CHUNK), lambda bi, ci: (0, 0)),
         pl.BlockSpec((CHUNK, CHUNK), lambda bi, ci: (0, 0))],
        (proj3, proj3, proj3, proj3, llb.reshape(1, w), l1lb.reshape(1, w), oml.reshape(1, w),
         gain.reshape(1, w), jnp.asarray(_HGRN_MSTACK, BF16), jnp.asarray(_HGRN_LEVEL)),
        w, ("parallel", "arbitrary"), cast,
        scratch_shapes=[pltpu.VMEM((HG_HEADS, HG_DK, HG_DK), F32)])


def _swa_kernel(q_ref, kp_ref, kc_ref, vp_ref, vc_ref, sink_ref, gain_ref, o_ref):
    n = pl.program_id(1)
    g = SW_Q_HEADS // SW_KV_HEADS
    hd = SW_HEAD_DIM
    assert (hd ** -0.5) == 2.0 ** round(np.log2(hd ** -0.5))
    q = (q_ref[...] * (hd ** -0.5)).astype(BF16)
    kband = jnp.concatenate([kp_ref[...], kc_ref[...]], axis=0).astype(BF16)
    vband = jnp.concatenate([vp_ref[...], vc_ref[...]], axis=0).astype(BF16)
    t = lax.broadcasted_iota(jnp.int32, (CHUNK, 2 * CHUNK), 0)
    s = lax.broadcasted_iota(jnp.int32, (CHUNK, 2 * CHUNK), 1)
    rel = t + CHUNK - s
    mask = (rel >= 0) & (rel < CHUNK) & ((s >= CHUNK) | (n > 0))
    bias = jnp.where(mask, 0.0, -jnp.inf)
    kvs = [(kband[:, kv * hd:(kv + 1) * hd], vband[:, kv * hd:(kv + 1) * hd]) for kv in range(SW_KV_HEADS)]
    outs = []
    for h in range(SW_Q_HEADS):
        k_h, v_h = kvs[h // g]
        sc = _dot_nt(q[:, h * hd:(h + 1) * hd], k_h) + bias
        sink = sink_ref[h]
        mx = jnp.maximum(jnp.max(sc, axis=-1, keepdims=True), sink)
        p = jnp.exp(sc - mx)
        denom = jnp.sum(p, axis=-1, keepdims=True) + jnp.exp(sink - mx)
        outs.append(_dot(p.astype(BF16), v_h) / denom)
    o = jnp.concatenate(outs, axis=-1)
    ms = jnp.mean(o * o, axis=-1, keepdims=True)
    o_ref[...] = (o * lax.rsqrt(ms + RMS_EPS) * gain_ref[...]).astype(o_ref.dtype)


def _swa(proj3, sinks, gain, cast):
    q_col = _SWQ_OFF // SW_WIDTH
    k_col = _SWK_OFF // SW_KV_WIDTH
    v_col = k_col + 1

    def kv_spec(col, prev):
        if prev:
            return pl.BlockSpec((None, CHUNK, SW_KV_WIDTH), lambda bi, ni: (bi, jnp.maximum(ni - 1, 0), col))
        return pl.BlockSpec((None, CHUNK, SW_KV_WIDTH), lambda bi, ni: (bi, ni, col))

    return _mixer_call(
        _swa_kernel, "swa", proj3,
        [pl.BlockSpec((None, CHUNK, SW_WIDTH), lambda bi, ni: (bi, ni, q_col)),
         kv_spec(k_col, True), kv_spec(k_col, False),
         kv_spec(v_col, True), kv_spec(v_col, False),
         pl.BlockSpec(memory_space=pltpu.SMEM),
         pl.BlockSpec((1, SW_WIDTH), lambda bi, ni: (0, 0))],
        (proj3, proj3, proj3, proj3, proj3, sinks.astype(F32), gain.reshape(1, SW_WIDTH)),
        SW_WIDTH, ("parallel", "parallel"), cast)


def _sgu_kernel(u_ref, v_ref, lng_ref, lnb_ref, w_ref, bs_ref, gain_ref, o_ref):
    u = _gelu(u_ref[...])
    v = _layer_norm(_gelu(v_ref[...]), lng_ref[...], lnb_ref[...]).astype(BF16)
    r = lax.broadcasted_iota(jnp.int32, (CHUNK, CHUNK), 0)
    c = lax.broadcasted_iota(jnp.int32, (CHUNK, CHUNK), 1)
    tril = c <= r
    bs = bs_ref[...]
    parts = []
    for gi in range(SG_GROUPS):
        gs = slice(gi * CHUNK, (gi + 1) * CHUNK)
        w = jnp.where(tril, w_ref[gi], 0.0).astype(BF16)
        parts.append(_dot(w, v[:, gs]) + bs[:, gi:gi + 1])
    o = u * jnp.concatenate(parts, axis=-1)
    ms = jnp.mean(o * o, axis=-1, keepdims=True)
    o_ref[...] = (o * lax.rsqrt(ms + RMS_EPS) * gain_ref[...]).astype(o_ref.dtype)


def _sgu(proj3, ln_g, ln_b, w_s, b_s, gain, cast):
    w = SG_WIDTH
    u_col = _SGU_OFF // SG_WIDTH
    vec = pl.BlockSpec((1, w), lambda bi, ci: (0, 0))
    return _mixer_call(
        _sgu_kernel, "sgu", proj3,
        [pl.BlockSpec((None, CHUNK, w), lambda bi, ci: (bi, ci, u_col)),
         pl.BlockSpec((None, CHUNK, w), lambda bi, ci: (bi, ci, u_col + 1)),
         vec, vec,
         pl.BlockSpec((SG_GROUPS, CHUNK, CHUNK), lambda bi, ci: (0, 0, 0)),
         pl.BlockSpec((CHUNK, SG_GROUPS), lambda bi, ci: (0, 0)),
         vec],
        (proj3, proj3, ln_g.reshape(1, w), ln_b.reshape(1, w), w_s, b_s.T, gain.reshape(1, w)),
        w, ("parallel", "parallel"), cast)


LANES = 128
SLAB = D_MODEL // 2 // LANES


def _pack_pair(low, high):
    def bits(v):
        return lax.bitcast_convert_type(v.astype(BF16).astype(F32), jnp.uint32)

    return (bits(high) & jnp.uint32(0xFFFF0000)) | (bits(low) >> 16)


def _pack_rows(x):
    half = x.shape[1] // 2
    return _pack_pair(x[:, :half], x[:, half:])


def _unpack_words(w):
    return (lax.bitcast_convert_type(w << 16, F32),
            lax.bitcast_convert_type(w & jnp.uint32(0xFFFF0000), F32))


def _store_slabs(ref, x):
    n = x.shape[0]
    for j in range(SLAB):
        ref[pl.ds(j, n, stride=SLAB), :] = x[:, j * LANES:(j + 1) * LANES]


def _load_packed_rows(ref, n, first=0, stride=SLAB):
    lows, highs = [], []
    for j in range(SLAB):
        lo, hi = _unpack_words(ref[pl.ds(first + j, n, stride=stride), :])
        lows.append(lo.astype(BF16))
        highs.append(hi.astype(BF16))
    return jnp.concatenate(lows + highs, axis=1)


def _out_proj_kernel(oa_ref, ob_ref, oc_ref, w_ref, h_ref, g_ref, b_ref, of_ref, os_ref):
    mixed = jnp.concatenate([oa_ref[...], ob_ref[...], oc_ref[...]], axis=-1)
    y = ALPHA * h_ref[...] + _dot(mixed, w_ref[...])
    out = _layer_norm(y, g_ref[...], b_ref[...])
    of_ref[...] = out
    _store_slabs(os_ref, _pack_rows(out))


def _out_proj(oa, ob, oc, w, h, g, b, tm=512):
    t, d = h.shape
    tm = min(tm, t)
    vec = pl.BlockSpec((1, d), lambda i: (0, 0))
    return pl.pallas_call(
        _out_proj_kernel,
        grid=(t // tm,),
        in_specs=[pl.BlockSpec((tm, oa.shape[1]), lambda i: (i, 0)),
                  pl.BlockSpec((tm, ob.shape[1]), lambda i: (i, 0)),
                  pl.BlockSpec((tm, oc.shape[1]), lambda i: (i, 0)),
                  pl.BlockSpec(w.shape, lambda i: (0, 0)),
                  pl.BlockSpec((tm, d), lambda i: (i, 0)), vec, vec],
        out_specs=[pl.BlockSpec((tm, d), lambda i: (i, 0)),
                   pl.BlockSpec((tm * SLAB, LANES), lambda i: (i, 0))],
        out_shape=[jax.ShapeDtypeStruct((t, d), F32), jax.ShapeDtypeStruct((t * SLAB, LANES), jnp.uint32)],
        compiler_params=pltpu.CompilerParams(vmem_limit_bytes=VMEM_LIMIT),
        name="out_proj_ln1",
    )(oa, ob, oc, w, h, g.reshape(1, d), b.reshape(1, d))


def _first_index_of_max(x, iota, size, axis):
    mx = jnp.max(x, axis=axis, keepdims=True)
    idx = jnp.min(jnp.where(x == mx, iota, size), axis=axis, keepdims=True)
    return mx, idx


def _router_kernel(tiles_per_group, h_ref, wt_ref, bias_ref, eidx_ref, gate_ref, count_ref, run_ref):
    i = pl.program_id(0)

    @pl.when(i % tiles_per_group == 0)
    def _():
        run_ref[...] = jnp.zeros_like(run_ref)

    tm = h_ref.shape[0]
    per_group = N_EXPERTS // N_EXPERT_GROUPS
    def split(v):
        hi = v.astype(BF16)
        return hi, (v - hi.astype(F32)).astype(BF16)

    w_hi, w_lo = split(wt_ref[...])
    h_hi, h_lo = split(h_ref[...])
    logits = _dot_nt(w_hi, h_hi) + (_dot_nt(w_hi, h_lo) + _dot_nt(w_lo, h_hi))
    scores = 1.0 / (1.0 + jnp.exp(-logits))
    sel = scores + bias_ref[...]
    sel3 = sel.reshape(N_EXPERT_GROUPS, per_group, tm)
    io3 = lax.broadcasted_iota(jnp.int32, sel3.shape, 1)
    m1, i1 = _first_index_of_max(sel3, io3, per_group, 1)
    m2 = jnp.max(jnp.where(io3 == i1, -jnp.inf, sel3), axis=1, keepdims=True)
    grp = (m1 + m2).reshape(N_EXPERT_GROUPS, tm)
    iog = lax.broadcasted_iota(jnp.int32, grp.shape, 0)
    keep = jnp.zeros(grp.shape, jnp.bool_)
    for _ in range(TOPK_GROUPS):
        _, gi = _first_index_of_max(grp, iog, N_EXPERT_GROUPS, 0)
        hit = iog == gi
        keep = keep | hit
        grp = jnp.where(hit, -jnp.inf, grp)
    keep3 = jnp.broadcast_to(keep.reshape(N_EXPERT_GROUPS, 1, tm), sel3.shape)
    cand = jnp.where(keep3, sel3, -jnp.inf).reshape(N_EXPERTS, tm)
    ioe = lax.broadcasted_iota(jnp.int32, cand.shape, 0)
    chosen = jnp.zeros(cand.shape, F32)
    idxs, gvals = [], []
    for _ in range(TOP_K):
        _, ei = _first_index_of_max(cand, ioe, N_EXPERTS, 0)
        hit = ioe == ei
        idxs.append(ei)
        gvals.append(jnp.sum(jnp.where(hit, scores, 0.0), axis=0, keepdims=True))
        chosen = jnp.where(hit, 1.0, chosen)
        cand = jnp.where(hit, -jnp.inf, cand)
    gsum = functools.reduce(lambda a, b: a + b, gvals)
    for kk in range(TOP_K):
        eidx_ref[kk:kk + 1, :] = idxs[kk]
        gate_ref[kk:kk + 1, :] = gvals[kk] / gsum * ROUTED_SCALE
    run = run_ref[...] + jnp.sum(chosen, axis=1, keepdims=True)
    run_ref[...] = run
    count_ref[...] = run.astype(jnp.int32)


def _router(h, router_w, router_bias, n_groups, tm=ROUTER_TILE):
    t, d = h.shape
    tm = min(tm, t // n_groups)
    tiles_per_group = t // n_groups // tm
    slot = pl.BlockSpec((TOP_K, tm), lambda i: (0, i))
    return pl.pallas_call(
        functools.partial(_router_kernel, tiles_per_group),
        grid=(t // tm,),
        in_specs=[pl.BlockSpec((tm, d), lambda i: (i, 0)),
                  pl.BlockSpec((N_EXPERTS, d), lambda i: (0, 0)),
                  pl.BlockSpec((N_EXPERTS, 1), lambda i: (0, 0))],
        out_specs=[slot, slot, pl.BlockSpec((None, N_EXPERTS, 1), lambda i: (i // tiles_per_group, 0, 0))],
        out_shape=[jax.ShapeDtypeStruct((TOP_K, t), jnp.int32),
                   jax.ShapeDtypeStruct((TOP_K, t), F32),
                   jax.ShapeDtypeStruct((n_groups, N_EXPERTS, 1), jnp.int32)],
        scratch_shapes=[pltpu.VMEM((N_EXPERTS, 1), F32)],
        compiler_params=pltpu.CompilerParams(dimension_semantics=("arbitrary",),
                                             vmem_limit_bytes=VMEM_LIMIT),
        name="router",
    )(h, router_w.T, router_bias.reshape(N_EXPERTS, 1))


_ISSUE_UNROLL = 8
_TOP_K_BITS = TOP_K.bit_length() - 1
assert 1 << _TOP_K_BITS == TOP_K


_OUT_BUFFERS = 3


def _experts_kernel(be_ref, bg_ref, wplan_ref, nu_ref, src_ref, dstp_ref, dstc_ref, h_hbm,
                    wg_hbm, wu_hbm, wd_hbm, y_hbm, tok_ref, xbuf, obuf0, obuf1, obuf2, wg_buf, wu_buf, wd_buf,
                    lsem, ssem, wsem):
    b = pl.program_id(0)
    nu = nu_ref[0]
    obuf = (obuf0, obuf1, obuf2)
    rows = xbuf.shape[0] // SLAB
    d = wg_buf.shape[1]
    f = wg_buf.shape[2]
    wslot = wplan_ref[1, b]

    def weight_copies(expert, slot_):
        return [pltpu.make_async_copy(src.at[expert], dst.at[slot_], wsem.at[slot_])
                for src, dst in ((wg_hbm, wg_buf), (wu_hbm, wu_buf), (wd_hbm, wd_buf))]

    @pl.when((b < nu) & (wplan_ref[0, b] == 1))
    def _():
        @pl.when(b == 0)
        def _():
            for c in weight_copies(be_ref[0], wslot):
                c.start()
        for c in weight_copies(be_ref[b], wslot):
            c.wait()
        nxt = wplan_ref[2, b]

        @pl.when(nxt >= 0)
        def _():
            for c in weight_copies(nxt, 1 - wslot):
                c.start()

    wg_ref, wu_ref, wd_ref = wg_buf.at[wslot], wu_buf.at[wslot], wd_buf.at[wslot]
    pad0 = y_hbm.shape[0] - 2 * rows * SLAB

    def scatter_row(dst_ref, r, src_slot, priority):
        dst = pl.multiple_of(dst_ref[0, 0, r], SLAB)
        buf_rows = pl.ds(pl.multiple_of(r * SLAB, SLAB), SLAB)
        pltpu.make_async_copy(obuf[src_slot].at[buf_rows], y_hbm.at[pl.ds(dst, SLAB)],
                              ssem.at[src_slot]).start(priority=priority)

    def wait_scatter(s):
        pltpu.make_async_copy(obuf[s], y_hbm.at[pl.ds(0, rows * SLAB)], ssem.at[s]).wait()

    @pl.when(b == 0)
    def _():
        for s in range(2):
            obuf[s][...] = jnp.zeros_like(obuf[s])
            pltpu.make_async_copy(obuf[s], y_hbm.at[pl.ds(pad0 + s * rows * SLAB, rows * SLAB)],
                                  ssem.at[s]).start()
        for s in range(2):
            wait_scatter(s)

    group = bg_ref[b]

    @pl.when((b < nu) & ((b == 0) | (group != bg_ref[jnp.maximum(b - 1, 0)])))
    def _():
        load = pltpu.make_async_copy(h_hbm.at[group], tok_ref, lsem.at[0])
        load.start()
        load.wait()

    def block(slot, with_scatter):
        other = (slot - 1) % _OUT_BUFFERS

        def gather(i, carry):
            for u in range(_ISSUE_UNROLL):
                r = i * _ISSUE_UNROLL + u
                src = pl.multiple_of(src_ref[0, 0, r], SLAB)
                xbuf[pl.ds(pl.multiple_of(r * SLAB, SLAB), SLAB), :] = tok_ref[pl.ds(src, SLAB), :]
            return carry
        lax.fori_loop(0, rows // _ISSUE_UNROLL, gather, 0)

        if with_scatter:
            for r in range(rows):
                scatter_row(dstp_ref, r, other, r % 2)

        x = _load_packed_rows(xbuf, rows)
        acts = []
        fh = f // 2
        for j in range(2):
            gate = _dot(x, wg_ref[:, j * fh:(j + 1) * fh])
            up = _dot(x, wu_ref[:, j * fh:(j + 1) * fh])
            acts.append((_silu(gate) * up).astype(BF16))
        act = jnp.concatenate(acts, axis=1)
        half = d // 2
        cw = 2 * LANES
        for q in range(half // cw):
            words = _pack_pair(_dot(act, wd_ref[:, q * cw:(q + 1) * cw]),
                               _dot(act, wd_ref[:, half + q * cw:half + (q + 1) * cw]))
            for jj in range(cw // LANES):
                obuf[slot][pl.ds(q * (cw // LANES) + jj, rows, stride=SLAB), :] = words[:, jj * LANES:(jj + 1) * LANES]

    def tail(slot):
        for back in (1, 2):
            @pl.when(b >= back)
            def _(back=back):
                wait_scatter((slot - back) % _OUT_BUFFERS)

        def body(r, carry):
            scatter_row(dstc_ref, r, slot, 0)
            return carry
        lax.fori_loop(0, rows, body, 0)
        wait_scatter(slot)

    @pl.when(b == 0)
    def _():
        block(0, False)

    for s in range(_OUT_BUFFERS):
        @pl.when((b >= 1) & (b < nu) & (b % _OUT_BUFFERS == s))
        def _(s=s):
            @pl.when(b >= _OUT_BUFFERS)
            def _():
                wait_scatter(s)
            block(s, True)

        @pl.when((b == nu - 1) & (b % _OUT_BUFFERS == s))
        def _(s=s):
            tail(s)


def _experts(h_groups, row_src, row_dst, block_e, block_g, weight_plan, n_used, wg, wu, wd, n_slots):
    nb, _, rows = row_src.shape
    d, f = wg.shape[-2], wg.shape[-1]
    group_rows = h_groups.shape[1]

    def smem(shift):
        def index(b, be, bg, wplan, nu):
            return (jnp.maximum(jnp.minimum(b + shift, nu[0] - 1), 0), 0, 0)
        return pl.BlockSpec((1, 1, rows), index, memory_space=pltpu.SMEM)

    hbm = pl.BlockSpec(memory_space=pl.ANY)
    grid_spec = pltpu.PrefetchScalarGridSpec(
        num_scalar_prefetch=4,
        grid=(nb,),
        in_specs=[smem(0), smem(-1), smem(0), hbm, hbm, hbm, hbm],
        out_specs=hbm,
        scratch_shapes=[pltpu.VMEM((group_rows, LANES), jnp.uint32)]
        + [pltpu.VMEM((rows * SLAB, LANES), jnp.uint32)] * (1 + _OUT_BUFFERS)
        + [pltpu.VMEM((2, d, f), BF16), pltpu.VMEM((2, d, f), BF16), pltpu.VMEM((2, f, d), BF16)]
        + [pltpu.SemaphoreType.DMA((1,)), pltpu.SemaphoreType.DMA((_OUT_BUFFERS,)),
           pltpu.SemaphoreType.DMA((2,))],
    )
    return pl.pallas_call(
        _experts_kernel,
        grid_spec=grid_spec,
        out_shape=jax.ShapeDtypeStruct(((n_slots + 2 * rows) * SLAB, LANES), jnp.uint32),
        compiler_params=pltpu.CompilerParams(dimension_semantics=("arbitrary",),
                                             vmem_limit_bytes=VMEM_LIMIT),
        name="experts",
    )(block_e, block_g, weight_plan, n_used, row_src, row_dst, row_dst, h_groups, wg, wu, wd)


_COMBINE_SUB = 32


def _combine_kernel(h_ref, gate_ref, *refs):
    y_refs = refs[:TOP_K]
    sg_ref, su_ref, sd_ref, g_ref, b_ref, of_ref, ob_ref, moe_ref = refs[TOP_K:]
    h = h_ref[...]
    tm = h.shape[0]
    half = SLAB * LANES

    def weighted_sum(s, carry):
        row0 = pl.multiple_of(s * _COMBINE_SUB, _COMBINE_SUB)
        rs = pl.ds(row0, _COMBINE_SUB)
        gates = gate_ref[rs, :]
        for j in range(SLAB):
            lo_acc = hi_acc = None
            for kk in range(TOP_K):
                lo, hi = _unpack_words(y_refs[kk][pl.ds(row0 * SLAB + j, _COMBINE_SUB, stride=SLAB), :])
                gk = gates[:, kk:kk + 1]
                lo_acc = gk * lo if kk == 0 else lo_acc + gk * lo
                hi_acc = gk * hi if kk == 0 else hi_acc + gk * hi
            moe_ref[rs, j * LANES:(j + 1) * LANES] = lo_acc
            moe_ref[rs, half + j * LANES:half + (j + 1) * LANES] = hi_acc
        return carry
    lax.fori_loop(0, tm // _COMBINE_SUB, weighted_sum, 0)
    hb = h.astype(BF16)
    act = _silu(_dot(hb, sg_ref[...])) * _dot(hb, su_ref[...])
    y = ALPHA * h + _dot(act.astype(BF16), sd_ref[...]) + moe_ref[...]
    out = _layer_norm(y, g_ref[...], b_ref[...])
    of_ref[...] = out
    ob_ref[...] = out.astype(BF16)


def _combine(h, gates_t, y_slots, sg, su, sd, g, b, tm=COMBINE_TILE):
    t, d = h.shape
    n = t // tm
    f = sg.shape[1]
    vec = pl.BlockSpec((1, d), lambda i: (0, 0))
    row = pl.BlockSpec((tm, d), lambda i: (i, 0))
    return pl.pallas_call(
        _combine_kernel,
        grid=(n,),
        in_specs=[row, pl.BlockSpec((tm, TOP_K), lambda i: (i, 0))]
        + [pl.BlockSpec((tm * SLAB, LANES), lambda i, kk=kk: (kk * n + i, 0)) for kk in range(TOP_K)]
        + [pl.BlockSpec((d, f), lambda i: (0, 0)),
            pl.BlockSpec((d, f), lambda i: (0, 0)),
            pl.BlockSpec((f, d), lambda i: (0, 0)),
            vec, vec],
        out_specs=[row, row],
        out_shape=[jax.ShapeDtypeStruct((t, d), F32), jax.ShapeDtypeStruct((t, d), BF16)],
        scratch_shapes=[pltpu.VMEM((tm, d), F32)],
        compiler_params=pltpu.CompilerParams(dimension_semantics=("parallel",),
                                             vmem_limit_bytes=VMEM_LIMIT),
        name="combine_ln2",
    )(h, gates_t, *([y_slots] * TOP_K), sg, su, sd, g.reshape(1, d), b.reshape(1, d))


def _moe(h, h_slab, router_w, router_bias, wg, wu, wd, sg, su, sd, ln_g, ln_b):
    t, d = h.shape
    tk = t * TOP_K
    n_groups = TOKEN_GROUPS if t % (TOKEN_GROUPS * ROUTER_TILE) == 0 else 1
    tg = t // n_groups
    n_buckets = n_groups * N_EXPERTS
    n_blocks = tk // MOE_ROWS + n_buckets
    eidx, gates, counts = _router(h, router_w, router_bias, n_groups)
    counts = counts.reshape(n_buckets)
    n_rows = n_blocks * MOE_ROWS
    padded = (counts + MOE_ROWS - 1) // MOE_ROWS * MOE_ROWS
    pend = jnp.cumsum(padded)
    pad_end = jnp.cumsum(padded - counts)
    pad_key = jnp.sum(pad_end[None, :] <= jnp.arange(n_rows - tk, dtype=jnp.int32)[:, None], axis=1)
    slot_group = (jnp.arange(tk, dtype=jnp.int32) >> _TOP_K_BITS) // tg
    keys = jnp.concatenate([slot_group * N_EXPERTS + eidx.T.reshape(tk), pad_key.astype(jnp.int32)])
    index_bits = (n_rows - 1).bit_length()
    assert (n_buckets + 1) << index_bits < 2 ** 31
    packed = lax.sort((keys << index_bits) | jnp.arange(n_rows, dtype=jnp.int32))
    row_slot = packed & ((1 << index_bits) - 1)
    real = row_slot < tk
    row = jnp.arange(n_rows, dtype=jnp.int32)
    row_src = jnp.where(real, ((row_slot >> _TOP_K_BITS) % tg) * SLAB, 0)
    plane_row = (row_slot & (TOP_K - 1)) * t + (row_slot >> _TOP_K_BITS)
    row_dst = jnp.where(real, plane_row, tk + row % (2 * MOE_ROWS)) * SLAB
    block_start = jnp.arange(n_blocks, dtype=jnp.int32) * MOE_ROWS
    block_bucket = jnp.minimum(jnp.sum(pend[None, :] <= block_start[:, None], axis=1), n_buckets - 1).astype(jnp.int32)
    n_used = (pend[-1] // MOE_ROWS).astype(jnp.int32)
    block_e = block_bucket % N_EXPERTS
    block = jnp.arange(n_blocks, dtype=jnp.int32)
    first = (block == 0) | (block_bucket != jnp.roll(block_bucket, 1))
    seq = jnp.cumsum(first.astype(jnp.int32)) - 1
    later_first = jnp.where(first & (block < n_used), block, n_blocks)
    next_first = jnp.concatenate([lax.cummin(later_first, reverse=True)[1:], jnp.full((1,), n_blocks, jnp.int32)])
    next_e = jnp.where(next_first < n_blocks, block_e[jnp.minimum(next_first, n_blocks - 1)], -1)
    weight_plan = jnp.stack([first.astype(jnp.int32), seq % 2, next_e]).astype(jnp.int32)
    y_slots = _experts(h_slab.reshape(n_groups, tg * SLAB, LANES),
                       row_src.reshape(n_blocks, 1, MOE_ROWS), row_dst.reshape(n_blocks, 1, MOE_ROWS),
                       block_e, block_bucket // N_EXPERTS, weight_plan, n_used.reshape(1),
                       wg, wu, wd, tk)
    return _combine(h, gates.T, y_slots, sg, su, sd, ln_g, ln_b, tm=min(COMBINE_TILE, t))


def kernel(x, ln0_gain, ln0_bias, w_in, b_in, hg_lb_logits, sw_sinks, sg_ln_gain, sg_ln_bias, sg_w_s, sg_b_s, mix_gain, w_out, ln1_gain, ln1_bias, router_w, router_bias, exp_w_gate, exp_w_up, exp_w_down, sh_w_gate, sh_w_up, sh_w_down, ln2_gain, ln2_bias):
    b, s, d = x.shape
    t = b * s
    p = jax.nn.softmax(hg_lb_logits.astype(F32), axis=0)
    cs = jnp.cumsum(p, axis=0)
    lbs = cs - cs[0:1]
    a0, a1 = HG_WIDTH, HG_WIDTH + SW_WIDTH

    n_e, d_e, f_e = exp_w_gate.shape[1:]
    wg_rows, wu_rows = exp_w_gate.reshape(-1, f_e), exp_w_up.reshape(-1, f_e)
    wd_rows = exp_w_down.reshape(-1, d_e)
    h, hb = _ln(x.reshape(t, d), ln0_gain, ln0_bias)
    for l in range(DEPTH):
        proj = _in_proj(hb, _regroup_columns(w_in[l]).astype(BF16), _regroup_columns(b_in[l]))
        proj3 = proj.reshape(b, s, D_IN)
        lb = lbs[l]
        o_a, wg = _hgrn(proj3, jnp.log(lb), jnp.log1p(-lb), 1.0 - lb, mix_gain[l, :a0], (wg_rows, l))
        o_b, wu = _swa(proj3, sw_sinks[l], mix_gain[l, a0:a1], (wu_rows, l))
        o_c, wd = _sgu(proj3, sg_ln_gain[l], sg_ln_bias[l], sg_w_s[l], sg_b_s[l], mix_gain[l, a1:], (wd_rows, l))
        h, h_slab = _out_proj(o_a.reshape(t, -1), o_b.reshape(t, -1), o_c.reshape(t, -1),
                              w_out[l].astype(BF16), h, ln1_gain[l], ln1_bias[l])
        h, hb = _moe(h, h_slab, router_w[l], router_bias[l], wg.reshape(n_e, d_e, f_e),
                     wu.reshape(n_e, d_e, f_e), wd.reshape(n_e, f_e, d_e),
                     sh_w_gate[l].astype(BF16), sh_w_up[l].astype(BF16), sh_w_down[l].astype(BF16),
                     ln2_gain[l], ln2_bias[l])
    return h.reshape(b, s, d)
```

```python
import functools

import numpy as np
import jax
import jax.numpy as jnp
from jax import lax
from jax.experimental import pallas as pl
from jax.experimental.pallas import tpu as pltpu

F32 = jnp.float32
BF16 = jnp.bfloat16

D_MODEL = 2048
DEPTH = 2
HG_HEADS = 4
HG_DK = 128
HG_WIDTH = HG_HEADS * HG_DK
CHUNK = 128
SW_Q_HEADS = 16
SW_KV_HEADS = 2
SW_HEAD_DIM = 64
SW_WIDTH = SW_Q_HEADS * SW_HEAD_DIM
SW_KV_WIDTH = SW_KV_HEADS * SW_HEAD_DIM
SG_GROUPS = 4
SG_WIDTH = SG_GROUPS * CHUNK
D_IN = 4 * HG_WIDTH + SW_WIDTH + 2 * SW_KV_WIDTH + 2 * SG_WIDTH
N_EXPERTS = 64
TOP_K = 8
D_EXPERT = 512
N_EXPERT_GROUPS = 8
TOPK_GROUPS = 4
ROUTED_SCALE = 2.5
ALPHA = (2 * DEPTH) ** 0.25
LN_EPS = 1e-5
RMS_EPS = 1e-6

_SWQ_OFF = 4 * HG_WIDTH
_SGU_OFF = _SWQ_OFF + SW_WIDTH
_SWK_OFF = _SGU_OFF + 2 * SG_WIDTH


def _regroup_columns(a):
    k0 = _SWQ_OFF + SW_WIDTH
    k1 = k0 + 2 * SW_KV_WIDTH
    return jnp.concatenate([a[..., :k0], a[..., k1:], a[..., k0:k1]], axis=-1)


MOE_ROWS = 256
TOKEN_GROUPS = 2
ROUTER_TILE = 512
COMBINE_TILE = 256
VMEM_LIMIT = 56 * 1024 * 1024

_LEVEL_SIZES = (64, 32, 16, 8, 4, 2, 1)
_N_LEVELS = len(_LEVEL_SIZES)


def _hgrn_constants():
    t = np.arange(CHUNK)[:, None]
    u = np.arange(CHUNK)[None, :]
    mats = []
    level = np.full((CHUNK, CHUNK), -1, np.int32)
    for li, s in enumerate(_LEVEL_SIZES):
        blk = t // s
        odd = (blk % 2) == 1
        m_query = (u >= blk * s) & (u <= t)
        m_key = (u > t) & (u <= (blk + 1) * s - 1)
        mats.append(np.where(odd, m_query, m_key))
        pair = ((t // s) % 2 == 1) & ((u // s) == (t // s) - 1)
        level[pair] = li
    mats.append(u <= t)
    mats.append(u > t)
    level[np.arange(CHUNK), np.arange(CHUNK)] = _N_LEVELS
    return np.concatenate(mats, 0).astype(np.float32), level


_HGRN_MSTACK, _HGRN_LEVEL = _hgrn_constants()


def _layer_norm(x, g, b):
    mu = jnp.mean(x, axis=-1, keepdims=True)
    xc = x - mu
    var = jnp.mean(xc * xc, axis=-1, keepdims=True)
    return xc * lax.rsqrt(var + LN_EPS) * g + b


def _silu(x):
    return x / (1.0 + jnp.exp(-x))


def _gelu(x):
    return 0.5 * x * (1.0 + lax.erf(x * (2.0 ** -0.5)))


def _dot(a, b):
    return jnp.dot(a, b, preferred_element_type=F32)


def _dot_nt(a, b):
    return lax.dot_general(a, b, (((1,), (1,)), ((), ())), preferred_element_type=F32)


def _dot_tn(a, b):
    return lax.dot_general(a, b, (((0,), (0,)), ((), ())), preferred_element_type=F32)


def _ln_kernel(x_ref, g_ref, b_ref, of_ref, ob_ref):
    y = _layer_norm(x_ref[...], g_ref[...], b_ref[...])
    of_ref[...] = y
    ob_ref[...] = y.astype(BF16)


def _ln(x, g, b, tm=512):
    t, d = x.shape
    tm = min(tm, t)
    row = pl.BlockSpec((tm, d), lambda i: (i, 0))
    vec = pl.BlockSpec((1, d), lambda i: (0, 0))
    return pl.pallas_call(
        _ln_kernel,
        grid=(t // tm,),
        in_specs=[row, vec, vec],
        out_specs=[row, row],
        out_shape=[jax.ShapeDtypeStruct((t, d), F32), jax.ShapeDtypeStruct((t, d), BF16)],
        name="ln0",
    )(x, g.reshape(1, d), b.reshape(1, d))


def _mm_bias_kernel(a_ref, w_ref, b_ref, o_ref):
    o_ref[...] = _dot(a_ref[...], w_ref[...]) + b_ref[...]


def _in_proj(a, w, b, tm=512):
    t, k = a.shape
    n = w.shape[1]
    tm = min(tm, t)
    tn = n // 2
    return pl.pallas_call(
        _mm_bias_kernel,
        grid=(n // tn, t // tm),
        in_specs=[pl.BlockSpec((tm, k), lambda j, i: (i, 0)),
                  pl.BlockSpec((k, tn), lambda j, i: (0, j)),
                  pl.BlockSpec((1, tn), lambda j, i: (0, j))],
        out_specs=pl.BlockSpec((tm, tn), lambda j, i: (i, j)),
        out_shape=jax.ShapeDtypeStruct((t, n), F32),
        compiler_params=pltpu.CompilerParams(vmem_limit_bytes=VMEM_LIMIT),
        name="in_proj",
    )(a, w, b.reshape(1, n))


def _mixer_call(kernel_fn, name, proj3, in_specs, args, width, semantics, casts, layer, scratch_shapes=()):
    b, s, _ = proj3.shape
    nc = s // CHUNK
    n_in, n_cast = len(in_specs), len(casts)
    cast_in, cast_out, cast_shapes = [], [], []
    for w_all in casts:
        cols = w_all.shape[1]
        rows = w_all.shape[0] // DEPTH
        per_step = rows // (b * nc)
        assert per_step * b * nc == rows and per_step % 16 == 0
        cast_in.append(pl.BlockSpec((per_step, cols), lambda bi, ci: (layer * (b * nc) + bi * nc + ci, 0)))
        cast_out.append(pl.BlockSpec((per_step, cols), lambda bi, ci: (bi * nc + ci, 0)))
        cast_shapes.append(jax.ShapeDtypeStruct((rows, cols), BF16))

    def body(*refs):
        w_in = refs[n_in:n_in + n_cast]
        out = refs[n_in + n_cast]
        w_out = refs[n_in + n_cast + 1:n_in + 2 * n_cast + 1]
        kernel_fn(*refs[:n_in], out, *refs[n_in + 2 * n_cast + 1:])
        for src, dst in zip(w_in, w_out):
            dst[...] = src[...].astype(BF16)

    return pl.pallas_call(
        body,
        grid=(b, nc),
        in_specs=list(in_specs) + cast_in,
        out_specs=[pl.BlockSpec((None, CHUNK, width), lambda bi, ci: (bi, ci, 0))] + cast_out,
        out_shape=[jax.ShapeDtypeStruct((b, s, width), BF16)] + cast_shapes,
        scratch_shapes=list(scratch_shapes),
        compiler_params=pltpu.CompilerParams(dimension_semantics=semantics, vmem_limit_bytes=VMEM_LIMIT),
        name=name,
    )(*args, *casts)


def _hgrn_kernel(q_ref, f_ref, i_ref, g_ref, llb_ref, l1lb_ref, oml_ref, gain_ref,
                 mstack_ref, level_ref, o_ref, state_ref):
    c = pl.program_id(1)

    @pl.when(c == 0)
    def _():
        state_ref[...] = jnp.zeros_like(state_ref)

    q = _silu(q_ref[...])
    z = f_ref[...]
    ez = jnp.exp(-jnp.abs(z))
    log_sig = jnp.minimum(z, 0.0) - jnp.log(1.0 + ez)
    sig_neg = jnp.where(z >= 0.0, ez, 1.0) / (1.0 + ez)
    k = oml_ref[...] * sig_neg
    a = llb_ref[...]
    bb = l1lb_ref[...] + log_sig
    log_f = jnp.maximum(a, bb) + jnp.log(1.0 + jnp.exp(-jnp.abs(a - bb)))

    lf_hi = log_f.astype(BF16)
    lf_lo = (log_f - lf_hi.astype(F32)).astype(BF16)
    m = mstack_ref[...]
    sums = _dot(m, lf_hi) + _dot(m, lf_lo)
    decay = jnp.exp(sums)

    rows = lax.broadcasted_iota(jnp.int32, (CHUNK, 1), 0)
    factors = []
    for li, s in enumerate(_LEVEL_SIZES):
        is_query = ((rows // s) % 2) == 1
        factors.append((jnp.where(is_query, q, k) * decay[li * CHUNK:(li + 1) * CHUNK]).astype(BF16))
    cum = _N_LEVELS * CHUNK
    q_in = (q * decay[cum:cum + CHUNK]).astype(BF16)
    k_dec = (k * decay[cum + CHUNK:cum + 2 * CHUNK]).astype(BF16)
    end_decay = decay[cum + CHUNK - 1:cum + CHUNK]
    qb = q.astype(BF16)
    kb = k.astype(BF16)
    vb = i_ref[...].astype(BF16)
    level = level_ref[...]
    gate = _silu(g_ref[...])
    gain = gain_ref[...]

    for h in range(HG_HEADS):
        hs = slice(h * HG_DK, (h + 1) * HG_DK)
        scores = jnp.zeros((CHUNK, CHUNK), F32)
        for li in range(_N_LEVELS):
            fl = factors[li][:, hs]
            scores = jnp.where(level == li, _dot_nt(fl, fl), scores)
        scores = jnp.where(level == _N_LEVELS, _dot_nt(qb[:, hs], kb[:, hs]), scores)
        st = state_ref[h]
        o = _dot(scores.astype(BF16), vb[:, hs]) + _dot_nt(q_in[:, hs], st.astype(BF16))
        state_ref[h] = st * end_decay[:, hs] + _dot_tn(vb[:, hs], k_dec[:, hs])
        ms = jnp.mean(o * o, axis=-1, keepdims=True)
        o = o * lax.rsqrt(ms + RMS_EPS) * gain[:, hs] * gate[:, hs]
        o_ref[:, hs] = o.astype(o_ref.dtype)


def _hgrn(proj3, llb, l1lb, oml, gain, casts, layer):
    w = HG_WIDTH

    def col(j):
        return pl.BlockSpec((None, CHUNK, w), lambda bi, ci, j=j: (bi, ci, j))

    vec = pl.BlockSpec((1, w), lambda bi, ci: (0, 0))
    nm = _HGRN_MSTACK.shape[0]
    return _mixer_call(
        _hgrn_kernel, "hgrn2", proj3,
        [col(0), col(1), col(2), col(3), vec, vec, vec, vec,
         pl.BlockSpec((nm, CHUNK), lambda bi, ci: (0, 0)),
         pl.BlockSpec((CHUNK, CHUNK), lambda bi, ci: (0, 0))],
        (proj3, proj3, proj3, proj3, llb.reshape(1, w), l1lb.reshape(1, w), oml.reshape(1, w),
         gain.reshape(1, w), jnp.asarray(_HGRN_MSTACK, BF16), jnp.asarray(_HGRN_LEVEL)),
        w, ("parallel", "arbitrary"), casts, layer,
        scratch_shapes=[pltpu.VMEM((HG_HEADS, HG_DK, HG_DK), F32)])


def _swa_kernel(q_ref, kp_ref, kc_ref, vp_ref, vc_ref, sink_ref, gain_ref, o_ref):
    n = pl.program_id(1)
    g = SW_Q_HEADS // SW_KV_HEADS
    hd = SW_HEAD_DIM
    assert (hd ** -0.5) == 2.0 ** round(np.log2(hd ** -0.5))
    q = (q_ref[...] * (hd ** -0.5)).astype(BF16)
    kband = jnp.concatenate([kp_ref[...], kc_ref[...]], axis=0).astype(BF16)
    vband = jnp.concatenate([vp_ref[...], vc_ref[...]], axis=0).astype(BF16)
    t = lax.broadcasted_iota(jnp.int32, (CHUNK, 2 * CHUNK), 0)
    s = lax.broadcasted_iota(jnp.int32, (CHUNK, 2 * CHUNK), 1)
    rel = t + CHUNK - s
    mask = (rel >= 0) & (rel < CHUNK) & ((s >= CHUNK) | (n > 0))
    bias = jnp.where(mask, 0.0, -jnp.inf)
    kvs = [(kband[:, kv * hd:(kv + 1) * hd], vband[:, kv * hd:(kv + 1) * hd]) for kv in range(SW_KV_HEADS)]
    outs = []
    for h in range(SW_Q_HEADS):
        k_h, v_h = kvs[h // g]
        sc = _dot_nt(q[:, h * hd:(h + 1) * hd], k_h) + bias
        sink = sink_ref[h]
        mx = jnp.maximum(jnp.max(sc, axis=-1, keepdims=True), sink)
        p = jnp.exp(sc - mx)
        denom = jnp.sum(p, axis=-1, keepdims=True) + jnp.exp(sink - mx)
        outs.append(_dot(p.astype(BF16), v_h) / denom)
    o = jnp.concatenate(outs, axis=-1)
    ms = jnp.mean(o * o, axis=-1, keepdims=True)
    o_ref[...] = (o * lax.rsqrt(ms + RMS_EPS) * gain_ref[...]).astype(o_ref.dtype)


def _swa(proj3, sinks, gain, casts, layer):
    q_col = _SWQ_OFF // SW_WIDTH
    k_col = _SWK_OFF // SW_KV_WIDTH
    v_col = k_col + 1

    def kv_spec(col, prev):
        if prev:
            return pl.BlockSpec((None, CHUNK, SW_KV_WIDTH), lambda bi, ni: (bi, jnp.maximum(ni - 1, 0), col))
        return pl.BlockSpec((None, CHUNK, SW_KV_WIDTH), lambda bi, ni: (bi, ni, col))

    return _mixer_call(
        _swa_kernel, "swa", proj3,
        [pl.BlockSpec((None, CHUNK, SW_WIDTH), lambda bi, ni: (bi, ni, q_col)),
         kv_spec(k_col, True), kv_spec(k_col, False),
         kv_spec(v_col, True), kv_spec(v_col, False),
         pl.BlockSpec(memory_space=pltpu.SMEM),
         pl.BlockSpec((1, SW_WIDTH), lambda bi, ni: (0, 0))],
        (proj3, proj3, proj3, proj3, proj3, sinks.astype(F32), gain.reshape(1, SW_WIDTH)),
        SW_WIDTH, ("parallel", "parallel"), casts, layer)


def _sgu_kernel(u_ref, v_ref, lng_ref, lnb_ref, w_ref, bs_ref, gain_ref, o_ref):
    u = _gelu(u_ref[...])
    v = _layer_norm(_gelu(v_ref[...]), lng_ref[...], lnb_ref[...]).astype(BF16)
    r = lax.broadcasted_iota(jnp.int32, (CHUNK, CHUNK), 0)
    c = lax.broadcasted_iota(jnp.int32, (CHUNK, CHUNK), 1)
    tril = c <= r
    bs = bs_ref[...]
    parts = []
    for gi in range(SG_GROUPS):
        gs = slice(gi * CHUNK, (gi + 1) * CHUNK)
        w = jnp.where(tril, w_ref[gi], 0.0).astype(BF16)
        parts.append(_dot(w, v[:, gs]) + bs[:, gi:gi + 1])
    o = u * jnp.concatenate(parts, axis=-1)
    ms = jnp.mean(o * o, axis=-1, keepdims=True)
    o_ref[...] = (o * lax.rsqrt(ms + RMS_EPS) * gain_ref[...]).astype(o_ref.dtype)


def _sgu(proj3, ln_g, ln_b, w_s, b_s, gain, casts, layer):
    w = SG_WIDTH
    u_col = _SGU_OFF // SG_WIDTH
    vec = pl.BlockSpec((1, w), lambda bi, ci: (0, 0))
    return _mixer_call(
        _sgu_kernel, "sgu", proj3,
        [pl.BlockSpec((None, CHUNK, w), lambda bi, ci: (bi, ci, u_col)),
         pl.BlockSpec((None, CHUNK, w), lambda bi, ci: (bi, ci, u_col + 1)),
         vec, vec,
         pl.BlockSpec((SG_GROUPS, CHUNK, CHUNK), lambda bi, ci: (0, 0, 0)),
         pl.BlockSpec((CHUNK, SG_GROUPS), lambda bi, ci: (0, 0)),
         vec],
        (proj3, proj3, ln_g.reshape(1, w), ln_b.reshape(1, w), w_s, b_s.T, gain.reshape(1, w)),
        w, ("parallel", "parallel"), casts, layer)


LANES = 128
SLAB = D_MODEL // 2 // LANES


def _pack_pair(low, high):
    def bits(v):
        return lax.bitcast_convert_type(v.astype(BF16).astype(F32), jnp.uint32)

    return (bits(high) & jnp.uint32(0xFFFF0000)) | (bits(low) >> 16)


def _pack_rows(x):
    half = x.shape[1] // 2
    return _pack_pair(x[:, :half], x[:, half:])


def _unpack_words(w):
    return (lax.bitcast_convert_type(w << 16, F32),
            lax.bitcast_convert_type(w & jnp.uint32(0xFFFF0000), F32))


def _store_slabs(ref, x):
    n = x.shape[0]
    for j in range(SLAB):
        ref[pl.ds(j, n, stride=SLAB), :] = x[:, j * LANES:(j + 1) * LANES]


def _load_packed_rows(ref, n, first=0, stride=SLAB):
    lows, highs = [], []
    for j in range(SLAB):
        lo, hi = _unpack_words(ref[pl.ds(first + j, n, stride=stride), :])
        lows.append(lo.astype(BF16))
        highs.append(hi.astype(BF16))
    return jnp.concatenate(lows + highs, axis=1)


def _out_proj_kernel(oa_ref, ob_ref, oc_ref, w_ref, h_ref, g_ref, b_ref, of_ref, os_ref):
    mixed = jnp.concatenate([oa_ref[...], ob_ref[...], oc_ref[...]], axis=-1)
    y = ALPHA * h_ref[...] + _dot(mixed, w_ref[...])
    out = _layer_norm(y, g_ref[...], b_ref[...])
    of_ref[...] = out
    _store_slabs(os_ref, _pack_rows(out))


def _out_proj(oa, ob, oc, w, h, g, b, tm=512):
    t, d = h.shape
    tm = min(tm, t)
    vec = pl.BlockSpec((1, d), lambda i: (0, 0))
    return pl.pallas_call(
        _out_proj_kernel,
        grid=(t // tm,),
        in_specs=[pl.BlockSpec((tm, oa.shape[1]), lambda i: (i, 0)),
                  pl.BlockSpec((tm, ob.shape[1]), lambda i: (i, 0)),
                  pl.BlockSpec((tm, oc.shape[1]), lambda i: (i, 0)),
                  pl.BlockSpec(w.shape, lambda i: (0, 0)),
                  pl.BlockSpec((tm, d), lambda i: (i, 0)), vec, vec],
        out_specs=[pl.BlockSpec((tm, d), lambda i: (i, 0)),
                   pl.BlockSpec((tm * SLAB, LANES), lambda i: (i, 0))],
        out_shape=[jax.ShapeDtypeStruct((t, d), F32), jax.ShapeDtypeStruct((t * SLAB, LANES), jnp.uint32)],
        compiler_params=pltpu.CompilerParams(vmem_limit_bytes=VMEM_LIMIT),
        name="out_proj_ln1",
    )(oa, ob, oc, w, h, g.reshape(1, d), b.reshape(1, d))


def _first_index_of_max(x, iota, size, axis):
    mx = jnp.max(x, axis=axis, keepdims=True)
    idx = jnp.min(jnp.where(x == mx, iota, size), axis=axis, keepdims=True)
    return mx, idx


def _router_kernel(tiles_per_group, h_ref, wt_ref, bias_ref, eidx_ref, gate_ref, count_ref, run_ref):
    i = pl.program_id(0)

    @pl.when(i % tiles_per_group == 0)
    def _():
        run_ref[...] = jnp.zeros_like(run_ref)

    tm = h_ref.shape[0]
    per_group = N_EXPERTS // N_EXPERT_GROUPS
    def split(v):
        hi = v.astype(BF16)
        return hi, (v - hi.astype(F32)).astype(BF16)

    w_hi, w_lo = split(wt_ref[...])
    h_hi, h_lo = split(h_ref[...])
    logits = _dot_nt(w_hi, h_hi) + (_dot_nt(w_hi, h_lo) + _dot_nt(w_lo, h_hi))
    scores = 1.0 / (1.0 + jnp.exp(-logits))
    sel = scores + bias_ref[...]
    sel3 = sel.reshape(N_EXPERT_GROUPS, per_group, tm)
    io3 = lax.broadcasted_iota(jnp.int32, sel3.shape, 1)
    m1, i1 = _first_index_of_max(sel3, io3, per_group, 1)
    m2 = jnp.max(jnp.where(io3 == i1, -jnp.inf, sel3), axis=1, keepdims=True)
    grp = (m1 + m2).reshape(N_EXPERT_GROUPS, tm)
    iog = lax.broadcasted_iota(jnp.int32, grp.shape, 0)
    keep = jnp.zeros(grp.shape, jnp.bool_)
    for _ in range(TOPK_GROUPS):
        _, gi = _first_index_of_max(grp, iog, N_EXPERT_GROUPS, 0)
        hit = iog == gi
        keep = keep | hit
        grp = jnp.where(hit, -jnp.inf, grp)
    keep3 = jnp.broadcast_to(keep.reshape(N_EXPERT_GROUPS, 1, tm), sel3.shape)
    cand = jnp.where(keep3, sel3, -jnp.inf).reshape(N_EXPERTS, tm)
    ioe = lax.broadcasted_iota(jnp.int32, cand.shape, 0)
    chosen = jnp.zeros(cand.shape, F32)
    idxs, gvals = [], []
    for _ in range(TOP_K):
        _, ei = _first_index_of_max(cand, ioe, N_EXPERTS, 0)
        hit = ioe == ei
        idxs.append(ei)
        gvals.append(jnp.sum(jnp.where(hit, scores, 0.0), axis=0, keepdims=True))
        chosen = jnp.where(hit, 1.0, chosen)
        cand = jnp.where(hit, -jnp.inf, cand)
    gsum = functools.reduce(lambda a, b: a + b, gvals)
    for kk in range(TOP_K):
        eidx_ref[kk:kk + 1, :] = idxs[kk]
        gate_ref[kk:kk + 1, :] = gvals[kk] / gsum * ROUTED_SCALE
    run = run_ref[...] + jnp.sum(chosen, axis=1, keepdims=True)
    run_ref[...] = run
    count_ref[...] = run.astype(jnp.int32)


def _router(h, router_w, router_bias, n_groups, tm=ROUTER_TILE):
    t, d = h.shape
    tm = min(tm, t // n_groups)
    tiles_per_group = t // n_groups // tm
    slot = pl.BlockSpec((TOP_K, tm), lambda i: (0, i))
    return pl.pallas_call(
        functools.partial(_router_kernel, tiles_per_group),
        grid=(t // tm,),
        in_specs=[pl.BlockSpec((tm, d), lambda i: (i, 0)),
                  pl.BlockSpec((N_EXPERTS, d), lambda i: (0, 0)),
                  pl.BlockSpec((N_EXPERTS, 1), lambda i: (0, 0))],
        out_specs=[slot, slot, pl.BlockSpec((None, N_EXPERTS, 1), lambda i: (i // tiles_per_group, 0, 0))],
        out_shape=[jax.ShapeDtypeStruct((TOP_K, t), jnp.int32),
                   jax.ShapeDtypeStruct((TOP_K, t), F32),
                   jax.ShapeDtypeStruct((n_groups, N_EXPERTS, 1), jnp.int32)],
        scratch_shapes=[pltpu.VMEM((N_EXPERTS, 1), F32)],
        compiler_params=pltpu.CompilerParams(dimension_semantics=("arbitrary",),
                                             vmem_limit_bytes=VMEM_LIMIT),
        name="router",
    )(h, router_w.T, router_bias.reshape(N_EXPERTS, 1))


_ISSUE_UNROLL = 8
_TOP_K_BITS = TOP_K.bit_length() - 1
assert 1 << _TOP_K_BITS == TOP_K


_OUT_BUFFERS = 3


def _experts_kernel(be_ref, bg_ref, wplan_ref, nu_ref, src_ref, dstp_ref, dstc_ref, h_hbm,
                    wg_hbm, wu_hbm, wd_hbm, y_hbm, tok_ref, xbuf, obuf0, obuf1, obuf2, wg_buf, wu_buf, wd_buf,
                    lsem, ssem, wsem):
    b = pl.program_id(0)
    nu = nu_ref[0]
    obuf = (obuf0, obuf1, obuf2)
    rows = xbuf.shape[0] // SLAB
    d = wg_buf.shape[1]
    f = wg_buf.shape[2]
    wslot = wplan_ref[1, b]

    def weight_copies(expert, slot_):
        return [pltpu.make_async_copy(src.at[expert], dst.at[slot_], wsem.at[slot_])
                for src, dst in ((wg_hbm, wg_buf), (wu_hbm, wu_buf), (wd_hbm, wd_buf))]

    @pl.when((b < nu) & (wplan_ref[0, b] == 1))
    def _():
        @pl.when(b == 0)
        def _():
            for c in weight_copies(be_ref[0], wslot):
                c.start()
        for c in weight_copies(be_ref[b], wslot):
            c.wait()
        nxt = wplan_ref[2, b]

        @pl.when(nxt >= 0)
        def _():
            for c in weight_copies(nxt, 1 - wslot):
                c.start()

    wg_ref, wu_ref, wd_ref = wg_buf.at[wslot], wu_buf.at[wslot], wd_buf.at[wslot]
    pad0 = y_hbm.shape[0] - 2 * rows * SLAB

    def scatter_row(dst_ref, r, src_slot, priority):
        dst = pl.multiple_of(dst_ref[0, 0, r], SLAB)
        buf_rows = pl.ds(pl.multiple_of(r * SLAB, SLAB), SLAB)
        pltpu.make_async_copy(obuf[src_slot].at[buf_rows], y_hbm.at[pl.ds(dst, SLAB)],
                              ssem.at[src_slot]).start(priority=priority)

    def wait_scatter(s):
        pltpu.make_async_copy(obuf[s], y_hbm.at[pl.ds(0, rows * SLAB)], ssem.at[s]).wait()

    @pl.when(b == 0)
    def _():
        for s in range(2):
            obuf[s][...] = jnp.zeros_like(obuf[s])
            pltpu.make_async_copy(obuf[s], y_hbm.at[pl.ds(pad0 + s * rows * SLAB, rows * SLAB)],
                                  ssem.at[s]).start()
        for s in range(2):
            wait_scatter(s)

    group = bg_ref[b]

    @pl.when((b < nu) & ((b == 0) | (group != bg_ref[jnp.maximum(b - 1, 0)])))
    def _():
        load = pltpu.make_async_copy(h_hbm.at[group], tok_ref, lsem.at[0])
        load.start()
        load.wait()

    def block(slot, with_scatter):
        other = (slot - 1) % _OUT_BUFFERS

        def gather(i, carry):
            for u in range(_ISSUE_UNROLL):
                r = i * _ISSUE_UNROLL + u
                src = pl.multiple_of(src_ref[0, 0, r], SLAB)
                xbuf[pl.ds(pl.multiple_of(r * SLAB, SLAB), SLAB), :] = tok_ref[pl.ds(src, SLAB), :]
            return carry
        lax.fori_loop(0, rows // _ISSUE_UNROLL, gather, 0)

        if with_scatter:
            for r in range(rows):
                scatter_row(dstp_ref, r, other, r % 2)

        x = _load_packed_rows(xbuf, rows)
        acts = []
        fh = f // 2
        for j in range(2):
            gate = _dot(x, wg_ref[:, j * fh:(j + 1) * fh])
            up = _dot(x, wu_ref[:, j * fh:(j + 1) * fh])
            acts.append((_silu(gate) * up).astype(BF16))
        act = jnp.concatenate(acts, axis=1)
        half = d // 2
        cw = 2 * LANES
        for q in range(half // cw):
            words = _pack_pair(_dot(act, wd_ref[:, q * cw:(q + 1) * cw]),
                               _dot(act, wd_ref[:, half + q * cw:half + (q + 1) * cw]))
            for jj in range(cw // LANES):
                obuf[slot][pl.ds(q * (cw // LANES) + jj, rows, stride=SLAB), :] = words[:, jj * LANES:(jj + 1) * LANES]

    def tail(slot):
        for back in (1, 2):
            @pl.when(b >= back)
            def _(back=back):
                wait_scatter((slot - back) % _OUT_BUFFERS)

        def body(r, carry):
            scatter_row(dstc_ref, r, slot, 0)
            return carry
        lax.fori_loop(0, rows, body, 0)
        wait_scatter(slot)

    @pl.when(b == 0)
    def _():
        block(0, False)

    for s in range(_OUT_BUFFERS):
        @pl.when((b >= 1) & (b < nu) & (b % _OUT_BUFFERS == s))
        def _(s=s):
            @pl.when(b >= _OUT_BUFFERS)
            def _():
                wait_scatter(s)
            block(s, True)

        @pl.when((b == nu - 1) & (b % _OUT_BUFFERS == s))
        def _(s=s):
            tail(s)


def _experts(h_groups, row_src, row_dst, block_e, block_g, weight_plan, n_used, wg, wu, wd, n_slots):
    nb, _, rows = row_src.shape
    d, f = wg.shape[-2], wg.shape[-1]
    group_rows = h_groups.shape[1]

    def smem(shift):
        def index(b, be, bg, wplan, nu):
            return (jnp.maximum(jnp.minimum(b + shift, nu[0] - 1), 0), 0, 0)
        return pl.BlockSpec((1, 1, rows), index, memory_space=pltpu.SMEM)

    hbm = pl.BlockSpec(memory_space=pl.ANY)
    grid_spec = pltpu.PrefetchScalarGridSpec(
        num_scalar_prefetch=4,
        grid=(nb,),
        in_specs=[smem(0), smem(-1), smem(0), hbm, hbm, hbm, hbm],
        out_specs=hbm,
        scratch_shapes=[pltpu.VMEM((group_rows, LANES), jnp.uint32)]
        + [pltpu.VMEM((rows * SLAB, LANES), jnp.uint32)] * (1 + _OUT_BUFFERS)
        + [pltpu.VMEM((2, d, f), BF16), pltpu.VMEM((2, d, f), BF16), pltpu.VMEM((2, f, d), BF16)]
        + [pltpu.SemaphoreType.DMA((1,)), pltpu.SemaphoreType.DMA((_OUT_BUFFERS,)),
           pltpu.SemaphoreType.DMA((2,))],
    )
    return pl.pallas_call(
        _experts_kernel,
        grid_spec=grid_spec,
        out_shape=jax.ShapeDtypeStruct(((n_slots + 2 * rows) * SLAB, LANES), jnp.uint32),
        compiler_params=pltpu.CompilerParams(dimension_semantics=("arbitrary",),
                                             vmem_limit_bytes=VMEM_LIMIT),
        name="experts",
    )(block_e, block_g, weight_plan, n_used, row_src, row_dst, row_dst, h_groups, wg, wu, wd)


_COMBINE_SUB = 32


def _combine_kernel(h_ref, gate_ref, *refs):
    y_refs = refs[:TOP_K]
    sg_ref, su_ref, sd_ref, g_ref, b_ref, of_ref, ob_ref, moe_ref = refs[TOP_K:]
    h = h_ref[...]
    tm = h.shape[0]
    half = SLAB * LANES

    def weighted_sum(s, carry):
        row0 = pl.multiple_of(s * _COMBINE_SUB, _COMBINE_SUB)
        rs = pl.ds(row0, _COMBINE_SUB)
        gates = gate_ref[rs, :]
        for j in range(SLAB):
            lo_acc = hi_acc = None
            for kk in range(TOP_K):
                lo, hi = _unpack_words(y_refs[kk][pl.ds(row0 * SLAB + j, _COMBINE_SUB, stride=SLAB), :])
                gk = gates[:, kk:kk + 1]
                lo_acc = gk * lo if kk == 0 else lo_acc + gk * lo
                hi_acc = gk * hi if kk == 0 else hi_acc + gk * hi
            moe_ref[rs, j * LANES:(j + 1) * LANES] = lo_acc
            moe_ref[rs, half + j * LANES:half + (j + 1) * LANES] = hi_acc
        return carry
    lax.fori_loop(0, tm // _COMBINE_SUB, weighted_sum, 0)
    hb = h.astype(BF16)
    act = _silu(_dot(hb, sg_ref[...])) * _dot(hb, su_ref[...])
    y = ALPHA * h + _dot(act.astype(BF16), sd_ref[...]) + moe_ref[...]
    out = _layer_norm(y, g_ref[...], b_ref[...])
    of_ref[...] = out
    ob_ref[...] = out.astype(BF16)


def _combine(h, gates_t, y_slots, sg, su, sd, g, b, tm=COMBINE_TILE):
    t, d = h.shape
    n = t // tm
    f = sg.shape[1]
    vec = pl.BlockSpec((1, d), lambda i: (0, 0))
    row = pl.BlockSpec((tm, d), lambda i: (i, 0))
    return pl.pallas_call(
        _combine_kernel,
        grid=(n,),
        in_specs=[row, pl.BlockSpec((tm, TOP_K), lambda i: (i, 0))]
        + [pl.BlockSpec((tm * SLAB, LANES), lambda i, kk=kk: (kk * n + i, 0)) for kk in range(TOP_K)]
        + [pl.BlockSpec((d, f), lambda i: (0, 0)),
            pl.BlockSpec((d, f), lambda i: (0, 0)),
            pl.BlockSpec((f, d), lambda i: (0, 0)),
            vec, vec],
        out_specs=[row, row],
        out_shape=[jax.ShapeDtypeStruct((t, d), F32), jax.ShapeDtypeStruct((t, d), BF16)],
        scratch_shapes=[pltpu.VMEM((tm, d), F32)],
        compiler_params=pltpu.CompilerParams(dimension_semantics=("parallel",),
                                             vmem_limit_bytes=VMEM_LIMIT),
        name="combine_ln2",
    )(h, gates_t, *([y_slots] * TOP_K), sg, su, sd, g.reshape(1, d), b.reshape(1, d))


def _moe(h, h_slab, router_w, router_bias, wg, wu, wd, sg, su, sd, ln_g, ln_b):
    t, d = h.shape
    tk = t * TOP_K
    n_groups = TOKEN_GROUPS if t % (TOKEN_GROUPS * ROUTER_TILE) == 0 else 1
    tg = t // n_groups
    n_buckets = n_groups * N_EXPERTS
    n_blocks = tk // MOE_ROWS + n_buckets
    eidx, gates, counts = _router(h, router_w, router_bias, n_groups)
    counts = counts.reshape(n_buckets)
    n_rows = n_blocks * MOE_ROWS
    padded = (counts + MOE_ROWS - 1) // MOE_ROWS * MOE_ROWS
    pend = jnp.cumsum(padded)
    pad_end = jnp.cumsum(padded - counts)
    pad_key = jnp.sum(pad_end[None, :] <= jnp.arange(n_rows - tk, dtype=jnp.int32)[:, None], axis=1)
    slot_group = (jnp.arange(tk, dtype=jnp.int32) >> _TOP_K_BITS) // tg
    keys = jnp.concatenate([slot_group * N_EXPERTS + eidx.T.reshape(tk), pad_key.astype(jnp.int32)])
    index_bits = (n_rows - 1).bit_length()
    assert (n_buckets + 1) << index_bits < 2 ** 31
    packed = lax.sort((keys << index_bits) | jnp.arange(n_rows, dtype=jnp.int32))
    row_slot = packed & ((1 << index_bits) - 1)
    real = row_slot < tk
    row = jnp.arange(n_rows, dtype=jnp.int32)
    row_src = jnp.where(real, ((row_slot >> _TOP_K_BITS) % tg) * SLAB, 0)
    plane_row = (row_slot & (TOP_K - 1)) * t + (row_slot >> _TOP_K_BITS)
    row_dst = jnp.where(real, plane_row, tk + row % (2 * MOE_ROWS)) * SLAB
    block_start = jnp.arange(n_blocks, dtype=jnp.int32) * MOE_ROWS
    block_bucket = jnp.minimum(jnp.sum(pend[None, :] <= block_start[:, None], axis=1), n_buckets - 1).astype(jnp.int32)
    n_used = (pend[-1] // MOE_ROWS).astype(jnp.int32)
    block_e = block_bucket % N_EXPERTS
    block = jnp.arange(n_blocks, dtype=jnp.int32)
    first = (block == 0) | (block_bucket != jnp.roll(block_bucket, 1))
    seq = jnp.cumsum(first.astype(jnp.int32)) - 1
    later_first = jnp.where(first & (block < n_used), block, n_blocks)
    next_first = jnp.concatenate([lax.cummin(later_first, reverse=True)[1:], jnp.full((1,), n_blocks, jnp.int32)])
    next_e = jnp.where(next_first < n_blocks, block_e[jnp.minimum(next_first, n_blocks - 1)], -1)
    weight_plan = jnp.stack([first.astype(jnp.int32), seq % 2, next_e]).astype(jnp.int32)
    y_slots = _experts(h_slab.reshape(n_groups, tg * SLAB, LANES),
                       row_src.reshape(n_blocks, 1, MOE_ROWS), row_dst.reshape(n_blocks, 1, MOE_ROWS),
                       block_e, block_bucket // N_EXPERTS, weight_plan, n_used.reshape(1),
                       wg, wu, wd, tk)
    return _combine(h, gates.T, y_slots, sg, su, sd, ln_g, ln_b, tm=min(COMBINE_TILE, t))


def kernel(x, ln0_gain, ln0_bias, w_in, b_in, hg_lb_logits, sw_sinks, sg_ln_gain, sg_ln_bias, sg_w_s, sg_b_s, mix_gain, w_out, ln1_gain, ln1_bias, router_w, router_bias, exp_w_gate, exp_w_up, exp_w_down, sh_w_gate, sh_w_up, sh_w_down, ln2_gain, ln2_bias):
    b, s, d = x.shape
    t = b * s
    p = jax.nn.softmax(hg_lb_logits.astype(F32), axis=0)
    cs = jnp.cumsum(p, axis=0)
    lbs = cs - cs[0:1]
    a0, a1 = HG_WIDTH, HG_WIDTH + SW_WIDTH

    n_e, d_e, f_e = exp_w_gate.shape[1:]
    wg_rows, wu_rows = exp_w_gate.reshape(-1, f_e), exp_w_up.reshape(-1, f_e)
    wd_rows = exp_w_down.reshape(-1, d_e)
    h, hb = _ln(x.reshape(t, d), ln0_gain, ln0_bias)
    for l in range(DEPTH):
        proj = _in_proj(hb, _regroup_columns(w_in[l]).astype(BF16), _regroup_columns(b_in[l]))
        proj3 = proj.reshape(b, s, D_IN)
        lb = lbs[l]
        o_a, wg, wd = _hgrn(proj3, jnp.log(lb), jnp.log1p(-lb), 1.0 - lb, mix_gain[l, :a0], [wg_rows, wd_rows], l)
        o_b, wu = _swa(proj3, sw_sinks[l], mix_gain[l, a0:a1], [wu_rows], l)
        o_c, = _sgu(proj3, sg_ln_gain[l], sg_ln_bias[l], sg_w_s[l], sg_b_s[l], mix_gain[l, a1:], [], l)
        h, h_slab = _out_proj(o_a.reshape(t, -1), o_b.reshape(t, -1), o_c.reshape(t, -1),
                              w_out[l].astype(BF16), h, ln1_gain[l], ln1_bias[l])
        h, hb = _moe(h, h_slab, router_w[l], router_bias[l], wg.reshape(n_e, d_e, f_e),
                     wu.reshape(n_e, d_e, f_e), wd.reshape(n_e, f_e, d_e),
                     sh_w_gate[l].astype(BF16), sh_w_up[l].astype(BF16), sh_w_down[l].astype(BF16),
                     ln2_gain[l], ln2_bias[l])
    return h.reshape(b, s, d)
```

```python
import functools

import numpy as np
import jax
import jax.numpy as jnp
from jax import lax
from jax.experimental import pallas as pl
from jax.experimental.pallas import tpu as pltpu

F32 = jnp.float32
BF16 = jnp.bfloat16

D_MODEL = 2048
DEPTH = 2
HG_HEADS = 4
HG_DK = 128
HG_WIDTH = HG_HEADS * HG_DK
CHUNK = 128
SW_Q_HEADS = 16
SW_KV_HEADS = 2
SW_HEAD_DIM = 64
SW_WIDTH = SW_Q_HEADS * SW_HEAD_DIM
SW_KV_WIDTH = SW_KV_HEADS * SW_HEAD_DIM
SG_GROUPS = 4
SG_WIDTH = SG_GROUPS * CHUNK
D_IN = 4 * HG_WIDTH + SW_WIDTH + 2 * SW_KV_WIDTH + 2 * SG_WIDTH
N_EXPERTS = 64
TOP_K = 8
D_EXPERT = 512
N_EXPERT_GROUPS = 8
TOPK_GROUPS = 4
ROUTED_SCALE = 2.5
ALPHA = (2 * DEPTH) ** 0.25
LN_EPS = 1e-5
RMS_EPS = 1e-6

_SWQ_OFF = 4 * HG_WIDTH
_SGU_OFF = _SWQ_OFF + SW_WIDTH
_SWK_OFF = _SGU_OFF + 2 * SG_WIDTH


def _regroup_columns(a):
    k0 = _SWQ_OFF + SW_WIDTH
    k1 = k0 + 2 * SW_KV_WIDTH
    return jnp.concatenate([a[..., :k0], a[..., k1:], a[..., k0:k1]], axis=-1)


MOE_ROWS = 256
TOKEN_GROUPS = 2
ROUTER_TILE = 512
COMBINE_TILE = 256
VMEM_LIMIT = 56 * 1024 * 1024

_LEVEL_SIZES = (64, 32, 16, 8, 4, 2, 1)
_N_LEVELS = len(_LEVEL_SIZES)


def _hgrn_constants():
    t = np.arange(CHUNK)[:, None]
    u = np.arange(CHUNK)[None, :]
    mats = []
    level = np.full((CHUNK, CHUNK), -1, np.int32)
    for li, s in enumerate(_LEVEL_SIZES):
        blk = t // s
        odd = (blk % 2) == 1
        m_query = (u >= blk * s) & (u <= t)
        m_key = (u > t) & (u <= (blk + 1) * s - 1)
        mats.append(np.where(odd, m_query, m_key))
        pair = ((t // s) % 2 == 1) & ((u // s) == (t // s) - 1)
        level[pair] = li
    mats.append(u <= t)
    mats.append(u > t)
    level[np.arange(CHUNK), np.arange(CHUNK)] = _N_LEVELS
    return np.concatenate(mats, 0).astype(np.float32), level


_HGRN_MSTACK, _HGRN_LEVEL = _hgrn_constants()


def _layer_norm(x, g, b):
    mu = jnp.mean(x, axis=-1, keepdims=True)
    xc = x - mu
    var = jnp.mean(xc * xc, axis=-1, keepdims=True)
    return xc * lax.rsqrt(var + LN_EPS) * g + b


def _silu(x):
    return x / (1.0 + jnp.exp(-x))


def _gelu(x):
    return 0.5 * x * (1.0 + lax.erf(x * (2.0 ** -0.5)))


def _dot(a, b):
    return jnp.dot(a, b, preferred_element_type=F32)


def _dot_nt(a, b):
    return lax.dot_general(a, b, (((1,), (1,)), ((), ())), preferred_element_type=F32)


def _dot_tn(a, b):
    return lax.dot_general(a, b, (((0,), (0,)), ((), ())), preferred_element_type=F32)


def _ln_kernel(x_ref, g_ref, b_ref, of_ref, ob_ref):
    y = _layer_norm(x_ref[...], g_ref[...], b_ref[...])
    of_ref[...] = y
    ob_ref[...] = y.astype(BF16)


def _ln(x, g, b, tm=512):
    t, d = x.shape
    tm = min(tm, t)
    row = pl.BlockSpec((tm, d), lambda i: (i, 0))
    vec = pl.BlockSpec((1, d), lambda i: (0, 0))
    return pl.pallas_call(
        _ln_kernel,
        grid=(t // tm,),
        in_specs=[row, vec, vec],
        out_specs=[row, row],
        out_shape=[jax.ShapeDtypeStruct((t, d), F32), jax.ShapeDtypeStruct((t, d), BF16)],
        name="ln0",
    )(x, g.reshape(1, d), b.reshape(1, d))


def _mm_bias_kernel(a_ref, w_ref, b_ref, c_ref, o_ref, co_ref):
    o_ref[...] = _dot(a_ref[...], w_ref[...]) + b_ref[...]
    co_ref[...] = c_ref[...].astype(BF16)


def _in_proj(a, w, b, cast, layer, tm=512):
    t, k = a.shape
    n = w.shape[1]
    tm = min(tm, t)
    tn = n // 2
    n_i = t // tm
    steps = (n // tn) * n_i
    rows, cols = cast.shape[0] // DEPTH, cast.shape[1]
    per_step = rows // steps
    assert per_step * steps == rows and per_step % 16 == 0
    return pl.pallas_call(
        _mm_bias_kernel,
        grid=(n // tn, n_i),
        in_specs=[pl.BlockSpec((tm, k), lambda j, i: (i, 0)),
                  pl.BlockSpec((k, tn), lambda j, i: (0, j)),
                  pl.BlockSpec((1, tn), lambda j, i: (0, j)),
                  pl.BlockSpec((per_step, cols), lambda j, i: (layer * steps + j * n_i + i, 0))],
        out_specs=[pl.BlockSpec((tm, tn), lambda j, i: (i, j)),
                   pl.BlockSpec((per_step, cols), lambda j, i: (j * n_i + i, 0))],
        out_shape=[jax.ShapeDtypeStruct((t, n), F32), jax.ShapeDtypeStruct((rows, cols), BF16)],
        compiler_params=pltpu.CompilerParams(vmem_limit_bytes=VMEM_LIMIT),
        name="in_proj",
    )(a, w, b.reshape(1, n), cast)


def _mixer_call(kernel_fn, name, proj3, in_specs, args, width, semantics, casts, layer, scratch_shapes=()):
    b, s, _ = proj3.shape
    nc = s // CHUNK
    n_in, n_cast = len(in_specs), len(casts)
    cast_in, cast_out, cast_shapes = [], [], []
    for w_all in casts:
        cols = w_all.shape[1]
        rows = w_all.shape[0] // DEPTH
        per_step = rows // (b * nc)
        assert per_step * b * nc == rows and per_step % 16 == 0
        cast_in.append(pl.BlockSpec((per_step, cols), lambda bi, ci: (layer * (b * nc) + bi * nc + ci, 0)))
        cast_out.append(pl.BlockSpec((per_step, cols), lambda bi, ci: (bi * nc + ci, 0)))
        cast_shapes.append(jax.ShapeDtypeStruct((rows, cols), BF16))

    def body(*refs):
        w_in = refs[n_in:n_in + n_cast]
        out = refs[n_in + n_cast]
        w_out = refs[n_in + n_cast + 1:n_in + 2 * n_cast + 1]
        kernel_fn(*refs[:n_in], out, *refs[n_in + 2 * n_cast + 1:])
        for src, dst in zip(w_in, w_out):
            dst[...] = src[...].astype(BF16)

    return pl.pallas_call(
        body,
        grid=(b, nc),
        in_specs=list(in_specs) + cast_in,
        out_specs=[pl.BlockSpec((None, CHUNK, width), lambda bi, ci: (bi, ci, 0))] + cast_out,
        out_shape=[jax.ShapeDtypeStruct((b, s, width), BF16)] + cast_shapes,
        scratch_shapes=list(scratch_shapes),
        compiler_params=pltpu.CompilerParams(dimension_semantics=semantics, vmem_limit_bytes=VMEM_LIMIT),
        name=name,
    )(*args, *casts)


def _hgrn_kernel(q_ref, f_ref, i_ref, g_ref, llb_ref, l1lb_ref, oml_ref, gain_ref,
                 mstack_ref, level_ref, o_ref, state_ref):
    c = pl.program_id(1)

    @pl.when(c == 0)
    def _():
        state_ref[...] = jnp.zeros_like(state_ref)

    q = _silu(q_ref[...])
    z = f_ref[...]
    ez = jnp.exp(-jnp.abs(z))
    log_sig = jnp.minimum(z, 0.0) - jnp.log(1.0 + ez)
    sig_neg = jnp.where(z >= 0.0, ez, 1.0) / (1.0 + ez)
    k = oml_ref[...] * sig_neg
    a = llb_ref[...]
    bb = l1lb_ref[...] + log_sig
    log_f = jnp.maximum(a, bb) + jnp.log(1.0 + jnp.exp(-jnp.abs(a - bb)))

    lf_hi = log_f.astype(BF16)
    lf_lo = (log_f - lf_hi.astype(F32)).astype(BF16)
    m = mstack_ref[...]
    sums = _dot(m, lf_hi) + _dot(m, lf_lo)
    decay = jnp.exp(sums)

    rows = lax.broadcasted_iota(jnp.int32, (CHUNK, 1), 0)
    factors = []
    for li, s in enumerate(_LEVEL_SIZES):
        is_query = ((rows // s) % 2) == 1
        factors.append((jnp.where(is_query, q, k) * decay[li * CHUNK:(li + 1) * CHUNK]).astype(BF16))
    cum = _N_LEVELS * CHUNK
    q_in = (q * decay[cum:cum + CHUNK]).astype(BF16)
    k_dec = (k * decay[cum + CHUNK:cum + 2 * CHUNK]).astype(BF16)
    end_decay = decay[cum + CHUNK - 1:cum + CHUNK]
    qb = q.astype(BF16)
    kb = k.astype(BF16)
    vb = i_ref[...].astype(BF16)
    level = level_ref[...]
    gate = _silu(g_ref[...])
    gain = gain_ref[...]

    for h in range(HG_HEADS):
        hs = slice(h * HG_DK, (h + 1) * HG_DK)
        scores = jnp.zeros((CHUNK, CHUNK), F32)
        for li in range(_N_LEVELS):
            fl = factors[li][:, hs]
            scores = jnp.where(level == li, _dot_nt(fl, fl), scores)
        scores = jnp.where(level == _N_LEVELS, _dot_nt(qb[:, hs], kb[:, hs]), scores)
        st = state_ref[h]
        o = _dot(scores.astype(BF16), vb[:, hs]) + _dot_nt(q_in[:, hs], st.astype(BF16))
        state_ref[h] = st * end_decay[:, hs] + _dot_tn(vb[:, hs], k_dec[:, hs])
        ms = jnp.mean(o * o, axis=-1, keepdims=True)
        o = o * lax.rsqrt(ms + RMS_EPS) * gain[:, hs] * gate[:, hs]
        o_ref[:, hs] = o.astype(o_ref.dtype)


def _hgrn(proj3, llb, l1lb, oml, gain, casts, layer):
    w = HG_WIDTH

    def col(j):
        return pl.BlockSpec((None, CHUNK, w), lambda bi, ci, j=j: (bi, ci, j))

    vec = pl.BlockSpec((1, w), lambda bi, ci: (0, 0))
    nm = _HGRN_MSTACK.shape[0]
    return _mixer_call(
        _hgrn_kernel, "hgrn2", proj3,
        [col(0), col(1), col(2), col(3), vec, vec, vec, vec,
         pl.BlockSpec((nm, CHUNK), lambda bi, ci: (0, 0)),
         pl.BlockSpec((CHUNK, CHUNK), lambda bi, ci: (0, 0))],
        (proj3, proj3, proj3, proj3, llb.reshape(1, w), l1lb.reshape(1, w), oml.reshape(1, w),
         gain.reshape(1, w), jnp.asarray(_HGRN_MSTACK, BF16), jnp.asarray(_HGRN_LEVEL)),
        w, ("parallel", "arbitrary"), casts, layer,
        scratch_shapes=[pltpu.VMEM((HG_HEADS, HG_DK, HG_DK), F32)])


def _swa_kernel(q_ref, kp_ref, kc_ref, vp_ref, vc_ref, sink_ref, gain_ref, o_ref):
    n = pl.program_id(1)
    g = SW_Q_HEADS // SW_KV_HEADS
    hd = SW_HEAD_DIM
    assert (hd ** -0.5) == 2.0 ** round(np.log2(hd ** -0.5))
    q = (q_ref[...] * (hd ** -0.5)).astype(BF16)
    kband = jnp.concatenate([kp_ref[...], kc_ref[...]], axis=0).astype(BF16)
    vband = jnp.concatenate([vp_ref[...], vc_ref[...]], axis=0).astype(BF16)
    t = lax.broadcasted_iota(jnp.int32, (CHUNK, 2 * CHUNK), 0)
    s = lax.broadcasted_iota(jnp.int32, (CHUNK, 2 * CHUNK), 1)
    rel = t + CHUNK - s
    mask = (rel >= 0) & (rel < CHUNK) & ((s >= CHUNK) | (n > 0))
    bias = jnp.where(mask, 0.0, -jnp.inf)
    kvs = [(kband[:, kv * hd:(kv + 1) * hd], vband[:, kv * hd:(kv + 1) * hd]) for kv in range(SW_KV_HEADS)]
    outs = []
    for h in range(SW_Q_HEADS):
        k_h, v_h = kvs[h // g]
        sc = _dot_nt(q[:, h * hd:(h + 1) * hd], k_h) + bias
        sink = sink_ref[h]
        mx = jnp.maximum(jnp.max(sc, axis=-1, keepdims=True), sink)
        p = jnp.exp(sc - mx)
        denom = jnp.sum(p, axis=-1, keepdims=True) + jnp.exp(sink - mx)
        outs.append(_dot(p.astype(BF16), v_h) / denom)
    o = jnp.concatenate(outs, axis=-1)
    ms = jnp.mean(o * o, axis=-1, keepdims=True)
    o_ref[...] = (o * lax.rsqrt(ms + RMS_EPS) * gain_ref[...]).astype(o_ref.dtype)


def _swa(proj3, sinks, gain, casts, layer):
    q_col = _SWQ_OFF // SW_WIDTH
    k_col = _SWK_OFF // SW_KV_WIDTH
    v_col = k_col + 1

    def kv_spec(col, prev):
        if prev:
            return pl.BlockSpec((None, CHUNK, SW_KV_WIDTH), lambda bi, ni: (bi, jnp.maximum(ni - 1, 0), col))
        return pl.BlockSpec((None, CHUNK, SW_KV_WIDTH), lambda bi, ni: (bi, ni, col))

    return _mixer_call(
        _swa_kernel, "swa", proj3,
        [pl.BlockSpec((None, CHUNK, SW_WIDTH), lambda bi, ni: (bi, ni, q_col)),
         kv_spec(k_col, True), kv_spec(k_col, False),
         kv_spec(v_col, True), kv_spec(v_col, False),
         pl.BlockSpec(memory_space=pltpu.SMEM),
         pl.BlockSpec((1, SW_WIDTH), lambda bi, ni: (0, 0))],
        (proj3, proj3, proj3, proj3, proj3, sinks.astype(F32), gain.reshape(1, SW_WIDTH)),
        SW_WIDTH, ("parallel", "parallel"), casts, layer)


def _sgu_kernel(u_ref, v_ref, lng_ref, lnb_ref, w_ref, bs_ref, gain_ref, o_ref):
    u = _gelu(u_ref[...])
    v = _layer_norm(_gelu(v_ref[...]), lng_ref[...], lnb_ref[...]).astype(BF16)
    r = lax.broadcasted_iota(jnp.int32, (CHUNK, CHUNK), 0)
    c = lax.broadcasted_iota(jnp.int32, (CHUNK, CHUNK), 1)
    tril = c <= r
    bs = bs_ref[...]
    parts = []
    for gi in range(SG_GROUPS):
        gs = slice(gi * CHUNK, (gi + 1) * CHUNK)
        w = jnp.where(tril, w_ref[gi], 0.0).astype(BF16)
        parts.append(_dot(w, v[:, gs]) + bs[:, gi:gi + 1])
    o = u * jnp.concatenate(parts, axis=-1)
    ms = jnp.mean(o * o, axis=-1, keepdims=True)
    o_ref[...] = (o * lax.rsqrt(ms + RMS_EPS) * gain_ref[...]).astype(o_ref.dtype)


def _sgu(proj3, ln_g, ln_b, w_s, b_s, gain, casts, layer):
    w = SG_WIDTH
    u_col = _SGU_OFF // SG_WIDTH
    vec = pl.BlockSpec((1, w), lambda bi, ci: (0, 0))
    return _mixer_call(
        _sgu_kernel, "sgu", proj3,
        [pl.BlockSpec((None, CHUNK, w), lambda bi, ci: (bi, ci, u_col)),
         pl.BlockSpec((None, CHUNK, w), lambda bi, ci: (bi, ci, u_col + 1)),
         vec, vec,
         pl.BlockSpec((SG_GROUPS, CHUNK, CHUNK), lambda bi, ci: (0, 0, 0)),
         pl.BlockSpec((CHUNK, SG_GROUPS), lambda bi, ci: (0, 0)),
         vec],
        (proj3, proj3, ln_g.reshape(1, w), ln_b.reshape(1, w), w_s, b_s.T, gain.reshape(1, w)),
        w, ("parallel", "parallel"), casts, layer)


LANES = 128
SLAB = D_MODEL // 2 // LANES


def _pack_pair(low, high):
    def bits(v):
        return lax.bitcast_convert_type(v.astype(BF16).astype(F32), jnp.uint32)

    return (bits(high) & jnp.uint32(0xFFFF0000)) | (bits(low) >> 16)


def _pack_rows(x):
    half = x.shape[1] // 2
    return _pack_pair(x[:, :half], x[:, half:])


def _unpack_words(w):
    return (lax.bitcast_convert_type(w << 16, F32),
            lax.bitcast_convert_type(w & jnp.uint32(0xFFFF0000), F32))


def _store_slabs(ref, x):
    n = x.shape[0]
    for j in range(SLAB):
        ref[pl.ds(j, n, stride=SLAB), :] = x[:, j * LANES:(j + 1) * LANES]


def _load_packed_rows(ref, n, first=0, stride=SLAB):
    lows, highs = [], []
    for j in range(SLAB):
        lo, hi = _unpack_words(ref[pl.ds(first + j, n, stride=stride), :])
        lows.append(lo.astype(BF16))
        highs.append(hi.astype(BF16))
    return jnp.concatenate(lows + highs, axis=1)


def _out_proj_kernel(oa_ref, ob_ref, oc_ref, w_ref, h_ref, g_ref, b_ref, of_ref, os_ref):
    mixed = jnp.concatenate([oa_ref[...], ob_ref[...], oc_ref[...]], axis=-1)
    y = ALPHA * h_ref[...] + _dot(mixed, w_ref[...])
    out = _layer_norm(y, g_ref[...], b_ref[...])
    of_ref[...] = out
    _store_slabs(os_ref, _pack_rows(out))


def _out_proj(oa, ob, oc, w, h, g, b, tm=512):
    t, d = h.shape
    tm = min(tm, t)
    vec = pl.BlockSpec((1, d), lambda i: (0, 0))
    return pl.pallas_call(
        _out_proj_kernel,
        grid=(t // tm,),
        in_specs=[pl.BlockSpec((tm, oa.shape[1]), lambda i: (i, 0)),
                  pl.BlockSpec((tm, ob.shape[1]), lambda i: (i, 0)),
                  pl.BlockSpec((tm, oc.shape[1]), lambda i: (i, 0)),
                  pl.BlockSpec(w.shape, lambda i: (0, 0)),
                  pl.BlockSpec((tm, d), lambda i: (i, 0)), vec, vec],
        out_specs=[pl.BlockSpec((tm, d), lambda i: (i, 0)),
                   pl.BlockSpec((tm * SLAB, LANES), lambda i: (i, 0))],
        out_shape=[jax.ShapeDtypeStruct((t, d), F32), jax.ShapeDtypeStruct((t * SLAB, LANES), jnp.uint32)],
        compiler_params=pltpu.CompilerParams(vmem_limit_bytes=VMEM_LIMIT),
        name="out_proj_ln1",
    )(oa, ob, oc, w, h, g.reshape(1, d), b.reshape(1, d))


def _first_index_of_max(x, iota, size, axis):
    mx = jnp.max(x, axis=axis, keepdims=True)
    idx = jnp.min(jnp.where(x == mx, iota, size), axis=axis, keepdims=True)
    return mx, idx


def _router_kernel(tiles_per_group, h_ref, wt_ref, bias_ref, eidx_ref, gate_ref, count_ref, run_ref):
    i = pl.program_id(0)

    @pl.when(i % tiles_per_group == 0)
    def _():
        run_ref[...] = jnp.zeros_like(run_ref)

    tm = h_ref.shape[0]
    per_group = N_EXPERTS // N_EXPERT_GROUPS
    def split(v):
        hi = v.astype(BF16)
        return hi, (v - hi.astype(F32)).astype(BF16)

    w_hi, w_lo = split(wt_ref[...])
    h_hi, h_lo = split(h_ref[...])
    logits = _dot_nt(w_hi, h_hi) + (_dot_nt(w_hi, h_lo) + _dot_nt(w_lo, h_hi))
    scores = 1.0 / (1.0 + jnp.exp(-logits))
    sel = scores + bias_ref[...]
    sel3 = sel.reshape(N_EXPERT_GROUPS, per_group, tm)
    io3 = lax.broadcasted_iota(jnp.int32, sel3.shape, 1)
    m1, i1 = _first_index_of_max(sel3, io3, per_group, 1)
    m2 = jnp.max(jnp.where(io3 == i1, -jnp.inf, sel3), axis=1, keepdims=True)
    grp = (m1 + m2).reshape(N_EXPERT_GROUPS, tm)
    iog = lax.broadcasted_iota(jnp.int32, grp.shape, 0)
    keep = jnp.zeros(grp.shape, jnp.bool_)
    for _ in range(TOPK_GROUPS):
        _, gi = _first_index_of_max(grp, iog, N_EXPERT_GROUPS, 0)
        hit = iog == gi
        keep = keep | hit
        grp = jnp.where(hit, -jnp.inf, grp)
    keep3 = jnp.broadcast_to(keep.reshape(N_EXPERT_GROUPS, 1, tm), sel3.shape)
    cand = jnp.where(keep3, sel3, -jnp.inf).reshape(N_EXPERTS, tm)
    ioe = lax.broadcasted_iota(jnp.int32, cand.shape, 0)
    chosen = jnp.zeros(cand.shape, F32)
    idxs, gvals = [], []
    for _ in range(TOP_K):
        _, ei = _first_index_of_max(cand, ioe, N_EXPERTS, 0)
        hit = ioe == ei
        idxs.append(ei)
        gvals.append(jnp.sum(jnp.where(hit, scores, 0.0), axis=0, keepdims=True))
        chosen = jnp.where(hit, 1.0, chosen)
        cand = jnp.where(hit, -jnp.inf, cand)
    gsum = functools.reduce(lambda a, b: a + b, gvals)
    for kk in range(TOP_K):
        eidx_ref[kk:kk + 1, :] = idxs[kk]
        gate_ref[kk:kk + 1, :] = gvals[kk] / gsum * ROUTED_SCALE
    run = run_ref[...] + jnp.sum(chosen, axis=1, keepdims=True)
    run_ref[...] = run
    count_ref[...] = run.astype(jnp.int32)


def _router(h, router_w, router_bias, n_groups, tm=ROUTER_TILE):
    t, d = h.shape
    tm = min(tm, t // n_groups)
    tiles_per_group = t // n_groups // tm
    slot = pl.BlockSpec((TOP_K, tm), lambda i: (0, i))
    return pl.pallas_call(
        functools.partial(_router_kernel, tiles_per_group),
        grid=(t // tm,),
        in_specs=[pl.BlockSpec((tm, d), lambda i: (i, 0)),
                  pl.BlockSpec((N_EXPERTS, d), lambda i: (0, 0)),
                  pl.BlockSpec((N_EXPERTS, 1), lambda i: (0, 0))],
        out_specs=[slot, slot, pl.BlockSpec((None, N_EXPERTS, 1), lambda i: (i // tiles_per_group, 0, 0))],
        out_shape=[jax.ShapeDtypeStruct((TOP_K, t), jnp.int32),
                   jax.ShapeDtypeStruct((TOP_K, t), F32),
                   jax.ShapeDtypeStruct((n_groups, N_EXPERTS, 1), jnp.int32)],
        scratch_shapes=[pltpu.VMEM((N_EXPERTS, 1), F32)],
        compiler_params=pltpu.CompilerParams(dimension_semantics=("arbitrary",),
                                             vmem_limit_bytes=VMEM_LIMIT),
        name="router",
    )(h, router_w.T, router_bias.reshape(N_EXPERTS, 1))


_ISSUE_UNROLL = 8
_TOP_K_BITS = TOP_K.bit_length() - 1
assert 1 << _TOP_K_BITS == TOP_K


_OUT_BUFFERS = 3


def _experts_kernel(be_ref, bg_ref, wplan_ref, nu_ref, src_ref, dstp_ref, dstc_ref, h_hbm,
                    wg_hbm, wu_hbm, wd_hbm, y_hbm, tok_ref, xbuf, obuf0, obuf1, obuf2, wg_buf, wu_buf, wd_buf,
                    lsem, ssem, wsem):
    b = pl.program_id(0)
    nu = nu_ref[0]
    obuf = (obuf0, obuf1, obuf2)
    rows = xbuf.shape[0] // SLAB
    d = wg_buf.shape[1]
    f = wg_buf.shape[2]
    wslot = wplan_ref[1, b]

    def weight_copies(expert, slot_):
        return [pltpu.make_async_copy(src.at[expert], dst.at[slot_], wsem.at[slot_])
                for src, dst in ((wg_hbm, wg_buf), (wu_hbm, wu_buf), (wd_hbm, wd_buf))]

    @pl.when((b < nu) & (wplan_ref[0, b] == 1))
    def _():
        @pl.when(b == 0)
        def _():
            for c in weight_copies(be_ref[0], wslot):
                c.start()
        for c in weight_copies(be_ref[b], wslot):
            c.wait()
        nxt = wplan_ref[2, b]

        @pl.when(nxt >= 0)
        def _():
            for c in weight_copies(nxt, 1 - wslot):
                c.start()

    wg_ref, wu_ref, wd_ref = wg_buf.at[wslot], wu_buf.at[wslot], wd_buf.at[wslot]
    pad0 = y_hbm.shape[0] - 2 * rows * SLAB

    def scatter_row(dst_ref, r, src_slot, priority):
        dst = pl.multiple_of(dst_ref[0, 0, r], SLAB)
        buf_rows = pl.ds(pl.multiple_of(r * SLAB, SLAB), SLAB)
        pltpu.make_async_copy(obuf[src_slot].at[buf_rows], y_hbm.at[pl.ds(dst, SLAB)],
                              ssem.at[src_slot]).start(priority=priority)

    def wait_scatter(s):
        pltpu.make_async_copy(obuf[s], y_hbm.at[pl.ds(0, rows * SLAB)], ssem.at[s]).wait()

    @pl.when(b == 0)
    def _():
        for s in range(2):
            obuf[s][...] = jnp.zeros_like(obuf[s])
            pltpu.make_async_copy(obuf[s], y_hbm.at[pl.ds(pad0 + s * rows * SLAB, rows * SLAB)],
                                  ssem.at[s]).start()
        for s in range(2):
            wait_scatter(s)

    group = bg_ref[b]

    @pl.when((b < nu) & ((b == 0) | (group != bg_ref[jnp.maximum(b - 1, 0)])))
    def _():
        load = pltpu.make_async_copy(h_hbm.at[group], tok_ref, lsem.at[0])
        load.start()
        load.wait()

    def block(slot, with_scatter):
        other = (slot - 1) % _OUT_BUFFERS

        def gather(i, carry):
            for u in range(_ISSUE_UNROLL):
                r = i * _ISSUE_UNROLL + u
                src = pl.multiple_of(src_ref[0, 0, r], SLAB)
                xbuf[pl.ds(pl.multiple_of(r * SLAB, SLAB), SLAB), :] = tok_ref[pl.ds(src, SLAB), :]
            return carry
        lax.fori_loop(0, rows // _ISSUE_UNROLL, gather, 0)

        if with_scatter:
            for r in range(rows):
                scatter_row(dstp_ref, r, other, r % 2)

        x = _load_packed_rows(xbuf, rows)
        acts = []
        fh = f // 2
        for j in range(2):
            gate = _dot(x, wg_ref[:, j * fh:(j + 1) * fh])
            up = _dot(x, wu_ref[:, j * fh:(j + 1) * fh])
            acts.append((_silu(gate) * up).astype(BF16))
        act = jnp.concatenate(acts, axis=1)
        half = d // 2
        cw = 2 * LANES
        for q in range(half // cw):
            words = _pack_pair(_dot(act, wd_ref[:, q * cw:(q + 1) * cw]),
                               _dot(act, wd_ref[:, half + q * cw:half + (q + 1) * cw]))
            for jj in range(cw // LANES):
                obuf[slot][pl.ds(q * (cw // LANES) + jj, rows, stride=SLAB), :] = words[:, jj * LANES:(jj + 1) * LANES]

    def tail(slot):
        for back in (1, 2):
            @pl.when(b >= back)
            def _(back=back):
                wait_scatter((slot - back) % _OUT_BUFFERS)

        def body(r, carry):
            scatter_row(dstc_ref, r, slot, 0)
            return carry
        lax.fori_loop(0, rows, body, 0)
        wait_scatter(slot)

    @pl.when(b == 0)
    def _():
        block(0, False)

    for s in range(_OUT_BUFFERS):
        @pl.when((b >= 1) & (b < nu) & (b % _OUT_BUFFERS == s))
        def _(s=s):
            @pl.when(b >= _OUT_BUFFERS)
            def _():
                wait_scatter(s)
            block(s, True)

        @pl.when((b == nu - 1) & (b % _OUT_BUFFERS == s))
        def _(s=s):
            tail(s)


def _experts(h_groups, row_src, row_dst, block_e, block_g, weight_plan, n_used, wg, wu, wd, n_slots):
    nb, _, rows = row_src.shape
    d, f = wg.shape[-2], wg.shape[-1]
    group_rows = h_groups.shape[1]

    def smem(shift):
        def index(b, be, bg, wplan, nu):
            return (jnp.maximum(jnp.minimum(b + shift, nu[0] - 1), 0), 0, 0)
        return pl.BlockSpec((1, 1, rows), index, memory_space=pltpu.SMEM)

    hbm = pl.BlockSpec(memory_space=pl.ANY)
    grid_spec = pltpu.PrefetchScalarGridSpec(
        num_scalar_prefetch=4,
        grid=(nb,),
        in_specs=[smem(0), smem(-1), smem(0), hbm, hbm, hbm, hbm],
        out_specs=hbm,
        scratch_shapes=[pltpu.VMEM((group_rows, LANES), jnp.uint32)]
        + [pltpu.VMEM((rows * SLAB, LANES), jnp.uint32)] * (1 + _OUT_BUFFERS)
        + [pltpu.VMEM((2, d, f), BF16), pltpu.VMEM((2, d, f), BF16), pltpu.VMEM((2, f, d), BF16)]
        + [pltpu.SemaphoreType.DMA((1,)), pltpu.SemaphoreType.DMA((_OUT_BUFFERS,)),
           pltpu.SemaphoreType.DMA((2,))],
    )
    return pl.pallas_call(
        _experts_kernel,
        grid_spec=grid_spec,
        out_shape=jax.ShapeDtypeStruct(((n_slots + 2 * rows) * SLAB, LANES), jnp.uint32),
        compiler_params=pltpu.CompilerParams(dimension_semantics=("arbitrary",),
                                             vmem_limit_bytes=VMEM_LIMIT),
        name="experts",
    )(block_e, block_g, weight_plan, n_used, row_src, row_dst, row_dst, h_groups, wg, wu, wd)


_COMBINE_SUB = 32


def _combine_kernel(h_ref, gate_ref, *refs):
    y_refs = refs[:TOP_K]
    sg_ref, su_ref, sd_ref, g_ref, b_ref, of_ref, ob_ref, moe_ref = refs[TOP_K:]
    h = h_ref[...]
    tm = h.shape[0]
    half = SLAB * LANES

    def weighted_sum(s, carry):
        row0 = pl.multiple_of(s * _COMBINE_SUB, _COMBINE_SUB)
        rs = pl.ds(row0, _COMBINE_SUB)
        gates = gate_ref[rs, :]
        for j in range(SLAB):
            lo_acc = hi_acc = None
            for kk in range(TOP_K):
                lo, hi = _unpack_words(y_refs[kk][pl.ds(row0 * SLAB + j, _COMBINE_SUB, stride=SLAB), :])
                gk = gates[:, kk:kk + 1]
                lo_acc = gk * lo if kk == 0 else lo_acc + gk * lo
                hi_acc = gk * hi if kk == 0 else hi_acc + gk * hi
            moe_ref[rs, j * LANES:(j + 1) * LANES] = lo_acc
            moe_ref[rs, half + j * LANES:half + (j + 1) * LANES] = hi_acc
        return carry
    lax.fori_loop(0, tm // _COMBINE_SUB, weighted_sum, 0)
    hb = h.astype(BF16)
    act = _silu(_dot(hb, sg_ref[...])) * _dot(hb, su_ref[...])
    y = ALPHA * h + _dot(act.astype(BF16), sd_ref[...]) + moe_ref[...]
    out = _layer_norm(y, g_ref[...], b_ref[...])
    of_ref[...] = out
    ob_ref[...] = out.astype(BF16)


def _combine(h, gates_t, y_slots, sg, su, sd, g, b, tm=COMBINE_TILE):
    t, d = h.shape
    n = t // tm
    f = sg.shape[1]
    vec = pl.BlockSpec((1, d), lambda i: (0, 0))
    row = pl.BlockSpec((tm, d), lambda i: (i, 0))
    return pl.pallas_call(
        _combine_kernel,
        grid=(n,),
        in_specs=[row, pl.BlockSpec((tm, TOP_K), lambda i: (i, 0))]
        + [pl.BlockSpec((tm * SLAB, LANES), lambda i, kk=kk: (kk * n + i, 0)) for kk in range(TOP_K)]
        + [pl.BlockSpec((d, f), lambda i: (0, 0)),
            pl.BlockSpec((d, f), lambda i: (0, 0)),
            pl.BlockSpec((f, d), lambda i: (0, 0)),
            vec, vec],
        out_specs=[row, row],
        out_shape=[jax.ShapeDtypeStruct((t, d), F32), jax.ShapeDtypeStruct((t, d), BF16)],
        scratch_shapes=[pltpu.VMEM((tm, d), F32)],
        compiler_params=pltpu.CompilerParams(dimension_semantics=("parallel",),
                                             vmem_limit_bytes=VMEM_LIMIT),
        name="combine_ln2",
    )(h, gates_t, *([y_slots] * TOP_K), sg, su, sd, g.reshape(1, d), b.reshape(1, d))


def _moe(h, h_slab, router_w, router_bias, wg, wu, wd, sg, su, sd, ln_g, ln_b):
    t, d = h.shape
    tk = t * TOP_K
    n_groups = TOKEN_GROUPS if t % (TOKEN_GROUPS * ROUTER_TILE) == 0 else 1
    tg = t // n_groups
    n_buckets = n_groups * N_EXPERTS
    n_blocks = tk // MOE_ROWS + n_buckets
    eidx, gates, counts = _router(h, router_w, router_bias, n_groups)
    counts = counts.reshape(n_buckets)
    n_rows = n_blocks * MOE_ROWS
    padded = (counts + MOE_ROWS - 1) // MOE_ROWS * MOE_ROWS
    pend = jnp.cumsum(padded)
    pad_end = jnp.cumsum(padded - counts)
    pad_key = jnp.sum(pad_end[None, :] <= jnp.arange(n_rows - tk, dtype=jnp.int32)[:, None], axis=1)
    slot_group = (jnp.arange(tk, dtype=jnp.int32) >> _TOP_K_BITS) // tg
    keys = jnp.concatenate([slot_group * N_EXPERTS + eidx.T.reshape(tk), pad_key.astype(jnp.int32)])
    index_bits = (n_rows - 1).bit_length()
    assert (n_buckets + 1) << index_bits < 2 ** 31
    packed = lax.sort((keys << index_bits) | jnp.arange(n_rows, dtype=jnp.int32))
    row_slot = packed & ((1 << index_bits) - 1)
    real = row_slot < tk
    row = jnp.arange(n_rows, dtype=jnp.int32)
    row_src = jnp.where(real, ((row_slot >> _TOP_K_BITS) % tg) * SLAB, 0)
    plane_row = (row_slot & (TOP_K - 1)) * t + (row_slot >> _TOP_K_BITS)
    row_dst = jnp.where(real, plane_row, tk + row % (2 * MOE_ROWS)) * SLAB
    block_start = jnp.arange(n_blocks, dtype=jnp.int32) * MOE_ROWS
    block_bucket = jnp.minimum(jnp.sum(pend[None, :] <= block_start[:, None], axis=1), n_buckets - 1).astype(jnp.int32)
    n_used = (pend[-1] // MOE_ROWS).astype(jnp.int32)
    block_e = block_bucket % N_EXPERTS
    block = jnp.arange(n_blocks, dtype=jnp.int32)
    first = (block == 0) | (block_bucket != jnp.roll(block_bucket, 1))
    seq = jnp.cumsum(first.astype(jnp.int32)) - 1
    later_first = jnp.where(first & (block < n_used), block, n_blocks)
    next_first = jnp.concatenate([lax.cummin(later_first, reverse=True)[1:], jnp.full((1,), n_blocks, jnp.int32)])
    next_e = jnp.where(next_first < n_blocks, block_e[jnp.minimum(next_first, n_blocks - 1)], -1)
    weight_plan = jnp.stack([first.astype(jnp.int32), seq % 2, next_e]).astype(jnp.int32)
    y_slots = _experts(h_slab.reshape(n_groups, tg * SLAB, LANES),
                       row_src.reshape(n_blocks, 1, MOE_ROWS), row_dst.reshape(n_blocks, 1, MOE_ROWS),
                       block_e, block_bucket // N_EXPERTS, weight_plan, n_used.reshape(1),
                       wg, wu, wd, tk)
    return _combine(h, gates.T, y_slots, sg, su, sd, ln_g, ln_b, tm=min(COMBINE_TILE, t))


def kernel(x, ln0_gain, ln0_bias, w_in, b_in, hg_lb_logits, sw_sinks, sg_ln_gain, sg_ln_bias, sg_w_s, sg_b_s, mix_gain, w_out, ln1_gain, ln1_bias, router_w, router_bias, exp_w_gate, exp_w_up, exp_w_down, sh_w_gate, sh_w_up, sh_w_down, ln2_gain, ln2_bias):
    b, s, d = x.shape
    t = b * s
    p = jax.nn.softmax(hg_lb_logits.astype(F32), axis=0)
    cs = jnp.cumsum(p, axis=0)
    lbs = cs - cs[0:1]
    a0, a1 = HG_WIDTH, HG_WIDTH + SW_WIDTH

    n_e, d_e, f_e = exp_w_gate.shape[1:]
    wg_rows, wu_rows = exp_w_gate.reshape(-1, f_e), exp_w_up.reshape(-1, f_e)
    wd_rows = exp_w_down.reshape(-1, d_e)
    h, hb = _ln(x.reshape(t, d), ln0_gain, ln0_bias)
    for l in range(DEPTH):
        proj, wd = _in_proj(hb, _regroup_columns(w_in[l]).astype(BF16), _regroup_columns(b_in[l]), wd_rows, l)
        proj3 = proj.reshape(b, s, D_IN)
        lb = lbs[l]
        o_a, wg = _hgrn(proj3, jnp.log(lb), jnp.log1p(-lb), 1.0 - lb, mix_gain[l, :a0], [wg_rows], l)
        o_b, wu = _swa(proj3, sw_sinks[l], mix_gain[l, a0:a1], [wu_rows], l)
        o_c, = _sgu(proj3, sg_ln_gain[l], sg_ln_bias[l], sg_w_s[l], sg_b_s[l], mix_gain[l, a1:], [], l)
        h, h_slab = _out_proj(o_a.reshape(t, -1), o_b.reshape(t, -1), o_c.reshape(t, -1),
                              w_out[l].astype(BF16), h, ln1_gain[l], ln1_bias[l])
        h, hb = _moe(h, h_slab, router_w[l], router_bias[l], wg.reshape(n_e, d_e, f_e),
                     wu.reshape(n_e, d_e, f_e), wd.reshape(n_e, f_e, d_e),
                     sh_w_gate[l].astype(BF16), sh_w_up[l].astype(BF16), sh_w_down[l].astype(BF16),
                     ln2_gain[l], ln2_bias[l])
    return h.reshape(b, s, d)
```

```python
import functools

import numpy as np
import jax
import jax.numpy as jnp
from jax import lax
from jax.experimental import pallas as pl
from jax.experimental.pallas import tpu as pltpu

F32 = jnp.float32
BF16 = jnp.bfloat16

D_MODEL = 2048
DEPTH = 2
HG_HEADS = 4
HG_DK = 128
HG_WIDTH = HG_HEADS * HG_DK
CHUNK = 128
SW_Q_HEADS = 16
SW_KV_HEADS = 2
SW_HEAD_DIM = 64
SW_WIDTH = SW_Q_HEADS * SW_HEAD_DIM
SW_KV_WIDTH = SW_KV_HEADS * SW_HEAD_DIM
SG_GROUPS = 4
SG_WIDTH = SG_GROUPS * CHUNK
D_IN = 4 * HG_WIDTH + SW_WIDTH + 2 * SW_KV_WIDTH + 2 * SG_WIDTH
N_EXPERTS = 64
TOP_K = 8
D_EXPERT = 512
N_EXPERT_GROUPS = 8
TOPK_GROUPS = 4
ROUTED_SCALE = 2.5
ALPHA = (2 * DEPTH) ** 0.25
LN_EPS = 1e-5
RMS_EPS = 1e-6

_SWQ_OFF = 4 * HG_WIDTH
_SGU_OFF = _SWQ_OFF + SW_WIDTH
_SWK_OFF = _SGU_OFF + 2 * SG_WIDTH


def _regroup_columns(a):
    k0 = _SWQ_OFF + SW_WIDTH
    k1 = k0 + 2 * SW_KV_WIDTH
    return jnp.concatenate([a[..., :k0], a[..., k1:], a[..., k0:k1]], axis=-1)


MOE_ROWS = 256
TOKEN_GROUPS = 2
ROUTER_TILE = 512
COMBINE_TILE = 256
VMEM_LIMIT = 56 * 1024 * 1024

_LEVEL_SIZES = (64, 32, 16, 8, 4, 2, 1)
_N_LEVELS = len(_LEVEL_SIZES)


def _hgrn_constants():
    t = np.arange(CHUNK)[:, None]
    u = np.arange(CHUNK)[None, :]
    mats = []
    level = np.full((CHUNK, CHUNK), -1, np.int32)
    for li, s in enumerate(_LEVEL_SIZES):
        blk = t // s
        odd = (blk % 2) == 1
        m_query = (u >= blk * s) & (u <= t)
        m_key = (u > t) & (u <= (blk + 1) * s - 1)
        mats.append(np.where(odd, m_query, m_key))
        pair = ((t // s) % 2 == 1) & ((u // s) == (t // s) - 1)
        level[pair] = li
    mats.append(u <= t)
    mats.append(u > t)
    level[np.arange(CHUNK), np.arange(CHUNK)] = _N_LEVELS
    return np.concatenate(mats, 0).astype(np.float32), level


_HGRN_MSTACK, _HGRN_LEVEL = _hgrn_constants()


def _layer_norm(x, g, b):
    mu = jnp.mean(x, axis=-1, keepdims=True)
    xc = x - mu
    var = jnp.mean(xc * xc, axis=-1, keepdims=True)
    return xc * lax.rsqrt(var + LN_EPS) * g + b


def _silu(x):
    return x / (1.0 + jnp.exp(-x))


def _gelu(x):
    return 0.5 * x * (1.0 + lax.erf(x * (2.0 ** -0.5)))


def _dot(a, b):
    return jnp.dot(a, b, preferred_element_type=F32)


def _dot_nt(a, b):
    return lax.dot_general(a, b, (((1,), (1,)), ((), ())), preferred_element_type=F32)


def _dot_tn(a, b):
    return lax.dot_general(a, b, (((0,), (0,)), ((), ())), preferred_element_type=F32)


def _ln_kernel(x_ref, g_ref, b_ref, of_ref, ob_ref):
    y = _layer_norm(x_ref[...], g_ref[...], b_ref[...])
    of_ref[...] = y
    ob_ref[...] = y.astype(BF16)


def _ln(x, g, b, tm=512):
    t, d = x.shape
    tm = min(tm, t)
    row = pl.BlockSpec((tm, d), lambda i: (i, 0))
    vec = pl.BlockSpec((1, d), lambda i: (0, 0))
    return pl.pallas_call(
        _ln_kernel,
        grid=(t // tm,),
        in_specs=[row, vec, vec],
        out_specs=[row, row],
        out_shape=[jax.ShapeDtypeStruct((t, d), F32), jax.ShapeDtypeStruct((t, d), BF16)],
        name="ln0",
    )(x, g.reshape(1, d), b.reshape(1, d))


def _mm_bias_kernel(a_ref, w_ref, b_ref, c_ref, o_ref, co_ref):
    o_ref[...] = _dot(a_ref[...], w_ref[...]) + b_ref[...]
    co_ref[...] = c_ref[...].astype(BF16)


def _in_proj(a, w, b, cast, layer, tm=512):
    t, k = a.shape
    n = w.shape[1]
    tm = min(tm, t)
    tn = n // 2
    n_i = t // tm
    steps = (n // tn) * n_i
    rows, cols = cast.shape[0] // DEPTH, cast.shape[1]
    per_step = rows // steps
    assert per_step * steps == rows and per_step % 16 == 0
    return pl.pallas_call(
        _mm_bias_kernel,
        grid=(n // tn, n_i),
        in_specs=[pl.BlockSpec((tm, k), lambda j, i: (i, 0)),
                  pl.BlockSpec((k, tn), lambda j, i: (0, j)),
                  pl.BlockSpec((1, tn), lambda j, i: (0, j)),
                  pl.BlockSpec((per_step, cols), lambda j, i: (layer * steps + j * n_i + i, 0))],
        out_specs=[pl.BlockSpec((tm, tn), lambda j, i: (i, j)),
                   pl.BlockSpec((per_step, cols), lambda j, i: (j * n_i + i, 0))],
        out_shape=[jax.ShapeDtypeStruct((t, n), F32), jax.ShapeDtypeStruct((rows, cols), BF16)],
        compiler_params=pltpu.CompilerParams(vmem_limit_bytes=VMEM_LIMIT),
        name="in_proj",
    )(a, w, b.reshape(1, n), cast)


def _mixer_call(kernel_fn, name, proj3, in_specs, args, width, semantics, casts, layer, scratch_shapes=()):
    b, s, _ = proj3.shape
    nc = s // CHUNK
    n_in, n_cast = len(in_specs), len(casts)
    cast_in, cast_out, cast_shapes = [], [], []
    for w_all in casts:
        cols = w_all.shape[1]
        rows = w_all.shape[0] // DEPTH
        per_step = rows // (b * nc)
        assert per_step * b * nc == rows and per_step % 16 == 0
        cast_in.append(pl.BlockSpec((per_step, cols), lambda bi, ci: (layer * (b * nc) + bi * nc + ci, 0)))
        cast_out.append(pl.BlockSpec((per_step, cols), lambda bi, ci: (bi * nc + ci, 0)))
        cast_shapes.append(jax.ShapeDtypeStruct((rows, cols), BF16))

    def body(*refs):
        w_in = refs[n_in:n_in + n_cast]
        out = refs[n_in + n_cast]
        w_out = refs[n_in + n_cast + 1:n_in + 2 * n_cast + 1]
        kernel_fn(*refs[:n_in], out, *refs[n_in + 2 * n_cast + 1:])
        for src, dst in zip(w_in, w_out):
            dst[...] = src[...].astype(BF16)

    return pl.pallas_call(
        body,
        grid=(b, nc),
        in_specs=list(in_specs) + cast_in,
        out_specs=[pl.BlockSpec((None, CHUNK, width), lambda bi, ci: (bi, ci, 0))] + cast_out,
        out_shape=[jax.ShapeDtypeStruct((b, s, width), BF16)] + cast_shapes,
        scratch_shapes=list(scratch_shapes),
        compiler_params=pltpu.CompilerParams(dimension_semantics=semantics, vmem_limit_bytes=VMEM_LIMIT),
        name=name,
    )(*args, *casts)


def _hgrn_kernel(q_ref, f_ref, i_ref, g_ref, llb_ref, l1lb_ref, oml_ref, gain_ref,
                 mstack_ref, level_ref, o_ref, state_ref):
    c = pl.program_id(1)

    @pl.when(c == 0)
    def _():
        state_ref[...] = jnp.zeros_like(state_ref)

    q = _silu(q_ref[...])
    z = f_ref[...]
    ez = jnp.exp(-jnp.abs(z))
    log_sig = jnp.minimum(z, 0.0) - jnp.log(1.0 + ez)
    sig_neg = jnp.where(z >= 0.0, ez, 1.0) / (1.0 + ez)
    k = oml_ref[...] * sig_neg
    a = llb_ref[...]
    bb = l1lb_ref[...] + log_sig
    log_f = jnp.maximum(a, bb) + jnp.log(1.0 + jnp.exp(-jnp.abs(a - bb)))

    lf_hi = log_f.astype(BF16)
    lf_lo = (log_f - lf_hi.astype(F32)).astype(BF16)
    m = mstack_ref[...]
    sums = _dot(m, lf_hi) + _dot(m, lf_lo)
    decay = jnp.exp(sums)

    rows = lax.broadcasted_iota(jnp.int32, (CHUNK, 1), 0)
    factors = []
    for li, s in enumerate(_LEVEL_SIZES):
        is_query = ((rows // s) % 2) == 1
        factors.append((jnp.where(is_query, q, k) * decay[li * CHUNK:(li + 1) * CHUNK]).astype(BF16))
    cum = _N_LEVELS * CHUNK
    q_in = (q * decay[cum:cum + CHUNK]).astype(BF16)
    k_dec = (k * decay[cum + CHUNK:cum + 2 * CHUNK]).astype(BF16)
    end_decay = decay[cum + CHUNK - 1:cum + CHUNK]
    qb = q.astype(BF16)
    kb = k.astype(BF16)
    vb = i_ref[...].astype(BF16)
    level = level_ref[...]
    gate = _silu(g_ref[...])
    gain = gain_ref[...]

    for h in range(HG_HEADS):
        hs = slice(h * HG_DK, (h + 1) * HG_DK)
        scores = jnp.zeros((CHUNK, CHUNK), F32)
        for li in range(_N_LEVELS):
            fl = factors[li][:, hs]
            scores = jnp.where(level == li, _dot_nt(fl, fl), scores)
        scores = jnp.where(level == _N_LEVELS, _dot_nt(qb[:, hs], kb[:, hs]), scores)
        st = state_ref[h]
        o = _dot(scores.astype(BF16), vb[:, hs]) + _dot_nt(q_in[:, hs], st.astype(BF16))
        state_ref[h] = st * end_decay[:, hs] + _dot_tn(vb[:, hs], k_dec[:, hs])
        ms = jnp.mean(o * o, axis=-1, keepdims=True)
        o = o * lax.rsqrt(ms + RMS_EPS) * gain[:, hs] * gate[:, hs]
        o_ref[:, hs] = o.astype(o_ref.dtype)


def _hgrn(proj3, llb, l1lb, oml, gain, casts, layer):
    w = HG_WIDTH

    def col(j):
        return pl.BlockSpec((None, CHUNK, w), lambda bi, ci, j=j: (bi, ci, j))

    vec = pl.BlockSpec((1, w), lambda bi, ci: (0, 0))
    nm = _HGRN_MSTACK.shape[0]
    return _mixer_call(
        _hgrn_kernel, "hgrn2", proj3,
        [col(0), col(1), col(2), col(3), vec, vec, vec, vec,
         pl.BlockSpec((nm, CHUNK), lambda bi, ci: (0, 0)),
         pl.BlockSpec((CHUNK, CHUNK), lambda bi, ci: (0, 0))],
        (proj3, proj3, proj3, proj3, llb.reshape(1, w), l1lb.reshape(1, w), oml.reshape(1, w),
         gain.reshape(1, w), jnp.asarray(_HGRN_MSTACK, BF16), jnp.asarray(_HGRN_LEVEL)),
        w, ("parallel", "arbitrary"), casts, layer,
        scratch_shapes=[pltpu.VMEM((HG_HEADS, HG_DK, HG_DK), F32)])


def _swa_kernel(q_ref, kp_ref, kc_ref, vp_ref, vc_ref, sink_ref, gain_ref, o_ref):
    n = pl.program_id(1)
    g = SW_Q_HEADS // SW_KV_HEADS
    hd = SW_HEAD_DIM
    assert (hd ** -0.5) == 2.0 ** round(np.log2(hd ** -0.5))
    q = (q_ref[...] * (hd ** -0.5)).astype(BF16)
    kband = jnp.concatenate([kp_ref[...], kc_ref[...]], axis=0).astype(BF16)
    vband = jnp.concatenate([vp_ref[...], vc_ref[...]], axis=0).astype(BF16)
    t = lax.broadcasted_iota(jnp.int32, (CHUNK, 2 * CHUNK), 0)
    s = lax.broadcasted_iota(jnp.int32, (CHUNK, 2 * CHUNK), 1)
    rel = t + CHUNK - s
    mask = (rel >= 0) & (rel < CHUNK) & ((s >= CHUNK) | (n > 0))
    bias = jnp.where(mask, 0.0, -jnp.inf)
    kvs = [(kband[:, kv * hd:(kv + 1) * hd], vband[:, kv * hd:(kv + 1) * hd]) for kv in range(SW_KV_HEADS)]
    outs = []
    for h in range(SW_Q_HEADS):
        k_h, v_h = kvs[h // g]
        sc = _dot_nt(q[:, h * hd:(h + 1) * hd], k_h) + bias
        sink = sink_ref[h]
        mx = jnp.maximum(jnp.max(sc, axis=-1, keepdims=True), sink)
        p = jnp.exp(sc - mx)
        denom = jnp.sum(p, axis=-1, keepdims=True) + jnp.exp(sink - mx)
        outs.append(_dot(p.astype(BF16), v_h) / denom)
    o = jnp.concatenate(outs, axis=-1)
    ms = jnp.mean(o * o, axis=-1, keepdims=True)
    o_ref[...] = (o * lax.rsqrt(ms + RMS_EPS) * gain_ref[...]).astype(o_ref.dtype)


def _swa(proj3, sinks, gain, casts, layer):
    q_col = _SWQ_OFF // SW_WIDTH
    k_col = _SWK_OFF // SW_KV_WIDTH
    v_col = k_col + 1

    def kv_spec(col, prev):
        if prev:
            return pl.BlockSpec((None, CHUNK, SW_KV_WIDTH), lambda bi, ni: (bi, jnp.maximum(ni - 1, 0), col))
        return pl.BlockSpec((None, CHUNK, SW_KV_WIDTH), lambda bi, ni: (bi, ni, col))

    return _mixer_call(
        _swa_kernel, "swa", proj3,
        [pl.BlockSpec((None, CHUNK, SW_WIDTH), lambda bi, ni: (bi, ni, q_col)),
         kv_spec(k_col, True), kv_spec(k_col, False),
         kv_spec(v_col, True), kv_spec(v_col, False),
         pl.BlockSpec(memory_space=pltpu.SMEM),
         pl.BlockSpec((1, SW_WIDTH), lambda bi, ni: (0, 0))],
        (proj3, proj3, proj3, proj3, proj3, sinks.astype(F32), gain.reshape(1, SW_WIDTH)),
        SW_WIDTH, ("parallel", "parallel"), casts, layer)


def _sgu_kernel(u_ref, v_ref, lng_ref, lnb_ref, w_ref, bs_ref, gain_ref, o_ref):
    u = _gelu(u_ref[...])
    v = _layer_norm(_gelu(v_ref[...]), lng_ref[...], lnb_ref[...]).astype(BF16)
    r = lax.broadcasted_iota(jnp.int32, (CHUNK, CHUNK), 0)
    c = lax.broadcasted_iota(jnp.int32, (CHUNK, CHUNK), 1)
    tril = c <= r
    bs = bs_ref[...]
    parts = []
    for gi in range(SG_GROUPS):
        gs = slice(gi * CHUNK, (gi + 1) * CHUNK)
        w = jnp.where(tril, w_ref[gi], 0.0).astype(BF16)
        parts.append(_dot(w, v[:, gs]) + bs[:, gi:gi + 1])
    o = u * jnp.concatenate(parts, axis=-1)
    ms = jnp.mean(o * o, axis=-1, keepdims=True)
    o_ref[...] = (o * lax.rsqrt(ms + RMS_EPS) * gain_ref[...]).astype(o_ref.dtype)


def _sgu(proj3, ln_g, ln_b, w_s, b_s, gain, casts, layer):
    w = SG_WIDTH
    u_col = _SGU_OFF // SG_WIDTH
    vec = pl.BlockSpec((1, w), lambda bi, ci: (0, 0))
    return _mixer_call(
        _sgu_kernel, "sgu", proj3,
        [pl.BlockSpec((None, CHUNK, w), lambda bi, ci: (bi, ci, u_col)),
         pl.BlockSpec((None, CHUNK, w), lambda bi, ci: (bi, ci, u_col + 1)),
         vec, vec,
         pl.BlockSpec((SG_GROUPS, CHUNK, CHUNK), lambda bi, ci: (0, 0, 0)),
         pl.BlockSpec((CHUNK, SG_GROUPS), lambda bi, ci: (0, 0)),
         vec],
        (proj3, proj3, ln_g.reshape(1, w), ln_b.reshape(1, w), w_s, b_s.T, gain.reshape(1, w)),
        w, ("parallel", "parallel"), casts, layer)


LANES = 128
SLAB = D_MODEL // 2 // LANES


def _pack_pair(low, high):
    def bits(v):
        return lax.bitcast_convert_type(v.astype(BF16).astype(F32), jnp.uint32)

    return (bits(high) & jnp.uint32(0xFFFF0000)) | (bits(low) >> 16)


def _pack_rows(x):
    half = x.shape[1] // 2
    return _pack_pair(x[:, :half], x[:, half:])


def _unpack_words(w):
    return (lax.bitcast_convert_type(w << 16, F32),
            lax.bitcast_convert_type(w & jnp.uint32(0xFFFF0000), F32))


def _store_slabs(ref, x):
    n = x.shape[0]
    for j in range(SLAB):
        ref[pl.ds(j, n, stride=SLAB), :] = x[:, j * LANES:(j + 1) * LANES]


def _load_packed_rows(ref, n, first=0, stride=SLAB):
    lows, highs = [], []
    for j in range(SLAB):
        lo, hi = _unpack_words(ref[pl.ds(first + j, n, stride=stride), :])
        lows.append(lo.astype(BF16))
        highs.append(hi.astype(BF16))
    return jnp.concatenate(lows + highs, axis=1)


def _out_proj_kernel(oa_ref, ob_ref, oc_ref, w_ref, h_ref, g_ref, b_ref, of_ref, os_ref):
    mixed = jnp.concatenate([oa_ref[...], ob_ref[...], oc_ref[...]], axis=-1)
    y = ALPHA * h_ref[...] + _dot(mixed, w_ref[...])
    out = _layer_norm(y, g_ref[...], b_ref[...])
    of_ref[...] = out
    _store_slabs(os_ref, _pack_rows(out))


def _out_proj(oa, ob, oc, w, h, g, b, tm=512):
    t, d = h.shape
    tm = min(tm, t)
    vec = pl.BlockSpec((1, d), lambda i: (0, 0))
    return pl.pallas_call(
        _out_proj_kernel,
        grid=(t // tm,),
        in_specs=[pl.BlockSpec((tm, oa.shape[1]), lambda i: (i, 0)),
                  pl.BlockSpec((tm, ob.shape[1]), lambda i: (i, 0)),
                  pl.BlockSpec((tm, oc.shape[1]), lambda i: (i, 0)),
                  pl.BlockSpec(w.shape, lambda i: (0, 0)),
                  pl.BlockSpec((tm, d), lambda i: (i, 0)), vec, vec],
        out_specs=[pl.BlockSpec((tm, d), lambda i: (i, 0)),
                   pl.BlockSpec((tm * SLAB, LANES), lambda i: (i, 0))],
        out_shape=[jax.ShapeDtypeStruct((t, d), F32), jax.ShapeDtypeStruct((t * SLAB, LANES), jnp.uint32)],
        compiler_params=pltpu.CompilerParams(vmem_limit_bytes=VMEM_LIMIT),
        name="out_proj_ln1",
    )(oa, ob, oc, w, h, g.reshape(1, d), b.reshape(1, d))


def _first_index_of_max(x, iota, size, axis):
    mx = jnp.max(x, axis=axis, keepdims=True)
    idx = jnp.min(jnp.where(x == mx, iota, size), axis=axis, keepdims=True)
    return mx, idx


def _router_kernel(tiles_per_group, h_ref, wt_ref, bias_ref, eidx_ref, gate_ref, count_ref, run_ref):
    i = pl.program_id(0)

    @pl.when(i % tiles_per_group == 0)
    def _():
        run_ref[...] = jnp.zeros_like(run_ref)

    tm = h_ref.shape[0]
    per_group = N_EXPERTS // N_EXPERT_GROUPS
    def split(v):
        hi = v.astype(BF16)
        return hi, (v - hi.astype(F32)).astype(BF16)

    w_hi, w_lo = split(wt_ref[...])
    h_hi, h_lo = split(h_ref[...])
    logits = _dot_nt(w_hi, h_hi) + (_dot_nt(w_hi, h_lo) + _dot_nt(w_lo, h_hi))
    scores = 1.0 / (1.0 + jnp.exp(-logits))
    sel = scores + bias_ref[...]
    sel3 = sel.reshape(N_EXPERT_GROUPS, per_group, tm)
    io3 = lax.broadcasted_iota(jnp.int32, sel3.shape, 1)
    m1, i1 = _first_index_of_max(sel3, io3, per_group, 1)
    m2 = jnp.max(jnp.where(io3 == i1, -jnp.inf, sel3), axis=1, keepdims=True)
    grp = (m1 + m2).reshape(N_EXPERT_GROUPS, tm)
    iog = lax.broadcasted_iota(jnp.int32, grp.shape, 0)
    keep = jnp.zeros(grp.shape, jnp.bool_)
    for _ in range(TOPK_GROUPS):
        _, gi = _first_index_of_max(grp, iog, N_EXPERT_GROUPS, 0)
        hit = iog == gi
        keep = keep | hit
        grp = jnp.where(hit, -jnp.inf, grp)
    keep3 = jnp.broadcast_to(keep.reshape(N_EXPERT_GROUPS, 1, tm), sel3.shape)
    cand = jnp.where(keep3, sel3, -jnp.inf).reshape(N_EXPERTS, tm)
    ioe = lax.broadcasted_iota(jnp.int32, cand.shape, 0)
    chosen = jnp.zeros(cand.shape, F32)
    idxs, gvals = [], []
    for _ in range(TOP_K):
        _, ei = _first_index_of_max(cand, ioe, N_EXPERTS, 0)
        hit = ioe == ei
        idxs.append(ei)
        gvals.append(jnp.sum(jnp.where(hit, scores, 0.0), axis=0, keepdims=True))
        chosen = jnp.where(hit, 1.0, chosen)
        cand = jnp.where(hit, -jnp.inf, cand)
    gsum = functools.reduce(lambda a, b: a + b, gvals)
    for kk in range(TOP_K):
        eidx_ref[kk:kk + 1, :] = idxs[kk]
        gate_ref[kk:kk + 1, :] = gvals[kk] / gsum * ROUTED_SCALE
    run = run_ref[...] + jnp.sum(chosen, axis=1, keepdims=True)
    run_ref[...] = run
    count_ref[...] = run.astype(jnp.int32)


def _router(h, router_w, router_bias, n_groups, tm=ROUTER_TILE):
    t, d = h.shape
    tm = min(tm, t // n_groups)
    tiles_per_group = t // n_groups // tm
    slot = pl.BlockSpec((TOP_K, tm), lambda i: (0, i))
    return pl.pallas_call(
        functools.partial(_router_kernel, tiles_per_group),
        grid=(t // tm,),
        in_specs=[pl.BlockSpec((tm, d), lambda i: (i, 0)),
                  pl.BlockSpec((N_EXPERTS, d), lambda i: (0, 0)),
                  pl.BlockSpec((N_EXPERTS, 1), lambda i: (0, 0))],
        out_specs=[slot, slot, pl.BlockSpec((None, N_EXPERTS, 1), lambda i: (i // tiles_per_group, 0, 0))],
        out_shape=[jax.ShapeDtypeStruct((TOP_K, t), jnp.int32),
                   jax.ShapeDtypeStruct((TOP_K, t), F32),
                   jax.ShapeDtypeStruct((n_groups, N_EXPERTS, 1), jnp.int32)],
        scratch_shapes=[pltpu.VMEM((N_EXPERTS, 1), F32)],
        compiler_params=pltpu.CompilerParams(dimension_semantics=("arbitrary",),
                                             vmem_limit_bytes=VMEM_LIMIT),
        name="router",
    )(h, router_w.T, router_bias.reshape(N_EXPERTS, 1))


_ISSUE_UNROLL = 8
_TOP_K_BITS = TOP_K.bit_length() - 1
assert 1 << _TOP_K_BITS == TOP_K


_OUT_BUFFERS = 3


def _experts_kernel(be_ref, bg_ref, wplan_ref, nu_ref, src_ref, dstp_ref, dstc_ref, h_hbm,
                    wg_hbm, wu_hbm, wd_hbm, y_hbm, tok_ref, xbuf, obuf0, obuf1, obuf2, wg_buf, wu_buf, wd_buf,
                    lsem, ssem, wsem):
    b = pl.program_id(0)
    nu = nu_ref[0]
    obuf = (obuf0, obuf1, obuf2)
    rows = xbuf.shape[0] // SLAB
    d = wg_buf.shape[1]
    f = wg_buf.shape[2]
    wslot = wplan_ref[1, b]

    def weight_copies(expert, slot_):
        return [pltpu.make_async_copy(src.at[expert], dst.at[slot_], wsem.at[slot_])
                for src, dst in ((wg_hbm, wg_buf), (wu_hbm, wu_buf), (wd_hbm, wd_buf))]

    @pl.when((b < nu) & (wplan_ref[0, b] == 1))
    def _():
        @pl.when(b == 0)
        def _():
            for c in weight_copies(be_ref[0], wslot):
                c.start()
        for c in weight_copies(be_ref[b], wslot):
            c.wait()
        nxt = wplan_ref[2, b]

        @pl.when(nxt >= 0)
        def _():
            for c in weight_copies(nxt, 1 - wslot):
                c.start()

    wg_ref, wu_ref, wd_ref = wg_buf.at[wslot], wu_buf.at[wslot], wd_buf.at[wslot]
    pad0 = y_hbm.shape[0] - 2 * rows * SLAB

    def scatter_row(dst_ref, r, src_slot, priority):
        dst = pl.multiple_of(dst_ref[0, 0, r], SLAB)
        buf_rows = pl.ds(pl.multiple_of(r * SLAB, SLAB), SLAB)
        pltpu.make_async_copy(obuf[src_slot].at[buf_rows], y_hbm.at[pl.ds(dst, SLAB)],
                              ssem.at[src_slot]).start(priority=priority)

    def wait_scatter(s):
        pltpu.make_async_copy(obuf[s], y_hbm.at[pl.ds(0, rows * SLAB)], ssem.at[s]).wait()

    @pl.when(b == 0)
    def _():
        for s in range(_OUT_BUFFERS):
            obuf[s][...] = jnp.zeros_like(obuf[s])
        for s in range(2):
            pltpu.make_async_copy(obuf[s], y_hbm.at[pl.ds(pad0 + s * rows * SLAB, rows * SLAB)],
                                  ssem.at[s]).start()
        for s in range(2):
            wait_scatter(s)

    group = bg_ref[b]

    @pl.when((b < nu) & ((b == 0) | (group != bg_ref[jnp.maximum(b - 1, 0)])))
    def _():
        load = pltpu.make_async_copy(h_hbm.at[group], tok_ref, lsem.at[0])
        load.start()
        load.wait()

    def block(slot, with_scatter, live=None):
        live = rows if live is None else live
        other = (slot - 1) % _OUT_BUFFERS

        def gather(i, carry):
            for u in range(_ISSUE_UNROLL):
                r = i * _ISSUE_UNROLL + u
                src = pl.multiple_of(src_ref[0, 0, r], SLAB)
                xbuf[pl.ds(pl.multiple_of(r * SLAB, SLAB), SLAB), :] = tok_ref[pl.ds(src, SLAB), :]
            return carry
        lax.fori_loop(0, live // _ISSUE_UNROLL, gather, 0)

        if with_scatter:
            for r in range(rows):
                scatter_row(dstp_ref, r, other, r % 2)

        x = _load_packed_rows(xbuf, live)
        acts = []
        fh = f // 2
        for j in range(2):
            gate = _dot(x, wg_ref[:, j * fh:(j + 1) * fh])
            up = _dot(x, wu_ref[:, j * fh:(j + 1) * fh])
            acts.append((_silu(gate) * up).astype(BF16))
        act = jnp.concatenate(acts, axis=1)
        half = d // 2
        cw = 2 * LANES
        for q in range(half // cw):
            words = _pack_pair(_dot(act, wd_ref[:, q * cw:(q + 1) * cw]),
                               _dot(act, wd_ref[:, half + q * cw:half + (q + 1) * cw]))
            for jj in range(cw // LANES):
                obuf[slot][pl.ds(q * (cw // LANES) + jj, live, stride=SLAB), :] = words[:, jj * LANES:(jj + 1) * LANES]

    def tail(slot):
        for back in (1, 2):
            @pl.when(b >= back)
            def _(back=back):
                wait_scatter((slot - back) % _OUT_BUFFERS)

        def body(r, carry):
            scatter_row(dstc_ref, r, slot, 0)
            return carry
        lax.fori_loop(0, rows, body, 0)
        wait_scatter(slot)

    @pl.when(b == 0)
    def _():
        block(0, False)

    small = wplan_ref[3, b] == 1
    for s in range(_OUT_BUFFERS):
        @pl.when((b >= 1) & (b < nu) & (b % _OUT_BUFFERS == s))
        def _(s=s):
            @pl.when(b >= _OUT_BUFFERS)
            def _():
                wait_scatter(s)

            @pl.when(small)
            def _():
                block(s, True, rows // 2)

            @pl.when(jnp.logical_not(small))
            def _():
                block(s, True)

        @pl.when((b == nu - 1) & (b % _OUT_BUFFERS == s))
        def _(s=s):
            tail(s)


def _experts(h_groups, row_src, row_dst, block_e, block_g, weight_plan, n_used, wg, wu, wd, n_slots):
    nb, _, rows = row_src.shape
    d, f = wg.shape[-2], wg.shape[-1]
    group_rows = h_groups.shape[1]

    def smem(shift):
        def index(b, be, bg, wplan, nu):
            return (jnp.maximum(jnp.minimum(b + shift, nu[0] - 1), 0), 0, 0)
        return pl.BlockSpec((1, 1, rows), index, memory_space=pltpu.SMEM)

    hbm = pl.BlockSpec(memory_space=pl.ANY)
    grid_spec = pltpu.PrefetchScalarGridSpec(
        num_scalar_prefetch=4,
        grid=(nb,),
        in_specs=[smem(0), smem(-1), smem(0), hbm, hbm, hbm, hbm],
        out_specs=hbm,
        scratch_shapes=[pltpu.VMEM((group_rows, LANES), jnp.uint32)]
        + [pltpu.VMEM((rows * SLAB, LANES), jnp.uint32)] * (1 + _OUT_BUFFERS)
        + [pltpu.VMEM((2, d, f), BF16), pltpu.VMEM((2, d, f), BF16), pltpu.VMEM((2, f, d), BF16)]
        + [pltpu.SemaphoreType.DMA((1,)), pltpu.SemaphoreType.DMA((_OUT_BUFFERS,)),
           pltpu.SemaphoreType.DMA((2,))],
    )
    return pl.pallas_call(
        _experts_kernel,
        grid_spec=grid_spec,
        out_shape=jax.ShapeDtypeStruct(((n_slots + 2 * rows) * SLAB, LANES), jnp.uint32),
        compiler_params=pltpu.CompilerParams(dimension_semantics=("arbitrary",),
                                             vmem_limit_bytes=VMEM_LIMIT),
        name="experts",
    )(block_e, block_g, weight_plan, n_used, row_src, row_dst, row_dst, h_groups, wg, wu, wd)


_COMBINE_SUB = 32


def _combine_kernel(h_ref, gate_ref, *refs):
    y_refs = refs[:TOP_K]
    sg_ref, su_ref, sd_ref, g_ref, b_ref, of_ref, ob_ref, moe_ref = refs[TOP_K:]
    h = h_ref[...]
    tm = h.shape[0]
    half = SLAB * LANES

    def weighted_sum(s, carry):
        row0 = pl.multiple_of(s * _COMBINE_SUB, _COMBINE_SUB)
        rs = pl.ds(row0, _COMBINE_SUB)
        gates = gate_ref[rs, :]
        for j in range(SLAB):
            lo_acc = hi_acc = None
            for kk in range(TOP_K):
                lo, hi = _unpack_words(y_refs[kk][pl.ds(row0 * SLAB + j, _COMBINE_SUB, stride=SLAB), :])
                gk = gates[:, kk:kk + 1]
                lo_acc = gk * lo if kk == 0 else lo_acc + gk * lo
                hi_acc = gk * hi if kk == 0 else hi_acc + gk * hi
            moe_ref[rs, j * LANES:(j + 1) * LANES] = lo_acc
            moe_ref[rs, half + j * LANES:half + (j + 1) * LANES] = hi_acc
        return carry
    lax.fori_loop(0, tm // _COMBINE_SUB, weighted_sum, 0)
    hb = h.astype(BF16)
    act = _silu(_dot(hb, sg_ref[...])) * _dot(hb, su_ref[...])
    y = ALPHA * h + _dot(act.astype(BF16), sd_ref[...]) + moe_ref[...]
    out = _layer_norm(y, g_ref[...], b_ref[...])
    of_ref[...] = out
    ob_ref[...] = out.astype(BF16)


def _combine(h, gates_t, y_slots, sg, su, sd, g, b, tm=COMBINE_TILE):
    t, d = h.shape
    n = t // tm
    f = sg.shape[1]
    vec = pl.BlockSpec((1, d), lambda i: (0, 0))
    row = pl.BlockSpec((tm, d), lambda i: (i, 0))
    return pl.pallas_call(
        _combine_kernel,
        grid=(n,),
        in_specs=[row, pl.BlockSpec((tm, TOP_K), lambda i: (i, 0))]
        + [pl.BlockSpec((tm * SLAB, LANES), lambda i, kk=kk: (kk * n + i, 0)) for kk in range(TOP_K)]
        + [pl.BlockSpec((d, f), lambda i: (0, 0)),
            pl.BlockSpec((d, f), lambda i: (0, 0)),
            pl.BlockSpec((f, d), lambda i: (0, 0)),
            vec, vec],
        out_specs=[row, row],
        out_shape=[jax.ShapeDtypeStruct((t, d), F32), jax.ShapeDtypeStruct((t, d), BF16)],
        scratch_shapes=[pltpu.VMEM((tm, d), F32)],
        compiler_params=pltpu.CompilerParams(dimension_semantics=("parallel",),
                                             vmem_limit_bytes=VMEM_LIMIT),
        name="combine_ln2",
    )(h, gates_t, *([y_slots] * TOP_K), sg, su, sd, g.reshape(1, d), b.reshape(1, d))


def _moe(h, h_slab, router_w, router_bias, wg, wu, wd, sg, su, sd, ln_g, ln_b):
    t, d = h.shape
    tk = t * TOP_K
    n_groups = TOKEN_GROUPS if t % (TOKEN_GROUPS * ROUTER_TILE) == 0 else 1
    tg = t // n_groups
    n_buckets = n_groups * N_EXPERTS
    n_blocks = tk // MOE_ROWS + n_buckets
    eidx, gates, counts = _router(h, router_w, router_bias, n_groups)
    counts = counts.reshape(n_buckets)
    n_rows = n_blocks * MOE_ROWS
    padded = (counts + MOE_ROWS - 1) // MOE_ROWS * MOE_ROWS
    pend = jnp.cumsum(padded)
    pad_end = jnp.cumsum(padded - counts)
    pad_key = jnp.sum(pad_end[None, :] <= jnp.arange(n_rows - tk, dtype=jnp.int32)[:, None], axis=1)
    slot_group = (jnp.arange(tk, dtype=jnp.int32) >> _TOP_K_BITS) // tg
    keys = jnp.concatenate([slot_group * N_EXPERTS + eidx.T.reshape(tk), pad_key.astype(jnp.int32)])
    index_bits = (n_rows - 1).bit_length()
    assert (n_buckets + 1) << index_bits < 2 ** 31
    packed = lax.sort((keys << index_bits) | jnp.arange(n_rows, dtype=jnp.int32))
    row_slot = packed & ((1 << index_bits) - 1)
    real = row_slot < tk
    row = jnp.arange(n_rows, dtype=jnp.int32)
    row_src = jnp.where(real, ((row_slot >> _TOP_K_BITS) % tg) * SLAB, 0)
    plane_row = (row_slot & (TOP_K - 1)) * t + (row_slot >> _TOP_K_BITS)
    row_dst = jnp.where(real, plane_row, tk + row % (2 * MOE_ROWS)) * SLAB
    block_start = jnp.arange(n_blocks, dtype=jnp.int32) * MOE_ROWS
    block_bucket = jnp.minimum(jnp.sum(pend[None, :] <= block_start[:, None], axis=1), n_buckets - 1).astype(jnp.int32)
    n_used = (pend[-1] // MOE_ROWS).astype(jnp.int32)
    block_e = block_bucket % N_EXPERTS
    block = jnp.arange(n_blocks, dtype=jnp.int32)
    first = (block == 0) | (block_bucket != jnp.roll(block_bucket, 1))
    seq = jnp.cumsum(first.astype(jnp.int32)) - 1
    later_first = jnp.where(first & (block < n_used), block, n_blocks)
    next_first = jnp.concatenate([lax.cummin(later_first, reverse=True)[1:], jnp.full((1,), n_blocks, jnp.int32)])
    next_e = jnp.where(next_first < n_blocks, block_e[jnp.minimum(next_first, n_blocks - 1)], -1)
    owner = block_bucket[:, None] == jnp.arange(n_buckets, dtype=jnp.int32)[None, :]
    row_in_bucket = block_start - jnp.sum(jnp.where(owner, (pend - padded)[None, :], 0), axis=1)
    block_real = jnp.sum(jnp.where(owner, counts[None, :], 0), axis=1) - row_in_bucket
    small = (block_real <= MOE_ROWS // 2).astype(jnp.int32)
    weight_plan = jnp.stack([first.astype(jnp.int32), seq % 2, next_e, small]).astype(jnp.int32)
    y_slots = _experts(h_slab.reshape(n_groups, tg * SLAB, LANES),
                       row_src.reshape(n_blocks, 1, MOE_ROWS), row_dst.reshape(n_blocks, 1, MOE_ROWS),
                       block_e, block_bucket // N_EXPERTS, weight_plan, n_used.reshape(1),
                       wg, wu, wd, tk)
    return _combine(h, gates.T, y_slots, sg, su, sd, ln_g, ln_b, tm=min(COMBINE_TILE, t))


def kernel(x, ln0_gain, ln0_bias, w_in, b_in, hg_lb_logits, sw_sinks, sg_ln_gain, sg_ln_bias, sg_w_s, sg_b_s, mix_gain, w_out, ln1_gain, ln1_bias, router_w, router_bias, exp_w_gate, exp_w_up, exp_w_down, sh_w_gate, sh_w_up, sh_w_down, ln2_gain, ln2_bias):
    b, s, d = x.shape
    t = b * s
    p = jax.nn.softmax(hg_lb_logits.astype(F32), axis=0)
    cs = jnp.cumsum(p, axis=0)
    lbs = cs - cs[0:1]
    a0, a1 = HG_WIDTH, HG_WIDTH + SW_WIDTH

    n_e, d_e, f_e = exp_w_gate.shape[1:]
    wg_rows, wu_rows = exp_w_gate.reshape(-1, f_e), exp_w_up.reshape(-1, f_e)
    wd_rows = exp_w_down.reshape(-1, d_e)
    h, hb = _ln(x.reshape(t, d), ln0_gain, ln0_bias)
    for l in range(DEPTH):
        proj, wd = _in_proj(hb, _regroup_columns(w_in[l]).astype(BF16), _regroup_columns(b_in[l]), wd_rows, l)
        proj3 = proj.reshape(b, s, D_IN)
        lb = lbs[l]
        o_a, wg = _hgrn(proj3, jnp.log(lb), jnp.log1p(-lb), 1.0 - lb, mix_gain[l, :a0], [wg_rows], l)
        o_b, wu = _swa(proj3, sw_sinks[l], mix_gain[l, a0:a1], [wu_rows], l)
        o_c, = _sgu(proj3, sg_ln_gain[l], sg_ln_bias[l], sg_w_s[l], sg_b_s[l], mix_gain[l, a1:], [], l)
        h, h_slab = _out_proj(o_a.reshape(t, -1), o_b.reshape(t, -1), o_c.reshape(t, -1),
                              w_out[l].astype(BF16), h, ln1_gain[l], ln1_bias[l])
        h, hb = _moe(h, h_slab, router_w[l], router_bias[l], wg.reshape(n_e, d_e, f_e),
                     wu.reshape(n_e, d_e, f_e), wd.reshape(n_e, f_e, d_e),
                     sh_w_gate[l].astype(BF16), sh_w_up[l].astype(BF16), sh_w_down[l].astype(BF16),
                     ln2_gain[l], ln2_bias[l])
    return h.reshape(b, s, d)
```

```python
import functools

import numpy as np
import jax
import jax.numpy as jnp
from jax import lax
from jax.experimental import pallas as pl
from jax.experimental.pallas import tpu as pltpu

F32 = jnp.float32
BF16 = jnp.bfloat16

D_MODEL = 2048
DEPTH = 2
HG_HEADS = 4
HG_DK = 128
HG_WIDTH = HG_HEADS * HG_DK
CHUNK = 128
SW_Q_HEADS = 16
SW_KV_HEADS = 2
SW_HEAD_DIM = 64
SW_WIDTH = SW_Q_HEADS * SW_HEAD_DIM
SW_KV_WIDTH = SW_KV_HEADS * SW_HEAD_DIM
SG_GROUPS = 4
SG_WIDTH = SG_GROUPS * CHUNK
D_IN = 4 * HG_WIDTH + SW_WIDTH + 2 * SW_KV_WIDTH + 2 * SG_WIDTH
N_EXPERTS = 64
TOP_K = 8
D_EXPERT = 512
N_EXPERT_GROUPS = 8
TOPK_GROUPS = 4
ROUTED_SCALE = 2.5
ALPHA = (2 * DEPTH) ** 0.25
LN_EPS = 1e-5
RMS_EPS = 1e-6

_SWQ_OFF = 4 * HG_WIDTH
_SGU_OFF = _SWQ_OFF + SW_WIDTH
_SWK_OFF = _SGU_OFF + 2 * SG_WIDTH


def _regroup_columns(a):
    k0 = _SWQ_OFF + SW_WIDTH
    k1 = k0 + 2 * SW_KV_WIDTH
    return jnp.concatenate([a[..., :k0], a[..., k1:], a[..., k0:k1]], axis=-1)


MOE_ROWS = 256
TOKEN_GROUPS = 2
ROUTER_TILE = 512
COMBINE_TILE = 256
VMEM_LIMIT = 56 * 1024 * 1024

_LEVEL_SIZES = (64, 32, 16, 8, 4, 2, 1)
_N_LEVELS = len(_LEVEL_SIZES)


def _hgrn_constants():
    t = np.arange(CHUNK)[:, None]
    u = np.arange(CHUNK)[None, :]
    mats = []
    level = np.full((CHUNK, CHUNK), -1, np.int32)
    for li, s in enumerate(_LEVEL_SIZES):
        blk = t // s
        odd = (blk % 2) == 1
        m_query = (u >= blk * s) & (u <= t)
        m_key = (u > t) & (u <= (blk + 1) * s - 1)
        mats.append(np.where(odd, m_query, m_key))
        pair = ((t // s) % 2 == 1) & ((u // s) == (t // s) - 1)
        level[pair] = li
    mats.append(u <= t)
    mats.append(u > t)
    level[np.arange(CHUNK), np.arange(CHUNK)] = _N_LEVELS
    return np.concatenate(mats, 0).astype(np.float32), level


_HGRN_MSTACK, _HGRN_LEVEL = _hgrn_constants()


def _layer_norm(x, g, b):
    mu = jnp.mean(x, axis=-1, keepdims=True)
    xc = x - mu
    var = jnp.mean(xc * xc, axis=-1, keepdims=True)
    return xc * lax.rsqrt(var + LN_EPS) * g + b


def _silu(x):
    return x / (1.0 + jnp.exp(-x))


def _gelu(x):
    return 0.5 * x * (1.0 + lax.erf(x * (2.0 ** -0.5)))


def _dot(a, b):
    return jnp.dot(a, b, preferred_element_type=F32)


def _dot_nt(a, b):
    return lax.dot_general(a, b, (((1,), (1,)), ((), ())), preferred_element_type=F32)


def _dot_tn(a, b):
    return lax.dot_general(a, b, (((0,), (0,)), ((), ())), preferred_element_type=F32)


def _ln_kernel(x_ref, g_ref, b_ref, of_ref, ob_ref):
    y = _layer_norm(x_ref[...], g_ref[...], b_ref[...])
    of_ref[...] = y
    ob_ref[...] = y.astype(BF16)


def _ln(x, g, b, tm=512):
    t, d = x.shape
    tm = min(tm, t)
    row = pl.BlockSpec((tm, d), lambda i: (i, 0))
    vec = pl.BlockSpec((1, d), lambda i: (0, 0))
    return pl.pallas_call(
        _ln_kernel,
        grid=(t // tm,),
        in_specs=[row, vec, vec],
        out_specs=[row, row],
        out_shape=[jax.ShapeDtypeStruct((t, d), F32), jax.ShapeDtypeStruct((t, d), BF16)],
        name="ln0",
    )(x, g.reshape(1, d), b.reshape(1, d))


def _mm_bias_kernel(a_ref, w_ref, b_ref, c_ref, o_ref, co_ref):
    o_ref[...] = _dot(a_ref[...], w_ref[...]) + b_ref[...]
    co_ref[...] = c_ref[...].astype(BF16)


def _in_proj(a, w, b, cast, layer, tm=512):
    t, k = a.shape
    n = w.shape[1]
    tm = min(tm, t)
    tn = n // 2
    n_i = t // tm
    steps = (n // tn) * n_i
    rows, cols = cast.shape[0] // DEPTH, cast.shape[1]
    per_step = rows // steps
    assert per_step * steps == rows and per_step % 16 == 0
    return pl.pallas_call(
        _mm_bias_kernel,
        grid=(n // tn, n_i),
        in_specs=[pl.BlockSpec((tm, k), lambda j, i: (i, 0)),
                  pl.BlockSpec((k, tn), lambda j, i: (0, j)),
                  pl.BlockSpec((1, tn), lambda j, i: (0, j)),
                  pl.BlockSpec((per_step, cols), lambda j, i: (layer * steps + j * n_i + i, 0))],
        out_specs=[pl.BlockSpec((tm, tn), lambda j, i: (i, j)),
                   pl.BlockSpec((per_step, cols), lambda j, i: (j * n_i + i, 0))],
        out_shape=[jax.ShapeDtypeStruct((t, n), F32), jax.ShapeDtypeStruct((rows, cols), BF16)],
        compiler_params=pltpu.CompilerParams(vmem_limit_bytes=VMEM_LIMIT),
        name="in_proj",
    )(a, w, b.reshape(1, n), cast)


def _mixer_call(kernel_fn, name, proj3, in_specs, args, width, semantics, casts, layer, scratch_shapes=()):
    b, s, _ = proj3.shape
    nc = s // CHUNK
    n_in, n_cast = len(in_specs), len(casts)
    cast_in, cast_out, cast_shapes = [], [], []
    for w_all in casts:
        cols = w_all.shape[1]
        rows = w_all.shape[0] // DEPTH
        per_step = rows // (b * nc)
        assert per_step * b * nc == rows and per_step % 16 == 0
        cast_in.append(pl.BlockSpec((per_step, cols), lambda bi, ci: (layer * (b * nc) + bi * nc + ci, 0)))
        cast_out.append(pl.BlockSpec((per_step, cols), lambda bi, ci: (bi * nc + ci, 0)))
        cast_shapes.append(jax.ShapeDtypeStruct((rows, cols), BF16))

    def body(*refs):
        w_in = refs[n_in:n_in + n_cast]
        out = refs[n_in + n_cast]
        w_out = refs[n_in + n_cast + 1:n_in + 2 * n_cast + 1]
        kernel_fn(*refs[:n_in], out, *refs[n_in + 2 * n_cast + 1:])
        for src, dst in zip(w_in, w_out):
            dst[...] = src[...].astype(BF16)

    return pl.pallas_call(
        body,
        grid=(b, nc),
        in_specs=list(in_specs) + cast_in,
        out_specs=[pl.BlockSpec((None, CHUNK, width), lambda bi, ci: (bi, ci, 0))] + cast_out,
        out_shape=[jax.ShapeDtypeStruct((b, s, width), BF16)] + cast_shapes,
        scratch_shapes=list(scratch_shapes),
        compiler_params=pltpu.CompilerParams(dimension_semantics=semantics, vmem_limit_bytes=VMEM_LIMIT),
        name=name,
    )(*args, *casts)


def _hgrn_kernel(q_ref, f_ref, i_ref, g_ref, llb_ref, l1lb_ref, oml_ref, gain_ref,
                 mstack_ref, level_ref, o_ref, state_ref):
    c = pl.program_id(1)

    @pl.when(c == 0)
    def _():
        state_ref[...] = jnp.zeros_like(state_ref)

    q = _silu(q_ref[...])
    z = f_ref[...]
    ez = jnp.exp(-jnp.abs(z))
    log_sig = jnp.minimum(z, 0.0) - jnp.log(1.0 + ez)
    sig_neg = jnp.where(z >= 0.0, ez, 1.0) / (1.0 + ez)
    k = oml_ref[...] * sig_neg
    a = llb_ref[...]
    bb = l1lb_ref[...] + log_sig
    log_f = jnp.maximum(a, bb) + jnp.log(1.0 + jnp.exp(-jnp.abs(a - bb)))

    lf_hi = log_f.astype(BF16)
    lf_lo = (log_f - lf_hi.astype(F32)).astype(BF16)
    m = mstack_ref[...]
    sums = _dot(m, lf_hi) + _dot(m, lf_lo)
    decay = jnp.exp(sums)

    rows = lax.broadcasted_iota(jnp.int32, (CHUNK, 1), 0)
    factors = []
    for li, s in enumerate(_LEVEL_SIZES):
        is_query = ((rows // s) % 2) == 1
        factors.append((jnp.where(is_query, q, k) * decay[li * CHUNK:(li + 1) * CHUNK]).astype(BF16))
    cum = _N_LEVELS * CHUNK
    q_in = (q * decay[cum:cum + CHUNK]).astype(BF16)
    k_dec = (k * decay[cum + CHUNK:cum + 2 * CHUNK]).astype(BF16)
    end_decay = decay[cum + CHUNK - 1:cum + CHUNK]
    qb = q.astype(BF16)
    kb = k.astype(BF16)
    vb = i_ref[...].astype(BF16)
    level = level_ref[...]
    gate = _silu(g_ref[...])
    gain = gain_ref[...]

    for h in range(HG_HEADS):
        hs = slice(h * HG_DK, (h + 1) * HG_DK)
        scores = jnp.zeros((CHUNK, CHUNK), F32)
        for li in range(_N_LEVELS):
            fl = factors[li][:, hs]
            scores = jnp.where(level == li, _dot_nt(fl, fl), scores)
        scores = jnp.where(level == _N_LEVELS, _dot_nt(qb[:, hs], kb[:, hs]), scores)
        st = state_ref[h]
        o = _dot(scores.astype(BF16), vb[:, hs]) + _dot_nt(q_in[:, hs], st.astype(BF16))
        state_ref[h] = st * end_decay[:, hs] + _dot_tn(vb[:, hs], k_dec[:, hs])
        ms = jnp.mean(o * o, axis=-1, keepdims=True)
        o = o * lax.rsqrt(ms + RMS_EPS) * gain[:, hs] * gate[:, hs]
        o_ref[:, hs] = o.astype(o_ref.dtype)


def _hgrn(proj3, llb, l1lb, oml, gain, casts, layer):
    w = HG_WIDTH

    def col(j):
        return pl.BlockSpec((None, CHUNK, w), lambda bi, ci, j=j: (bi, ci, j))

    vec = pl.BlockSpec((1, w), lambda bi, ci: (0, 0))
    nm = _HGRN_MSTACK.shape[0]
    return _mixer_call(
        _hgrn_kernel, "hgrn2", proj3,
        [col(0), col(1), col(2), col(3), vec, vec, vec, vec,
         pl.BlockSpec((nm, CHUNK), lambda bi, ci: (0, 0)),
         pl.BlockSpec((CHUNK, CHUNK), lambda bi, ci: (0, 0))],
        (proj3, proj3, proj3, proj3, llb.reshape(1, w), l1lb.reshape(1, w), oml.reshape(1, w),
         gain.reshape(1, w), jnp.asarray(_HGRN_MSTACK, BF16), jnp.asarray(_HGRN_LEVEL)),
        w, ("parallel", "arbitrary"), casts, layer,
        scratch_shapes=[pltpu.VMEM((HG_HEADS, HG_DK, HG_DK), F32)])


def _swa_kernel(q_ref, kp_ref, kc_ref, vp_ref, vc_ref, sink_ref, gain_ref, o_ref):
    n = pl.program_id(1)
    g = SW_Q_HEADS // SW_KV_HEADS
    hd = SW_HEAD_DIM
    assert (hd ** -0.5) == 2.0 ** round(np.log2(hd ** -0.5))
    q = (q_ref[...] * (hd ** -0.5)).astype(BF16)
    kband = jnp.concatenate([kp_ref[...], kc_ref[...]], axis=0).astype(BF16)
    vband = jnp.concatenate([vp_ref[...], vc_ref[...]], axis=0).astype(BF16)
    t = lax.broadcasted_iota(jnp.int32, (CHUNK, 2 * CHUNK), 0)
    s = lax.broadcasted_iota(jnp.int32, (CHUNK, 2 * CHUNK), 1)
    rel = t + CHUNK - s
    mask = (rel >= 0) & (rel < CHUNK) & ((s >= CHUNK) | (n > 0))
    bias = jnp.where(mask, 0.0, -jnp.inf)
    kvs = [(kband[:, kv * hd:(kv + 1) * hd], vband[:, kv * hd:(kv + 1) * hd]) for kv in range(SW_KV_HEADS)]
    outs = []
    for h in range(SW_Q_HEADS):
        k_h, v_h = kvs[h // g]
        sc = _dot_nt(q[:, h * hd:(h + 1) * hd], k_h) + bias
        sink = sink_ref[h]
        mx = jnp.maximum(jnp.max(sc, axis=-1, keepdims=True), sink)
        p = jnp.exp(sc - mx)
        denom = jnp.sum(p, axis=-1, keepdims=True) + jnp.exp(sink - mx)
        outs.append(_dot(p.astype(BF16), v_h) / denom)
    o = jnp.concatenate(outs, axis=-1)
    ms = jnp.mean(o * o, axis=-1, keepdims=True)
    o_ref[...] = (o * lax.rsqrt(ms + RMS_EPS) * gain_ref[...]).astype(o_ref.dtype)


def _swa(proj3, sinks, gain, casts, layer):
    q_col = _SWQ_OFF // SW_WIDTH
    k_col = _SWK_OFF // SW_KV_WIDTH
    v_col = k_col + 1

    def kv_spec(col, prev):
        if prev:
            return pl.BlockSpec((None, CHUNK, SW_KV_WIDTH), lambda bi, ni: (bi, jnp.maximum(ni - 1, 0), col))
        return pl.BlockSpec((None, CHUNK, SW_KV_WIDTH), lambda bi, ni: (bi, ni, col))

    return _mixer_call(
        _swa_kernel, "swa", proj3,
        [pl.BlockSpec((None, CHUNK, SW_WIDTH), lambda bi, ni: (bi, ni, q_col)),
         kv_spec(k_col, True), kv_spec(k_col, False),
         kv_spec(v_col, True), kv_spec(v_col, False),
         pl.BlockSpec(memory_space=pltpu.SMEM),
         pl.BlockSpec((1, SW_WIDTH), lambda bi, ni: (0, 0))],
        (proj3, proj3, proj3, proj3, proj3, sinks.astype(F32), gain.reshape(1, SW_WIDTH)),
        SW_WIDTH, ("parallel", "parallel"), casts, layer)


def _sgu_kernel(u_ref, v_ref, lng_ref, lnb_ref, w_ref, bs_ref, gain_ref, o_ref):
    u = _gelu(u_ref[...])
    v = _layer_norm(_gelu(v_ref[...]), lng_ref[...], lnb_ref[...]).astype(BF16)
    r = lax.broadcasted_iota(jnp.int32, (CHUNK, CHUNK), 0)
    c = lax.broadcasted_iota(jnp.int32, (CHUNK, CHUNK), 1)
    tril = c <= r
    bs = bs_ref[...]
    parts = []
    for gi in range(SG_GROUPS):
        gs = slice(gi * CHUNK, (gi + 1) * CHUNK)
        w = jnp.where(tril, w_ref[gi], 0.0).astype(BF16)
        parts.append(_dot(w, v[:, gs]) + bs[:, gi:gi + 1])
    o = u * jnp.concatenate(parts, axis=-1)
    ms = jnp.mean(o * o, axis=-1, keepdims=True)
    o_ref[...] = (o * lax.rsqrt(ms + RMS_EPS) * gain_ref[...]).astype(o_ref.dtype)


def _sgu(proj3, ln_g, ln_b, w_s, b_s, gain, casts, layer):
    w = SG_WIDTH
    u_col = _SGU_OFF // SG_WIDTH
    vec = pl.BlockSpec((1, w), lambda bi, ci: (0, 0))
    return _mixer_call(
        _sgu_kernel, "sgu", proj3,
        [pl.BlockSpec((None, CHUNK, w), lambda bi, ci: (bi, ci, u_col)),
         pl.BlockSpec((None, CHUNK, w), lambda bi, ci: (bi, ci, u_col + 1)),
         vec, vec,
         pl.BlockSpec((SG_GROUPS, CHUNK, CHUNK), lambda bi, ci: (0, 0, 0)),
         pl.BlockSpec((CHUNK, SG_GROUPS), lambda bi, ci: (0, 0)),
         vec],
        (proj3, proj3, ln_g.reshape(1, w), ln_b.reshape(1, w), w_s, b_s.T, gain.reshape(1, w)),
        w, ("parallel", "parallel"), casts, layer)


LANES = 128
SLAB = D_MODEL // 2 // LANES


def _pack_pair(low, high):
    def bits(v):
        return lax.bitcast_convert_type(v.astype(BF16).astype(F32), jnp.uint32)

    return (bits(high) & jnp.uint32(0xFFFF0000)) | (bits(low) >> 16)


def _pack_rows(x):
    half = x.shape[1] // 2
    return _pack_pair(x[:, :half], x[:, half:])


def _unpack_words(w):
    return (lax.bitcast_convert_type(w << 16, F32),
            lax.bitcast_convert_type(w & jnp.uint32(0xFFFF0000), F32))


def _store_slabs(ref, x):
    n = x.shape[0]
    for j in range(SLAB):
        ref[pl.ds(j, n, stride=SLAB), :] = x[:, j * LANES:(j + 1) * LANES]


def _load_packed_rows(ref, n, first=0, stride=SLAB):
    lows, highs = [], []
    for j in range(SLAB):
        lo, hi = _unpack_words(ref[pl.ds(first + j, n, stride=stride), :])
        lows.append(lo.astype(BF16))
        highs.append(hi.astype(BF16))
    return jnp.concatenate(lows + highs, axis=1)


def _out_proj_kernel(oa_ref, ob_ref, oc_ref, w_ref, h_ref, g_ref, b_ref, of_ref, os_ref):
    mixed = jnp.concatenate([oa_ref[...], ob_ref[...], oc_ref[...]], axis=-1)
    y = ALPHA * h_ref[...] + _dot(mixed, w_ref[...])
    out = _layer_norm(y, g_ref[...], b_ref[...])
    of_ref[...] = out
    _store_slabs(os_ref, _pack_rows(out))


def _out_proj(oa, ob, oc, w, h, g, b, tm=512):
    t, d = h.shape
    tm = min(tm, t)
    vec = pl.BlockSpec((1, d), lambda i: (0, 0))
    return pl.pallas_call(
        _out_proj_kernel,
        grid=(t // tm,),
        in_specs=[pl.BlockSpec((tm, oa.shape[1]), lambda i: (i, 0)),
                  pl.BlockSpec((tm, ob.shape[1]), lambda i: (i, 0)),
                  pl.BlockSpec((tm, oc.shape[1]), lambda i: (i, 0)),
                  pl.BlockSpec(w.shape, lambda i: (0, 0)),
                  pl.BlockSpec((tm, d), lambda i: (i, 0)), vec, vec],
        out_specs=[pl.BlockSpec((tm, d), lambda i: (i, 0)),
                   pl.BlockSpec((tm * SLAB, LANES), lambda i: (i, 0))],
        out_shape=[jax.ShapeDtypeStruct((t, d), F32), jax.ShapeDtypeStruct((t * SLAB, LANES), jnp.uint32)],
        compiler_params=pltpu.CompilerParams(vmem_limit_bytes=VMEM_LIMIT),
        name="out_proj_ln1",
    )(oa, ob, oc, w, h, g.reshape(1, d), b.reshape(1, d))


def _first_index_of_max(x, iota, size, axis):
    mx = jnp.max(x, axis=axis, keepdims=True)
    idx = jnp.min(jnp.where(x == mx, iota, size), axis=axis, keepdims=True)
    return mx, idx


def _router_kernel(tiles_per_group, h_ref, wt_ref, bias_ref, eidx_ref, gate_ref, count_ref, run_ref):
    i = pl.program_id(0)

    @pl.when(i % tiles_per_group == 0)
    def _():
        run_ref[...] = jnp.zeros_like(run_ref)

    tm = h_ref.shape[0]
    per_group = N_EXPERTS // N_EXPERT_GROUPS
    def split(v):
        hi = v.astype(BF16)
        return hi, (v - hi.astype(F32)).astype(BF16)

    w_hi, w_lo = split(wt_ref[...])
    h_hi, h_lo = split(h_ref[...])
    logits = _dot_nt(w_hi, h_hi) + (_dot_nt(w_hi, h_lo) + _dot_nt(w_lo, h_hi))
    scores = 1.0 / (1.0 + jnp.exp(-logits))
    sel = scores + bias_ref[...]
    sel3 = sel.reshape(N_EXPERT_GROUPS, per_group, tm)
    io3 = lax.broadcasted_iota(jnp.int32, sel3.shape, 1)
    m1, i1 = _first_index_of_max(sel3, io3, per_group, 1)
    m2 = jnp.max(jnp.where(io3 == i1, -jnp.inf, sel3), axis=1, keepdims=True)
    grp = (m1 + m2).reshape(N_EXPERT_GROUPS, tm)
    iog = lax.broadcasted_iota(jnp.int32, grp.shape, 0)
    keep = jnp.zeros(grp.shape, jnp.bool_)
    for _ in range(TOPK_GROUPS):
        _, gi = _first_index_of_max(grp, iog, N_EXPERT_GROUPS, 0)
        hit = iog == gi
        keep = keep | hit
        grp = jnp.where(hit, -jnp.inf, grp)
    keep3 = jnp.broadcast_to(keep.reshape(N_EXPERT_GROUPS, 1, tm), sel3.shape)
    cand = jnp.where(keep3, sel3, -jnp.inf).reshape(N_EXPERTS, tm)
    ioe = lax.broadcasted_iota(jnp.int32, cand.shape, 0)
    chosen = jnp.zeros(cand.shape, F32)
    idxs, gvals = [], []
    for _ in range(TOP_K):
        _, ei = _first_index_of_max(cand, ioe, N_EXPERTS, 0)
        hit = ioe == ei
        idxs.append(ei)
        gvals.append(jnp.sum(jnp.where(hit, scores, 0.0), axis=0, keepdims=True))
        chosen = jnp.where(hit, 1.0, chosen)
        cand = jnp.where(hit, -jnp.inf, cand)
    gsum = functools.reduce(lambda a, b: a + b, gvals)
    for kk in range(TOP_K):
        eidx_ref[kk:kk + 1, :] = idxs[kk]
        gate_ref[kk:kk + 1, :] = gvals[kk] / gsum * ROUTED_SCALE
    run = run_ref[...] + jnp.sum(chosen, axis=1, keepdims=True)
    run_ref[...] = run
    count_ref[...] = run.astype(jnp.int32)


def _router(h, router_w, router_bias, n_groups, tm=ROUTER_TILE):
    t, d = h.shape
    tm = min(tm, t // n_groups)
    tiles_per_group = t // n_groups // tm
    slot = pl.BlockSpec((TOP_K, tm), lambda i: (0, i))
    return pl.pallas_call(
        functools.partial(_router_kernel, tiles_per_group),
        grid=(t // tm,),
        in_specs=[pl.BlockSpec((tm, d), lambda i: (i, 0)),
                  pl.BlockSpec((N_EXPERTS, d), lambda i: (0, 0)),
                  pl.BlockSpec((N_EXPERTS, 1), lambda i: (0, 0))],
        out_specs=[slot, slot, pl.BlockSpec((None, N_EXPERTS, 1), lambda i: (i // tiles_per_group, 0, 0))],
        out_shape=[jax.ShapeDtypeStruct((TOP_K, t), jnp.int32),
                   jax.ShapeDtypeStruct((TOP_K, t), F32),
                   jax.ShapeDtypeStruct((n_groups, N_EXPERTS, 1), jnp.int32)],
        scratch_shapes=[pltpu.VMEM((N_EXPERTS, 1), F32)],
        compiler_params=pltpu.CompilerParams(dimension_semantics=("arbitrary",),
                                             vmem_limit_bytes=VMEM_LIMIT),
        name="router",
    )(h, router_w.T, router_bias.reshape(N_EXPERTS, 1))


_ISSUE_UNROLL = 16
_TOP_K_BITS = TOP_K.bit_length() - 1
assert 1 << _TOP_K_BITS == TOP_K


_OUT_BUFFERS = 3


def _experts_kernel(be_ref, bg_ref, wplan_ref, nu_ref, src_ref, dstp_ref, dstc_ref, h_hbm,
                    wg_hbm, wu_hbm, wd_hbm, y_hbm, tok_ref, xbuf, obuf0, obuf1, obuf2, wg_buf, wu_buf, wd_buf,
                    lsem, ssem, wsem):
    b = pl.program_id(0)
    nu = nu_ref[0]
    obuf = (obuf0, obuf1, obuf2)
    rows = xbuf.shape[0] // SLAB
    d = wg_buf.shape[1]
    f = wg_buf.shape[2]
    wslot = wplan_ref[1, b]

    def weight_copies(expert, slot_):
        return [pltpu.make_async_copy(src.at[expert], dst.at[slot_], wsem.at[slot_])
                for src, dst in ((wg_hbm, wg_buf), (wu_hbm, wu_buf), (wd_hbm, wd_buf))]

    @pl.when((b < nu) & (wplan_ref[0, b] == 1))
    def _():
        @pl.when(b == 0)
        def _():
            for c in weight_copies(be_ref[0], wslot):
                c.start()
        for c in weight_copies(be_ref[b], wslot):
            c.wait()
        nxt = wplan_ref[2, b]

        @pl.when(nxt >= 0)
        def _():
            for c in weight_copies(nxt, 1 - wslot):
                c.start()

    wg_ref, wu_ref, wd_ref = wg_buf.at[wslot], wu_buf.at[wslot], wd_buf.at[wslot]
    pad0 = y_hbm.shape[0] - 2 * rows * SLAB

    def scatter_row(dst_ref, r, src_slot, priority):
        dst = pl.multiple_of(dst_ref[0, 0, r], SLAB)
        buf_rows = pl.ds(pl.multiple_of(r * SLAB, SLAB), SLAB)
        pltpu.make_async_copy(obuf[src_slot].at[buf_rows], y_hbm.at[pl.ds(dst, SLAB)],
                              ssem.at[src_slot]).start(priority=priority)

    def wait_scatter(s):
        pltpu.make_async_copy(obuf[s], y_hbm.at[pl.ds(0, rows * SLAB)], ssem.at[s]).wait()

    @pl.when(b == 0)
    def _():
        for s in range(_OUT_BUFFERS):
            obuf[s][...] = jnp.zeros_like(obuf[s])
        for s in range(2):
            pltpu.make_async_copy(obuf[s], y_hbm.at[pl.ds(pad0 + s * rows * SLAB, rows * SLAB)],
                                  ssem.at[s]).start()
        for s in range(2):
            wait_scatter(s)

    group = bg_ref[b]

    @pl.when((b < nu) & ((b == 0) | (group != bg_ref[jnp.maximum(b - 1, 0)])))
    def _():
        load = pltpu.make_async_copy(h_hbm.at[group], tok_ref, lsem.at[0])
        load.start()
        load.wait()

    def block(slot, with_scatter, live=None):
        live = rows if live is None else live
        other = (slot - 1) % _OUT_BUFFERS

        def gather(i, carry):
            for u in range(_ISSUE_UNROLL):
                r = i * _ISSUE_UNROLL + u
                src = pl.multiple_of(src_ref[0, 0, r], SLAB)
                xbuf[pl.ds(pl.multiple_of(r * SLAB, SLAB), SLAB), :] = tok_ref[pl.ds(src, SLAB), :]
            return carry
        lax.fori_loop(0, live // _ISSUE_UNROLL, gather, 0)

        if with_scatter:
            for r in range(rows):
                scatter_row(dstp_ref, r, other, r % 2)

        x = _load_packed_rows(xbuf, live)
        acts = []
        fh = f // 2
        for j in range(2):
            gate = _dot(x, wg_ref[:, j * fh:(j + 1) * fh])
            up = _dot(x, wu_ref[:, j * fh:(j + 1) * fh])
            acts.append((_silu(gate) * up).astype(BF16))
        act = jnp.concatenate(acts, axis=1)
        half = d // 2
        cw = 2 * LANES
        for q in range(half // cw):
            words = _pack_pair(_dot(act, wd_ref[:, q * cw:(q + 1) * cw]),
                               _dot(act, wd_ref[:, half + q * cw:half + (q + 1) * cw]))
            for jj in range(cw // LANES):
                obuf[slot][pl.ds(q * (cw // LANES) + jj, live, stride=SLAB), :] = words[:, jj * LANES:(jj + 1) * LANES]

    def tail(slot):
        for back in (1, 2):
            @pl.when(b >= back)
            def _(back=back):
                wait_scatter((slot - back) % _OUT_BUFFERS)

        def body(r, carry):
            scatter_row(dstc_ref, r, slot, 0)
            return carry
        lax.fori_loop(0, rows, body, 0)
        wait_scatter(slot)

    @pl.when(b == 0)
    def _():
        block(0, False)

    small = wplan_ref[3, b] == 1
    for s in range(_OUT_BUFFERS):
        @pl.when((b >= 1) & (b < nu) & (b % _OUT_BUFFERS == s))
        def _(s=s):
            @pl.when(b >= _OUT_BUFFERS)
            def _():
                wait_scatter(s)

            @pl.when(small)
            def _():
                block(s, True, rows // 2)

            @pl.when(jnp.logical_not(small))
            def _():
                block(s, True)

        @pl.when((b == nu - 1) & (b % _OUT_BUFFERS == s))
        def _(s=s):
            tail(s)


def _experts(h_groups, row_src, row_dst, block_e, block_g, weight_plan, n_used, wg, wu, wd, n_slots):
    nb, _, rows = row_src.shape
    d, f = wg.shape[-2], wg.shape[-1]
    group_rows = h_groups.shape[1]

    def smem(shift):
        def index(b, be, bg, wplan, nu):
            return (jnp.maximum(jnp.minimum(b + shift, nu[0] - 1), 0), 0, 0)
        return pl.BlockSpec((1, 1, rows), index, memory_space=pltpu.SMEM)

    hbm = pl.BlockSpec(memory_space=pl.ANY)
    grid_spec = pltpu.PrefetchScalarGridSpec(
        num_scalar_prefetch=4,
        grid=(nb,),
        in_specs=[smem(0), smem(-1), smem(0), hbm, hbm, hbm, hbm],
        out_specs=hbm,
        scratch_shapes=[pltpu.VMEM((group_rows, LANES), jnp.uint32)]
        + [pltpu.VMEM((rows * SLAB, LANES), jnp.uint32)] * (1 + _OUT_BUFFERS)
        + [pltpu.VMEM((2, d, f), BF16), pltpu.VMEM((2, d, f), BF16), pltpu.VMEM((2, f, d), BF16)]
        + [pltpu.SemaphoreType.DMA((1,)), pltpu.SemaphoreType.DMA((_OUT_BUFFERS,)),
           pltpu.SemaphoreType.DMA((2,))],
    )
    return pl.pallas_call(
        _experts_kernel,
        grid_spec=grid_spec,
        out_shape=jax.ShapeDtypeStruct(((n_slots + 2 * rows) * SLAB, LANES), jnp.uint32),
        compiler_params=pltpu.CompilerParams(dimension_semantics=("arbitrary",),
                                             vmem_limit_bytes=VMEM_LIMIT),
        name="experts",
    )(block_e, block_g, weight_plan, n_used, row_src, row_dst, row_dst, h_groups, wg, wu, wd)


_COMBINE_SUB = 32


def _combine_kernel(h_ref, gate_ref, *refs):
    y_refs = refs[:TOP_K]
    sg_ref, su_ref, sd_ref, g_ref, b_ref, of_ref, ob_ref, moe_ref = refs[TOP_K:]
    h = h_ref[...]
    tm = h.shape[0]
    half = SLAB * LANES

    def weighted_sum(s, carry):
        row0 = pl.multiple_of(s * _COMBINE_SUB, _COMBINE_SUB)
        rs = pl.ds(row0, _COMBINE_SUB)
        gates = gate_ref[rs, :]
        for j in range(SLAB):
            lo_acc = hi_acc = None
            for kk in range(TOP_K):
                lo, hi = _unpack_words(y_refs[kk][pl.ds(row0 * SLAB + j, _COMBINE_SUB, stride=SLAB), :])
                gk = gates[:, kk:kk + 1]
                lo_acc = gk * lo if kk == 0 else lo_acc + gk * lo
                hi_acc = gk * hi if kk == 0 else hi_acc + gk * hi
            moe_ref[rs, j * LANES:(j + 1) * LANES] = lo_acc
            moe_ref[rs, half + j * LANES:half + (j + 1) * LANES] = hi_acc
        return carry
    lax.fori_loop(0, tm // _COMBINE_SUB, weighted_sum, 0)
    hb = h.astype(BF16)
    act = _silu(_dot(hb, sg_ref[...])) * _dot(hb, su_ref[...])
    y = ALPHA * h + _dot(act.astype(BF16), sd_ref[...]) + moe_ref[...]
    out = _layer_norm(y, g_ref[...], b_ref[...])
    of_ref[...] = out
    ob_ref[...] = out.astype(BF16)


def _combine(h, gates_t, y_slots, sg, su, sd, g, b, tm=COMBINE_TILE):
    t, d = h.shape
    n = t // tm
    f = sg.shape[1]
    vec = pl.BlockSpec((1, d), lambda i: (0, 0))
    row = pl.BlockSpec((tm, d), lambda i: (i, 0))
    return pl.pallas_call(
        _combine_kernel,
        grid=(n,),
        in_specs=[row, pl.BlockSpec((tm, TOP_K), lambda i: (i, 0))]
        + [pl.BlockSpec((tm * SLAB, LANES), lambda i, kk=kk: (kk * n + i, 0)) for kk in range(TOP_K)]
        + [pl.BlockSpec((d, f), lambda i: (0, 0)),
            pl.BlockSpec((d, f), lambda i: (0, 0)),
            pl.BlockSpec((f, d), lambda i: (0, 0)),
            vec, vec],
        out_specs=[row, row],
        out_shape=[jax.ShapeDtypeStruct((t, d), F32), jax.ShapeDtypeStruct((t, d), BF16)],
        scratch_shapes=[pltpu.VMEM((tm, d), F32)],
        compiler_params=pltpu.CompilerParams(dimension_semantics=("parallel",),
                                             vmem_limit_bytes=VMEM_LIMIT),
        name="combine_ln2",
    )(h, gates_t, *([y_slots] * TOP_K), sg, su, sd, g.reshape(1, d), b.reshape(1, d))


def _moe(h, h_slab, router_w, router_bias, wg, wu, wd, sg, su, sd, ln_g, ln_b):
    t, d = h.shape
    tk = t * TOP_K
    n_groups = TOKEN_GROUPS if t % (TOKEN_GROUPS * ROUTER_TILE) == 0 else 1
    tg = t // n_groups
    n_buckets = n_groups * N_EXPERTS
    n_blocks = tk // MOE_ROWS + n_buckets
    eidx, gates, counts = _router(h, router_w, router_bias, n_groups)
    counts = counts.reshape(n_buckets)
    n_rows = n_blocks * MOE_ROWS
    padded = (counts + MOE_ROWS - 1) // MOE_ROWS * MOE_ROWS
    pend = jnp.cumsum(padded)
    pad_end = jnp.cumsum(padded - counts)
    pad_key = jnp.sum(pad_end[None, :] <= jnp.arange(n_rows - tk, dtype=jnp.int32)[:, None], axis=1)
    slot_group = (jnp.arange(tk, dtype=jnp.int32) >> _TOP_K_BITS) // tg
    keys = jnp.concatenate([slot_group * N_EXPERTS + eidx.T.reshape(tk), pad_key.astype(jnp.int32)])
    index_bits = (n_rows - 1).bit_length()
    assert (n_buckets + 1) << index_bits < 2 ** 31
    packed = lax.sort((keys << index_bits) | jnp.arange(n_rows, dtype=jnp.int32))
    row_slot = packed & ((1 << index_bits) - 1)
    real = row_slot < tk
    row = jnp.arange(n_rows, dtype=jnp.int32)
    row_src = jnp.where(real, ((row_slot >> _TOP_K_BITS) % tg) * SLAB, 0)
    plane_row = (row_slot & (TOP_K - 1)) * t + (row_slot >> _TOP_K_BITS)
    row_dst = jnp.where(real, plane_row, tk + row % (2 * MOE_ROWS)) * SLAB
    block_start = jnp.arange(n_blocks, dtype=jnp.int32) * MOE_ROWS
    block_bucket = jnp.minimum(jnp.sum(pend[None, :] <= block_start[:, None], axis=1), n_buckets - 1).astype(jnp.int32)
    n_used = (pend[-1] // MOE_ROWS).astype(jnp.int32)
    block_e = block_bucket % N_EXPERTS
    block = jnp.arange(n_blocks, dtype=jnp.int32)
    first = (block == 0) | (block_bucket != jnp.roll(block_bucket, 1))
    seq = jnp.cumsum(first.astype(jnp.int32)) - 1
    later_first = jnp.where(first & (block < n_used), block, n_blocks)
    next_first = jnp.concatenate([lax.cummin(later_first, reverse=True)[1:], jnp.full((1,), n_blocks, jnp.int32)])
    next_e = jnp.where(next_first < n_blocks, block_e[jnp.minimum(next_first, n_blocks - 1)], -1)
    owner = block_bucket[:, None] == jnp.arange(n_buckets, dtype=jnp.int32)[None, :]
    row_in_bucket = block_start - jnp.sum(jnp.where(owner, (pend - padded)[None, :], 0), axis=1)
    block_real = jnp.sum(jnp.where(owner, counts[None, :], 0), axis=1) - row_in_bucket
    small = (block_real <= MOE_ROWS // 2).astype(jnp.int32)
    weight_plan = jnp.stack([first.astype(jnp.int32), seq % 2, next_e, small]).astype(jnp.int32)
    y_slots = _experts(h_slab.reshape(n_groups, tg * SLAB, LANES),
                       row_src.reshape(n_blocks, 1, MOE_ROWS), row_dst.reshape(n_blocks, 1, MOE_ROWS),
                       block_e, block_bucket // N_EXPERTS, weight_plan, n_used.reshape(1),
                       wg, wu, wd, tk)
    return _combine(h, gates.T, y_slots, sg, su, sd, ln_g, ln_b, tm=min(COMBINE_TILE, t))


def kernel(x, ln0_gain, ln0_bias, w_in, b_in, hg_lb_logits, sw_sinks, sg_ln_gain, sg_ln_bias, sg_w_s, sg_b_s, mix_gain, w_out, ln1_gain, ln1_bias, router_w, router_bias, exp_w_gate, exp_w_up, exp_w_down, sh_w_gate, sh_w_up, sh_w_down, ln2_gain, ln2_bias):
    b, s, d = x.shape
    t = b * s
    p = jax.nn.softmax(hg_lb_logits.astype(F32), axis=0)
    cs = jnp.cumsum(p, axis=0)
    lbs = cs - cs[0:1]
    a0, a1 = HG_WIDTH, HG_WIDTH + SW_WIDTH

    n_e, d_e, f_e = exp_w_gate.shape[1:]
    wg_rows, wu_rows = exp_w_gate.reshape(-1, f_e), exp_w_up.reshape(-1, f_e)
    wd_rows = exp_w_down.reshape(-1, d_e)
    h, hb = _ln(x.reshape(t, d), ln0_gain, ln0_bias)
    for l in range(DEPTH):
        proj, wd = _in_proj(hb, _regroup_columns(w_in[l]).astype(BF16), _regroup_columns(b_in[l]), wd_rows, l)
        proj3 = proj.reshape(b, s, D_IN)
        lb = lbs[l]
        o_a, wg = _hgrn(proj3, jnp.log(lb), jnp.log1p(-lb), 1.0 - lb, mix_gain[l, :a0], [wg_rows], l)
        o_b, wu = _swa(proj3, sw_sinks[l], mix_gain[l, a0:a1], [wu_rows], l)
        o_c, = _sgu(proj3, sg_ln_gain[l], sg_ln_bias[l], sg_w_s[l], sg_b_s[l], mix_gain[l, a1:], [], l)
        h, h_slab = _out_proj(o_a.reshape(t, -1), o_b.reshape(t, -1), o_c.reshape(t, -1),
                              w_out[l].astype(BF16), h, ln1_gain[l], ln1_bias[l])
        h, hb = _moe(h, h_slab, router_w[l], router_bias[l], wg.reshape(n_e, d_e, f_e),
                     wu.reshape(n_e, d_e, f_e), wd.reshape(n_e, f_e, d_e),
                     sh_w_gate[l].astype(BF16), sh_w_up[l].astype(BF16), sh_w_down[l].astype(BF16),
                     ln2_gain[l], ln2_bias[l])
    return h.reshape(b, s, d)
```
